```python
import math
import jax
import jax.numpy as jnp
from jax import lax
import numpy as np

D_MODEL = 1024
BATCH = 8
SEQ = 2048
DEPTH = 2
DEC_BATCH = 32
DEC_SEQ = 1
PAST_LEN = 8192
PAGE_SIZE = 128

N_EVEN = (DEPTH + 1) // 2
N_ODD = DEPTH // 2
H_A = 8
HD_A = 64
GDN_CONV = 4
GDN_CHUNK = 64
H_B = 8
DK_B = 64
DV_B = 64
HGRN_CHUNK = 32
H_C = 8
HD_C = 64
MOBA_BLOCK = 256
MOBA_TOPK = 3
MOBA_QBLOCK = 64
N_BUCKETS = 32
MAX_DISTANCE = 128
S5_GROUPS = 32
S5_GROUP_CH = 16
S5_STATE = 64
N_MEM = 256
MEM_H = 4
MEM_HD = 128
D_FF = 2816
FFN_CONV = 3
EPS = 1e-6
EV_IN = 4 * H_A * HD_A + 2 * H_A + H_B * (2 * DK_B + 2 * DV_B)
OD_IN = 3 * H_C * HD_C + S5_GROUPS * S5_GROUP_CH
EV_OUT = H_A * HD_A + H_B * DV_B
OD_OUT = H_C * HD_C + S5_GROUPS * S5_GROUP_CH

kernel_name = 'hybrid_gdn_hgrn2_moba_s5_decoder_step'


def _split(x, sizes):
    out, off = [], 0
    for s in sizes:
        out.append(x[..., off:off + s])
        off += s
    return out


def rmsnorm(x, g):
    xf = x.astype(jnp.float32)
    y = xf * lax.rsqrt(jnp.mean(xf * xf, axis=-1, keepdims=True) + EPS)
    return (y * g.astype(jnp.float32)).astype(x.dtype)


def l2norm(x):
    xf = x.astype(jnp.float32)
    return xf * lax.rsqrt(jnp.sum(xf * xf, axis=-1, keepdims=True) + EPS)


def causal_dwconv(x, buf, w):
    xp = jnp.concatenate([buf.astype(x.dtype), x], axis=1)
    y = lax.conv_general_dilated(xp, w[:, None, :].astype(x.dtype), window_strides=(1,), padding='VALID',
                                 dimension_numbers=('NWC', 'WIO', 'NWC'), feature_group_count=x.shape[-1])
    return y, xp[:, xp.shape[1] - (w.shape[0] - 1):]


def rel_bucket(dist):
    n = jnp.maximum(dist, 0)
    exact = N_BUCKETS // 2
    nf = jnp.maximum(n, exact).astype(jnp.float32)
    large = exact + (jnp.log(nf / exact) / math.log(MAX_DISTANCE / exact) * (N_BUCKETS - exact)).astype(jnp.int32)
    return jnp.where(n < exact, n, jnp.minimum(large, N_BUCKETS - 1))


def _pad_time(t, pad):
    return jnp.pad(t, [(0, 0), (0, pad)] + [(0, 0)] * (t.ndim - 2))


def _chunks(t, n, c):
    t = t.reshape(t.shape[0], n, c, *t.shape[2:])
    return jnp.moveaxis(t, (1, 2), (0, 3))


def _unchunk(o, L):
    n, B, H, c, d = o.shape
    return jnp.moveaxis(o, (0, 3), (1, 2)).reshape(B, n * c, H, d)[:, :L]


def gated_delta_rule(q, k, v, beta, g, S0):
    B, L, H, dk = q.shape
    dv = v.shape[-1]
    c = min(GDN_CHUNK, L)
    n = -(-L // c)
    pad = n * c - L
    q, k, v, beta, g = [_chunks(_pad_time(t.astype(jnp.float32), pad), n, c) for t in (q, k, v, beta, g)]
    G = jnp.cumsum(g, axis=-1)
    causal = jnp.tril(jnp.ones((c, c), bool))
    decay = jnp.exp(jnp.where(causal, G[..., :, None] - G[..., None, :], -jnp.inf))
    A = beta[..., :, None] * jnp.einsum('nbhid,nbhjd->nbhij', k, k) * decay
    A = jnp.where(jnp.tril(jnp.ones((c, c), bool), -1), A, 0.0)
    rhs = jnp.concatenate([beta[..., None] * v, (beta * jnp.exp(G))[..., None] * k], axis=-1)
    sol = lax.linalg.triangular_solve(A, rhs, left_side=True, lower=True, unit_diagonal=True)
    u_v, w = sol[..., :dv], sol[..., dv:]
    qk = jnp.einsum('nbhid,nbhjd->nbhij', q, k) * decay
    q_in = q * jnp.exp(G)[..., None]
    k_out = k * jnp.exp(G[..., -1:] - G)[..., None]
    g_last = jnp.exp(G[..., -1])

    def step(S, xs):
        u_v_c, w_c, qk_c, q_c, k_c, gl = xs
        u = u_v_c - jnp.einsum('bhck,bhkv->bhcv', w_c, S)
        o = jnp.einsum('bhck,bhkv->bhcv', q_c, S) + jnp.einsum('bhij,bhjv->bhiv', qk_c, u)
        S = S * gl[..., None, None] + jnp.einsum('bhck,bhcv->bhkv', k_c, u)
        return S, o

    S, o = lax.scan(step, S0.astype(jnp.float32), (u_v, w, qk, q_in, k_out, g_last))
    return _unchunk(o, L), S


def hgrn2_recurrence(q, k, logf, v, S0):
    B, L, H, dk = q.shape
    c = min(HGRN_CHUNK, L)
    n = -(-L // c)
    pad = n * c - L
    q, k, logf, v = [_chunks(_pad_time(t.astype(jnp.float32), pad), n, c) for t in (q, k, logf, v)]
    Bc = jnp.cumsum(logf, axis=-2)
    r = (c - 1) // 2
    ref = Bc[..., r:r + 1, :]
    att = jnp.einsum('nbhid,nbhjd->nbhij', q * jnp.exp(Bc - ref), k * jnp.exp(ref - Bc))
    att = jnp.where(jnp.tril(jnp.ones((c, c), bool)), att, 0.0)
    o_intra = jnp.einsum('nbhij,nbhjv->nbhiv', att, v)
    q_in = q * jnp.exp(Bc)
    k_out = k * jnp.exp(Bc[..., -1:, :] - Bc)
    f_last = jnp.exp(Bc[..., -1, :])

    def step(S, xs):
        q_c, k_c, v_c, fl, oi = xs
        o = oi + jnp.einsum('bhck,bhkv->bhcv', q_c, S)
        S = S * fl[..., None] + jnp.einsum('bhck,bhcv->bhkv', k_c, v_c)
        return S, o

    S, o = lax.scan(step, S0.astype(jnp.float32), (q_in, k_out, v, f_last, o_intra))
    return _unchunk(o, L), S


def _cplx_combine(e1, e2):
    a1r, a1i, b1r, b1i = e1
    a2r, a2i, b2r, b2i = e2
    return (a2r * a1r - a2i * a1i, a2r * a1i + a2i * a1r,
            a2r * b1r - a2i * b1i + b2r, a2r * b1i + a2i * b1r + b2i)


def s5_scan(u, h0_re, h0_im, A_re, A_im, log_dt, B_re, B_im, C_re, C_im, D):
    dt = jnp.exp(log_dt.astype(jnp.float32))[:, None]
    lr, li = A_re.astype(jnp.float32), A_im.astype(jnp.float32)
    mag = jnp.exp(lr * dt)
    ar, ai = mag * jnp.cos(li * dt), mag * jnp.sin(li * dt)
    den = lr * lr + li * li
    nr = ar - 1.0
    cr, ci = (nr * lr + ai * li) / den, (ai * lr - nr * li) / den
    Br, Bi = B_re.astype(jnp.float32), B_im.astype(jnp.float32)
    bbr = cr[..., None] * Br - ci[..., None] * Bi
    bbi = cr[..., None] * Bi + ci[..., None] * Br
    xr = jnp.einsum('blgn,gpn->blgp', u, bbr)
    xi = jnp.einsum('blgn,gpn->blgp', u, bbi)
    h0r, h0i = h0_re.astype(jnp.float32), h0_im.astype(jnp.float32)
    xr = xr.at[:, 0].add(ar * h0r - ai * h0i)
    xi = xi.at[:, 0].add(ar * h0i + ai * h0r)
    _, _, hr, hi = lax.associative_scan(_cplx_combine, (jnp.broadcast_to(ar, xr.shape), jnp.broadcast_to(ai, xr.shape), xr, xi), axis=1)
    y = (jnp.einsum('blgp,gnp->blgn', hr, C_re.astype(jnp.float32))
         - jnp.einsum('blgp,gnp->blgn', hi, C_im.astype(jnp.float32)) + D.astype(jnp.float32) * u)
    return y, hr[:, -1], hi[:, -1]


def moba_attend(q, k_all, v_all, pos0, rel_bias):
    B, L, H, hd = q.shape
    T = k_all.shape[1]
    nb = -(-T // MOBA_BLOCK)
    padT = nb * MOBA_BLOCK - T

    def blocks(t):
        return jnp.pad(t, ((0, 0), (0, padT), (0, 0), (0, 0))).reshape(B, nb, MOBA_BLOCK, H, hd).transpose(0, 3, 1, 2, 4)

    kb, vb = blocks(k_all), blocks(v_all)
    kmean = jnp.mean(kb.astype(jnp.float32), axis=3)
    qpos = pos0 + jnp.arange(L, dtype=jnp.int32)
    qblk = qpos // MOBA_BLOCK
    gate = jnp.einsum('blhd,bhnd->bhln', q.astype(jnp.float32), kmean)
    gate = jnp.where(jnp.arange(nb)[None, :] < qblk[:, None], gate, -jnp.inf)
    if nb < MOBA_TOPK:
        gate = jnp.pad(gate, ((0, 0), (0, 0), (0, 0), (0, MOBA_TOPK - nb)), constant_values=-jnp.inf)
    top_s, top_i = lax.top_k(gate, MOBA_TOPK)
    n_sel = MOBA_TOPK + 1
    idx = jnp.concatenate([jnp.minimum(top_i, nb - 1), jnp.broadcast_to(qblk[None, None, :, None], (B, H, L, 1))], axis=-1)
    ok = jnp.concatenate([jnp.isfinite(top_s), jnp.ones((B, H, L, 1), bool)], axis=-1)

    qb = MOBA_QBLOCK if L % MOBA_QBLOCK == 0 else L
    nq = L // qb
    q_x = q.reshape(B, nq, qb, H, hd).transpose(0, 1, 3, 2, 4).reshape(B * nq, H, qb, hd)
    idx_x = idx.reshape(B, H, nq, qb, n_sel).transpose(0, 2, 1, 3, 4).reshape(B * nq, H, qb, n_sel)
    ok_x = ok.reshape(B, H, nq, qb, n_sel).transpose(0, 2, 1, 3, 4).reshape(B * nq, H, qb, n_sel)
    pos_x = jnp.broadcast_to(qpos.reshape(1, nq, qb), (B, nq, qb)).reshape(B * nq, qb)
    b_x = jnp.repeat(jnp.arange(B, dtype=jnp.int32), nq)
    bias_t = rel_bias.T
    hix = jnp.arange(H)[:, None, None]
    scale = HD_C ** -0.5

    def attend_block(args):
        b, qq, ii, oo, pp = args
        ksel = kb[b][hix, ii]
        vsel = vb[b][hix, ii]
        kpos = ii[..., None] * MOBA_BLOCK + jnp.arange(MOBA_BLOCK, dtype=jnp.int32)
        rel = pp[None, :, None, None] - kpos
        s = jnp.einsum('hqd,hqnkd->hqnk', qq, ksel).astype(jnp.float32) * scale
        s = s + bias_t[hix[..., None], rel_bucket(rel)].astype(jnp.float32)
        s = jnp.where(oo[..., None] & (rel >= 0), s, -jnp.inf)
        p = jax.nn.softmax(s.reshape(H, qb, -1), axis=-1).reshape(s.shape)
        return jnp.einsum('hqnk,hqnkd->hqd', p.astype(vsel.dtype), vsel)

    o = lax.map(attend_block, (b_x, q_x, idx_x, ok_x, pos_x))
    return o.reshape(B, nq, H, qb, hd).transpose(0, 1, 3, 2, 4).reshape(B, L, H * hd)


def even_mixer(h, conv_buf, S_gdn, S_hg, lb, P, e):
    B, L, _ = h.shape
    dA, dB = H_A * HD_A, H_B * DK_B
    z = h @ P['ev_w_in'][e]
    qkv, zg, b_raw, a_raw, hq, hf, hi, hg = _split(z, [3 * dA, dA, H_A, H_A, dB, dB, H_B * DV_B, H_B * DV_B])
    qkv, conv_new = causal_dwconv(qkv, conv_buf, P['gdn_conv_w'][e])
    q, k, v = _split(jax.nn.silu(qkv), [dA, dA, dA])
    q = l2norm(q.reshape(B, L, H_A, HD_A)) * HD_A ** -0.5
    k = l2norm(k.reshape(B, L, H_A, HD_A))
    v = v.reshape(B, L, H_A, HD_A)
    beta = jax.nn.sigmoid(b_raw.astype(jnp.float32))
    g = -jnp.exp(P['gdn_A_log'][e].astype(jnp.float32)) * jax.nn.softplus(a_raw.astype(jnp.float32) + P['gdn_dt_bias'][e].astype(jnp.float32))
    o_a, S_gdn_new = gated_delta_rule(q, k, v, beta, g, S_gdn)
    o_a = rmsnorm(o_a, P['gdn_norm'][e]) * jax.nn.silu(zg.reshape(B, L, H_A, HD_A).astype(jnp.float32))
    f = lb + (1.0 - lb) * jax.nn.sigmoid(hf.astype(jnp.float32))
    qh = jax.nn.silu(hq.astype(jnp.float32)).reshape(B, L, H_B, DK_B)
    o_b, S_hg_new = hgrn2_recurrence(qh, (1.0 - f).reshape(B, L, H_B, DK_B), jnp.log(f).reshape(B, L, H_B, DK_B),
                                     hi.reshape(B, L, H_B, DV_B), S_hg)
    o_b = rmsnorm(o_b, P['hgrn_norm'][e]) * jax.nn.silu(hg.reshape(B, L, H_B, DV_B).astype(jnp.float32))
    mixed = jnp.concatenate([o_a.reshape(B, L, -1), o_b.reshape(B, L, -1)], axis=-1).astype(h.dtype)
    return mixed @ P['ev_w_out'][e], conv_new, S_gdn_new.astype(S_gdn.dtype), S_hg_new.astype(S_hg.dtype)


def odd_mixer(h, pos0, k_past, v_past, s_re, s_im, P, o):
    B, L, _ = h.shape
    dC, dD = H_C * HD_C, S5_GROUPS * S5_GROUP_CH
    z = h @ P['od_w_in'][o]
    q, k, v, u = _split(z, [dC, dC, dC, dD])
    q = rmsnorm(q.reshape(B, L, H_C, HD_C), P['moba_gq'][o])
    k = rmsnorm(k.reshape(B, L, H_C, HD_C), P['moba_gk'][o])
    v = v.reshape(B, L, H_C, HD_C)
    k_all = jnp.concatenate([k_past.astype(k.dtype), k], axis=1)
    v_all = jnp.concatenate([v_past.astype(v.dtype), v], axis=1)
    o_c = moba_attend(q, k_all, v_all, pos0, P['rel_bias'])
    y, hr, hi = s5_scan(u.reshape(B, L, S5_GROUPS, S5_GROUP_CH).astype(jnp.float32), s_re, s_im,
                        P['s5_A_re'][o], P['s5_A_im'][o], P['s5_log_dt'][o], P['s5_B_re'][o], P['s5_B_im'][o],
                        P['s5_C_re'][o], P['s5_C_im'][o], P['s5_D'][o])
    y = jax.nn.gelu(y.reshape(B, L, dD))
    ga, gb = _split(y @ P['s5_glu_w'][o].astype(jnp.float32), [dD, dD])
    o_d = ga * jax.nn.sigmoid(gb)
    mixed = jnp.concatenate([o_c.astype(h.dtype), o_d.astype(h.dtype)], axis=-1)
    return mixed @ P['od_w_out'][o], k, v, hr.astype(s_re.dtype), hi.astype(s_im.dtype)


def mem_kv(mem, g_norm, w_kv, g_k):
    B, M, _ = mem.shape
    kv = rmsnorm(mem, g_norm) @ w_kv
    k, v = _split(kv, [MEM_H * MEM_HD, MEM_H * MEM_HD])
    return rmsnorm(k.reshape(B, M, MEM_H, MEM_HD), g_k), v.reshape(B, M, MEM_H, MEM_HD)


def mem_attend(h, mk, mv, w_q, g_q, w_o):
    B, L, _ = h.shape
    q = rmsnorm((h @ w_q).reshape(B, L, MEM_H, MEM_HD), g_q)
    s = jnp.einsum('blhd,bmhd->bhlm', q, mk.astype(q.dtype)).astype(jnp.float32) * MEM_HD ** -0.5
    p = jax.nn.softmax(s, axis=-1).astype(h.dtype)
    o = jnp.einsum('bhlm,bmhd->blhd', p, mv.astype(h.dtype)).reshape(B, L, MEM_H * MEM_HD)
    return o @ w_o


def conv_ffn(h, buf, w_gu, conv_w, conv_b, w_down):
    gt, up = _split(h @ w_gu, [D_FF, D_FF])
    gt, buf_new = causal_dwconv(gt, buf, conv_w)
    return (jax.nn.silu(gt + conv_b) * up) @ w_down, buf_new


def trunk(x, pos0, gdn_conv0, gdn0, hgrn0, k_past, v_past, s5re0, s5im0, mem_k, mem_v, ffn0, P):
    lb_all = jnp.cumsum(jax.nn.softmax(P['hgrn_lb_raw'].astype(jnp.float32), axis=0), axis=0)
    conv_s, gdn_s, hgrn_s, k_s, v_s, re_s, im_s, ffn_s = [], [], [], [], [], [], [], []
    for l in range(DEPTH):
        h = rmsnorm(x, P['norm_mix'][l])
        if l % 2 == 0:
            e = l // 2
            out, c, sg, sh = even_mixer(h, gdn_conv0[e], gdn0[e], hgrn0[e], lb_all[l], P, e)
            conv_s.append(c)
            gdn_s.append(sg)
            hgrn_s.append(sh)
        else:
            o = l // 2
            out, kn, vn, sr, si = odd_mixer(h, pos0, k_past[o], v_past[o], s5re0[o], s5im0[o], P, o)
            k_s.append(kn)
            v_s.append(vn)
            re_s.append(sr)
            im_s.append(si)
        x = x + out.astype(x.dtype)
        x = x + mem_attend(rmsnorm(x, P['norm_mem'][l]), mem_k[l], mem_v[l], P['mem_wq'][l], P['mem_gq'][l], P['mem_wo'][l]).astype(x.dtype)
        f, fs = conv_ffn(rmsnorm(x, P['norm_ffn'][l]), ffn0[l], P['ffn_w_gu'][l], P['ffn_conv_w'][l], P['ffn_conv_b'][l], P['ffn_w_down'][l])
        x = x + f.astype(x.dtype)
        ffn_s.append(fs)
    return (x, jnp.stack(conv_s), jnp.stack(gdn_s), jnp.stack(hgrn_s), jnp.stack(k_s), jnp.stack(v_s),
            jnp.stack(re_s), jnp.stack(im_s), jnp.stack(ffn_s))


def setup_inputs(seed: int = 0) -> dict:
    key = jax.random.key(seed)
    cnt = [0]

    def nk():
        cnt[0] += 1
        return jax.random.fold_in(key, cnt[0])

    def nrm(shape, scale=1.0):
        return scale * jax.random.normal(nk(), shape, jnp.float32)

    def gain(shape):
        return 1.0 + nrm(shape, 0.05)

    def unif(shape, lo, hi):
        return jax.random.uniform(nk(), shape, jnp.float32, lo, hi)

    n_pages = PAST_LEN // PAGE_SIZE
    n_pool = (DEC_BATCH * n_pages * 5) // 4
    page_table = jax.random.permutation(nk(), n_pool)[: DEC_BATCH * n_pages].reshape(DEC_BATCH, n_pages).astype(jnp.int32)
    dt_gdn = jnp.exp(unif((N_EVEN, H_A), math.log(1e-3), math.log(1e-1)))
    a_im0 = jnp.pi * jnp.arange(S5_STATE, dtype=jnp.float32)
    dA3 = 3 * H_A * HD_A
    return {
        'x_prompt': nrm((BATCH, SEQ, D_MODEL)),
        'x_sample': nrm((DEC_BATCH, DEC_SEQ, D_MODEL)),
        'state_gdn_conv': nrm((N_EVEN, DEC_BATCH, GDN_CONV - 1, dA3)),
        'state_gdn': nrm((N_EVEN, DEC_BATCH, H_A, HD_A, HD_A), 0.1),
        'state_hgrn': nrm((N_EVEN, DEC_BATCH, H_B, DK_B, DV_B), 0.5),
        'cache_moba_k': nrm((N_ODD, n_pool, PAGE_SIZE, H_C, HD_C)),
        'cache_moba_v': nrm((N_ODD, n_pool, PAGE_SIZE, H_C, HD_C)),
        'state_s5_re': nrm((N_ODD, DEC_BATCH, S5_GROUPS, S5_STATE), 0.3),
        'state_s5_im': nrm((N_ODD, DEC_BATCH, S5_GROUPS, S5_STATE), 0.3),
        'cache_mem_k': nrm((DEPTH, DEC_BATCH, N_MEM, MEM_H, MEM_HD)),
        'cache_mem_v': nrm((DEPTH, DEC_BATCH, N_MEM, MEM_H, MEM_HD)),
        'state_ffn_conv': nrm((DEPTH, DEC_BATCH, FFN_CONV - 1, D_FF)),
        'page_table': page_table,
        'mem_prompt': nrm((BATCH, N_MEM, D_MODEL)),
        'rel_bias': nrm((N_BUCKETS, H_C), 0.5),
        'hgrn_lb_raw': nrm((DEPTH + 1, H_B * DK_B), 0.1),
        'norm_mix': gain((DEPTH, D_MODEL)),
        'norm_mem': gain((DEPTH, D_MODEL)),
        'norm_memkv': gain((DEPTH, D_MODEL)),
        'norm_ffn': gain((DEPTH, D_MODEL)),
        'mem_wq': nrm((DEPTH, D_MODEL, MEM_H * MEM_HD), D_MODEL ** -0.5),
        'mem_wkv': nrm((DEPTH, D_MODEL, 2 * MEM_H * MEM_HD), D_MODEL ** -0.5),
        'mem_gq': gain((DEPTH, MEM_HD)),
        'mem_gk': gain((DEPTH, MEM_HD)),
        'mem_wo': nrm((DEPTH, MEM_H * MEM_HD, D_MODEL), 0.5 * (MEM_H * MEM_HD) ** -0.5),
        'ffn_w_gu': nrm((DEPTH, D_MODEL, 2 * D_FF), D_MODEL ** -0.5),
        'ffn_conv_w': nrm((DEPTH, FFN_CONV, D_FF), FFN_CONV ** -0.5),
        'ffn_conv_b': nrm((DEPTH, D_FF), 0.02),
        'ffn_w_down': nrm((DEPTH, D_FF, D_MODEL), 0.5 * D_FF ** -0.5),
        'ev_w_in': nrm((N_EVEN, D_MODEL, EV_IN), D_MODEL ** -0.5),
        'ev_w_out': nrm((N_EVEN, EV_OUT, D_MODEL), 0.5 * EV_OUT ** -0.5),
        'gdn_conv_w': nrm((N_EVEN, GDN_CONV, dA3), GDN_CONV ** -0.5),
        'gdn_A_log': jnp.log(unif((N_EVEN, H_A), 1.0, 16.0)),
        'gdn_dt_bias': dt_gdn + jnp.log(-jnp.expm1(-dt_gdn)),
        'gdn_norm': gain((N_EVEN, HD_A)),
        'hgrn_norm': gain((N_EVEN, DV_B)),
        'od_w_in': nrm((N_ODD, D_MODEL, OD_IN), D_MODEL ** -0.5),
        'od_w_out': nrm((N_ODD, OD_OUT, D_MODEL), 0.5 * OD_OUT ** -0.5),
        'moba_gq': gain((N_ODD, HD_C)),
        'moba_gk': gain((N_ODD, HD_C)),
        's5_A_re': -0.5 + nrm((N_ODD, S5_GROUPS, S5_STATE), 0.01),
        's5_A_im': a_im0 + nrm((N_ODD, S5_GROUPS, S5_STATE), 0.01),
        's5_log_dt': unif((N_ODD, S5_GROUPS), math.log(1e-3), math.log(1e-1)),
        's5_B_re': nrm((N_ODD, S5_GROUPS, S5_STATE, S5_GROUP_CH), (2 * S5_GROUP_CH) ** -0.5),
        's5_B_im': nrm((N_ODD, S5_GROUPS, S5_STATE, S5_GROUP_CH), (2 * S5_GROUP_CH) ** -0.5),
        's5_C_re': nrm((N_ODD, S5_GROUPS, S5_GROUP_CH, S5_STATE), (2 * S5_STATE) ** -0.5),
        's5_C_im': nrm((N_ODD, S5_GROUPS, S5_GROUP_CH, S5_STATE), (2 * S5_STATE) ** -0.5),
        's5_D': nrm((N_ODD, S5_GROUPS, S5_GROUP_CH), 0.5),
        's5_glu_w': nrm((N_ODD, S5_GROUPS * S5_GROUP_CH, 2 * S5_GROUPS * S5_GROUP_CH), (S5_GROUPS * S5_GROUP_CH) ** -0.5),
    }


def reference(x_prompt, x_sample, state_gdn_conv, state_gdn, state_hgrn, cache_moba_k, cache_moba_v,
              state_s5_re, state_s5_im, cache_mem_k, cache_mem_v, state_ffn_conv, page_table, mem_prompt,
              rel_bias, hgrn_lb_raw, norm_mix, norm_mem, norm_memkv, norm_ffn, mem_wq, mem_wkv, mem_gq, mem_gk,
              mem_wo, ffn_w_gu, ffn_conv_w, ffn_conv_b, ffn_w_down, ev_w_in, ev_w_out, gdn_conv_w, gdn_A_log,
              gdn_dt_bias, gdn_norm, hgrn_norm, od_w_in, od_w_out, moba_gq, moba_gk, s5_A_re, s5_A_im, s5_log_dt,
              s5_B_re, s5_B_im, s5_C_re, s5_C_im, s5_D, s5_glu_w):
    P = dict(rel_bias=rel_bias, hgrn_lb_raw=hgrn_lb_raw, norm_mix=norm_mix, norm_mem=norm_mem, norm_ffn=norm_ffn,
             mem_wq=mem_wq, mem_gq=mem_gq, mem_wo=mem_wo, ffn_w_gu=ffn_w_gu, ffn_conv_w=ffn_conv_w,
             ffn_conv_b=ffn_conv_b, ffn_w_down=ffn_w_down, ev_w_in=ev_w_in, ev_w_out=ev_w_out,
             gdn_conv_w=gdn_conv_w, gdn_A_log=gdn_A_log, gdn_dt_bias=gdn_dt_bias, gdn_norm=gdn_norm,
             hgrn_norm=hgrn_norm, od_w_in=od_w_in, od_w_out=od_w_out, moba_gq=moba_gq, moba_gk=moba_gk,
             s5_A_re=s5_A_re, s5_A_im=s5_A_im, s5_log_dt=s5_log_dt, s5_B_re=s5_B_re, s5_B_im=s5_B_im,
             s5_C_re=s5_C_re, s5_C_im=s5_C_im, s5_D=s5_D, s5_glu_w=s5_glu_w)

    Bp = x_prompt.shape[0]
    dt = x_prompt.dtype
    mk_l, mv_l = [], []
    for l in range(DEPTH):
        mk, mv = mem_kv(mem_prompt, norm_memkv[l], mem_wkv[l], mem_gk[l])
        mk_l.append(mk)
        mv_l.append(mv)
    p_mem_k = jnp.stack(mk_l)
    p_mem_v = jnp.stack(mv_l)
    (y_prompt, p_gdn_conv, p_gdn, p_hgrn, p_moba_k, p_moba_v, p_s5_re, p_s5_im, p_ffn_conv) = trunk(
        x_prompt, 0,
        jnp.zeros((N_EVEN, Bp, GDN_CONV - 1, 3 * H_A * HD_A), dt),
        jnp.zeros((N_EVEN, Bp, H_A, HD_A, HD_A), dt),
        jnp.zeros((N_EVEN, Bp, H_B, DK_B, DV_B), dt),
        jnp.zeros((N_ODD, Bp, 0, H_C, HD_C), dt), jnp.zeros((N_ODD, Bp, 0, H_C, HD_C), dt),
        jnp.zeros((N_ODD, Bp, S5_GROUPS, S5_STATE), dt), jnp.zeros((N_ODD, Bp, S5_GROUPS, S5_STATE), dt),
        p_mem_k, p_mem_v, jnp.zeros((DEPTH, Bp, FFN_CONV - 1, D_FF), dt), P)

    Bs = x_sample.shape[0]
    n_pages = page_table.shape[1]
    past = n_pages * PAGE_SIZE
    k_past = cache_moba_k[:, page_table].reshape(N_ODD, Bs, past, H_C, HD_C)
    v_past = cache_moba_v[:, page_table].reshape(N_ODD, Bs, past, H_C, HD_C)
    (y_sample, s_gdn_conv, s_gdn, s_hgrn, s_moba_k, s_moba_v, s_s5_re, s_s5_im, s_ffn_conv) = trunk(
        x_sample, past, state_gdn_conv, state_gdn, state_hgrn, k_past, v_past, state_s5_re, state_s5_im,
        cache_mem_k, cache_mem_v, state_ffn_conv, P)

    return (y_prompt, y_sample, p_gdn_conv, p_gdn, p_hgrn, p_moba_k, p_moba_v, p_s5_re, p_s5_im, p_mem_k, p_mem_v,
            p_ffn_conv, s_gdn_conv, s_gdn, s_hgrn, s_moba_k, s_moba_v, s_s5_re, s_s5_im, s_ffn_conv)
```

```python
import functools
import math

import jax
import jax.numpy as jnp
from jax import lax
from jax.experimental import pallas as pl
from jax.experimental.pallas import tpu as pltpu

F32 = jnp.float32
BF16 = jnp.bfloat16
HI = lax.Precision.HIGHEST
EPS = 1e-6

LANES = 128
SUBLANES = 8
VMEM_LIMIT_BYTES = 56 * 1024 * 1024

D_MODEL = 1024
H_A, HD_A, GDN_CONV = 8, 64, 4
H_B, DK_B, DV_B = 8, 64, 64
GDN_CHUNK, HGRN_CHUNK = 64, 32
H_C, HD_C = 8, 64
MOBA_BLOCK, MOBA_TOPK = 256, 3
N_BUCKETS, MAX_DISTANCE = 32, 128
S5_GROUPS, S5_GROUP_CH, S5_STATE = 32, 16, 64
MEM_H, MEM_HD = 4, 128
D_FF, FFN_CONV = 2816, 3
PAGE_SIZE = 128
D_A = H_A * HD_A
D_S5 = S5_GROUPS * S5_GROUP_CH
D_S5_STATE = S5_GROUPS * S5_STATE
FF_CHUNK = D_FF // 2
EV_IN_COLS = 8 * D_A + LANES
EV_IN_TILE = EV_IN_COLS // 3


def _cparams(*sem):
    return pltpu.CompilerParams(dimension_semantics=sem, vmem_limit_bytes=VMEM_LIMIT_BYTES)


def _dot(a, b, precision=None):
    return jnp.dot(a, b, preferred_element_type=F32, precision=precision)


def _dot_nt(a, b, precision=None):
    return lax.dot_general(a, b, (((1,), (1,)), ((), ())), preferred_element_type=F32, precision=precision)


def _dot_tn(a, b, precision=None):
    return lax.dot_general(a, b, (((0,), (0,)), ((), ())), preferred_element_type=F32, precision=precision)


def _rms(x, g):
    return x * lax.rsqrt(jnp.mean(x * x, axis=-1, keepdims=True) + EPS) * g


def _silu(x):
    return x * jax.nn.sigmoid(x)


def _tri(n, strict=False):
    r = lax.broadcasted_iota(jnp.int32, (n, n), 0)
    c = lax.broadcasted_iota(jnp.int32, (n, n), 1)
    return (r > c) if strict else (r >= c)


def _norm_proj_kernel(x_ref, g_ref, w_ref, e_ref, gain_ref, o_ref, *, n_norm_tiles, group):
    xn = _rms(x_ref[...], g_ref[...])
    z = _dot(xn.astype(BF16), w_ref[...])
    if n_norm_tiles == 0:
        o_ref[...] = z
        return
    j = pl.program_id(1)

    @pl.when(j < n_norm_tiles)
    def _():
        zz = z * z
        hi = zz.astype(BF16)
        lo = (zz - hi.astype(F32)).astype(BF16)
        s = _dot(hi, e_ref[...]) + _dot(lo, e_ref[...])
        o_ref[...] = z * lax.rsqrt(s * (1.0 / group) + EPS) * gain_ref[...]

    @pl.when(j >= n_norm_tiles)
    def _():
        o_ref[...] = z


def _norm_proj(x, g, w, *, tm, tn, n_norm_tiles=0, group=1, gain=None, time_major=None, name):
    T, D = x.shape
    N = w.shape[1]
    nj = N // tn
    if gain is None:
        gain = jnp.ones((N,), F32)
    gi = lax.broadcasted_iota(jnp.int32, (tn, tn), 0) // group
    gj = lax.broadcasted_iota(jnp.int32, (tn, tn), 1) // group
    e = (gi == gj).astype(BF16)
    if time_major is None:
        out_shape = jax.ShapeDtypeStruct((T, N), F32)
        out_spec = pl.BlockSpec((tm, tn), lambda i, j: (i, j))
    else:
        B, L = time_major
        nt = L // tm
        out_shape = jax.ShapeDtypeStruct((L, B * N), F32)
        out_spec = pl.BlockSpec((tm, tn), lambda i, j: (i % nt, (i // nt) * nj + j))
    return pl.pallas_call(
        functools.partial(_norm_proj_kernel, n_norm_tiles=n_norm_tiles, group=group),
        grid=(T // tm, nj),
        in_specs=[
            pl.BlockSpec((tm, D), lambda i, j: (i, 0)),
            pl.BlockSpec((1, D), lambda i, j: (0, 0)),
            pl.BlockSpec((D, tn), lambda i, j: (0, j)),
            pl.BlockSpec((tn, tn), lambda i, j: (0, 0)),
            pl.BlockSpec((1, tn), lambda i, j: (0, j)),
        ],
        out_specs=out_spec,
        out_shape=out_shape,
        compiler_params=_cparams("parallel", "arbitrary"),
        name=name,
    )(x, g.reshape(1, D), w, e, gain.reshape(1, N))


def _proj_res_kernel(*refs, n_in):
    x_ref, o_ref = refs[0], refs[-1]
    acc = x_ref[...]
    for k in range(n_in):
        a_ref, w_ref = refs[1 + 2 * k], refs[2 + 2 * k]
        acc = acc + _dot(a_ref[...].astype(BF16), w_ref[...])
    o_ref[...] = acc


def _proj_res(x, parts, *, tm, name):
    T, N = x.shape
    in_specs = [pl.BlockSpec((tm, N), lambda i: (i, 0))]
    args = [x]
    for a, w, spec in parts:
        in_specs += [spec, pl.BlockSpec(w.shape, lambda i: (0, 0))]
        args += [a, w]
    return pl.pallas_call(
        functools.partial(_proj_res_kernel, n_in=len(parts)),
        grid=(T // tm,),
        in_specs=in_specs,
        out_specs=pl.BlockSpec((tm, N), lambda i: (i, 0)),
        out_shape=jax.ShapeDtypeStruct((T, N), F32),
        compiler_params=_cparams("parallel"),
        name=name,
    )(*args)


def _neumann_inverse(a, n):
    eye = (lax.broadcasted_iota(jnp.int32, (n, n), 0) == lax.broadcasted_iota(jnp.int32, (n, n), 1)).astype(F32)
    t = eye - a
    p = a
    k = 2
    while k < n:
        p = _dot(p, p, HI)
        t = t + _dot(t, p, HI)
        k *= 2
    return t


def _even_mixer_kernel(qkv_ref, zg_ref, hq_ref, hf_ref, hi_ref, hg_ref, ba_ref,
                       conv0_ref, sg0_ref, sh0_ref,
                       convw_ref, nega_ref, dtb_ref, gnorm_ref, hnorm_ref, lb_ref,
                       mixed_ref, tail_ref, sg_ref, sh_ref,
                       xp_ref, qkvc_ref, beta_ref, g_ref,
                       *, tt, cg, ch, l_real):
    ti = pl.program_id(1)
    n_tiles = pl.num_programs(1)

    @pl.when(ti == 0)
    def _():
        xp_ref[0:SUBLANES, :] = conv0_ref[...]
        sg_ref[...] = sg0_ref[...]
        sh_ref[...] = sh0_ref[...]

    xp_ref[SUBLANES:SUBLANES + tt, :] = qkv_ref[...]
    conv = convw_ref[GDN_CONV - 1:GDN_CONV, :] * qkv_ref[...]
    for j in range(GDN_CONV - 1):
        off = SUBLANES - (GDN_CONV - 1) + j
        conv = conv + convw_ref[j:j + 1, :] * xp_ref[off:off + tt, :]
    qkvc_ref[...] = _silu(conv)
    lr_last = l_real - (l_real - 1) // tt * tt
    tail_ref[...] = xp_ref[lr_last:lr_last + SUBLANES, :]
    carry = xp_ref[tt:tt + SUBLANES, :]
    xp_ref[0:SUBLANES, :] = carry

    padded = (l_real % tt) != 0
    if padded:
        row = lax.broadcasted_iota(jnp.int32, (tt, 1), 0) + ti * tt
        valid = row < l_real
    ba = ba_ref[...]
    beta_all = jax.nn.sigmoid(ba)
    g_all = nega_ref[...] * jax.nn.softplus(ba + dtb_ref[...])
    if padded:
        beta_all = jnp.where(valid, beta_all, 0.0)
        g_all = jnp.where(valid, g_all, 0.0)
    beta_ref[...] = beta_all
    g_ref[...] = g_all

    ltri_g = _tri(cg).astype(F32)
    tril_g = _tri(cg)
    stril_g = _tri(cg, strict=True)
    gnorm = gnorm_ref[...]
    hnorm = hnorm_ref[...]

    def gdn_chunk(ci, _):
        r0 = pl.multiple_of(ci * cg, cg)
        rows = pl.ds(r0, cg)
        gcum = _dot(ltri_g, g_ref[rows, :], HI)
        gpad = gcum if cg == LANES else jnp.concatenate([gcum, jnp.zeros((LANES - cg, LANES), F32)], axis=0)
        gt = gpad.T
        bet = beta_ref[rows, :]
        outs = []
        for h in range(H_A):
            q = qkvc_ref[rows, h * HD_A:(h + 1) * HD_A]
            k = qkvc_ref[rows, D_A + h * HD_A:D_A + (h + 1) * HD_A]
            v = qkvc_ref[rows, 2 * D_A + h * HD_A:2 * D_A + (h + 1) * HD_A]
            q = q * lax.rsqrt(jnp.sum(q * q, axis=-1, keepdims=True) + EPS) * (HD_A ** -0.5)
            k = k * lax.rsqrt(jnp.sum(k * k, axis=-1, keepdims=True) + EPS)
            beta = bet[:, h:h + 1]
            gc = gcum[:, H_A + h:H_A + h + 1]
            gr = gt[H_A + h:H_A + h + 1, 0:cg]
            glast = gcum[cg - 1:cg, H_A + h:H_A + h + 1]
            decay = jnp.exp(jnp.where(tril_g, gc - gr, -jnp.inf))
            a = jnp.where(stril_g, beta * _dot_nt(k, k, HI) * decay, 0.0)
            t = _neumann_inverse(a, cg)
            eg = jnp.exp(gc)
            u_v = _dot(t, beta * v, HI)
            w = _dot(t, (beta * eg) * k, HI)
            qk = _dot_nt(q, k) * decay
            s = sg_ref[h]
            u = u_v - _dot(w, s, HI)
            o = _dot(q * eg, s) + _dot(qk, u)
            k_out = k * jnp.exp(glast - gc)
            sg_ref[h] = s * jnp.exp(glast) + _dot_tn(k_out, u, HI)
            zg = zg_ref[rows, h * HD_A:(h + 1) * HD_A]
            outs.append(_rms(o, gnorm) * _silu(zg))
        for p in range(H_A // 2):
            mixed_ref[rows, p * LANES:(p + 1) * LANES] = jnp.concatenate([outs[2 * p], outs[2 * p + 1]], axis=-1)
        return 0

    lax.fori_loop(0, tt // cg, gdn_chunk, 0)

    ltri_h = _tri(ch).astype(F32)
    tril_h = _tri(ch)
    lb = lb_ref[...]
    r_mid = (ch - 1) // 2

    def hgrn_chunk(ci, _):
        r0 = pl.multiple_of(ci * ch, ch)
        rows = pl.ds(r0, ch)
        f = lb + (1.0 - lb) * jax.nn.sigmoid(hf_ref[rows, :])
        logf = jnp.log(f)
        kk = 1.0 - f
        if padded:
            vrow = (lax.broadcasted_iota(jnp.int32, (ch, 1), 0) + r0 + ti * tt) < l_real
            logf = jnp.where(vrow, logf, 0.0)
            kk = jnp.where(vrow, kk, 0.0)
        qh = _silu(hq_ref[rows, :])
        bc = _dot(ltri_h, logf, HI)
        bref = bc[r_mid:r_mid + 1, :]
        blast = bc[ch - 1:ch, :]
        qa = qh * jnp.exp(bc - bref)
        kb = kk * jnp.exp(bref - bc)
        q_in = qh * jnp.exp(bc)
        k_out = kk * jnp.exp(blast - bc)
        fl = jnp.exp(blast)
        vv = hi_ref[rows, :]
        outs = []
        for h in range(H_B):
            sl = slice(h * DK_B, (h + 1) * DK_B)
            att = jnp.where(tril_h, _dot_nt(qa[:, sl], kb[:, sl]), 0.0)
            st = sh_ref[h]
            v = vv[:, sl]
            o = _dot(att, v) + _dot_nt(q_in[:, sl], st)
            sh_ref[h] = st * fl[:, sl] + _dot_tn(v, k_out[:, sl], HI)
            hg = hg_ref[rows, h * DV_B:(h + 1) * DV_B]
            outs.append(_rms(o, hnorm) * _silu(hg))
        for p in range(H_B // 2):
            c0 = D_A + p * LANES
            mixed_ref[rows, c0:c0 + LANES] = jnp.concatenate([outs[2 * p], outs[2 * p + 1]], axis=-1)
        return 0

    lax.fori_loop(0, tt // ch, hgrn_chunk, 0)


def _even_mixer(z, conv0, sg0, sh0_t, conv_w, neg_a, dt_bias, gnorm, hnorm, lb, *, l_real, tt):
    B, Lp, _ = z.shape
    cg = min(GDN_CHUNK, tt)
    ch = min(HGRN_CHUNK, tt)
    d3 = 3 * D_A
    nt = Lp // tt
    col = lambda w, idx: pl.BlockSpec((None, tt, w), lambda b, i, idx=idx: (b, i, idx))
    state = lambda: pl.BlockSpec((None, H_A, HD_A, HD_A), lambda b, i: (b, 0, 0, 0))
    vec = lambda n: pl.BlockSpec((1, n), lambda b, i: (0, 0))
    nega_row = jnp.zeros((1, LANES), F32).at[0, H_A:2 * H_A].set(neg_a)
    dtb_row = jnp.zeros((1, LANES), F32).at[0, H_A:2 * H_A].set(dt_bias)
    return pl.pallas_call(
        functools.partial(_even_mixer_kernel, tt=tt, cg=cg, ch=ch, l_real=l_real),
        grid=(B, nt),
        in_specs=[
            col(d3, 0), col(D_A, 3), col(D_A, 4), col(D_A, 5), col(D_A, 6), col(D_A, 7), col(LANES, 32),
            pl.BlockSpec((None, SUBLANES, d3), lambda b, i: (b, 0, 0)),
            state(), state(),
            pl.BlockSpec((GDN_CONV, d3), lambda b, i: (0, 0)),
            vec(LANES), vec(LANES), vec(HD_A), vec(DV_B), vec(D_A),
        ],
        out_specs=[
            pl.BlockSpec((None, tt, 2 * D_A), lambda b, i: (b, i, 0)),
            pl.BlockSpec((None, SUBLANES, d3), lambda b, i: (b, 0, 0)),
            state(), state(),
        ],
        out_shape=[
            jax.ShapeDtypeStruct((B, Lp, 2 * D_A), F32),
            jax.ShapeDtypeStruct((B, SUBLANES, d3), F32),
            jax.ShapeDtypeStruct((B, H_A, HD_A, HD_A), F32),
            jax.ShapeDtypeStruct((B, H_B, DV_B, DK_B), F32),
        ],
        scratch_shapes=[
            pltpu.VMEM((tt + SUBLANES, d3), F32),
            pltpu.VMEM((tt, d3), F32),
            pltpu.VMEM((tt, LANES), F32),
            pltpu.VMEM((tt, LANES), F32),
        ],
        compiler_params=_cparams("parallel", "arbitrary"),
        name="even_mixer",
    )(z, z, z, z, z, z, z, conv0, sg0, sh0_t, conv_w, nega_row, dtb_row,
      gnorm.reshape(1, HD_A), hnorm.reshape(1, DV_B), lb.reshape(1, D_A))


def _mem_attn_kernel(x_ref, mk_ref, mv_ref, g_ref, wq_ref, gq_ref, wo_ref, o_ref):
    x = x_ref[...]
    q = _dot(_rms(x, g_ref[...]).astype(BF16), wq_ref[...])
    scale = MEM_HD ** -0.5
    outs = []
    for h in range(MEM_H):
        sl = slice(h * MEM_HD, (h + 1) * MEM_HD)
        qh = _rms(q[:, sl], gq_ref[...])
        s = _dot_nt(qh.astype(BF16), mk_ref[:, sl].astype(BF16)) * scale
        s = s - jnp.max(s, axis=-1, keepdims=True)
        p = jnp.exp(s)
        p = p / jnp.sum(p, axis=-1, keepdims=True)
        outs.append(_dot(p.astype(BF16), mv_ref[:, sl].astype(BF16)))
    o = jnp.concatenate(outs, axis=-1)
    o_ref[...] = x + _dot(o.astype(BF16), wo_ref[...])


def _mem_attn(x, mk, mv, g, wq, gq, wo, *, tm):
    B, L, D = x.shape
    M = mk.shape[1]
    dq = MEM_H * MEM_HD
    return pl.pallas_call(
        _mem_attn_kernel,
        grid=(B, L // tm),
        in_specs=[
            pl.BlockSpec((None, tm, D), lambda b, i: (b, i, 0)),
            pl.BlockSpec((None, M, dq), lambda b, i: (b, 0, 0)),
            pl.BlockSpec((None, M, dq), lambda b, i: (b, 0, 0)),
            pl.BlockSpec((1, D), lambda b, i: (0, 0)),
            pl.BlockSpec((D, dq), lambda b, i: (0, 0)),
            pl.BlockSpec((1, MEM_HD), lambda b, i: (0, 0)),
            pl.BlockSpec((dq, D), lambda b, i: (0, 0)),
        ],
        out_specs=pl.BlockSpec((None, tm, D), lambda b, i: (b, i, 0)),
        out_shape=jax.ShapeDtypeStruct((B, L, D), F32),
        compiler_params=_cparams("parallel", "parallel"),
        name="mem_attn",
    )(x, mk, mv, g.reshape(1, D), wq, gq.reshape(1, MEM_HD), wo)


def _ffn_seq_kernel(x_ref, buf0_ref, g_ref, wgu_ref, cw_ref, cb_ref, wd_ref, o_ref, tail_ref, gp_ref, carry_ref, *, tm):
    ti = pl.program_id(1)

    @pl.when(ti == 0)
    def _():
        carry_ref[...] = buf0_ref[...]

    x = x_ref[...]
    xn = _rms(x, g_ref[...]).astype(BF16)
    acc = x
    for c in range(D_FF // FF_CHUNK):
        cs = slice(c * FF_CHUNK, (c + 1) * FF_CHUNK)
        gate = _dot(xn, wgu_ref[:, cs])
        up = _dot(xn, wgu_ref[:, D_FF + c * FF_CHUNK:D_FF + (c + 1) * FF_CHUNK])
        gp_ref[0:SUBLANES, :] = carry_ref[:, cs]
        gp_ref[SUBLANES:SUBLANES + tm, :] = gate
        conv = cw_ref[FFN_CONV - 1:FFN_CONV, cs] * gate + cb_ref[:, cs]
        for j in range(FFN_CONV - 1):
            off = SUBLANES - (FFN_CONV - 1) + j
            conv = conv + cw_ref[j:j + 1, cs] * gp_ref[off:off + tm, :]
        last = gp_ref[tm:tm + SUBLANES, :]
        carry_ref[:, cs] = last
        tail_ref[:, cs] = last
        act = _silu(conv) * up
        acc = acc + _dot(act.astype(BF16), wd_ref[cs, :])
    o_ref[...] = acc


def _ffn_seq(x, buf0, g, wgu, cw, cb, wd, *, tm):
    B, L, D = x.shape
    const = lambda shape: pl.BlockSpec(shape, lambda b, i: (0,) * len(shape), pipeline_mode=pl.Buffered(1))
    return pl.pallas_call(
        functools.partial(_ffn_seq_kernel, tm=tm),
        grid=(B, L // tm),
        in_specs=[
            pl.BlockSpec((None, tm, D), lambda b, i: (b, i, 0)),
            pl.BlockSpec((None, SUBLANES, D_FF), lambda b, i: (b, 0, 0)),
            const((1, D)), const((D, 2 * D_FF)), const((FFN_CONV, D_FF)), const((1, D_FF)), const((D_FF, D)),
        ],
        out_specs=[
            pl.BlockSpec((None, tm, D), lambda b, i: (b, i, 0)),
            pl.BlockSpec((None, SUBLANES, D_FF), lambda b, i: (b, 0, 0)),
        ],
        out_shape=[jax.ShapeDtypeStruct((B, L, D), F32), jax.ShapeDtypeStruct((B, SUBLANES, D_FF), F32)],
        scratch_shapes=[pltpu.VMEM((tm + SUBLANES, FF_CHUNK), F32), pltpu.VMEM((SUBLANES, D_FF), F32)],
        compiler_params=_cparams("parallel", "arbitrary"),
        name="conv_ffn_seq",
    )(x, buf0, g.reshape(1, D), wgu, cw, cb.reshape(1, D_FF), wd)


def _ffn_step_kernel(x_ref, p2_ref, p1_ref, g_ref, wgu_ref, cw_ref, cb_ref, wd_ref, o_ref, gate_ref):
    x = x_ref[...]
    xn = _rms(x, g_ref[...]).astype(BF16)
    acc = x
    for c in range(D_FF // FF_CHUNK):
        cs = slice(c * FF_CHUNK, (c + 1) * FF_CHUNK)
        gate = _dot(xn, wgu_ref[:, cs])
        up = _dot(xn, wgu_ref[:, D_FF + c * FF_CHUNK:D_FF + (c + 1) * FF_CHUNK])
        gate_ref[:, cs] = gate
        conv = cw_ref[0:1, cs] * p2_ref[:, cs] + cw_ref[1:2, cs] * p1_ref[:, cs] + cw_ref[2:3, cs] * gate + cb_ref[:, cs]
        act = _silu(conv) * up
        acc = acc + _dot(act.astype(BF16), wd_ref[cs, :])
    o_ref[...] = acc


def _ffn_step(x, p2, p1, g, wgu, cw, cb, wd):
    R, D = x.shape
    full = lambda a: pl.BlockSpec(a.shape, lambda i: (0,) * a.ndim)
    args = (x, p2, p1, g.reshape(1, D), wgu, cw, cb.reshape(1, D_FF), wd)
    return pl.pallas_call(
        _ffn_step_kernel,
        grid=(1,),
        in_specs=[full(a) for a in args],
        out_specs=[pl.BlockSpec((R, D), lambda i: (0, 0)), pl.BlockSpec((R, D_FF), lambda i: (0, 0))],
        out_shape=[jax.ShapeDtypeStruct((R, D), F32), jax.ShapeDtypeStruct((R, D_FF), F32)],
        compiler_params=_cparams("arbitrary"),
        name="conv_ffn_step",
    )(*args)


def _topk_select(gate, n_valid_mask, iota_n):
    g = jnp.where(n_valid_mask, gate, -jnp.inf)
    sel = jnp.zeros(gate.shape, F32)
    iota_f = iota_n.astype(F32)
    for _ in range(MOBA_TOPK):
        m = jnp.max(g, axis=-1, keepdims=True)
        idx = jnp.min(jnp.where(g == m, iota_f, 1e9), axis=-1, keepdims=True)
        hit = iota_f == idx
        sel = jnp.where(hit & (m > -jnp.inf), 1.0, sel)
        g = jnp.where(hit, -jnp.inf, g)
    return sel


def _moba_seq_kernel(q_ref, k_ref, v_ref, bias_ref, o_ref, kmean_ref, *, nb):
    j = pl.program_id(1)
    blk = MOBA_BLOCK

    @pl.when(j == 0)
    def _():
        for n in range(nb):
            kmean_ref[n:n + 1, :] = jnp.mean(k_ref[n * blk:(n + 1) * blk, :], axis=0, keepdims=True)

    scale = HD_C ** -0.5
    causal = _tri(blk)
    iota_n = lax.broadcasted_iota(jnp.int32, (blk, nb), 1)
    r_own = pl.ds(pl.multiple_of(j * blk, blk), blk)
    outs = []
    for h in range(H_C):
        sl = slice(h * HD_C, (h + 1) * HD_C)
        q = q_ref[:, sl]
        gate = _dot_nt(q, kmean_ref[:, sl], HI)
        sel = _topk_select(gate, iota_n < j, iota_n)
        qb = q.astype(BF16)
        s = _dot_nt(qb, k_ref[r_own, sl].astype(BF16)) * scale + bias_ref[h, 0]
        s = jnp.where(causal, s, -jnp.inf)
        m0 = jnp.max(s, axis=-1, keepdims=True)
        p = jnp.exp(s - m0)
        l0 = jnp.sum(p, axis=-1, keepdims=True)
        acc0 = _dot(p.astype(BF16), v_ref[r_own, sl].astype(BF16))

        def past_block(n, carry):
            m, l, acc = carry
            rn = pl.ds(pl.multiple_of(n * blk, blk), blk)
            dd = jnp.minimum(j - n, 2)
            s = _dot_nt(qb, k_ref[rn, sl].astype(BF16)) * scale + bias_ref[h, dd]
            on = jnp.sum(jnp.where(iota_n == n, sel, 0.0), axis=-1, keepdims=True) > 0.5
            s = jnp.where(on, s, -jnp.inf)
            m_new = jnp.maximum(m, jnp.max(s, axis=-1, keepdims=True))
            alpha = jnp.exp(m - m_new)
            p = jnp.exp(s - m_new)
            l = alpha * l + jnp.sum(p, axis=-1, keepdims=True)
            acc = alpha * acc + _dot(p.astype(BF16), v_ref[rn, sl].astype(BF16))
            return m_new, l, acc

        m, l, acc = lax.fori_loop(0, j, past_block, (m0, l0, acc0))
        outs.append(acc / l)
    for p in range(H_C // 2):
        o_ref[:, p * LANES:(p + 1) * LANES] = jnp.concatenate([outs[2 * p], outs[2 * p + 1]], axis=-1)


def _moba_seq(qkv, bias_tiles):
    B, L, _ = qkv.shape
    nb = L // MOBA_BLOCK
    dc = H_C * HD_C
    return pl.pallas_call(
        functools.partial(_moba_seq_kernel, nb=nb),
        grid=(B, nb),
        in_specs=[
            pl.BlockSpec((None, MOBA_BLOCK, dc), lambda b, j: (b, j, 0)),
            pl.BlockSpec((None, L, dc), lambda b, j: (b, 0, 1)),
            pl.BlockSpec((None, L, dc), lambda b, j: (b, 0, 2)),
            pl.BlockSpec(bias_tiles.shape, lambda b, j: (0, 0, 0, 0), pipeline_mode=pl.Buffered(1)),
        ],
        out_specs=pl.BlockSpec((None, MOBA_BLOCK, dc), lambda b, j: (b, j, 0)),
        out_shape=jax.ShapeDtypeStruct((B, L, dc), F32),
        scratch_shapes=[pltpu.VMEM((nb, dc), F32)],
        compiler_params=_cparams("parallel", "arbitrary"),
        name="moba_seq",
    )(qkv, qkv, qkv, bias_tiles)


def _moba_step_kernel(pt_ref, qkv_ref, kp_ref, vp_ref, bias_ref, o_ref,
                      gate_ref, m_ref, l_ref, acc_ref, ksum_ref, *, n_pages):
    del pt_ref
    p = pl.program_id(1)
    ppb = MOBA_BLOCK // PAGE_SIZE
    dc = H_C * HD_C
    scale = HD_C ** -0.5
    lane = lax.broadcasted_iota(jnp.int32, (H_C, LANES), 1)
    hmask = (lax.broadcasted_iota(jnp.int32, (H_C, dc), 1) // HD_C) == lax.broadcasted_iota(jnp.int32, (H_C, dc), 0)
    q = qkv_ref[:, 0:dc]
    qbd = jnp.where(hmask, q, 0.0)

    @pl.when(p == 0)
    def _():
        gate_ref[...] = jnp.full((H_C, LANES), -jnp.inf, F32)
        m_ref[...] = jnp.full((H_C, LANES), -jnp.inf, F32)
        l_ref[...] = jnp.zeros((H_C, LANES), F32)

    kpage = kp_ref[...]
    vpage = vp_ref[...]
    ksum = jnp.sum(kpage, axis=0, keepdims=True)

    @pl.when(p % ppb == 0)
    def _():
        ksum_ref[...] = ksum

    @pl.when(p % ppb == ppb - 1)
    def _():
        tot = ksum if ppb == 1 else ksum_ref[...] + ksum
        gate = jnp.sum(qbd * (tot * (1.0 / MOBA_BLOCK)), axis=-1, keepdims=True)
        gate_ref[...] = jnp.where(lane == p // ppb, gate, gate_ref[...])

    last = p == n_pages - 1
    bias = jnp.where(last, bias_ref[0], bias_ref[1])
    s = _dot_nt(qbd.astype(BF16), kpage.astype(BF16)) * scale + bias
    mp = jnp.max(s, axis=-1, keepdims=True)
    e = jnp.exp(s - mp)
    lp = jnp.sum(e, axis=-1, keepdims=True)
    pv = _dot(e.astype(BF16), vpage.astype(BF16))
    acc_ref[p] = jnp.where(hmask, pv, 0.0)
    m_ref[...] = jnp.where(lane == p, mp, m_ref[...])
    l_ref[...] = jnp.where(lane == p, lp, l_ref[...])

    @pl.when(last)
    def _():
        nbp = n_pages // ppb
        sel = _topk_select(gate_ref[...], lane < nbp, lane)
        knew = qkv_ref[:, dc:2 * dc]
        vnew = qkv_ref[:, 2 * dc:3 * dc]
        s_new = jnp.sum(qbd * knew, axis=-1, keepdims=True) * scale + bias_ref[2][:, 0:1]
        mm = m_ref[...]
        ll = l_ref[...]
        selp = jnp.zeros((H_C, LANES), F32)
        for n in range(nbp):
            on = sel[:, n:n + 1] > 0.5
            for r in range(ppb):
                selp = jnp.where((lane == n * ppb + r) & on, 1.0, selp)
        mm = jnp.where(selp > 0.5, mm, -jnp.inf)
        mx = jnp.maximum(jnp.max(mm, axis=-1, keepdims=True), s_new)
        wgt = jnp.exp(mm - mx)
        w_new = jnp.exp(s_new - mx)
        den = jnp.sum(wgt * ll, axis=-1, keepdims=True) + w_new
        num = w_new * jnp.where(hmask, vnew, 0.0)
        for pg in range(n_pages):
            num = num + wgt[:, pg:pg + 1] * acc_ref[pg]
        o_ref[...] = jnp.sum(num / den, axis=0, keepdims=True)


def _moba_step(qkv, cache_k, cache_v, page_table, bias_rows):
    B = qkv.shape[0]
    n_pages = page_table.shape[1]
    dc = H_C * HD_C
    grid_spec = pltpu.PrefetchScalarGridSpec(
        num_scalar_prefetch=1,
        grid=(B, n_pages),
        in_specs=[
            pl.BlockSpec((None, 1, 3 * dc), lambda b, p, pt: (b, 0, 0)),
            pl.BlockSpec((None, PAGE_SIZE, dc), lambda b, p, pt: (pt[b * n_pages + p], 0, 0)),
            pl.BlockSpec((None, PAGE_SIZE, dc), lambda b, p, pt: (pt[b * n_pages + p], 0, 0)),
            pl.BlockSpec(bias_rows.shape, lambda b, p, pt: (0, 0, 0)),
        ],
        out_specs=pl.BlockSpec((None, 1, dc), lambda b, p, pt: (b, 0, 0)),
        scratch_shapes=[
            pltpu.VMEM((H_C, LANES), F32), pltpu.VMEM((H_C, LANES), F32), pltpu.VMEM((H_C, LANES), F32),
            pltpu.VMEM((n_pages, H_C, dc), F32), pltpu.VMEM((1, dc), F32),
        ],
    )
    return pl.pallas_call(
        functools.partial(_moba_step_kernel, n_pages=n_pages),
        grid_spec=grid_spec,
        out_shape=jax.ShapeDtypeStruct((B, 1, dc), F32),
        compiler_params=_cparams("parallel", "arbitrary"),
        name="moba_step",
    )(page_table.reshape(-1), qkv, cache_k, cache_v, bias_rows)


def _s5_kernel(u_ref, h0r_ref, h0i_ref, wb_ref, ar_ref, ai_ref, wc_ref, d_ref, wglu_ref,
               o_ref, hr_ref, hi_ref, x_ref, *, tt, nb):
    ti = pl.program_id(0)
    ns = D_S5_STATE

    @pl.when(ti == 0)
    def _():
        hr_ref[...] = h0r_ref[...]
        hi_ref[...] = h0i_ref[...]

    u = u_ref[...]
    x_ref[...] = _dot(u.astype(BF16), wb_ref[...])
    ar = ar_ref[...]
    ai = ai_ref[...]

    def step(t, carry):
        hr, hi = carry
        rows = pl.ds(pl.multiple_of(t * nb, nb), nb)
        nr = ar * hr - ai * hi + x_ref[rows, 0:ns]
        ni = ar * hi + ai * hr + x_ref[rows, ns:2 * ns]
        x_ref[rows, 0:ns] = nr
        x_ref[rows, ns:2 * ns] = ni
        return nr, ni

    hr, hi = lax.fori_loop(0, tt, step, (hr_ref[...], hi_ref[...]))
    hr_ref[...] = hr
    hi_ref[...] = hi
    y = _dot(x_ref[...].astype(BF16), wc_ref[...]) + d_ref[...] * u
    y = jax.nn.gelu(y)
    gl = _dot(y.astype(BF16), wglu_ref[...])
    o_ref[...] = gl[:, 0:D_S5] * jax.nn.sigmoid(gl[:, D_S5:2 * D_S5])


def _s5(u_tm, h0r, h0i, wb, ar, ai, wc, d, wglu, *, tt, nb):
    R = u_tm.shape[0]
    L = R // nb
    const = lambda a: pl.BlockSpec(a.shape, lambda i: (0,) * a.ndim)
    rows = pl.BlockSpec((tt * nb, D_S5), lambda i: (i, 0))
    st = pl.BlockSpec((nb, D_S5_STATE), lambda i: (0, 0))
    args = (u_tm, h0r, h0i, wb, ar, ai, wc, d, wglu)
    return pl.pallas_call(
        functools.partial(_s5_kernel, tt=tt, nb=nb),
        grid=(L // tt,),
        in_specs=[rows] + [const(a) for a in args[1:]],
        out_specs=[rows, st, st],
        out_shape=[jax.ShapeDtypeStruct((R, D_S5), F32),
                   jax.ShapeDtypeStruct((nb, D_S5_STATE), F32), jax.ShapeDtypeStruct((nb, D_S5_STATE), F32)],
        scratch_shapes=[pltpu.VMEM((tt * nb, 2 * D_S5_STATE), F32)],
        compiler_params=_cparams("arbitrary"),
        name="s5",
    )(*args)


def _rel_bucket(dist):
    n = jnp.maximum(dist, 0)
    exact = N_BUCKETS // 2
    nf = jnp.maximum(n, exact).astype(F32)
    large = exact + (jnp.log(nf / exact) / math.log(MAX_DISTANCE / exact) * (N_BUCKETS - exact)).astype(jnp.int32)
    return jnp.where(n < exact, n, jnp.minimum(large, N_BUCKETS - 1))


def _block_diag(w):
    G, a, b = w.shape
    eye = jnp.eye(G, dtype=w.dtype)
    return (eye[:, None, :, None] * w[:, :, None, :]).reshape(G * a, G * b)


def _s5_weights(A_re, A_im, log_dt, B_re, B_im, C_re, C_im, D):
    dt = jnp.exp(log_dt.astype(F32))[:, None]
    lr, li = A_re.astype(F32), A_im.astype(F32)
    mag = jnp.exp(lr * dt)
    ar, ai = mag * jnp.cos(li * dt), mag * jnp.sin(li * dt)
    den = lr * lr + li * li
    nr = ar - 1.0
    cr, ci = (nr * lr + ai * li) / den, (ai * lr - nr * li) / den
    bbr = cr[..., None] * B_re - ci[..., None] * B_im
    bbi = cr[..., None] * B_im + ci[..., None] * B_re
    wb = jnp.concatenate([_block_diag(bbr.transpose(0, 2, 1)), _block_diag(bbi.transpose(0, 2, 1))], axis=1)
    wc = jnp.concatenate([_block_diag(C_re.transpose(0, 2, 1)), -_block_diag(C_im.transpose(0, 2, 1))], axis=0)
    return (wb.astype(BF16), ar.reshape(1, -1), ai.reshape(1, -1), wc.astype(BF16), D.reshape(1, -1).astype(F32))


def _pad_rows_front(a, rows):
    return jnp.pad(a, ((0, 0), (rows - a.shape[1], 0), (0, 0)))


def _trunk(x, st, W, *, decode):
    B, L, D = x.shape
    T = B * L
    tm = min(512, T)
    out = {}

    x2 = x.reshape(T, D)
    z = _norm_proj(x2, W['norm_mix'][0], W['ev_w_in'], tm=tm, tn=EV_IN_TILE, name="ev_in_proj")
    zc = z.shape[1]
    if decode:
        lp = SUBLANES
        z3 = jnp.pad(z.reshape(B, L, zc), ((0, 0), (0, lp - L), (0, 0)))
        tt = lp
    else:
        lp = L
        z3 = z.reshape(B, L, zc)
        tt = min(512, L)
    conv0 = _pad_rows_front(st['gdn_conv'], SUBLANES)
    mixed, tail, sg, sh_t = _even_mixer(z3, conv0, st['gdn'], jnp.swapaxes(st['hgrn'], -1, -2), W['gdn_conv_w'],
                                        W['gdn_neg_a'], W['gdn_dt_bias'], W['gdn_norm'], W['hgrn_norm'], W['hgrn_lb'],
                                        l_real=L, tt=tt)
    out['gdn_conv'] = tail[:, SUBLANES - (GDN_CONV - 1):, :]
    out['gdn'] = sg
    out['hgrn'] = jnp.swapaxes(sh_t, -1, -2)
    mixed2 = mixed[:, :L].reshape(T, 2 * D_A)
    x2 = _proj_res(x2, [(mixed2, W['ev_w_out'], pl.BlockSpec((tm, 2 * D_A), lambda i: (i, 0)))], tm=tm, name="ev_out_proj")
    x2, fs = _post_mixer(x2, st, W, 0, B, L, decode)
    out['ffn0'] = fs

    dc = H_C * HD_C
    gqk = jnp.concatenate([jnp.tile(W['moba_gq'], H_C), jnp.tile(W['moba_gk'], H_C), jnp.ones((dc,), F32)])
    qkv = _norm_proj(x2, W['norm_mix'][1], W['od_w_in'][:, :3 * dc], tm=tm, tn=dc, n_norm_tiles=2, group=HD_C,
                     gain=gqk, name="od_in_proj_qkv")
    tmu = min(128, L) if not decode else T
    u_tm = _norm_proj(x2, W['norm_mix'][1], W['od_w_in'][:, 3 * dc:], tm=tmu, tn=D_S5,
                      time_major=None if decode else (B, L), name="od_in_proj_u")
    u_tm = u_tm.reshape(L * B, D_S5)
    out['moba_k'] = qkv[:, dc:2 * dc].reshape(B, L, H_C, HD_C)
    out['moba_v'] = qkv[:, 2 * dc:].reshape(B, L, H_C, HD_C)
    if decode:
        o_c = _moba_step(qkv.reshape(B, 1, 3 * dc), st['cache_k'], st['cache_v'], st['page_table'], W['moba_bias_rows'])
        o_c = o_c.reshape(T, dc)
        tts = 1
    else:
        o_c = _moba_seq(qkv.reshape(B, L, 3 * dc), W['moba_bias_tiles']).reshape(T, dc)
        tts = min(64, L)
    o_d, hr, hi = _s5(u_tm, st['s5_re'], st['s5_im'], *W['s5'], W['s5_glu_w'], tt=tts, nb=B)
    out['s5_re'] = hr.reshape(B, S5_GROUPS, S5_STATE)
    out['s5_im'] = hi.reshape(B, S5_GROUPS, S5_STATE)
    if decode:
        od_spec = pl.BlockSpec((tm, D_S5), lambda i: (i, 0))
        o_d2 = o_d
    else:
        nt = L // tm
        od_spec = pl.BlockSpec((tm, D_S5), lambda i: (i % nt, i // nt))
        o_d2 = o_d.reshape(L, B * D_S5)
    x2 = _proj_res(x2, [(o_c, W['od_w_out'][:dc], pl.BlockSpec((tm, dc), lambda i: (i, 0))),
                        (o_d2, W['od_w_out'][dc:], od_spec)], tm=tm, name="od_out_proj")
    x2, fs = _post_mixer(x2, st, W, 1, B, L, decode)
    out['ffn1'] = fs
    out['y'] = x2.reshape(B, L, D)
    return out


def _post_mixer(x2, st, W, l, B, L, decode):
    T, D = x2.shape
    x3 = _mem_attn(x2.reshape(B, L, D), st['mem_k'][l], st['mem_v'][l], W['norm_mem'][l], W['mem_wq'][l],
                   W['mem_gq'][l], W['mem_wo'][l], tm=min(512, L))
    buf = st['ffn_conv'][l]
    if decode:
        y, gate = _ffn_step(x3.reshape(T, D), buf[:, 0], buf[:, 1], W['norm_ffn'][l], W['ffn_w_gu'][l],
                            W['ffn_conv_w'][l], W['ffn_conv_b'][l], W['ffn_w_down'][l])
        fs = jnp.stack([buf[:, 1], gate], axis=1)
        return y, fs
    y, tail = _ffn_seq(x3, _pad_rows_front(buf, SUBLANES), W['norm_ffn'][l], W['ffn_w_gu'][l], W['ffn_conv_w'][l],
                       W['ffn_conv_b'][l], W['ffn_w_down'][l], tm=min(512, L))
    return y.reshape(T, D), tail[:, SUBLANES - (FFN_CONV - 1):, :]


def _mem_kv(mem, g_norm, w_kv, g_k):
    B, M, D = mem.shape
    dq = MEM_H * MEM_HD
    gain = jnp.concatenate([jnp.tile(g_k, MEM_H), jnp.ones((dq,), F32)])
    kv = _norm_proj(mem.reshape(B * M, D), g_norm, w_kv, tm=min(512, B * M), tn=dq, n_norm_tiles=1, group=MEM_HD,
                    gain=gain, name="mem_kv_proj")
    return kv[:, :dq].reshape(B, M, dq), kv[:, dq:].reshape(B, M, dq)


def kernel(x_prompt, x_sample, state_gdn_conv, state_gdn, state_hgrn, cache_moba_k, cache_moba_v, state_s5_re, state_s5_im, cache_mem_k, cache_mem_v, state_ffn_conv, page_table, mem_prompt, rel_bias, hgrn_lb_raw, norm_mix, norm_mem, norm_memkv, norm_ffn, mem_wq, mem_wkv, mem_gq, mem_gk, mem_wo, ffn_w_gu, ffn_conv_w, ffn_conv_b, ffn_w_down, ev_w_in, ev_w_out, gdn_conv_w, gdn_A_log, gdn_dt_bias, gdn_norm, hgrn_norm, od_w_in, od_w_out, moba_gq, moba_gk, s5_A_re, s5_A_im, s5_log_dt, s5_B_re, s5_B_im, s5_C_re, s5_C_im, s5_D, s5_glu_w):
    Bp, Lq, D = x_prompt.shape
    Bs = x_sample.shape[0]
    dc = H_C * HD_C
    dq = MEM_H * MEM_HD
    n_pages = page_table.shape[1]
    past = n_pages * PAGE_SIZE

    d3 = 3 * D_A
    w_in = ev_w_in[0]
    ba_cols = w_in[:, 4 * D_A:4 * D_A + 2 * H_A]
    w_in = jnp.concatenate([w_in[:, :4 * D_A], w_in[:, 4 * D_A + 2 * H_A:], ba_cols,
                            jnp.zeros((D, LANES - 2 * H_A), F32)], axis=1)
    lb_all = jnp.cumsum(jax.nn.softmax(hgrn_lb_raw.astype(F32), axis=0), axis=0)
    ii = jnp.arange(MOBA_BLOCK, dtype=jnp.int32)
    rel0 = ii[:, None] - ii[None, :]
    bias_t = rel_bias.T.astype(F32)
    tiles = jnp.stack([bias_t[:, _rel_bucket(rel0)], bias_t[:, _rel_bucket(rel0 + MOBA_BLOCK)],
                       jnp.broadcast_to(bias_t[:, _rel_bucket(jnp.int32(2 * MOBA_BLOCK))][:, None, None],
                                        (H_C, MOBA_BLOCK, MOBA_BLOCK))], axis=1)
    jj = jnp.arange(PAGE_SIZE, dtype=jnp.int32)
    bias_rows = jnp.stack([bias_t[:, _rel_bucket(PAGE_SIZE - jj)],
                           jnp.broadcast_to(bias_t[:, _rel_bucket(jnp.int32(PAGE_SIZE + 1))][:, None], (H_C, PAGE_SIZE)),
                           jnp.broadcast_to(bias_t[:, 0][:, None], (H_C, PAGE_SIZE))], axis=0)
    W = dict(
        norm_mix=norm_mix, norm_mem=norm_mem, norm_ffn=norm_ffn,
        ev_w_in=w_in.astype(BF16), ev_w_out=ev_w_out[0].astype(BF16),
        gdn_conv_w=gdn_conv_w[0], gdn_neg_a=-jnp.exp(gdn_A_log[0].astype(F32)), gdn_dt_bias=gdn_dt_bias[0],
        gdn_norm=gdn_norm[0], hgrn_norm=hgrn_norm[0], hgrn_lb=lb_all[0],
        od_w_in=od_w_in[0].astype(BF16), od_w_out=od_w_out[0].astype(BF16),
        moba_gq=moba_gq[0], moba_gk=moba_gk[0], moba_bias_tiles=tiles, moba_bias_rows=bias_rows,
        s5=_s5_weights(s5_A_re[0], s5_A_im[0], s5_log_dt[0], s5_B_re[0], s5_B_im[0], s5_C_re[0], s5_C_im[0], s5_D[0]),
        s5_glu_w=s5_glu_w[0].astype(BF16),
        mem_wq=mem_wq.astype(BF16), mem_gq=mem_gq, mem_wo=mem_wo.astype(BF16),
        ffn_w_gu=ffn_w_gu.astype(BF16), ffn_conv_w=ffn_conv_w, ffn_conv_b=ffn_conv_b, ffn_w_down=ffn_w_down.astype(BF16),
    )

    mk_l, mv_l = [], []
    for l in range(2):
        mk, mv = _mem_kv(mem_prompt, norm_memkv[l], mem_wkv[l].astype(BF16), mem_gk[l])
        mk_l.append(mk)
        mv_l.append(mv)
    p_mem_k, p_mem_v = jnp.stack(mk_l), jnp.stack(mv_l)
    zeros = lambda *s: jnp.zeros(s, F32)
    st_p = dict(gdn_conv=zeros(Bp, GDN_CONV - 1, d3), gdn=zeros(Bp, H_A, HD_A, HD_A), hgrn=zeros(Bp, H_B, DK_B, DV_B),
                s5_re=zeros(Bp, D_S5_STATE), s5_im=zeros(Bp, D_S5_STATE), mem_k=p_mem_k, mem_v=p_mem_v,
                ffn_conv=zeros(2, Bp, FFN_CONV - 1, D_FF))
    op = _trunk(x_prompt, st_p, W, decode=False)

    st_s = dict(gdn_conv=state_gdn_conv[0], gdn=state_gdn[0], hgrn=state_hgrn[0],
                s5_re=state_s5_re[0].reshape(Bs, D_S5_STATE), s5_im=state_s5_im[0].reshape(Bs, D_S5_STATE),
                mem_k=cache_mem_k.reshape(2, Bs, -1, dq), mem_v=cache_mem_v.reshape(2, Bs, -1, dq),
                ffn_conv=state_ffn_conv,
                cache_k=cache_moba_k[0].reshape(-1, PAGE_SIZE, dc), cache_v=cache_moba_v[0].reshape(-1, PAGE_SIZE, dc),
                page_table=page_table)
    os_ = _trunk(x_sample, st_s, W, decode=True)

    M = mem_prompt.shape[1]
    return (op['y'], os_['y'],
            op['gdn_conv'][None], op['gdn'][None], op['hgrn'][None], op['moba_k'][None], op['moba_v'][None],
            op['s5_re'][None], op['s5_im'][None],
            p_mem_k.reshape(2, Bp, M, MEM_H, MEM_HD), p_mem_v.reshape(2, Bp, M, MEM_H, MEM_HD),
            jnp.stack([op['ffn0'], op['ffn1']]),
            os_['gdn_conv'][None], os_['gdn'][None], os_['hgrn'][None], os_['moba_k'][None], os_['moba_v'][None],
            os_['s5_re'][None], os_['s5_im'][None], jnp.stack([os_['ffn0'], os_['ffn1']]))
```

```python
import functools
import math

import jax
import jax.numpy as jnp
from jax import lax
from jax.experimental import pallas as pl
from jax.experimental.pallas import tpu as pltpu

F32 = jnp.float32
BF16 = jnp.bfloat16
HI = lax.Precision.HIGHEST
EPS = 1e-6

LANES = 128
SUBLANES = 8
VMEM_LIMIT_BYTES = 56 * 1024 * 1024

D_MODEL = 1024
H_A, HD_A, GDN_CONV = 8, 64, 4
H_B, DK_B, DV_B = 8, 64, 64
GDN_CHUNK, HGRN_CHUNK = 64, 32
H_C, HD_C = 8, 64
MOBA_BLOCK, MOBA_TOPK = 256, 3
N_BUCKETS, MAX_DISTANCE = 32, 128
S5_GROUPS, S5_GROUP_CH, S5_STATE = 32, 16, 64
MEM_H, MEM_HD = 4, 128
D_FF, FFN_CONV = 2816, 3
PAGE_SIZE = 128
D_A = H_A * HD_A
D_S5 = S5_GROUPS * S5_GROUP_CH
D_S5_STATE = S5_GROUPS * S5_STATE
FF_CHUNK = D_FF // 2
EV_IN_COLS = 8 * D_A + LANES
EV_IN_TILE = EV_IN_COLS // 3


def _cparams(*sem):
    return pltpu.CompilerParams(dimension_semantics=sem, vmem_limit_bytes=VMEM_LIMIT_BYTES)


def _dot(a, b, precision=None):
    return jnp.dot(a, b, preferred_element_type=F32, precision=precision)


def _dot_nt(a, b, precision=None):
    return lax.dot_general(a, b, (((1,), (1,)), ((), ())), preferred_element_type=F32, precision=precision)


def _dot_tn(a, b, precision=None):
    return lax.dot_general(a, b, (((0,), (0,)), ((), ())), preferred_element_type=F32, precision=precision)


def _rms(x, g):
    return x * lax.rsqrt(jnp.mean(x * x, axis=-1, keepdims=True) + EPS) * g


def _silu(x):
    return x * jax.nn.sigmoid(x)


def _tri(n, strict=False):
    r = lax.broadcasted_iota(jnp.int32, (n, n), 0)
    c = lax.broadcasted_iota(jnp.int32, (n, n), 1)
    return (r > c) if strict else (r >= c)


def _norm_proj_kernel(x_ref, g_ref, w_ref, e_ref, gain_ref, o_ref, *, n_norm_tiles, group):
    xn = _rms(x_ref[...], g_ref[...])
    z = _dot(xn.astype(BF16), w_ref[...])
    if n_norm_tiles == 0:
        o_ref[...] = z
        return
    j = pl.program_id(1)

    @pl.when(j < n_norm_tiles)
    def _():
        zz = z * z
        hi = zz.astype(BF16)
        lo = (zz - hi.astype(F32)).astype(BF16)
        s = _dot(hi, e_ref[...]) + _dot(lo, e_ref[...])
        o_ref[...] = z * lax.rsqrt(s * (1.0 / group) + EPS) * gain_ref[...]

    @pl.when(j >= n_norm_tiles)
    def _():
        o_ref[...] = z


def _norm_proj(x, g, w, *, tm, tn, n_norm_tiles=0, group=1, gain=None, time_major=None, name):
    T, D = x.shape
    N = w.shape[1]
    nj = N // tn
    if gain is None:
        gain = jnp.ones((N,), F32)
    gi = lax.broadcasted_iota(jnp.int32, (tn, tn), 0) // group
    gj = lax.broadcasted_iota(jnp.int32, (tn, tn), 1) // group
    e = (gi == gj).astype(BF16)
    if time_major is None:
        out_shape = jax.ShapeDtypeStruct((T, N), F32)
        out_spec = pl.BlockSpec((tm, tn), lambda i, j: (i, j))
    else:
        B, L = time_major
        nt = L // tm
        out_shape = jax.ShapeDtypeStruct((L, B * N), F32)
        out_spec = pl.BlockSpec((tm, tn), lambda i, j: (i % nt, (i // nt) * nj + j))
    return pl.pallas_call(
        functools.partial(_norm_proj_kernel, n_norm_tiles=n_norm_tiles, group=group),
        grid=(T // tm, nj),
        in_specs=[
            pl.BlockSpec((tm, D), lambda i, j: (i, 0)),
            pl.BlockSpec((1, D), lambda i, j: (0, 0)),
            pl.BlockSpec((D, tn), lambda i, j: (0, j)),
            pl.BlockSpec((tn, tn), lambda i, j: (0, 0)),
            pl.BlockSpec((1, tn), lambda i, j: (0, j)),
        ],
        out_specs=out_spec,
        out_shape=out_shape,
        compiler_params=_cparams("parallel", "arbitrary"),
        name=name,
    )(x, g.reshape(1, D), w, e, gain.reshape(1, N))


def _proj_res_kernel(*refs, n_in):
    x_ref, o_ref = refs[0], refs[-1]
    acc = x_ref[...]
    for k in range(n_in):
        a_ref, w_ref = refs[1 + 2 * k], refs[2 + 2 * k]
        acc = acc + _dot(a_ref[...].astype(BF16), w_ref[...])
    o_ref[...] = acc


def _proj_res(x, parts, *, tm, name):
    T, N = x.shape
    in_specs = [pl.BlockSpec((tm, N), lambda i: (i, 0))]
    args = [x]
    for a, w, spec in parts:
        in_specs += [spec, pl.BlockSpec(w.shape, lambda i: (0, 0))]
        args += [a, w]
    return pl.pallas_call(
        functools.partial(_proj_res_kernel, n_in=len(parts)),
        grid=(T // tm,),
        in_specs=in_specs,
        out_specs=pl.BlockSpec((tm, N), lambda i: (i, 0)),
        out_shape=jax.ShapeDtypeStruct((T, N), F32),
        compiler_params=_cparams("parallel"),
        name=name,
    )(*args)


def _neumann_inverse(mats, n):
    eye = (lax.broadcasted_iota(jnp.int32, (n, n), 0) == lax.broadcasted_iota(jnp.int32, (n, n), 1)).astype(F32)
    ts = [eye - a for a in mats]
    ps = [a.astype(BF16) for a in mats]
    k = 2
    while k < n:
        ps = [_dot(p, p).astype(BF16) for p in ps]
        ts = [t + _dot(t.astype(BF16), p) for t, p in zip(ts, ps)]
        k *= 2
    return ts


def _even_mixer_kernel(qkv_ref, zg_ref, hq_ref, hf_ref, hi_ref, hg_ref, ba_ref,
                       conv0_ref, sg0_ref, sh0_ref,
                       convw_ref, nega_ref, dtb_ref, gnorm_ref, hnorm_ref, lb_ref,
                       mixed_ref, tail_ref, sg_ref, sh_ref,
                       xp_ref, qkvc_ref, beta_ref, g_ref,
                       *, tt, cg, ch, l_real):
    ti = pl.program_id(1)
    n_tiles = pl.num_programs(1)

    @pl.when(ti == 0)
    def _():
        xp_ref[0:SUBLANES, :] = conv0_ref[...]
        sg_ref[...] = sg0_ref[...]
        sh_ref[...] = sh0_ref[...]

    xp_ref[SUBLANES:SUBLANES + tt, :] = qkv_ref[...]
    conv = convw_ref[GDN_CONV - 1:GDN_CONV, :] * qkv_ref[...]
    for j in range(GDN_CONV - 1):
        off = SUBLANES - (GDN_CONV - 1) + j
        conv = conv + convw_ref[j:j + 1, :] * xp_ref[off:off + tt, :]
    qkvc_ref[...] = _silu(conv)
    lr_last = l_real - (l_real - 1) // tt * tt
    tail_ref[...] = xp_ref[lr_last:lr_last + SUBLANES, :]
    carry = xp_ref[tt:tt + SUBLANES, :]
    xp_ref[0:SUBLANES, :] = carry

    padded = (l_real % tt) != 0
    if padded:
        row = lax.broadcasted_iota(jnp.int32, (tt, 1), 0) + ti * tt
        valid = row < l_real
    ba = ba_ref[...]
    beta_all = jax.nn.sigmoid(ba)
    g_all = nega_ref[...] * jax.nn.softplus(ba + dtb_ref[...])
    if padded:
        beta_all = jnp.where(valid, beta_all, 0.0)
        g_all = jnp.where(valid, g_all, 0.0)
    beta_ref[...] = beta_all
    g_ref[...] = g_all

    ltri_g = _tri(cg).astype(F32)
    tril_g = _tri(cg)
    stril_g = _tri(cg, strict=True)
    gnorm = gnorm_ref[...]
    hnorm = hnorm_ref[...]

    def gdn_chunk(ci, _):
        r0 = pl.multiple_of(ci * cg, cg)
        rows = pl.ds(r0, cg)
        gcum = _dot(ltri_g, g_ref[rows, :], HI)
        gpad = gcum if cg == LANES else jnp.concatenate([gcum, jnp.zeros((LANES - cg, LANES), F32)], axis=0)
        gt = gpad.T
        bet = beta_ref[rows, :]
        heads = range(H_A)
        qs, ks, vs, betas, gcs, decays, glasts = [], [], [], [], [], [], []
        for h in heads:
            q = qkvc_ref[rows, h * HD_A:(h + 1) * HD_A]
            k = qkvc_ref[rows, D_A + h * HD_A:D_A + (h + 1) * HD_A]
            qs.append(q * lax.rsqrt(jnp.sum(q * q, axis=-1, keepdims=True) + EPS) * (HD_A ** -0.5))
            ks.append(k * lax.rsqrt(jnp.sum(k * k, axis=-1, keepdims=True) + EPS))
            vs.append(qkvc_ref[rows, 2 * D_A + h * HD_A:2 * D_A + (h + 1) * HD_A])
            betas.append(bet[:, h:h + 1])
            gc = gcum[:, H_A + h:H_A + h + 1]
            gr = gt[H_A + h:H_A + h + 1, 0:cg]
            gcs.append(gc)
            glasts.append(gcum[cg - 1:cg, H_A + h:H_A + h + 1])
            decays.append(jnp.exp(jnp.where(tril_g, gc - gr, -jnp.inf)))
        kqs = [_dot_nt(jnp.concatenate([ks[h], qs[h]], axis=0).astype(BF16), ks[h].astype(BF16))
               for h in heads]
        ts = _neumann_inverse([jnp.where(stril_g, betas[h] * kqs[h][0:cg] * decays[h], 0.0) for h in heads], cg)
        egs = [jnp.exp(gcs[h]) for h in heads]
        sols = [_dot(ts[h].astype(BF16),
                     jnp.concatenate([betas[h] * vs[h], (betas[h] * egs[h]) * ks[h]], axis=-1).astype(BF16))
                for h in heads]
        ss = [sg_ref[h] for h in heads]
        wss = [_dot(jnp.concatenate([sols[h][:, HD_A:2 * HD_A], qs[h] * egs[h]], axis=0).astype(BF16),
                    ss[h].astype(BF16)) for h in heads]
        ubs = [(sols[h][:, 0:HD_A] - wss[h][0:cg]).astype(BF16) for h in heads]
        os_ = [wss[h][cg:2 * cg] + _dot((kqs[h][cg:2 * cg] * decays[h]).astype(BF16), ubs[h]) for h in heads]
        for h in heads:
            k_out = ks[h] * jnp.exp(glasts[h] - gcs[h])
            sg_ref[h] = ss[h] * jnp.exp(glasts[h]) + _dot_tn(k_out.astype(BF16), ubs[h])
        outs = [_rms(os_[h], gnorm) * _silu(zg_ref[rows, h * HD_A:(h + 1) * HD_A]) for h in heads]
        for p in range(H_A // 2):
            mixed_ref[rows, p * LANES:(p + 1) * LANES] = jnp.concatenate([outs[2 * p], outs[2 * p + 1]], axis=-1)
        return 0

    lax.fori_loop(0, tt // cg, gdn_chunk, 0)

    ltri_h = _tri(ch).astype(F32)
    tril_h = _tri(ch)
    lb = lb_ref[...]
    r_mid = (ch - 1) // 2

    def hgrn_chunk(ci, _):
        r0 = pl.multiple_of(ci * ch, ch)
        rows = pl.ds(r0, ch)
        f = lb + (1.0 - lb) * jax.nn.sigmoid(hf_ref[rows, :])
        logf = jnp.log(f)
        kk = 1.0 - f
        if padded:
            vrow = (lax.broadcasted_iota(jnp.int32, (ch, 1), 0) + r0 + ti * tt) < l_real
            logf = jnp.where(vrow, logf, 0.0)
            kk = jnp.where(vrow, kk, 0.0)
        qh = _silu(hq_ref[rows, :])
        bc = _dot(ltri_h, logf, HI)
        bref = bc[r_mid:r_mid + 1, :]
        blast = bc[ch - 1:ch, :]
        qa = (qh * jnp.exp(bc - bref)).astype(BF16)
        kb = (kk * jnp.exp(bref - bc)).astype(BF16)
        q_in = (qh * jnp.exp(bc)).astype(BF16)
        k_out = (kk * jnp.exp(blast - bc)).astype(BF16)
        fl = jnp.exp(blast)
        vv = hi_ref[rows, :].astype(BF16)
        heads = range(H_B)
        sls = [slice(h * DK_B, (h + 1) * DK_B) for h in heads]
        atts = [jnp.where(tril_h, _dot_nt(qa[:, sl], kb[:, sl]), 0.0).astype(BF16) for sl in sls]
        sts = [sh_ref[h] for h in heads]
        inter = [_dot_nt(q_in[:, sls[h]], sts[h].astype(BF16)) for h in heads]
        os_ = [inter[h] + _dot(atts[h], vv[:, sls[h]]) for h in heads]
        for h in heads:
            sh_ref[h] = sts[h] * fl[:, sls[h]] + _dot_tn(vv[:, sls[h]], k_out[:, sls[h]])
        outs = [_rms(os_[h], hnorm) * _silu(hg_ref[rows, h * DV_B:(h + 1) * DV_B]) for h in heads]
        for p in range(H_B // 2):
            c0 = D_A + p * LANES
            mixed_ref[rows, c0:c0 + LANES] = jnp.concatenate([outs[2 * p], outs[2 * p + 1]], axis=-1)
        return 0

    lax.fori_loop(0, tt // ch, hgrn_chunk, 0)


def _even_mixer(z, conv0, sg0, sh0_t, conv_w, neg_a, dt_bias, gnorm, hnorm, lb, *, l_real, tt):
    B, Lp, _ = z.shape
    cg = min(GDN_CHUNK, tt)
    ch = min(HGRN_CHUNK, tt)
    d3 = 3 * D_A
    nt = Lp // tt
    col = lambda w, idx: pl.BlockSpec((None, tt, w), lambda b, i, idx=idx: (b, i, idx))
    state = lambda: pl.BlockSpec((None, H_A, HD_A, HD_A), lambda b, i: (b, 0, 0, 0))
    vec = lambda n: pl.BlockSpec((1, n), lambda b, i: (0, 0))
    nega_row = jnp.zeros((1, LANES), F32).at[0, H_A:2 * H_A].set(neg_a)
    dtb_row = jnp.zeros((1, LANES), F32).at[0, H_A:2 * H_A].set(dt_bias)
    return pl.pallas_call(
        functools.partial(_even_mixer_kernel, tt=tt, cg=cg, ch=ch, l_real=l_real),
        grid=(B, nt),
        in_specs=[
            col(d3, 0), col(D_A, 3), col(D_A, 4), col(D_A, 5), col(D_A, 6), col(D_A, 7), col(LANES, 32),
            pl.BlockSpec((None, SUBLANES, d3), lambda b, i: (b, 0, 0)),
            state(), state(),
            pl.BlockSpec((GDN_CONV, d3), lambda b, i: (0, 0)),
            vec(LANES), vec(LANES), vec(HD_A), vec(DV_B), vec(D_A),
        ],
        out_specs=[
            pl.BlockSpec((None, tt, 2 * D_A), lambda b, i: (b, i, 0)),
            pl.BlockSpec((None, SUBLANES, d3), lambda b, i: (b, 0, 0)),
            state(), state(),
        ],
        out_shape=[
            jax.ShapeDtypeStruct((B, Lp, 2 * D_A), F32),
            jax.ShapeDtypeStruct((B, SUBLANES, d3), F32),
            jax.ShapeDtypeStruct((B, H_A, HD_A, HD_A), F32),
            jax.ShapeDtypeStruct((B, H_B, DV_B, DK_B), F32),
        ],
        scratch_shapes=[
            pltpu.VMEM((tt + SUBLANES, d3), F32),
            pltpu.VMEM((tt, d3), F32),
            pltpu.VMEM((tt, LANES), F32),
            pltpu.VMEM((tt, LANES), F32),
        ],
        compiler_params=_cparams("parallel", "arbitrary"),
        name="even_mixer",
    )(z, z, z, z, z, z, z, conv0, sg0, sh0_t, conv_w, nega_row, dtb_row,
      gnorm.reshape(1, HD_A), hnorm.reshape(1, DV_B), lb.reshape(1, D_A))


def _mem_attn_kernel(x_ref, mk_ref, mv_ref, g_ref, wq_ref, gq_ref, wo_ref, o_ref):
    x = x_ref[...]
    q = _dot(_rms(x, g_ref[...]).astype(BF16), wq_ref[...])
    scale = MEM_HD ** -0.5
    outs = []
    for h in range(MEM_H):
        sl = slice(h * MEM_HD, (h + 1) * MEM_HD)
        qh = _rms(q[:, sl], gq_ref[...])
        s = _dot_nt(qh.astype(BF16), mk_ref[:, sl].astype(BF16)) * scale
        s = s - jnp.max(s, axis=-1, keepdims=True)
        p = jnp.exp(s)
        p = p / jnp.sum(p, axis=-1, keepdims=True)
        outs.append(_dot(p.astype(BF16), mv_ref[:, sl].astype(BF16)))
    o = jnp.concatenate(outs, axis=-1)
    o_ref[...] = x + _dot(o.astype(BF16), wo_ref[...])


def _mem_attn(x, mk, mv, g, wq, gq, wo, *, tm):
    B, L, D = x.shape
    M = mk.shape[1]
    dq = MEM_H * MEM_HD
    return pl.pallas_call(
        _mem_attn_kernel,
        grid=(B, L // tm),
        in_specs=[
            pl.BlockSpec((None, tm, D), lambda b, i: (b, i, 0)),
            pl.BlockSpec((None, M, dq), lambda b, i: (b, 0, 0)),
            pl.BlockSpec((None, M, dq), lambda b, i: (b, 0, 0)),
            pl.BlockSpec((1, D), lambda b, i: (0, 0)),
            pl.BlockSpec((D, dq), lambda b, i: (0, 0)),
            pl.BlockSpec((1, MEM_HD), lambda b, i: (0, 0)),
            pl.BlockSpec((dq, D), lambda b, i: (0, 0)),
        ],
        out_specs=pl.BlockSpec((None, tm, D), lambda b, i: (b, i, 0)),
        out_shape=jax.ShapeDtypeStruct((B, L, D), F32),
        compiler_params=_cparams("parallel", "parallel"),
        name="mem_attn",
    )(x, mk, mv, g.reshape(1, D), wq, gq.reshape(1, MEM_HD), wo)


def _ffn_seq_kernel(x_ref, buf0_ref, g_ref, wgu_ref, cw_ref, cb_ref, wd_ref, o_ref, tail_ref, gp_ref, carry_ref, *, tm):
    ti = pl.program_id(1)

    @pl.when(ti == 0)
    def _():
        carry_ref[...] = buf0_ref[...]

    x = x_ref[...]
    xn = _rms(x, g_ref[...]).astype(BF16)
    acc = x
    for c in range(D_FF // FF_CHUNK):
        cs = slice(c * FF_CHUNK, (c + 1) * FF_CHUNK)
        gate = _dot(xn, wgu_ref[:, cs])
        up = _dot(xn, wgu_ref[:, D_FF + c * FF_CHUNK:D_FF + (c + 1) * FF_CHUNK])
        gp_ref[0:SUBLANES, :] = carry_ref[:, cs]
        gp_ref[SUBLANES:SUBLANES + tm, :] = gate
        conv = cw_ref[FFN_CONV - 1:FFN_CONV, cs] * gate + cb_ref[:, cs]
        for j in range(FFN_CONV - 1):
            off = SUBLANES - (FFN_CONV - 1) + j
            conv = conv + cw_ref[j:j + 1, cs] * gp_ref[off:off + tm, :]
        last = gp_ref[tm:tm + SUBLANES, :]
        carry_ref[:, cs] = last
        tail_ref[:, cs] = last
        act = _silu(conv) * up
        acc = acc + _dot(act.astype(BF16), wd_ref[cs, :])
    o_ref[...] = acc


def _ffn_seq(x, buf0, g, wgu, cw, cb, wd, *, tm):
    B, L, D = x.shape
    const = lambda shape: pl.BlockSpec(shape, lambda b, i: (0,) * len(shape), pipeline_mode=pl.Buffered(1))
    return pl.pallas_call(
        functools.partial(_ffn_seq_kernel, tm=tm),
        grid=(B, L // tm),
        in_specs=[
            pl.BlockSpec((None, tm, D), lambda b, i: (b, i, 0)),
            pl.BlockSpec((None, SUBLANES, D_FF), lambda b, i: (b, 0, 0)),
            const((1, D)), const((D, 2 * D_FF)), const((FFN_CONV, D_FF)), const((1, D_FF)), const((D_FF, D)),
        ],
        out_specs=[
            pl.BlockSpec((None, tm, D), lambda b, i: (b, i, 0)),
            pl.BlockSpec((None, SUBLANES, D_FF), lambda b, i: (b, 0, 0)),
        ],
        out_shape=[jax.ShapeDtypeStruct((B, L, D), F32), jax.ShapeDtypeStruct((B, SUBLANES, D_FF), F32)],
        scratch_shapes=[pltpu.VMEM((tm + SUBLANES, FF_CHUNK), F32), pltpu.VMEM((SUBLANES, D_FF), F32)],
        compiler_params=_cparams("parallel", "arbitrary"),
        name="conv_ffn_seq",
    )(x, buf0, g.reshape(1, D), wgu, cw, cb.reshape(1, D_FF), wd)


def _ffn_step_kernel(x_ref, p2_ref, p1_ref, g_ref, wgu_ref, cw_ref, cb_ref, wd_ref, o_ref, gate_ref):
    x = x_ref[...]
    xn = _rms(x, g_ref[...]).astype(BF16)
    acc = x
    for c in range(D_FF // FF_CHUNK):
        cs = slice(c * FF_CHUNK, (c + 1) * FF_CHUNK)
        gate = _dot(xn, wgu_ref[:, cs])
        up = _dot(xn, wgu_ref[:, D_FF + c * FF_CHUNK:D_FF + (c + 1) * FF_CHUNK])
        gate_ref[:, cs] = gate
        conv = cw_ref[0:1, cs] * p2_ref[:, cs] + cw_ref[1:2, cs] * p1_ref[:, cs] + cw_ref[2:3, cs] * gate + cb_ref[:, cs]
        act = _silu(conv) * up
        acc = acc + _dot(act.astype(BF16), wd_ref[cs, :])
    o_ref[...] = acc


def _ffn_step(x, p2, p1, g, wgu, cw, cb, wd):
    R, D = x.shape
    full = lambda a: pl.BlockSpec(a.shape, lambda i: (0,) * a.ndim)
    args = (x, p2, p1, g.reshape(1, D), wgu, cw, cb.reshape(1, D_FF), wd)
    return pl.pallas_call(
        _ffn_step_kernel,
        grid=(1,),
        in_specs=[full(a) for a in args],
        out_specs=[pl.BlockSpec((R, D), lambda i: (0, 0)), pl.BlockSpec((R, D_FF), lambda i: (0, 0))],
        out_shape=[jax.ShapeDtypeStruct((R, D), F32), jax.ShapeDtypeStruct((R, D_FF), F32)],
        compiler_params=_cparams("arbitrary"),
        name="conv_ffn_step",
    )(*args)


def _topk_select(gate, n_valid_mask, iota_n):
    g = jnp.where(n_valid_mask, gate, -jnp.inf)
    sel = jnp.zeros(gate.shape, F32)
    iota_f = iota_n.astype(F32)
    for _ in range(MOBA_TOPK):
        m = jnp.max(g, axis=-1, keepdims=True)
        idx = jnp.min(jnp.where(g == m, iota_f, 1e9), axis=-1, keepdims=True)
        hit = iota_f == idx
        sel = jnp.where(hit & (m > -jnp.inf), 1.0, sel)
        g = jnp.where(hit, -jnp.inf, g)
    return sel


def _moba_seq_kernel(q_ref, k_ref, v_ref, bias_ref, o_ref, kmean_ref, *, nb):
    j = pl.program_id(1)
    blk = MOBA_BLOCK

    @pl.when(j == 0)
    def _():
        for n in range(nb):
            kmean_ref[n:n + 1, :] = jnp.mean(k_ref[n * blk:(n + 1) * blk, :], axis=0, keepdims=True)

    scale = HD_C ** -0.5
    causal = _tri(blk)
    iota_n = lax.broadcasted_iota(jnp.int32, (blk, nb), 1)
    r_own = pl.ds(pl.multiple_of(j * blk, blk), blk)
    outs = []
    for h in range(H_C):
        sl = slice(h * HD_C, (h + 1) * HD_C)
        q = q_ref[:, sl]
        gate = _dot_nt(q, kmean_ref[:, sl], HI)
        sel = _topk_select(gate, iota_n < j, iota_n)
        qb = q.astype(BF16)
        s = _dot_nt(qb, k_ref[r_own, sl].astype(BF16)) * scale + bias_ref[h, 0]
        s = jnp.where(causal, s, -jnp.inf)
        m0 = jnp.max(s, axis=-1, keepdims=True)
        p = jnp.exp(s - m0)
        l0 = jnp.sum(p, axis=-1, keepdims=True)
        acc0 = _dot(p.astype(BF16), v_ref[r_own, sl].astype(BF16))

        def past_block(n, carry):
            m, l, acc = carry
            rn = pl.ds(pl.multiple_of(n * blk, blk), blk)
            dd = jnp.minimum(j - n, 2)
            s = _dot_nt(qb, k_ref[rn, sl].astype(BF16)) * scale + bias_ref[h, dd]
            on = jnp.sum(jnp.where(iota_n == n, sel, 0.0), axis=-1, keepdims=True) > 0.5
            s = jnp.where(on, s, -jnp.inf)
            m_new = jnp.maximum(m, jnp.max(s, axis=-1, keepdims=True))
            alpha = jnp.exp(m - m_new)
            p = jnp.exp(s - m_new)
            l = alpha * l + jnp.sum(p, axis=-1, keepdims=True)
            acc = alpha * acc + _dot(p.astype(BF16), v_ref[rn, sl].astype(BF16))
            return m_new, l, acc

        m, l, acc = lax.fori_loop(0, j, past_block, (m0, l0, acc0))
        outs.append(acc / l)
    for p in range(H_C // 2):
        o_ref[:, p * LANES:(p + 1) * LANES] = jnp.concatenate([outs[2 * p], outs[2 * p + 1]], axis=-1)


def _moba_seq(qkv, bias_tiles):
    B, L, _ = qkv.shape
    nb = L // MOBA_BLOCK
    dc = H_C * HD_C
    return pl.pallas_call(
        functools.partial(_moba_seq_kernel, nb=nb),
        grid=(B, nb),
        in_specs=[
            pl.BlockSpec((None, MOBA_BLOCK, dc), lambda b, j: (b, j, 0)),
            pl.BlockSpec((None, L, dc), lambda b, j: (b, 0, 1)),
            pl.BlockSpec((None, L, dc), lambda b, j: (b, 0, 2)),
            pl.BlockSpec(bias_tiles.shape, lambda b, j: (0, 0, 0, 0), pipeline_mode=pl.Buffered(1)),
        ],
        out_specs=pl.BlockSpec((None, MOBA_BLOCK, dc), lambda b, j: (b, j, 0)),
        out_shape=jax.ShapeDtypeStruct((B, L, dc), F32),
        scratch_shapes=[pltpu.VMEM((nb, dc), F32)],
        compiler_params=_cparams("parallel", "arbitrary"),
        name="moba_seq",
    )(qkv, qkv, qkv, bias_tiles)


MOBA_PAGES_PER_BLOCK = MOBA_BLOCK // PAGE_SIZE
MOBA_STEP_PAGES = 8


def _moba_step_kernel(pt_ref, q_ref, kn_ref, vn_ref, blast_ref, bfar_ref, bzero_ref, *refs, n_pages):
    del pt_ref
    ppg, ppb = MOBA_STEP_PAGES, MOBA_PAGES_PER_BLOCK
    kp_refs, vp_refs = refs[:ppg], refs[ppg:2 * ppg]
    o_ref, gate_ref, m_ref, l_ref, acc_ref = refs[2 * ppg:]
    step = pl.program_id(1)
    nbp = n_pages // ppb
    q = q_ref[...]
    qs = q * (HD_C ** -0.5)
    ones = jnp.ones((HD_C, LANES), BF16)
    ksum = None
    for r in range(ppg):
        pg = step * ppg + r
        kp = kp_refs[r][...]
        vp = vp_refs[r][...]
        prod = (kp * qs[None]).reshape(PAGE_SIZE * H_C, HD_C)
        sc = _dot(prod.astype(BF16), ones).reshape(PAGE_SIZE, H_C, LANES)
        sc = sc + jnp.where(pg == n_pages - 1, blast_ref[...], bfar_ref[...][None])
        mp = jnp.max(sc, axis=0)
        e = jnp.exp(sc - mp[None])
        blk = step * (ppg // ppb) + r // ppb
        m_ref[blk, r % ppb] = mp
        l_ref[blk, r % ppb] = jnp.sum(e, axis=0)
        acc_ref[blk, r % ppb] = jnp.sum(e[:, :, 0:HD_C] * vp, axis=0)
        ks = jnp.sum(kp, axis=0)
        ksum = ks if r % ppb == 0 else ksum + ks
        if r % ppb == ppb - 1:
            gate = jnp.sum(q * (ksum * (1.0 / MOBA_BLOCK)), axis=-1, keepdims=True)
            gate_ref[blk] = jnp.broadcast_to(gate, (H_C, LANES))

    @pl.when(step == n_pages // ppg - 1)
    def _():
        g = gate_ref[...]
        iota_b = lax.broadcasted_iota(jnp.int32, (nbp, H_C, LANES), 0).astype(F32)
        sel = jnp.zeros((nbp, H_C, LANES), F32)
        for _ in range(MOBA_TOPK):
            m = jnp.max(g, axis=0, keepdims=True)
            idx = jnp.min(jnp.where(g == m, iota_b, 1e9), axis=0, keepdims=True)
            hit = iota_b == idx
            sel = jnp.where(hit & (m > -jnp.inf), 1.0, sel)
            g = jnp.where(hit, -jnp.inf, g)
        s_new = jnp.sum(qs * kn_ref[...], axis=-1, keepdims=True) + bzero_ref[...]
        mm = jnp.where(sel[:, None] > 0.5, m_ref[...], -jnp.inf)
        mx = jnp.maximum(jnp.max(jnp.max(mm, axis=0), axis=0), s_new)
        wgt = jnp.exp(mm - mx[None, None])
        w_new = jnp.exp(s_new - mx)
        den = jnp.sum(jnp.sum(wgt * l_ref[...], axis=0), axis=0) + w_new
        num = jnp.sum(jnp.sum(wgt[:, :, :, 0:HD_C] * acc_ref[...], axis=0), axis=0) + w_new[:, 0:HD_C] * vn_ref[...]
        o_ref[...] = num / den[:, 0:HD_C]


def _moba_step(q, k_new, v_new, cache_k, cache_v, page_table, bias_last, bias_far, bias_zero):
    B = q.shape[0]
    n_pages = page_table.shape[1]
    ppg, ppb = MOBA_STEP_PAGES, MOBA_PAGES_PER_BLOCK
    assert n_pages % ppg == 0 and ppg % ppb == 0
    nbp = n_pages // ppb
    tok = pl.BlockSpec((None, H_C, HD_C), lambda b, s, pt: (b, 0, 0))
    page = lambda r: pl.BlockSpec((None, PAGE_SIZE, H_C, HD_C),
                                  lambda b, s, pt, r=r: (pt[b * n_pages + s * ppg + r], 0, 0, 0))
    grid_spec = pltpu.PrefetchScalarGridSpec(
        num_scalar_prefetch=1,
        grid=(B, n_pages // ppg),
        in_specs=[tok, tok, tok,
                  pl.BlockSpec(bias_last.shape, lambda b, s, pt: (0, 0, 0)),
                  pl.BlockSpec(bias_far.shape, lambda b, s, pt: (0, 0)),
                  pl.BlockSpec(bias_zero.shape, lambda b, s, pt: (0, 0))]
                 + [page(r) for r in range(ppg)] + [page(r) for r in range(ppg)],
        out_specs=tok,
        scratch_shapes=[
            pltpu.VMEM((nbp, H_C, LANES), F32),
            pltpu.VMEM((nbp, ppb, H_C, LANES), F32), pltpu.VMEM((nbp, ppb, H_C, LANES), F32),
            pltpu.VMEM((nbp, ppb, H_C, HD_C), F32),
        ],
    )
    return pl.pallas_call(
        functools.partial(_moba_step_kernel, n_pages=n_pages),
        grid_spec=grid_spec,
        out_shape=jax.ShapeDtypeStruct((B, H_C, HD_C), F32),
        compiler_params=_cparams("parallel", "arbitrary"),
        name="moba_step",
    )(page_table.reshape(-1), q, k_new, v_new, bias_last, bias_far, bias_zero,
      *([cache_k] * ppg), *([cache_v] * ppg))


def _s5_kernel(u_ref, h0r_ref, h0i_ref, wb_ref, ar_ref, ai_ref, wc_ref, d_ref, wglu_ref,
               o_ref, hr_ref, hi_ref, x_ref, *, tt, nb):
    ti = pl.program_id(0)
    ns = D_S5_STATE

    @pl.when(ti == 0)
    def _():
        hr_ref[...] = h0r_ref[...]
        hi_ref[...] = h0i_ref[...]

    u = u_ref[...]
    x_ref[...] = _dot(u.astype(BF16), wb_ref[...])
    ar = ar_ref[...]
    ai = ai_ref[...]

    def step(t, carry):
        hr, hi = carry
        rows = pl.ds(pl.multiple_of(t * nb, nb), nb)
        nr = ar * hr - ai * hi + x_ref[rows, 0:ns]
        ni = ar * hi + ai * hr + x_ref[rows, ns:2 * ns]
        x_ref[rows, 0:ns] = nr
        x_ref[rows, ns:2 * ns] = ni
        return nr, ni

    hr, hi = lax.fori_loop(0, tt, step, (hr_ref[...], hi_ref[...]))
    hr_ref[...] = hr
    hi_ref[...] = hi
    y = _dot(x_ref[...].astype(BF16), wc_ref[...]) + d_ref[...] * u
    y = jax.nn.gelu(y)
    gl = _dot(y.astype(BF16), wglu_ref[...])
    o_ref[...] = gl[:, 0:D_S5] * jax.nn.sigmoid(gl[:, D_S5:2 * D_S5])


def _s5(u_tm, h0r, h0i, wb, ar, ai, wc, d, wglu, *, tt, nb):
    R = u_tm.shape[0]
    L = R // nb
    const = lambda a: pl.BlockSpec(a.shape, lambda i: (0,) * a.ndim)
    rows = pl.BlockSpec((tt * nb, D_S5), lambda i: (i, 0))
    st = pl.BlockSpec((nb, D_S5_STATE), lambda i: (0, 0))
    args = (u_tm, h0r, h0i, wb, ar, ai, wc, d, wglu)
    return pl.pallas_call(
        functools.partial(_s5_kernel, tt=tt, nb=nb),
        grid=(L // tt,),
        in_specs=[rows] + [const(a) for a in args[1:]],
        out_specs=[rows, st, st],
        out_shape=[jax.ShapeDtypeStruct((R, D_S5), F32),
                   jax.ShapeDtypeStruct((nb, D_S5_STATE), F32), jax.ShapeDtypeStruct((nb, D_S5_STATE), F32)],
        scratch_shapes=[pltpu.VMEM((tt * nb, 2 * D_S5_STATE), F32)],
        compiler_params=_cparams("arbitrary"),
        name="s5",
    )(*args)


def _rel_bucket(dist):
    n = jnp.maximum(dist, 0)
    exact = N_BUCKETS // 2
    nf = jnp.maximum(n, exact).astype(F32)
    large = exact + (jnp.log(nf / exact) / math.log(MAX_DISTANCE / exact) * (N_BUCKETS - exact)).astype(jnp.int32)
    return jnp.where(n < exact, n, jnp.minimum(large, N_BUCKETS - 1))


def _block_diag(w):
    G, a, b = w.shape
    eye = jnp.eye(G, dtype=w.dtype)
    return (eye[:, None, :, None] * w[:, :, None, :]).reshape(G * a, G * b)


def _s5_weights(A_re, A_im, log_dt, B_re, B_im, C_re, C_im, D):
    dt = jnp.exp(log_dt.astype(F32))[:, None]
    lr, li = A_re.astype(F32), A_im.astype(F32)
    mag = jnp.exp(lr * dt)
    ar, ai = mag * jnp.cos(li * dt), mag * jnp.sin(li * dt)
    den = lr * lr + li * li
    nr = ar - 1.0
    cr, ci = (nr * lr + ai * li) / den, (ai * lr - nr * li) / den
    bbr = cr[..., None] * B_re - ci[..., None] * B_im
    bbi = cr[..., None] * B_im + ci[..., None] * B_re
    wb = jnp.concatenate([_block_diag(bbr.transpose(0, 2, 1)), _block_diag(bbi.transpose(0, 2, 1))], axis=1)
    wc = jnp.concatenate([_block_diag(C_re.transpose(0, 2, 1)), -_block_diag(C_im.transpose(0, 2, 1))], axis=0)
    return (wb.astype(BF16), ar.reshape(1, -1), ai.reshape(1, -1), wc.astype(BF16), D.reshape(1, -1).astype(F32))


def _pad_rows_front(a, rows):
    return jnp.pad(a, ((0, 0), (rows - a.shape[1], 0), (0, 0)))


def _trunk(x, st, W, *, decode):
    B, L, D = x.shape
    T = B * L
    tm = min(512, T)
    out = {}

    x2 = x.reshape(T, D)
    z = _norm_proj(x2, W['norm_mix'][0], W['ev_w_in'], tm=tm, tn=EV_IN_TILE, name="ev_in_proj")
    zc = z.shape[1]
    if decode:
        lp = SUBLANES
        z3 = jnp.pad(z.reshape(B, L, zc), ((0, 0), (0, lp - L), (0, 0)))
        tt = lp
    else:
        lp = L
        z3 = z.reshape(B, L, zc)
        tt = min(512, L)
    conv0 = _pad_rows_front(st['gdn_conv'], SUBLANES)
    mixed, tail, sg, sh_t = _even_mixer(z3, conv0, st['gdn'], jnp.swapaxes(st['hgrn'], -1, -2), W['gdn_conv_w'],
                                        W['gdn_neg_a'], W['gdn_dt_bias'], W['gdn_norm'], W['hgrn_norm'], W['hgrn_lb'],
                                        l_real=L, tt=tt)
    out['gdn_conv'] = tail[:, SUBLANES - (GDN_CONV - 1):, :]
    out['gdn'] = sg
    out['hgrn'] = jnp.swapaxes(sh_t, -1, -2)
    mixed2 = mixed[:, :L].reshape(T, 2 * D_A)
    x2 = _proj_res(x2, [(mixed2, W['ev_w_out'], pl.BlockSpec((tm, 2 * D_A), lambda i: (i, 0)))], tm=tm, name="ev_out_proj")
    x2, fs = _post_mixer(x2, st, W, 0, B, L, decode)
    out['ffn0'] = fs

    dc = H_C * HD_C
    gqk = jnp.concatenate([jnp.tile(W['moba_gq'], H_C), jnp.tile(W['moba_gk'], H_C), jnp.ones((dc,), F32)])
    qkv = _norm_proj(x2, W['norm_mix'][1], W['od_w_in'][:, :3 * dc], tm=tm, tn=dc, n_norm_tiles=2, group=HD_C,
                     gain=gqk, name="od_in_proj_qkv")
    tmu = min(128, L) if not decode else T
    u_tm = _norm_proj(x2, W['norm_mix'][1], W['od_w_in'][:, 3 * dc:], tm=tmu, tn=D_S5,
                      time_major=None if decode else (B, L), name="od_in_proj_u")
    u_tm = u_tm.reshape(L * B, D_S5)
    out['moba_k'] = qkv[:, dc:2 * dc].reshape(B, L, H_C, HD_C)
    out['moba_v'] = qkv[:, 2 * dc:].reshape(B, L, H_C, HD_C)
    if decode:
        heads = lambda a: a.reshape(B, H_C, HD_C)
        o_c = _moba_step(heads(qkv[:, :dc]), heads(qkv[:, dc:2 * dc]), heads(qkv[:, 2 * dc:]),
                         st['cache_k'], st['cache_v'], st['page_table'], *W['moba_bias_step'])
        o_c = o_c.reshape(T, dc)
        tts = 1
    else:
        o_c = _moba_seq(qkv.reshape(B, L, 3 * dc), W['moba_bias_tiles']).reshape(T, dc)
        tts = min(64, L)
    o_d, hr, hi = _s5(u_tm, st['s5_re'], st['s5_im'], *W['s5'], W['s5_glu_w'], tt=tts, nb=B)
    out['s5_re'] = hr.reshape(B, S5_GROUPS, S5_STATE)
    out['s5_im'] = hi.reshape(B, S5_GROUPS, S5_STATE)
    if decode:
        od_spec = pl.BlockSpec((tm, D_S5), lambda i: (i, 0))
        o_d2 = o_d
    else:
        nt = L // tm
        od_spec = pl.BlockSpec((tm, D_S5), lambda i: (i % nt, i // nt))
        o_d2 = o_d.reshape(L, B * D_S5)
    x2 = _proj_res(x2, [(o_c, W['od_w_out'][:dc], pl.BlockSpec((tm, dc), lambda i: (i, 0))),
                        (o_d2, W['od_w_out'][dc:], od_spec)], tm=tm, name="od_out_proj")
    x2, fs = _post_mixer(x2, st, W, 1, B, L, decode)
    out['ffn1'] = fs
    out['y'] = x2.reshape(B, L, D)
    return out


def _post_mixer(x2, st, W, l, B, L, decode):
    T, D = x2.shape
    x3 = _mem_attn(x2.reshape(B, L, D), st['mem_k'][l], st['mem_v'][l], W['norm_mem'][l], W['mem_wq'][l],
                   W['mem_gq'][l], W['mem_wo'][l], tm=min(512, L))
    buf = st['ffn_conv'][l]
    if decode:
        y, gate = _ffn_step(x3.reshape(T, D), buf[:, 0], buf[:, 1], W['norm_ffn'][l], W['ffn_w_gu'][l],
                            W['ffn_conv_w'][l], W['ffn_conv_b'][l], W['ffn_w_down'][l])
        fs = jnp.stack([buf[:, 1], gate], axis=1)
        return y, fs
    y, tail = _ffn_seq(x3, _pad_rows_front(buf, SUBLANES), W['norm_ffn'][l], W['ffn_w_gu'][l], W['ffn_conv_w'][l],
                       W['ffn_conv_b'][l], W['ffn_w_down'][l], tm=min(512, L))
    return y.reshape(T, D), tail[:, SUBLANES - (FFN_CONV - 1):, :]


def _mem_kv(mem, g_norm, w_kv, g_k):
    B, M, D = mem.shape
    dq = MEM_H * MEM_HD
    gain = jnp.concatenate([jnp.tile(g_k, MEM_H), jnp.ones((dq,), F32)])
    kv = _norm_proj(mem.reshape(B * M, D), g_norm, w_kv, tm=min(512, B * M), tn=dq, n_norm_tiles=1, group=MEM_HD,
                    gain=gain, name="mem_kv_proj")
    return kv[:, :dq].reshape(B, M, dq), kv[:, dq:].reshape(B, M, dq)


def kernel(x_prompt, x_sample, state_gdn_conv, state_gdn, state_hgrn, cache_moba_k, cache_moba_v, state_s5_re, state_s5_im, cache_mem_k, cache_mem_v, state_ffn_conv, page_table, mem_prompt, rel_bias, hgrn_lb_raw, norm_mix, norm_mem, norm_memkv, norm_ffn, mem_wq, mem_wkv, mem_gq, mem_gk, mem_wo, ffn_w_gu, ffn_conv_w, ffn_conv_b, ffn_w_down, ev_w_in, ev_w_out, gdn_conv_w, gdn_A_log, gdn_dt_bias, gdn_norm, hgrn_norm, od_w_in, od_w_out, moba_gq, moba_gk, s5_A_re, s5_A_im, s5_log_dt, s5_B_re, s5_B_im, s5_C_re, s5_C_im, s5_D, s5_glu_w):
    Bp, Lq, D = x_prompt.shape
    Bs = x_sample.shape[0]
    dc = H_C * HD_C
    dq = MEM_H * MEM_HD
    n_pages = page_table.shape[1]
    past = n_pages * PAGE_SIZE

    d3 = 3 * D_A
    w_in = ev_w_in[0]
    ba_cols = w_in[:, 4 * D_A:4 * D_A + 2 * H_A]
    w_in = jnp.concatenate([w_in[:, :4 * D_A], w_in[:, 4 * D_A + 2 * H_A:], ba_cols,
                            jnp.zeros((D, LANES - 2 * H_A), F32)], axis=1)
    lb_all = jnp.cumsum(jax.nn.softmax(hgrn_lb_raw.astype(F32), axis=0), axis=0)
    bias_t = rel_bias.T.astype(F32)
    nblk = MOBA_BLOCK
    by_rel = bias_t[:, _rel_bucket(jnp.arange(2 * nblk, dtype=jnp.int32))]
    mm = jnp.arange(2 * nblk, dtype=jnp.int32)

    def toeplitz(shift):
        y = by_rel[:, (shift - mm) % (2 * nblk)]
        flat = jnp.tile(y, (1, nblk))[:, :nblk * (2 * nblk - 1)]
        return flat.reshape(H_C, nblk, 2 * nblk - 1)[:, :, :nblk]

    far = bias_t[:, _rel_bucket(jnp.int32(2 * nblk))]
    tiles = jnp.stack([toeplitz(0), toeplitz(nblk),
                       jnp.broadcast_to(far[:, None, None], (H_C, nblk, nblk))], axis=1)
    jj = jnp.arange(PAGE_SIZE, dtype=jnp.int32)
    rep = lambda a: jnp.broadcast_to(a[..., None], a.shape + (LANES,))
    bias_step = (rep(bias_t[:, _rel_bucket(PAGE_SIZE - jj)].T),
                 rep(bias_t[:, _rel_bucket(jnp.int32(PAGE_SIZE + 1))]), rep(bias_t[:, 0]))
    W = dict(
        norm_mix=norm_mix, norm_mem=norm_mem, norm_ffn=norm_ffn,
        ev_w_in=w_in.astype(BF16), ev_w_out=ev_w_out[0].astype(BF16),
        gdn_conv_w=gdn_conv_w[0], gdn_neg_a=-jnp.exp(gdn_A_log[0].astype(F32)), gdn_dt_bias=gdn_dt_bias[0],
        gdn_norm=gdn_norm[0], hgrn_norm=hgrn_norm[0], hgrn_lb=lb_all[0],
        od_w_in=od_w_in[0].astype(BF16), od_w_out=od_w_out[0].astype(BF16),
        moba_gq=moba_gq[0], moba_gk=moba_gk[0], moba_bias_tiles=tiles, moba_bias_step=bias_step,
        s5=_s5_weights(s5_A_re[0], s5_A_im[0], s5_log_dt[0], s5_B_re[0], s5_B_im[0], s5_C_re[0], s5_C_im[0], s5_D[0]),
        s5_glu_w=s5_glu_w[0].astype(BF16),
        mem_wq=mem_wq.astype(BF16), mem_gq=mem_gq, mem_wo=mem_wo.astype(BF16),
        ffn_w_gu=ffn_w_gu.astype(BF16), ffn_conv_w=ffn_conv_w, ffn_conv_b=ffn_conv_b, ffn_w_down=ffn_w_down.astype(BF16),
    )

    mk_l, mv_l = [], []
    for l in range(2):
        mk, mv = _mem_kv(mem_prompt, norm_memkv[l], mem_wkv[l].astype(BF16), mem_gk[l])
        mk_l.append(mk)
        mv_l.append(mv)
    p_mem_k, p_mem_v = jnp.stack(mk_l), jnp.stack(mv_l)
    zeros = lambda *s: jnp.zeros(s, F32)
    st_p = dict(gdn_conv=zeros(Bp, GDN_CONV - 1, d3), gdn=zeros(Bp, H_A, HD_A, HD_A), hgrn=zeros(Bp, H_B, DK_B, DV_B),
                s5_re=zeros(Bp, D_S5_STATE), s5_im=zeros(Bp, D_S5_STATE), mem_k=p_mem_k, mem_v=p_mem_v,
                ffn_conv=zeros(2, Bp, FFN_CONV - 1, D_FF))
    op = _trunk(x_prompt, st_p, W, decode=False)

    st_s = dict(gdn_conv=state_gdn_conv[0], gdn=state_gdn[0], hgrn=state_hgrn[0],
                s5_re=state_s5_re[0].reshape(Bs, D_S5_STATE), s5_im=state_s5_im[0].reshape(Bs, D_S5_STATE),
                mem_k=cache_mem_k.reshape(2, Bs, -1, dq), mem_v=cache_mem_v.reshape(2, Bs, -1, dq),
                ffn_conv=state_ffn_conv,
                cache_k=cache_moba_k[0], cache_v=cache_moba_v[0],
                page_table=page_table)
    os_ = _trunk(x_sample, st_s, W, decode=True)

    M = mem_prompt.shape[1]
    return (op['y'], os_['y'],
            op['gdn_conv'][None], op['gdn'][None], op['hgrn'][None], op['moba_k'][None], op['moba_v'][None],
            op['s5_re'][None], op['s5_im'][None],
            p_mem_k.reshape(2, Bp, M, MEM_H, MEM_HD), p_mem_v.reshape(2, Bp, M, MEM_H, MEM_HD),
            jnp.stack([op['ffn0'], op['ffn1']]),
            os_['gdn_conv'][None], os_['gdn'][None], os_['hgrn'][None], os_['moba_k'][None], os_['moba_v'][None],
            os_['s5_re'][None], os_['s5_im'][None], jnp.stack([os_['ffn0'], os_['ffn1']]))
```

```python
import functools
import math

import jax
import jax.numpy as jnp
from jax import lax
from jax.experimental import pallas as pl
from jax.experimental.pallas import tpu as pltpu

F32 = jnp.float32
BF16 = jnp.bfloat16
HI = lax.Precision.HIGHEST
EPS = 1e-6

LANES = 128
SUBLANES = 8
VMEM_LIMIT_BYTES = 56 * 1024 * 1024

D_MODEL = 1024
H_A, HD_A, GDN_CONV = 8, 64, 4
H_B, DK_B, DV_B = 8, 64, 64
GDN_CHUNK, HGRN_CHUNK = 64, 32
H_C, HD_C = 8, 64
MOBA_BLOCK, MOBA_TOPK = 256, 3
N_BUCKETS, MAX_DISTANCE = 32, 128
S5_GROUPS, S5_GROUP_CH, S5_STATE = 32, 16, 64
MEM_H, MEM_HD = 4, 128
D_FF, FFN_CONV = 2816, 3
PAGE_SIZE = 128
D_A = H_A * HD_A
D_S5 = S5_GROUPS * S5_GROUP_CH
D_S5_STATE = S5_GROUPS * S5_STATE
FF_CHUNK = D_FF // 2
EV_IN_COLS = 8 * D_A + LANES
EV_IN_TILE = EV_IN_COLS // 3


def _cparams(*sem):
    return pltpu.CompilerParams(dimension_semantics=sem, vmem_limit_bytes=VMEM_LIMIT_BYTES)


def _dot(a, b, precision=None):
    return jnp.dot(a, b, preferred_element_type=F32, precision=precision)


def _dot_nt(a, b, precision=None):
    return lax.dot_general(a, b, (((1,), (1,)), ((), ())), preferred_element_type=F32, precision=precision)


def _dot_tn(a, b, precision=None):
    return lax.dot_general(a, b, (((0,), (0,)), ((), ())), preferred_element_type=F32, precision=precision)


def _rms(x, g):
    return x * lax.rsqrt(jnp.mean(x * x, axis=-1, keepdims=True) + EPS) * g


def _silu(x):
    return x * jax.nn.sigmoid(x)


def _tri(n, strict=False):
    r = lax.broadcasted_iota(jnp.int32, (n, n), 0)
    c = lax.broadcasted_iota(jnp.int32, (n, n), 1)
    return (r > c) if strict else (r >= c)


def _norm_proj_kernel(x_ref, g_ref, w_ref, e_ref, gain_ref, o_ref, *, n_norm_tiles, group):
    xn = _rms(x_ref[...], g_ref[...])
    z = _dot(xn.astype(BF16), w_ref[...])
    if n_norm_tiles == 0:
        o_ref[...] = z
        return
    j = pl.program_id(1)

    @pl.when(j < n_norm_tiles)
    def _():
        zz = z * z
        hi = zz.astype(BF16)
        lo = (zz - hi.astype(F32)).astype(BF16)
        s = _dot(hi, e_ref[...]) + _dot(lo, e_ref[...])
        o_ref[...] = z * lax.rsqrt(s * (1.0 / group) + EPS) * gain_ref[...]

    @pl.when(j >= n_norm_tiles)
    def _():
        o_ref[...] = z


def _norm_proj(x, g, w, *, tm, tn, n_norm_tiles=0, group=1, gain=None, time_major=None, name):
    T, D = x.shape
    N = w.shape[1]
    nj = N // tn
    if gain is None:
        gain = jnp.ones((N,), F32)
    gi = lax.broadcasted_iota(jnp.int32, (tn, tn), 0) // group
    gj = lax.broadcasted_iota(jnp.int32, (tn, tn), 1) // group
    e = (gi == gj).astype(BF16)
    if time_major is None:
        out_shape = jax.ShapeDtypeStruct((T, N), F32)
        out_spec = pl.BlockSpec((tm, tn), lambda i, j: (i, j))
    else:
        B, L = time_major
        nt = L // tm
        out_shape = jax.ShapeDtypeStruct((L, B * N), F32)
        out_spec = pl.BlockSpec((tm, tn), lambda i, j: (i % nt, (i // nt) * nj + j))
    return pl.pallas_call(
        functools.partial(_norm_proj_kernel, n_norm_tiles=n_norm_tiles, group=group),
        grid=(T // tm, nj),
        in_specs=[
            pl.BlockSpec((tm, D), lambda i, j: (i, 0)),
            pl.BlockSpec((1, D), lambda i, j: (0, 0)),
            pl.BlockSpec((D, tn), lambda i, j: (0, j)),
            pl.BlockSpec((tn, tn), lambda i, j: (0, 0)),
            pl.BlockSpec((1, tn), lambda i, j: (0, j)),
        ],
        out_specs=out_spec,
        out_shape=out_shape,
        compiler_params=_cparams("parallel", "arbitrary"),
        name=name,
    )(x, g.reshape(1, D), w, e, gain.reshape(1, N))


def _proj_res_kernel(*refs, n_in):
    x_ref, o_ref = refs[0], refs[-1]
    acc = x_ref[...]
    for k in range(n_in):
        a_ref, w_ref = refs[1 + 2 * k], refs[2 + 2 * k]
        acc = acc + _dot(a_ref[...].astype(BF16), w_ref[...])
    o_ref[...] = acc


def _proj_res(x, parts, *, tm, name):
    T, N = x.shape
    in_specs = [pl.BlockSpec((tm, N), lambda i: (i, 0))]
    args = [x]
    for a, w, spec in parts:
        in_specs += [spec, pl.BlockSpec(w.shape, lambda i: (0, 0))]
        args += [a, w]
    return pl.pallas_call(
        functools.partial(_proj_res_kernel, n_in=len(parts)),
        grid=(T // tm,),
        in_specs=in_specs,
        out_specs=pl.BlockSpec((tm, N), lambda i: (i, 0)),
        out_shape=jax.ShapeDtypeStruct((T, N), F32),
        compiler_params=_cparams("parallel"),
        name=name,
    )(*args)


def _neumann_inverse(mats, n):
    eye = (lax.broadcasted_iota(jnp.int32, (n, n), 0) == lax.broadcasted_iota(jnp.int32, (n, n), 1)).astype(F32)
    ts = [eye - a for a in mats]
    ps = [a.astype(BF16) for a in mats]
    k = 2
    while k < n:
        ps = [_dot(p, p).astype(BF16) for p in ps]
        ts = [t + _dot(t.astype(BF16), p) for t, p in zip(ts, ps)]
        k *= 2
    return ts


def _even_mixer_kernel(qkv_ref, zg_ref, hq_ref, hf_ref, hi_ref, hg_ref, ba_ref,
                       conv0_ref, sg0_ref, sh0_ref,
                       convw_ref, nega_ref, dtb_ref, gnorm_ref, hnorm_ref, lb_ref,
                       mixed_ref, tail_ref, sg_ref, sh_ref,
                       xp_ref, qkvc_ref, beta_ref, g_ref,
                       *, tt, cg, ch, l_real):
    ti = pl.program_id(1)
    n_tiles = pl.num_programs(1)

    @pl.when(ti == 0)
    def _():
        xp_ref[0:SUBLANES, :] = conv0_ref[...]
        sg_ref[...] = sg0_ref[...]
        sh_ref[...] = sh0_ref[...]

    xp_ref[SUBLANES:SUBLANES + tt, :] = qkv_ref[...]
    conv = convw_ref[GDN_CONV - 1:GDN_CONV, :] * qkv_ref[...]
    for j in range(GDN_CONV - 1):
        off = SUBLANES - (GDN_CONV - 1) + j
        conv = conv + convw_ref[j:j + 1, :] * xp_ref[off:off + tt, :]
    qkvc_ref[...] = _silu(conv)
    lr_last = l_real - (l_real - 1) // tt * tt
    tail_ref[...] = xp_ref[lr_last:lr_last + SUBLANES, :]
    carry = xp_ref[tt:tt + SUBLANES, :]
    xp_ref[0:SUBLANES, :] = carry

    padded = (l_real % tt) != 0
    if padded:
        row = lax.broadcasted_iota(jnp.int32, (tt, 1), 0) + ti * tt
        valid = row < l_real
    ba = ba_ref[...]
    beta_all = jax.nn.sigmoid(ba)
    g_all = nega_ref[...] * jax.nn.softplus(ba + dtb_ref[...])
    if padded:
        beta_all = jnp.where(valid, beta_all, 0.0)
        g_all = jnp.where(valid, g_all, 0.0)
    beta_ref[...] = beta_all
    g_ref[...] = g_all

    ltri_g = _tri(cg).astype(F32)
    tril_g = _tri(cg)
    stril_g = _tri(cg, strict=True)
    gnorm = gnorm_ref[...]
    hnorm = hnorm_ref[...]

    def gdn_chunk(ci, _):
        r0 = pl.multiple_of(ci * cg, cg)
        rows = pl.ds(r0, cg)
        gcum = _dot(ltri_g, g_ref[rows, :], HI)
        gpad = gcum if cg == LANES else jnp.concatenate([gcum, jnp.zeros((LANES - cg, LANES), F32)], axis=0)
        gt = gpad.T
        bet = beta_ref[rows, :]
        heads = range(H_A)
        qs, ks, vs, betas, gcs, decays, glasts = [], [], [], [], [], [], []
        for h in heads:
            q = qkvc_ref[rows, h * HD_A:(h + 1) * HD_A]
            k = qkvc_ref[rows, D_A + h * HD_A:D_A + (h + 1) * HD_A]
            qs.append(q * lax.rsqrt(jnp.sum(q * q, axis=-1, keepdims=True) + EPS) * (HD_A ** -0.5))
            ks.append(k * lax.rsqrt(jnp.sum(k * k, axis=-1, keepdims=True) + EPS))
            vs.append(qkvc_ref[rows, 2 * D_A + h * HD_A:2 * D_A + (h + 1) * HD_A])
            betas.append(bet[:, h:h + 1])
            gc = gcum[:, H_A + h:H_A + h + 1]
            gr = gt[H_A + h:H_A + h + 1, 0:cg]
            gcs.append(gc)
            glasts.append(gcum[cg - 1:cg, H_A + h:H_A + h + 1])
            decays.append(jnp.exp(jnp.where(tril_g, gc - gr, -jnp.inf)))
        kqs = [_dot_nt(jnp.concatenate([ks[h], qs[h]], axis=0).astype(BF16), ks[h].astype(BF16))
               for h in heads]
        ts = _neumann_inverse([jnp.where(stril_g, betas[h] * kqs[h][0:cg] * decays[h], 0.0) for h in heads], cg)
        egs = [jnp.exp(gcs[h]) for h in heads]
        sols = [_dot(ts[h].astype(BF16),
                     jnp.concatenate([betas[h] * vs[h], (betas[h] * egs[h]) * ks[h]], axis=-1).astype(BF16))
                for h in heads]
        ss = [sg_ref[h] for h in heads]
        wss = [_dot(jnp.concatenate([sols[h][:, HD_A:2 * HD_A], qs[h] * egs[h]], axis=0).astype(BF16),
                    ss[h].astype(BF16)) for h in heads]
        ubs = [(sols[h][:, 0:HD_A] - wss[h][0:cg]).astype(BF16) for h in heads]
        os_ = [wss[h][cg:2 * cg] + _dot((kqs[h][cg:2 * cg] * decays[h]).astype(BF16), ubs[h]) for h in heads]
        for h in heads:
            k_out = ks[h] * jnp.exp(glasts[h] - gcs[h])
            sg_ref[h] = ss[h] * jnp.exp(glasts[h]) + _dot_tn(k_out.astype(BF16), ubs[h])
        outs = [_rms(os_[h], gnorm) * _silu(zg_ref[rows, h * HD_A:(h + 1) * HD_A]) for h in heads]
        for p in range(H_A // 2):
            mixed_ref[rows, p * LANES:(p + 1) * LANES] = jnp.concatenate([outs[2 * p], outs[2 * p + 1]], axis=-1)
        return 0

    lax.fori_loop(0, tt // cg, gdn_chunk, 0)

    ltri_h = _tri(ch).astype(F32)
    tril_h = _tri(ch)
    lb = lb_ref[...]
    r_mid = (ch - 1) // 2

    def hgrn_chunk(ci, _):
        r0 = pl.multiple_of(ci * ch, ch)
        rows = pl.ds(r0, ch)
        f = lb + (1.0 - lb) * jax.nn.sigmoid(hf_ref[rows, :])
        logf = jnp.log(f)
        kk = 1.0 - f
        if padded:
            vrow = (lax.broadcasted_iota(jnp.int32, (ch, 1), 0) + r0 + ti * tt) < l_real
            logf = jnp.where(vrow, logf, 0.0)
            kk = jnp.where(vrow, kk, 0.0)
        qh = _silu(hq_ref[rows, :])
        bc = _dot(ltri_h, logf, HI)
        bref = bc[r_mid:r_mid + 1, :]
        blast = bc[ch - 1:ch, :]
        qa = (qh * jnp.exp(bc - bref)).astype(BF16)
        kb = (kk * jnp.exp(bref - bc)).astype(BF16)
        q_in = (qh * jnp.exp(bc)).astype(BF16)
        k_out = (kk * jnp.exp(blast - bc)).astype(BF16)
        fl = jnp.exp(blast)
        vv = hi_ref[rows, :].astype(BF16)
        heads = range(H_B)
        sls = [slice(h * DK_B, (h + 1) * DK_B) for h in heads]
        atts = [jnp.where(tril_h, _dot_nt(qa[:, sl], kb[:, sl]), 0.0).astype(BF16) for sl in sls]
        sts = [sh_ref[h] for h in heads]
        inter = [_dot_nt(q_in[:, sls[h]], sts[h].astype(BF16)) for h in heads]
        os_ = [inter[h] + _dot(atts[h], vv[:, sls[h]]) for h in heads]
        for h in heads:
            sh_ref[h] = sts[h] * fl[:, sls[h]] + _dot_tn(vv[:, sls[h]], k_out[:, sls[h]])
        outs = [_rms(os_[h], hnorm) * _silu(hg_ref[rows, h * DV_B:(h + 1) * DV_B]) for h in heads]
        for p in range(H_B // 2):
            c0 = D_A + p * LANES
            mixed_ref[rows, c0:c0 + LANES] = jnp.concatenate([outs[2 * p], outs[2 * p + 1]], axis=-1)
        return 0

    lax.fori_loop(0, tt // ch, hgrn_chunk, 0)


def _even_mixer(z, conv0, sg0, sh0_t, conv_w, neg_a, dt_bias, gnorm, hnorm, lb, *, l_real, tt):
    B, Lp, _ = z.shape
    cg = min(GDN_CHUNK, tt)
    ch = min(HGRN_CHUNK, tt)
    d3 = 3 * D_A
    nt = Lp // tt
    col = lambda w, idx: pl.BlockSpec((None, tt, w), lambda b, i, idx=idx: (b, i, idx))
    state = lambda: pl.BlockSpec((None, H_A, HD_A, HD_A), lambda b, i: (b, 0, 0, 0))
    vec = lambda n: pl.BlockSpec((1, n), lambda b, i: (0, 0))
    nega_row = jnp.zeros((1, LANES), F32).at[0, H_A:2 * H_A].set(neg_a)
    dtb_row = jnp.zeros((1, LANES), F32).at[0, H_A:2 * H_A].set(dt_bias)
    return pl.pallas_call(
        functools.partial(_even_mixer_kernel, tt=tt, cg=cg, ch=ch, l_real=l_real),
        grid=(B, nt),
        in_specs=[
            col(d3, 0), col(D_A, 3), col(D_A, 4), col(D_A, 5), col(D_A, 6), col(D_A, 7), col(LANES, 32),
            pl.BlockSpec((None, SUBLANES, d3), lambda b, i: (b, 0, 0)),
            state(), state(),
            pl.BlockSpec((GDN_CONV, d3), lambda b, i: (0, 0)),
            vec(LANES), vec(LANES), vec(HD_A), vec(DV_B), vec(D_A),
        ],
        out_specs=[
            pl.BlockSpec((None, tt, 2 * D_A), lambda b, i: (b, i, 0)),
            pl.BlockSpec((None, SUBLANES, d3), lambda b, i: (b, 0, 0)),
            state(), state(),
        ],
        out_shape=[
            jax.ShapeDtypeStruct((B, Lp, 2 * D_A), F32),
            jax.ShapeDtypeStruct((B, SUBLANES, d3), F32),
            jax.ShapeDtypeStruct((B, H_A, HD_A, HD_A), F32),
            jax.ShapeDtypeStruct((B, H_B, DV_B, DK_B), F32),
        ],
        scratch_shapes=[
            pltpu.VMEM((tt + SUBLANES, d3), F32),
            pltpu.VMEM((tt, d3), F32),
            pltpu.VMEM((tt, LANES), F32),
            pltpu.VMEM((tt, LANES), F32),
        ],
        compiler_params=_cparams("parallel", "arbitrary"),
        name="even_mixer",
    )(z, z, z, z, z, z, z, conv0, sg0, sh0_t, conv_w, nega_row, dtb_row,
      gnorm.reshape(1, HD_A), hnorm.reshape(1, DV_B), lb.reshape(1, D_A))


def _mem_attn_kernel(x_ref, mk_ref, mv_ref, g_ref, wq_ref, gq_ref, wo_ref, o_ref):
    x = x_ref[...]
    q = _dot(_rms(x, g_ref[...]).astype(BF16), wq_ref[...])
    scale = MEM_HD ** -0.5
    outs = []
    for h in range(MEM_H):
        sl = slice(h * MEM_HD, (h + 1) * MEM_HD)
        qh = _rms(q[:, sl], gq_ref[...])
        s = _dot_nt(qh.astype(BF16), mk_ref[:, sl].astype(BF16)) * scale
        s = s - jnp.max(s, axis=-1, keepdims=True)
        p = jnp.exp(s)
        p = p / jnp.sum(p, axis=-1, keepdims=True)
        outs.append(_dot(p.astype(BF16), mv_ref[:, sl].astype(BF16)))
    o = jnp.concatenate(outs, axis=-1)
    o_ref[...] = x + _dot(o.astype(BF16), wo_ref[...])


def _mem_attn(x, mk, mv, g, wq, gq, wo, *, tm):
    B, L, D = x.shape
    M = mk.shape[1]
    dq = MEM_H * MEM_HD
    return pl.pallas_call(
        _mem_attn_kernel,
        grid=(B, L // tm),
        in_specs=[
            pl.BlockSpec((None, tm, D), lambda b, i: (b, i, 0)),
            pl.BlockSpec((None, M, dq), lambda b, i: (b, 0, 0)),
            pl.BlockSpec((None, M, dq), lambda b, i: (b, 0, 0)),
            pl.BlockSpec((1, D), lambda b, i: (0, 0)),
            pl.BlockSpec((D, dq), lambda b, i: (0, 0)),
            pl.BlockSpec((1, MEM_HD), lambda b, i: (0, 0)),
            pl.BlockSpec((dq, D), lambda b, i: (0, 0)),
        ],
        out_specs=pl.BlockSpec((None, tm, D), lambda b, i: (b, i, 0)),
        out_shape=jax.ShapeDtypeStruct((B, L, D), F32),
        compiler_params=_cparams("parallel", "parallel"),
        name="mem_attn",
    )(x, mk, mv, g.reshape(1, D), wq, gq.reshape(1, MEM_HD), wo)


def _ffn_seq_kernel(x_ref, buf0_ref, g_ref, wgu_ref, cw_ref, cb_ref, wd_ref, o_ref, tail_ref, gp_ref, carry_ref, *, tm):
    ti = pl.program_id(1)

    @pl.when(ti == 0)
    def _():
        carry_ref[...] = buf0_ref[...]

    x = x_ref[...]
    xn = _rms(x, g_ref[...]).astype(BF16)
    acc = x
    for c in range(D_FF // FF_CHUNK):
        cs = slice(c * FF_CHUNK, (c + 1) * FF_CHUNK)
        gate = _dot(xn, wgu_ref[:, cs])
        up = _dot(xn, wgu_ref[:, D_FF + c * FF_CHUNK:D_FF + (c + 1) * FF_CHUNK])
        gp_ref[0:SUBLANES, :] = carry_ref[:, cs]
        gp_ref[SUBLANES:SUBLANES + tm, :] = gate
        conv = cw_ref[FFN_CONV - 1:FFN_CONV, cs] * gate + cb_ref[:, cs]
        for j in range(FFN_CONV - 1):
            off = SUBLANES - (FFN_CONV - 1) + j
            conv = conv + cw_ref[j:j + 1, cs] * gp_ref[off:off + tm, :]
        last = gp_ref[tm:tm + SUBLANES, :]
        carry_ref[:, cs] = last
        tail_ref[:, cs] = last
        act = _silu(conv) * up
        acc = acc + _dot(act.astype(BF16), wd_ref[cs, :])
    o_ref[...] = acc


def _ffn_seq(x, buf0, g, wgu, cw, cb, wd, *, tm):
    B, L, D = x.shape
    const = lambda shape: pl.BlockSpec(shape, lambda b, i: (0,) * len(shape), pipeline_mode=pl.Buffered(1))
    return pl.pallas_call(
        functools.partial(_ffn_seq_kernel, tm=tm),
        grid=(B, L // tm),
        in_specs=[
            pl.BlockSpec((None, tm, D), lambda b, i: (b, i, 0)),
            pl.BlockSpec((None, SUBLANES, D_FF), lambda b, i: (b, 0, 0)),
            const((1, D)), const((D, 2 * D_FF)), const((FFN_CONV, D_FF)), const((1, D_FF)), const((D_FF, D)),
        ],
        out_specs=[
            pl.BlockSpec((None, tm, D), lambda b, i: (b, i, 0)),
            pl.BlockSpec((None, SUBLANES, D_FF), lambda b, i: (b, 0, 0)),
        ],
        out_shape=[jax.ShapeDtypeStruct((B, L, D), F32), jax.ShapeDtypeStruct((B, SUBLANES, D_FF), F32)],
        scratch_shapes=[pltpu.VMEM((tm + SUBLANES, FF_CHUNK), F32), pltpu.VMEM((SUBLANES, D_FF), F32)],
        compiler_params=_cparams("parallel", "arbitrary"),
        name="conv_ffn_seq",
    )(x, buf0, g.reshape(1, D), wgu, cw, cb.reshape(1, D_FF), wd)


def _ffn_step_kernel(x_ref, p2_ref, p1_ref, g_ref, wgu_ref, cw_ref, cb_ref, wd_ref, o_ref, gate_ref):
    x = x_ref[...]
    xn = _rms(x, g_ref[...]).astype(BF16)
    acc = x
    for c in range(D_FF // FF_CHUNK):
        cs = slice(c * FF_CHUNK, (c + 1) * FF_CHUNK)
        gate = _dot(xn, wgu_ref[:, cs])
        up = _dot(xn, wgu_ref[:, D_FF + c * FF_CHUNK:D_FF + (c + 1) * FF_CHUNK])
        gate_ref[:, cs] = gate
        conv = cw_ref[0:1, cs] * p2_ref[:, cs] + cw_ref[1:2, cs] * p1_ref[:, cs] + cw_ref[2:3, cs] * gate + cb_ref[:, cs]
        act = _silu(conv) * up
        acc = acc + _dot(act.astype(BF16), wd_ref[cs, :])
    o_ref[...] = acc


def _ffn_step(x, p2, p1, g, wgu, cw, cb, wd):
    R, D = x.shape
    full = lambda a: pl.BlockSpec(a.shape, lambda i: (0,) * a.ndim)
    args = (x, p2, p1, g.reshape(1, D), wgu, cw, cb.reshape(1, D_FF), wd)
    return pl.pallas_call(
        _ffn_step_kernel,
        grid=(1,),
        in_specs=[full(a) for a in args],
        out_specs=[pl.BlockSpec((R, D), lambda i: (0, 0)), pl.BlockSpec((R, D_FF), lambda i: (0, 0))],
        out_shape=[jax.ShapeDtypeStruct((R, D), F32), jax.ShapeDtypeStruct((R, D_FF), F32)],
        compiler_params=_cparams("arbitrary"),
        name="conv_ffn_step",
    )(*args)


def _topk_select(gate, n_valid_mask, iota_n):
    g = jnp.where(n_valid_mask, gate, -jnp.inf)
    sel = jnp.zeros(gate.shape, F32)
    iota_f = iota_n.astype(F32)
    for _ in range(MOBA_TOPK):
        m = jnp.max(g, axis=-1, keepdims=True)
        idx = jnp.min(jnp.where(g == m, iota_f, 1e9), axis=-1, keepdims=True)
        hit = iota_f == idx
        sel = jnp.where(hit & (m > -jnp.inf), 1.0, sel)
        g = jnp.where(hit, -jnp.inf, g)
    return sel


def _moba_query_tile(j, q_ref, kb_ref, vb_ref, bias_ref, o_ref, kmean_ref):
    blk = MOBA_BLOCK
    w = (j + 1) * blk
    causal = _tri(blk)
    heads = range(H_C)
    sls = [slice(h * HD_C, (h + 1) * HD_C) for h in heads]
    qs = [q_ref[:, sl] for sl in sls]
    if j > MOBA_TOPK:
        iota_n = lax.broadcasted_iota(jnp.int32, (blk, j), 1)
        gates = [_dot_nt(qs[h], kmean_ref[0:j, sls[h]], HI) for h in heads]
        sels = [_topk_select(g, iota_n >= 0, iota_n) for g in gates]
    scores = [_dot_nt((qs[h] * (HD_C ** -0.5)).astype(BF16), kb_ref[0:w, sls[h]]) for h in heads]
    outs = []
    for h in heads:
        parts = []
        for n in range(j + 1):
            s = scores[h][:, n * blk:(n + 1) * blk]
            if j - n < 2:
                s = s + bias_ref[h, j - n]
            if n == j:
                s = jnp.where(causal, s, -jnp.inf)
            elif j > MOBA_TOPK:
                s = jnp.where(sels[h][:, n:n + 1] > 0.5, s, -jnp.inf)
            parts.append(s)
        m = jnp.max(parts[0], axis=-1, keepdims=True)
        for s in parts[1:]:
            m = jnp.maximum(m, jnp.max(s, axis=-1, keepdims=True))
        ps = [jnp.exp(s - m) for s in parts]
        l = jnp.sum(ps[0], axis=-1, keepdims=True)
        for p in ps[1:]:
            l = l + jnp.sum(p, axis=-1, keepdims=True)
        pcat = ps[0] if j == 0 else jnp.concatenate(ps, axis=-1)
        outs.append(_dot(pcat.astype(BF16), vb_ref[0:w, sls[h]]) / l)
    for p in range(H_C // 2):
        o_ref[:, p * LANES:(p + 1) * LANES] = jnp.concatenate([outs[2 * p], outs[2 * p + 1]], axis=-1)


def _moba_seq_kernel(q_ref, k_ref, v_ref, bias_ref, o_ref, kmean_ref, kb_ref, vb_ref, *, nb):
    j = pl.program_id(1)
    blk = MOBA_BLOCK

    @pl.when(j == 0)
    def _():
        for n in range(nb):
            kmean_ref[n:n + 1, :] = jnp.mean(k_ref[n * blk:(n + 1) * blk, :], axis=0, keepdims=True)
        kb_ref[...] = k_ref[...].astype(BF16)
        vb_ref[...] = v_ref[...].astype(BF16)

    for jj in range(nb):
        pl.when(j == jj)(functools.partial(_moba_query_tile, jj, q_ref, kb_ref, vb_ref, bias_ref, o_ref, kmean_ref))


def _moba_seq(qkv, bias_tiles):
    B, L, _ = qkv.shape
    nb = L // MOBA_BLOCK
    dc = H_C * HD_C
    return pl.pallas_call(
        functools.partial(_moba_seq_kernel, nb=nb),
        grid=(B, nb),
        in_specs=[
            pl.BlockSpec((None, MOBA_BLOCK, dc), lambda b, j: (b, j, 0)),
            pl.BlockSpec((None, L, dc), lambda b, j: (b, 0, 1)),
            pl.BlockSpec((None, L, dc), lambda b, j: (b, 0, 2)),
            pl.BlockSpec(bias_tiles.shape, lambda b, j: (0, 0, 0, 0), pipeline_mode=pl.Buffered(1)),
        ],
        out_specs=pl.BlockSpec((None, MOBA_BLOCK, dc), lambda b, j: (b, j, 0)),
        out_shape=jax.ShapeDtypeStruct((B, L, dc), F32),
        scratch_shapes=[pltpu.VMEM((nb, dc), F32), pltpu.VMEM((L, dc), BF16), pltpu.VMEM((L, dc), BF16)],
        compiler_params=_cparams("parallel", "arbitrary"),
        name="moba_seq",
    )(qkv, qkv, qkv, bias_tiles)


MOBA_PAGES_PER_BLOCK = MOBA_BLOCK // PAGE_SIZE
MOBA_STEP_PAGES = 8


def _moba_score_kernel(pt_ref, q_ref, *refs, n_pages):
    del pt_ref
    ppg, ppb = MOBA_STEP_PAGES, MOBA_PAGES_PER_BLOCK
    kp_refs = refs[:ppg]
    s_ref, sel_ref, gate_ref = refs[ppg:]
    step = pl.program_id(1)
    nbp = n_pages // ppb
    q = q_ref[...]
    inv = 1.0 / (MOBA_BLOCK * HD_C ** -0.5)
    gsum = None
    for r in range(ppg):
        sc = jnp.sum(kp_refs[r][...] * q, axis=1)
        s_ref[:, r * PAGE_SIZE:(r + 1) * PAGE_SIZE] = sc
        gs = jnp.sum(sc, axis=-1, keepdims=True)
        gsum = gs if r % ppb == 0 else gsum + gs
        if r % ppb == ppb - 1:
            gate_ref[step * (ppg // ppb) + r // ppb] = jnp.broadcast_to(gsum * inv, (H_C, LANES))

    @pl.when(step == n_pages // ppg - 1)
    def _():
        g = gate_ref[...]
        iota_b = lax.broadcasted_iota(jnp.int32, (nbp, H_C, LANES), 0).astype(F32)
        for i in range(MOBA_TOPK):
            m = jnp.max(g, axis=0, keepdims=True)
            idx = jnp.min(jnp.where(g == m, iota_b, 1e9), axis=0, keepdims=True)
            sel_ref[i] = idx[0]
            g = jnp.where(iota_b == idx, -jnp.inf, g)


def _moba_attend_kernel(pt_ref, sel_ref, q_ref, kn_ref, vn_ref, blast_ref, bsc_ref, *refs, nbp):
    del pt_ref
    nsel, ppb = MOBA_TOPK, MOBA_PAGES_PER_BLOCK
    s_refs, v_refs, o_ref = refs[:nsel], refs[nsel:nsel + nsel * ppb], refs[-1]
    b, h = pl.program_id(0), pl.program_id(1)
    far = bsc_ref[h, 0]
    s_new = jnp.sum(q_ref[...] * kn_ref[...], axis=-1, keepdims=True) * (HD_C ** -0.5) + bsc_ref[h, 1]
    ss = []
    for i in range(nsel):
        blk = sel_ref[(b * H_C + h) * nsel + i]
        ss.append(s_refs[i][...] + jnp.where(blk == nbp - 1, blast_ref[...], far))
    m = s_new
    for s in ss:
        m = jnp.maximum(m, jnp.max(s, axis=-1, keepdims=True))
    w_new = jnp.exp(s_new - m)
    den = w_new
    num = w_new * vn_ref[...]
    for i in range(nsel):
        e = jnp.exp(ss[i] - m)
        den = den + jnp.sum(e, axis=-1, keepdims=True)
        eb = jnp.broadcast_to(e, (SUBLANES, MOBA_BLOCK)).astype(BF16)
        for r in range(ppb):
            pv = _dot_nt(eb[:, r * PAGE_SIZE:(r + 1) * PAGE_SIZE], v_refs[i * ppb + r][...].astype(BF16))
            num = num + pv[0:1]
    o_ref[...] = num / den


def _moba_step(q, k_new, v_new, cache_k, cache_v, page_table, bias_last, bias_scalars):
    B = q.shape[0]
    n_pages = page_table.shape[1]
    ppg, ppb, nsel = MOBA_STEP_PAGES, MOBA_PAGES_PER_BLOCK, MOBA_TOPK
    assert n_pages % ppg == 0 and ppg % ppb == 0
    nbp = n_pages // ppb
    assert nbp >= nsel
    pt = page_table.reshape(-1)
    q_rep = jnp.broadcast_to((q * (HD_C ** -0.5))[..., None], (B, H_C, HD_C, LANES))
    kpage = lambda r: pl.BlockSpec((None, H_C, HD_C, PAGE_SIZE),
                                   lambda b, s, pt, r=r: (pt[b * n_pages + s * ppg + r], 0, 0, 0))
    scores, sel = pl.pallas_call(
        functools.partial(_moba_score_kernel, n_pages=n_pages),
        grid_spec=pltpu.PrefetchScalarGridSpec(
            num_scalar_prefetch=1,
            grid=(B, n_pages // ppg),
            in_specs=[pl.BlockSpec((None, H_C, HD_C, LANES), lambda b, s, pt: (b, 0, 0, 0))]
                     + [kpage(r) for r in range(ppg)],
            out_specs=[pl.BlockSpec((None, H_C, ppg * PAGE_SIZE), lambda b, s, pt: (b, 0, s)),
                       pl.BlockSpec((None, nsel, H_C, LANES), lambda b, s, pt: (b, 0, 0, 0))],
            scratch_shapes=[pltpu.VMEM((nbp, H_C, LANES), F32)],
        ),
        out_shape=[jax.ShapeDtypeStruct((B, H_C, n_pages * PAGE_SIZE), F32),
                   jax.ShapeDtypeStruct((B, nsel, H_C, LANES), F32)],
        compiler_params=_cparams("parallel", "arbitrary"),
        name="moba_step_score",
    )(pt, q_rep, *([cache_k] * ppg))
    sel_flat = jnp.swapaxes(sel[:, :, :, 0], 1, 2).astype(jnp.int32).reshape(-1)
    tok = pl.BlockSpec((None, None, 1, HD_C), lambda b, h, pt, sl: (b, h, 0, 0))
    srow = lambda i: pl.BlockSpec((None, None, None, 1, MOBA_BLOCK),
                                  lambda b, h, pt, sl, i=i: (b, h, sl[(b * H_C + h) * nsel + i], 0, 0))
    vpage = lambda i, r: pl.BlockSpec(
        (None, None, HD_C, PAGE_SIZE),
        lambda b, h, pt, sl, i=i, r=r: (pt[b * n_pages + sl[(b * H_C + h) * nsel + i] * ppb + r], h, 0, 0))
    row = lambda a: a.reshape(B, H_C, 1, HD_C)
    out = pl.pallas_call(
        functools.partial(_moba_attend_kernel, nbp=nbp),
        grid_spec=pltpu.PrefetchScalarGridSpec(
            num_scalar_prefetch=2,
            grid=(B, H_C),
            in_specs=[tok, tok, tok,
                      pl.BlockSpec((None, 1, MOBA_BLOCK), lambda b, h, pt, sl: (h, 0, 0)),
                      pl.BlockSpec(memory_space=pltpu.SMEM)]
                     + [srow(i) for i in range(nsel)]
                     + [vpage(i, r) for i in range(nsel) for r in range(ppb)],
            out_specs=tok,
        ),
        out_shape=jax.ShapeDtypeStruct((B, H_C, 1, HD_C), F32),
        compiler_params=_cparams("parallel", "arbitrary"),
        name="moba_step_attend",
    )(pt, sel_flat, row(q), row(k_new), row(v_new), bias_last, bias_scalars,
      *([scores.reshape(B, H_C, nbp, 1, MOBA_BLOCK)] * nsel), *([cache_v] * (nsel * ppb)))
    return out.reshape(B, H_C, HD_C)


def _s5_kernel(u_ref, h0r_ref, h0i_ref, wb_ref, ar_ref, ai_ref, wc_ref, d_ref, wglu_ref,
               o_ref, hr_ref, hi_ref, x_ref, *, tt, nb):
    ti = pl.program_id(0)
    ns = D_S5_STATE

    @pl.when(ti == 0)
    def _():
        hr_ref[...] = h0r_ref[...]
        hi_ref[...] = h0i_ref[...]

    u = u_ref[...]
    x_ref[...] = _dot(u.astype(BF16), wb_ref[...])
    ar = ar_ref[...]
    ai = ai_ref[...]

    def step(t, carry):
        hr, hi = carry
        rows = pl.ds(pl.multiple_of(t * nb, nb), nb)
        nr = ar * hr - ai * hi + x_ref[rows, 0:ns]
        ni = ar * hi + ai * hr + x_ref[rows, ns:2 * ns]
        x_ref[rows, 0:ns] = nr
        x_ref[rows, ns:2 * ns] = ni
        return nr, ni

    hr, hi = lax.fori_loop(0, tt, step, (hr_ref[...], hi_ref[...]))
    hr_ref[...] = hr
    hi_ref[...] = hi
    y = _dot(x_ref[...].astype(BF16), wc_ref[...]) + d_ref[...] * u
    y = jax.nn.gelu(y)
    gl = _dot(y.astype(BF16), wglu_ref[...])
    o_ref[...] = gl[:, 0:D_S5] * jax.nn.sigmoid(gl[:, D_S5:2 * D_S5])


def _s5(u_tm, h0r, h0i, wb, ar, ai, wc, d, wglu, *, tt, nb):
    R = u_tm.shape[0]
    L = R // nb
    const = lambda a: pl.BlockSpec(a.shape, lambda i: (0,) * a.ndim)
    rows = pl.BlockSpec((tt * nb, D_S5), lambda i: (i, 0))
    st = pl.BlockSpec((nb, D_S5_STATE), lambda i: (0, 0))
    args = (u_tm, h0r, h0i, wb, ar, ai, wc, d, wglu)
    return pl.pallas_call(
        functools.partial(_s5_kernel, tt=tt, nb=nb),
        grid=(L // tt,),
        in_specs=[rows] + [const(a) for a in args[1:]],
        out_specs=[rows, st, st],
        out_shape=[jax.ShapeDtypeStruct((R, D_S5), F32),
                   jax.ShapeDtypeStruct((nb, D_S5_STATE), F32), jax.ShapeDtypeStruct((nb, D_S5_STATE), F32)],
        scratch_shapes=[pltpu.VMEM((tt * nb, 2 * D_S5_STATE), F32)],
        compiler_params=_cparams("arbitrary"),
        name="s5",
    )(*args)


def _rel_bucket(dist):
    n = jnp.maximum(dist, 0)
    exact = N_BUCKETS // 2
    nf = jnp.maximum(n, exact).astype(F32)
    large = exact + (jnp.log(nf / exact) / math.log(MAX_DISTANCE / exact) * (N_BUCKETS - exact)).astype(jnp.int32)
    return jnp.where(n < exact, n, jnp.minimum(large, N_BUCKETS - 1))


def _block_diag(w):
    G, a, b = w.shape
    eye = jnp.eye(G, dtype=w.dtype)
    return (eye[:, None, :, None] * w[:, :, None, :]).reshape(G * a, G * b)


def _s5_weights(A_re, A_im, log_dt, B_re, B_im, C_re, C_im, D):
    dt = jnp.exp(log_dt.astype(F32))[:, None]
    lr, li = A_re.astype(F32), A_im.astype(F32)
    mag = jnp.exp(lr * dt)
    ar, ai = mag * jnp.cos(li * dt), mag * jnp.sin(li * dt)
    den = lr * lr + li * li
    nr = ar - 1.0
    cr, ci = (nr * lr + ai * li) / den, (ai * lr - nr * li) / den
    bbr = cr[..., None] * B_re - ci[..., None] * B_im
    bbi = cr[..., None] * B_im + ci[..., None] * B_re
    wb = jnp.concatenate([_block_diag(bbr.transpose(0, 2, 1)), _block_diag(bbi.transpose(0, 2, 1))], axis=1)
    wc = jnp.concatenate([_block_diag(C_re.transpose(0, 2, 1)), -_block_diag(C_im.transpose(0, 2, 1))], axis=0)
    return (wb.astype(BF16), ar.reshape(1, -1), ai.reshape(1, -1), wc.astype(BF16), D.reshape(1, -1).astype(F32))


def _pad_rows_front(a, rows):
    return jnp.pad(a, ((0, 0), (rows - a.shape[1], 0), (0, 0)))


def _trunk(x, st, W, *, decode):
    B, L, D = x.shape
    T = B * L
    tm = min(512, T)
    out = {}

    x2 = x.reshape(T, D)
    z = _norm_proj(x2, W['norm_mix'][0], W['ev_w_in'], tm=tm, tn=EV_IN_TILE, name="ev_in_proj")
    zc = z.shape[1]
    if decode:
        lp = SUBLANES
        z3 = jnp.pad(z.reshape(B, L, zc), ((0, 0), (0, lp - L), (0, 0)))
        tt = lp
    else:
        lp = L
        z3 = z.reshape(B, L, zc)
        tt = min(512, L)
    conv0 = _pad_rows_front(st['gdn_conv'], SUBLANES)
    mixed, tail, sg, sh_t = _even_mixer(z3, conv0, st['gdn'], jnp.swapaxes(st['hgrn'], -1, -2), W['gdn_conv_w'],
                                        W['gdn_neg_a'], W['gdn_dt_bias'], W['gdn_norm'], W['hgrn_norm'], W['hgrn_lb'],
                                        l_real=L, tt=tt)
    out['gdn_conv'] = tail[:, SUBLANES - (GDN_CONV - 1):, :]
    out['gdn'] = sg
    out['hgrn'] = jnp.swapaxes(sh_t, -1, -2)
    mixed2 = mixed[:, :L].reshape(T, 2 * D_A)
    x2 = _proj_res(x2, [(mixed2, W['ev_w_out'], pl.BlockSpec((tm, 2 * D_A), lambda i: (i, 0)))], tm=tm, name="ev_out_proj")
    x2, fs = _post_mixer(x2, st, W, 0, B, L, decode)
    out['ffn0'] = fs

    dc = H_C * HD_C
    gqk = jnp.concatenate([jnp.tile(W['moba_gq'], H_C), jnp.tile(W['moba_gk'], H_C), jnp.ones((dc,), F32)])
    qkv = _norm_proj(x2, W['norm_mix'][1], W['od_w_in'][:, :3 * dc], tm=tm, tn=dc, n_norm_tiles=2, group=HD_C,
                     gain=gqk, name="od_in_proj_qkv")
    tmu = min(128, L) if not decode else T
    u_tm = _norm_proj(x2, W['norm_mix'][1], W['od_w_in'][:, 3 * dc:], tm=tmu, tn=D_S5,
                      time_major=None if decode else (B, L), name="od_in_proj_u")
    u_tm = u_tm.reshape(L * B, D_S5)
    out['moba_k'] = qkv[:, dc:2 * dc].reshape(B, L, H_C, HD_C)
    out['moba_v'] = qkv[:, 2 * dc:].reshape(B, L, H_C, HD_C)
    if decode:
        heads = lambda a: a.reshape(B, H_C, HD_C)
        o_c = _moba_step(heads(qkv[:, :dc]), heads(qkv[:, dc:2 * dc]), heads(qkv[:, 2 * dc:]),
                         st['cache_k'], st['cache_v'], st['page_table'], *W['moba_bias_step'])
        o_c = o_c.reshape(T, dc)
        tts = 1
    else:
        o_c = _moba_seq(qkv.reshape(B, L, 3 * dc), W['moba_bias_tiles']).reshape(T, dc)
        tts = min(64, L)
    o_d, hr, hi = _s5(u_tm, st['s5_re'], st['s5_im'], *W['s5'], W['s5_glu_w'], tt=tts, nb=B)
    out['s5_re'] = hr.reshape(B, S5_GROUPS, S5_STATE)
    out['s5_im'] = hi.reshape(B, S5_GROUPS, S5_STATE)
    if decode:
        od_spec = pl.BlockSpec((tm, D_S5), lambda i: (i, 0))
        o_d2 = o_d
    else:
        nt = L // tm
        od_spec = pl.BlockSpec((tm, D_S5), lambda i: (i % nt, i // nt))
        o_d2 = o_d.reshape(L, B * D_S5)
    x2 = _proj_res(x2, [(o_c, W['od_w_out'][:dc], pl.BlockSpec((tm, dc), lambda i: (i, 0))),
                        (o_d2, W['od_w_out'][dc:], od_spec)], tm=tm, name="od_out_proj")
    x2, fs = _post_mixer(x2, st, W, 1, B, L, decode)
    out['ffn1'] = fs
    out['y'] = x2.reshape(B, L, D)
    return out


def _post_mixer(x2, st, W, l, B, L, decode):
    T, D = x2.shape
    x3 = _mem_attn(x2.reshape(B, L, D), st['mem_k'][l], st['mem_v'][l], W['norm_mem'][l], W['mem_wq'][l],
                   W['mem_gq'][l], W['mem_wo'][l], tm=min(512, L))
    buf = st['ffn_conv'][l]
    if decode:
        y, gate = _ffn_step(x3.reshape(T, D), buf[:, 0], buf[:, 1], W['norm_ffn'][l], W['ffn_w_gu'][l],
                            W['ffn_conv_w'][l], W['ffn_conv_b'][l], W['ffn_w_down'][l])
        fs = jnp.stack([buf[:, 1], gate], axis=1)
        return y, fs
    y, tail = _ffn_seq(x3, _pad_rows_front(buf, SUBLANES), W['norm_ffn'][l], W['ffn_w_gu'][l], W['ffn_conv_w'][l],
                       W['ffn_conv_b'][l], W['ffn_w_down'][l], tm=min(512, L))
    return y.reshape(T, D), tail[:, SUBLANES - (FFN_CONV - 1):, :]


def _mem_kv(mem, g_norm, w_kv, g_k):
    B, M, D = mem.shape
    dq = MEM_H * MEM_HD
    gain = jnp.concatenate([jnp.tile(g_k, MEM_H), jnp.ones((dq,), F32)])
    kv = _norm_proj(mem.reshape(B * M, D), g_norm, w_kv, tm=min(512, B * M), tn=dq, n_norm_tiles=1, group=MEM_HD,
                    gain=gain, name="mem_kv_proj")
    return kv[:, :dq].reshape(B, M, dq), kv[:, dq:].reshape(B, M, dq)


def kernel(x_prompt, x_sample, state_gdn_conv, state_gdn, state_hgrn, cache_moba_k, cache_moba_v, state_s5_re, state_s5_im, cache_mem_k, cache_mem_v, state_ffn_conv, page_table, mem_prompt, rel_bias, hgrn_lb_raw, norm_mix, norm_mem, norm_memkv, norm_ffn, mem_wq, mem_wkv, mem_gq, mem_gk, mem_wo, ffn_w_gu, ffn_conv_w, ffn_conv_b, ffn_w_down, ev_w_in, ev_w_out, gdn_conv_w, gdn_A_log, gdn_dt_bias, gdn_norm, hgrn_norm, od_w_in, od_w_out, moba_gq, moba_gk, s5_A_re, s5_A_im, s5_log_dt, s5_B_re, s5_B_im, s5_C_re, s5_C_im, s5_D, s5_glu_w):
    Bp, Lq, D = x_prompt.shape
    Bs = x_sample.shape[0]
    dc = H_C * HD_C
    dq = MEM_H * MEM_HD
    n_pages = page_table.shape[1]
    past = n_pages * PAGE_SIZE

    d3 = 3 * D_A
    w_in = ev_w_in[0]
    ba_cols = w_in[:, 4 * D_A:4 * D_A + 2 * H_A]
    w_in = jnp.concatenate([w_in[:, :4 * D_A], w_in[:, 4 * D_A + 2 * H_A:], ba_cols,
                            jnp.zeros((D, LANES - 2 * H_A), F32)], axis=1)
    lb_all = jnp.cumsum(jax.nn.softmax(hgrn_lb_raw.astype(F32), axis=0), axis=0)
    bias_t = rel_bias.T.astype(F32)
    nblk = MOBA_BLOCK
    by_rel = bias_t[:, _rel_bucket(jnp.arange(2 * nblk, dtype=jnp.int32))]
    mm = jnp.arange(2 * nblk, dtype=jnp.int32)

    def toeplitz(shift):
        y = by_rel[:, (shift - mm) % (2 * nblk)]
        flat = jnp.tile(y, (1, nblk))[:, :nblk * (2 * nblk - 1)]
        return flat.reshape(H_C, nblk, 2 * nblk - 1)[:, :, :nblk]

    far = bias_t[:, _rel_bucket(jnp.int32(2 * nblk))]
    tiles = jnp.stack([toeplitz(0), toeplitz(nblk)], axis=1) - far[:, None, None, None]
    bias_step = (by_rel[:, nblk - jnp.arange(nblk, dtype=jnp.int32)][:, None, :],
                 jnp.stack([far, bias_t[:, 0]], axis=1))
    W = dict(
        norm_mix=norm_mix, norm_mem=norm_mem, norm_ffn=norm_ffn,
        ev_w_in=w_in.astype(BF16), ev_w_out=ev_w_out[0].astype(BF16),
        gdn_conv_w=gdn_conv_w[0], gdn_neg_a=-jnp.exp(gdn_A_log[0].astype(F32)), gdn_dt_bias=gdn_dt_bias[0],
        gdn_norm=gdn_norm[0], hgrn_norm=hgrn_norm[0], hgrn_lb=lb_all[0],
        od_w_in=od_w_in[0].astype(BF16), od_w_out=od_w_out[0].astype(BF16),
        moba_gq=moba_gq[0], moba_gk=moba_gk[0], moba_bias_tiles=tiles, moba_bias_step=bias_step,
        s5=_s5_weights(s5_A_re[0], s5_A_im[0], s5_log_dt[0], s5_B_re[0], s5_B_im[0], s5_C_re[0], s5_C_im[0], s5_D[0]),
        s5_glu_w=s5_glu_w[0].astype(BF16),
        mem_wq=mem_wq.astype(BF16), mem_gq=mem_gq, mem_wo=mem_wo.astype(BF16),
        ffn_w_gu=ffn_w_gu.astype(BF16), ffn_conv_w=ffn_conv_w, ffn_conv_b=ffn_conv_b, ffn_w_down=ffn_w_down.astype(BF16),
    )

    mk_l, mv_l = [], []
    for l in range(2):
        mk, mv = _mem_kv(mem_prompt, norm_memkv[l], mem_wkv[l].astype(BF16), mem_gk[l])
        mk_l.append(mk)
        mv_l.append(mv)
    p_mem_k, p_mem_v = jnp.stack(mk_l), jnp.stack(mv_l)
    zeros = lambda *s: jnp.zeros(s, F32)
    st_p = dict(gdn_conv=zeros(Bp, GDN_CONV - 1, d3), gdn=zeros(Bp, H_A, HD_A, HD_A), hgrn=zeros(Bp, H_B, DK_B, DV_B),
                s5_re=zeros(Bp, D_S5_STATE), s5_im=zeros(Bp, D_S5_STATE), mem_k=p_mem_k, mem_v=p_mem_v,
                ffn_conv=zeros(2, Bp, FFN_CONV - 1, D_FF))
    op = _trunk(x_prompt, st_p, W, decode=False)

    st_s = dict(gdn_conv=state_gdn_conv[0], gdn=state_gdn[0], hgrn=state_hgrn[0],
                s5_re=state_s5_re[0].reshape(Bs, D_S5_STATE), s5_im=state_s5_im[0].reshape(Bs, D_S5_STATE),
                mem_k=cache_mem_k.reshape(2, Bs, -1, dq), mem_v=cache_mem_v.reshape(2, Bs, -1, dq),
                ffn_conv=state_ffn_conv,
                cache_k=jnp.transpose(cache_moba_k[0], (0, 2, 3, 1)), cache_v=jnp.transpose(cache_moba_v[0], (0, 2, 3, 1)),
                page_table=page_table)
    os_ = _trunk(x_sample, st_s, W, decode=True)

    M = mem_prompt.shape[1]
    return (op['y'], os_['y'],
            op['gdn_conv'][None], op['gdn'][None], op['hgrn'][None], op['moba_k'][None], op['moba_v'][None],
            op['s5_re'][None], op['s5_im'][None],
            p_mem_k.reshape(2, Bp, M, MEM_H, MEM_HD), p_mem_v.reshape(2, Bp, M, MEM_H, MEM_HD),
            jnp.stack([op['ffn0'], op['ffn1']]),
            os_['gdn_conv'][None], os_['gdn'][None], os_['hgrn'][None], os_['moba_k'][None], os_['moba_v'][None],
            os_['s5_re'][None], os_['s5_im'][None], jnp.stack([os_['ffn0'], os_['ffn1']]))
```

```python
import functools
import math

import jax
import jax.numpy as jnp
from jax import lax
from jax.experimental import pallas as pl
from jax.experimental.pallas import tpu as pltpu

F32 = jnp.float32
BF16 = jnp.bfloat16
HI = lax.Precision.HIGHEST
EPS = 1e-6

LANES = 128
SUBLANES = 8
VMEM_LIMIT_BYTES = 56 * 1024 * 1024

D_MODEL = 1024
H_A, HD_A, GDN_CONV = 8, 64, 4
H_B, DK_B, DV_B = 8, 64, 64
GDN_CHUNK, HGRN_CHUNK = 64, 32
H_C, HD_C = 8, 64
MOBA_BLOCK, MOBA_TOPK = 256, 3
N_BUCKETS, MAX_DISTANCE = 32, 128
S5_GROUPS, S5_GROUP_CH, S5_STATE = 32, 16, 64
MEM_H, MEM_HD = 4, 128
D_FF, FFN_CONV = 2816, 3
PAGE_SIZE = 128
D_A = H_A * HD_A
D_S5 = S5_GROUPS * S5_GROUP_CH
D_S5_STATE = S5_GROUPS * S5_STATE
FF_CHUNK = D_FF // 2
EV_IN_COLS = 8 * D_A + LANES
EV_IN_TILE = EV_IN_COLS // 3


def _cparams(*sem):
    return pltpu.CompilerParams(dimension_semantics=sem, vmem_limit_bytes=VMEM_LIMIT_BYTES)


def _dot(a, b, precision=None):
    return jnp.dot(a, b, preferred_element_type=F32, precision=precision)


def _dot_nt(a, b, precision=None):
    return lax.dot_general(a, b, (((1,), (1,)), ((), ())), preferred_element_type=F32, precision=precision)


def _dot_tn(a, b, precision=None):
    return lax.dot_general(a, b, (((0,), (0,)), ((), ())), preferred_element_type=F32, precision=precision)


def _rms(x, g):
    return x * lax.rsqrt(jnp.mean(x * x, axis=-1, keepdims=True) + EPS) * g


def _silu(x):
    return x * jax.nn.sigmoid(x)


def _tri(n, strict=False):
    r = lax.broadcasted_iota(jnp.int32, (n, n), 0)
    c = lax.broadcasted_iota(jnp.int32, (n, n), 1)
    return (r > c) if strict else (r >= c)


def _norm_proj_kernel(x_ref, g_ref, w_ref, e_ref, gain_ref, o_ref, *, n_norm_tiles, group):
    xn = _rms(x_ref[...], g_ref[...])
    z = _dot(xn.astype(BF16), w_ref[...])
    if n_norm_tiles == 0:
        o_ref[...] = z
        return
    j = pl.program_id(1)

    @pl.when(j < n_norm_tiles)
    def _():
        zz = z * z
        hi = zz.astype(BF16)
        lo = (zz - hi.astype(F32)).astype(BF16)
        s = _dot(hi, e_ref[...]) + _dot(lo, e_ref[...])
        o_ref[...] = z * lax.rsqrt(s * (1.0 / group) + EPS) * gain_ref[...]

    @pl.when(j >= n_norm_tiles)
    def _():
        o_ref[...] = z


def _norm_proj(x, g, w, *, tm, tn, n_norm_tiles=0, group=1, gain=None, time_major=None, name):
    T, D = x.shape
    N = w.shape[1]
    nj = N // tn
    if gain is None:
        gain = jnp.ones((N,), F32)
    gi = lax.broadcasted_iota(jnp.int32, (tn, tn), 0) // group
    gj = lax.broadcasted_iota(jnp.int32, (tn, tn), 1) // group
    e = (gi == gj).astype(BF16)
    if time_major is None:
        out_shape = jax.ShapeDtypeStruct((T, N), F32)
        out_spec = pl.BlockSpec((tm, tn), lambda i, j: (i, j))
    else:
        B, L = time_major
        nt = L // tm
        out_shape = jax.ShapeDtypeStruct((L, B * N), F32)
        out_spec = pl.BlockSpec((tm, tn), lambda i, j: (i % nt, (i // nt) * nj + j))
    return pl.pallas_call(
        functools.partial(_norm_proj_kernel, n_norm_tiles=n_norm_tiles, group=group),
        grid=(T // tm, nj),
        in_specs=[
            pl.BlockSpec((tm, D), lambda i, j: (i, 0)),
            pl.BlockSpec((1, D), lambda i, j: (0, 0)),
            pl.BlockSpec((D, tn), lambda i, j: (0, j)),
            pl.BlockSpec((tn, tn), lambda i, j: (0, 0)),
            pl.BlockSpec((1, tn), lambda i, j: (0, j)),
        ],
        out_specs=out_spec,
        out_shape=out_shape,
        compiler_params=_cparams("parallel", "arbitrary"),
        name=name,
    )(x, g.reshape(1, D), w, e, gain.reshape(1, N))


def _proj_res_kernel(*refs, n_in):
    x_ref, o_ref = refs[0], refs[-1]
    acc = x_ref[...]
    for k in range(n_in):
        a_ref, w_ref = refs[1 + 2 * k], refs[2 + 2 * k]
        acc = acc + _dot(a_ref[...].astype(BF16), w_ref[...])
    o_ref[...] = acc


def _proj_res(x, parts, *, tm, name):
    T, N = x.shape
    in_specs = [pl.BlockSpec((tm, N), lambda i: (i, 0))]
    args = [x]
    for a, w, spec in parts:
        in_specs += [spec, pl.BlockSpec(w.shape, lambda i: (0, 0))]
        args += [a, w]
    return pl.pallas_call(
        functools.partial(_proj_res_kernel, n_in=len(parts)),
        grid=(T // tm,),
        in_specs=in_specs,
        out_specs=pl.BlockSpec((tm, N), lambda i: (i, 0)),
        out_shape=jax.ShapeDtypeStruct((T, N), F32),
        compiler_params=_cparams("parallel"),
        name=name,
    )(*args)


def _neumann_inverse(mats, n):
    eye = (lax.broadcasted_iota(jnp.int32, (n, n), 0) == lax.broadcasted_iota(jnp.int32, (n, n), 1)).astype(F32)
    ts = [eye - a for a in mats]
    ps = [a.astype(BF16) for a in mats]
    k = 2
    while k < n:
        ps = [_dot(p, p).astype(BF16) for p in ps]
        ts = [t + _dot(t.astype(BF16), p) for t, p in zip(ts, ps)]
        k *= 2
    return ts


def _even_mixer_kernel(qkv_ref, zg_ref, hq_ref, hf_ref, hi_ref, hg_ref, ba_ref,
                       conv0_ref, sg0_ref, sh0_ref,
                       convw_ref, nega_ref, dtb_ref, gnorm_ref, hnorm_ref, lb_ref,
                       mixed_ref, tail_ref, sg_ref, sh_ref,
                       xp_ref, qkvc_ref, beta_ref, g_ref,
                       *, tt, cg, ch, l_real):
    ti = pl.program_id(1)
    n_tiles = pl.num_programs(1)

    @pl.when(ti == 0)
    def _():
        xp_ref[0:SUBLANES, :] = conv0_ref[...]
        sg_ref[...] = sg0_ref[...]
        sh_ref[...] = sh0_ref[...]

    xp_ref[SUBLANES:SUBLANES + tt, :] = qkv_ref[...]
    conv = convw_ref[GDN_CONV - 1:GDN_CONV, :] * qkv_ref[...]
    for j in range(GDN_CONV - 1):
        off = SUBLANES - (GDN_CONV - 1) + j
        conv = conv + convw_ref[j:j + 1, :] * xp_ref[off:off + tt, :]
    qkvc_ref[...] = _silu(conv)
    lr_last = l_real - (l_real - 1) // tt * tt
    tail_ref[...] = xp_ref[lr_last:lr_last + SUBLANES, :]
    carry = xp_ref[tt:tt + SUBLANES, :]
    xp_ref[0:SUBLANES, :] = carry

    padded = (l_real % tt) != 0
    if padded:
        row = lax.broadcasted_iota(jnp.int32, (tt, 1), 0) + ti * tt
        valid = row < l_real
    ba = ba_ref[...]
    beta_all = jax.nn.sigmoid(ba)
    g_all = nega_ref[...] * jax.nn.softplus(ba + dtb_ref[...])
    if padded:
        beta_all = jnp.where(valid, beta_all, 0.0)
        g_all = jnp.where(valid, g_all, 0.0)
    beta_ref[...] = beta_all
    g_ref[...] = g_all

    ltri_g = _tri(cg).astype(F32)
    tril_g = _tri(cg)
    stril_g = _tri(cg, strict=True)
    gnorm = gnorm_ref[...]
    hnorm = hnorm_ref[...]

    def gdn_chunk(ci, _):
        r0 = pl.multiple_of(ci * cg, cg)
        rows = pl.ds(r0, cg)
        gcum = _dot(ltri_g, g_ref[rows, :], HI)
        gpad = gcum if cg == LANES else jnp.concatenate([gcum, jnp.zeros((LANES - cg, LANES), F32)], axis=0)
        gt = gpad.T
        bet = beta_ref[rows, :]
        heads = range(H_A)
        qs, ks, vs, betas, gcs, decays, glasts = [], [], [], [], [], [], []
        for h in heads:
            q = qkvc_ref[rows, h * HD_A:(h + 1) * HD_A]
            k = qkvc_ref[rows, D_A + h * HD_A:D_A + (h + 1) * HD_A]
            qs.append(q * lax.rsqrt(jnp.sum(q * q, axis=-1, keepdims=True) + EPS) * (HD_A ** -0.5))
            ks.append(k * lax.rsqrt(jnp.sum(k * k, axis=-1, keepdims=True) + EPS))
            vs.append(qkvc_ref[rows, 2 * D_A + h * HD_A:2 * D_A + (h + 1) * HD_A])
            betas.append(bet[:, h:h + 1])
            gc = gcum[:, H_A + h:H_A + h + 1]
            gr = gt[H_A + h:H_A + h + 1, 0:cg]
            gcs.append(gc)
            glasts.append(gcum[cg - 1:cg, H_A + h:H_A + h + 1])
            decays.append(jnp.exp(jnp.where(tril_g, gc - gr, -jnp.inf)))
        kqs = [_dot_nt(jnp.concatenate([ks[h], qs[h]], axis=0).astype(BF16), ks[h].astype(BF16))
               for h in heads]
        ts = _neumann_inverse([jnp.where(stril_g, betas[h] * kqs[h][0:cg] * decays[h], 0.0) for h in heads], cg)
        egs = [jnp.exp(gcs[h]) for h in heads]
        sols = [_dot(ts[h].astype(BF16),
                     jnp.concatenate([betas[h] * vs[h], (betas[h] * egs[h]) * ks[h]], axis=-1).astype(BF16))
                for h in heads]
        ss = [sg_ref[h] for h in heads]
        wss = [_dot(jnp.concatenate([sols[h][:, HD_A:2 * HD_A], qs[h] * egs[h]], axis=0).astype(BF16),
                    ss[h].astype(BF16)) for h in heads]
        ubs = [(sols[h][:, 0:HD_A] - wss[h][0:cg]).astype(BF16) for h in heads]
        os_ = [wss[h][cg:2 * cg] + _dot((kqs[h][cg:2 * cg] * decays[h]).astype(BF16), ubs[h]) for h in heads]
        for h in heads:
            k_out = ks[h] * jnp.exp(glasts[h] - gcs[h])
            sg_ref[h] = ss[h] * jnp.exp(glasts[h]) + _dot_tn(k_out.astype(BF16), ubs[h])
        outs = [_rms(os_[h], gnorm) * _silu(zg_ref[rows, h * HD_A:(h + 1) * HD_A]) for h in heads]
        for p in range(H_A // 2):
            mixed_ref[rows, p * LANES:(p + 1) * LANES] = jnp.concatenate([outs[2 * p], outs[2 * p + 1]], axis=-1)
        return 0

    lax.fori_loop(0, tt // cg, gdn_chunk, 0)

    ltri_h = _tri(ch).astype(F32)
    tril_h = _tri(ch)
    lb = lb_ref[...]
    r_mid = (ch - 1) // 2

    def hgrn_chunk(ci, _):
        r0 = pl.multiple_of(ci * ch, ch)
        rows = pl.ds(r0, ch)
        f = lb + (1.0 - lb) * jax.nn.sigmoid(hf_ref[rows, :])
        logf = jnp.log(f)
        kk = 1.0 - f
        if padded:
            vrow = (lax.broadcasted_iota(jnp.int32, (ch, 1), 0) + r0 + ti * tt) < l_real
            logf = jnp.where(vrow, logf, 0.0)
            kk = jnp.where(vrow, kk, 0.0)
        qh = _silu(hq_ref[rows, :])
        bc = _dot(ltri_h, logf, HI)
        bref = bc[r_mid:r_mid + 1, :]
        blast = bc[ch - 1:ch, :]
        qa = (qh * jnp.exp(bc - bref)).astype(BF16)
        kb = (kk * jnp.exp(bref - bc)).astype(BF16)
        q_in = (qh * jnp.exp(bc)).astype(BF16)
        k_out = (kk * jnp.exp(blast - bc)).astype(BF16)
        fl = jnp.exp(blast)
        vv = hi_ref[rows, :].astype(BF16)
        heads = range(H_B)
        sls = [slice(h * DK_B, (h + 1) * DK_B) for h in heads]
        atts = [jnp.where(tril_h, _dot_nt(qa[:, sl], kb[:, sl]), 0.0).astype(BF16) for sl in sls]
        sts = [sh_ref[h] for h in heads]
        inter = [_dot_nt(q_in[:, sls[h]], sts[h].astype(BF16)) for h in heads]
        os_ = [inter[h] + _dot(atts[h], vv[:, sls[h]]) for h in heads]
        for h in heads:
            sh_ref[h] = sts[h] * fl[:, sls[h]] + _dot_tn(vv[:, sls[h]], k_out[:, sls[h]])
        outs = [_rms(os_[h], hnorm) * _silu(hg_ref[rows, h * DV_B:(h + 1) * DV_B]) for h in heads]
        for p in range(H_B // 2):
            c0 = D_A + p * LANES
            mixed_ref[rows, c0:c0 + LANES] = jnp.concatenate([outs[2 * p], outs[2 * p + 1]], axis=-1)
        return 0

    lax.fori_loop(0, tt // ch, hgrn_chunk, 0)


def _even_mixer(z, conv0, sg0, sh0_t, conv_w, neg_a, dt_bias, gnorm, hnorm, lb, *, l_real, tt):
    B, Lp, _ = z.shape
    cg = min(GDN_CHUNK, tt)
    ch = min(HGRN_CHUNK, tt)
    d3 = 3 * D_A
    nt = Lp // tt
    col = lambda w, idx: pl.BlockSpec((None, tt, w), lambda b, i, idx=idx: (b, i, idx))
    state = lambda: pl.BlockSpec((None, H_A, HD_A, HD_A), lambda b, i: (b, 0, 0, 0))
    vec = lambda n: pl.BlockSpec((1, n), lambda b, i: (0, 0))
    nega_row = jnp.zeros((1, LANES), F32).at[0, H_A:2 * H_A].set(neg_a)
    dtb_row = jnp.zeros((1, LANES), F32).at[0, H_A:2 * H_A].set(dt_bias)
    return pl.pallas_call(
        functools.partial(_even_mixer_kernel, tt=tt, cg=cg, ch=ch, l_real=l_real),
        grid=(B, nt),
        in_specs=[
            col(d3, 0), col(D_A, 3), col(D_A, 4), col(D_A, 5), col(D_A, 6), col(D_A, 7), col(LANES, 32),
            pl.BlockSpec((None, SUBLANES, d3), lambda b, i: (b, 0, 0)),
            state(), state(),
            pl.BlockSpec((GDN_CONV, d3), lambda b, i: (0, 0)),
            vec(LANES), vec(LANES), vec(HD_A), vec(DV_B), vec(D_A),
        ],
        out_specs=[
            pl.BlockSpec((None, tt, 2 * D_A), lambda b, i: (b, i, 0)),
            pl.BlockSpec((None, SUBLANES, d3), lambda b, i: (b, 0, 0)),
            state(), state(),
        ],
        out_shape=[
            jax.ShapeDtypeStruct((B, Lp, 2 * D_A), F32),
            jax.ShapeDtypeStruct((B, SUBLANES, d3), F32),
            jax.ShapeDtypeStruct((B, H_A, HD_A, HD_A), F32),
            jax.ShapeDtypeStruct((B, H_B, DV_B, DK_B), F32),
        ],
        scratch_shapes=[
            pltpu.VMEM((tt + SUBLANES, d3), F32),
            pltpu.VMEM((tt, d3), F32),
            pltpu.VMEM((tt, LANES), F32),
            pltpu.VMEM((tt, LANES), F32),
        ],
        compiler_params=_cparams("parallel", "arbitrary"),
        name="even_mixer",
    )(z, z, z, z, z, z, z, conv0, sg0, sh0_t, conv_w, nega_row, dtb_row,
      gnorm.reshape(1, HD_A), hnorm.reshape(1, DV_B), lb.reshape(1, D_A))


def _mem_attn_kernel(x_ref, mk_ref, mv_ref, g_ref, wq_ref, gq_ref, wo_ref, o_ref):
    x = x_ref[...]
    q = _dot(_rms(x, g_ref[...]).astype(BF16), wq_ref[...])
    scale = MEM_HD ** -0.5
    outs = []
    for h in range(MEM_H):
        sl = slice(h * MEM_HD, (h + 1) * MEM_HD)
        qh = _rms(q[:, sl], gq_ref[...])
        s = _dot_nt(qh.astype(BF16), mk_ref[:, sl].astype(BF16)) * scale
        s = s - jnp.max(s, axis=-1, keepdims=True)
        p = jnp.exp(s)
        p = p / jnp.sum(p, axis=-1, keepdims=True)
        outs.append(_dot(p.astype(BF16), mv_ref[:, sl].astype(BF16)))
    o = jnp.concatenate(outs, axis=-1)
    o_ref[...] = x + _dot(o.astype(BF16), wo_ref[...])


def _mem_attn(x, mk, mv, g, wq, gq, wo, *, tm):
    B, L, D = x.shape
    M = mk.shape[1]
    dq = MEM_H * MEM_HD
    return pl.pallas_call(
        _mem_attn_kernel,
        grid=(B, L // tm),
        in_specs=[
            pl.BlockSpec((None, tm, D), lambda b, i: (b, i, 0)),
            pl.BlockSpec((None, M, dq), lambda b, i: (b, 0, 0)),
            pl.BlockSpec((None, M, dq), lambda b, i: (b, 0, 0)),
            pl.BlockSpec((1, D), lambda b, i: (0, 0)),
            pl.BlockSpec((D, dq), lambda b, i: (0, 0)),
            pl.BlockSpec((1, MEM_HD), lambda b, i: (0, 0)),
            pl.BlockSpec((dq, D), lambda b, i: (0, 0)),
        ],
        out_specs=pl.BlockSpec((None, tm, D), lambda b, i: (b, i, 0)),
        out_shape=jax.ShapeDtypeStruct((B, L, D), F32),
        compiler_params=_cparams("parallel", "parallel"),
        name="mem_attn",
    )(x, mk, mv, g.reshape(1, D), wq, gq.reshape(1, MEM_HD), wo)


def _ffn_seq_kernel(x_ref, buf0_ref, g_ref, wgu_ref, cw_ref, cb_ref, wd_ref, o_ref, tail_ref, gp_ref, carry_ref, *, tm):
    ti = pl.program_id(1)

    @pl.when(ti == 0)
    def _():
        carry_ref[...] = buf0_ref[...]

    x = x_ref[...]
    xn = _rms(x, g_ref[...]).astype(BF16)
    acc = x
    for c in range(D_FF // FF_CHUNK):
        cs = slice(c * FF_CHUNK, (c + 1) * FF_CHUNK)
        gate = _dot(xn, wgu_ref[:, cs])
        up = _dot(xn, wgu_ref[:, D_FF + c * FF_CHUNK:D_FF + (c + 1) * FF_CHUNK])
        gp_ref[0:SUBLANES, :] = carry_ref[:, cs]
        gp_ref[SUBLANES:SUBLANES + tm, :] = gate
        conv = cw_ref[FFN_CONV - 1:FFN_CONV, cs] * gate + cb_ref[:, cs]
        for j in range(FFN_CONV - 1):
            off = SUBLANES - (FFN_CONV - 1) + j
            conv = conv + cw_ref[j:j + 1, cs] * gp_ref[off:off + tm, :]
        last = gp_ref[tm:tm + SUBLANES, :]
        carry_ref[:, cs] = last
        tail_ref[:, cs] = last
        act = _silu(conv) * up
        acc = acc + _dot(act.astype(BF16), wd_ref[cs, :])
    o_ref[...] = acc


def _ffn_seq(x, buf0, g, wgu, cw, cb, wd, *, tm):
    B, L, D = x.shape
    const = lambda shape: pl.BlockSpec(shape, lambda b, i: (0,) * len(shape), pipeline_mode=pl.Buffered(1))
    return pl.pallas_call(
        functools.partial(_ffn_seq_kernel, tm=tm),
        grid=(B, L // tm),
        in_specs=[
            pl.BlockSpec((None, tm, D), lambda b, i: (b, i, 0)),
            pl.BlockSpec((None, SUBLANES, D_FF), lambda b, i: (b, 0, 0)),
            const((1, D)), const((D, 2 * D_FF)), const((FFN_CONV, D_FF)), const((1, D_FF)), const((D_FF, D)),
        ],
        out_specs=[
            pl.BlockSpec((None, tm, D), lambda b, i: (b, i, 0)),
            pl.BlockSpec((None, SUBLANES, D_FF), lambda b, i: (b, 0, 0)),
        ],
        out_shape=[jax.ShapeDtypeStruct((B, L, D), F32), jax.ShapeDtypeStruct((B, SUBLANES, D_FF), F32)],
        scratch_shapes=[pltpu.VMEM((tm + SUBLANES, FF_CHUNK), F32), pltpu.VMEM((SUBLANES, D_FF), F32)],
        compiler_params=_cparams("parallel", "arbitrary"),
        name="conv_ffn_seq",
    )(x, buf0, g.reshape(1, D), wgu, cw, cb.reshape(1, D_FF), wd)


def _ffn_step_kernel(x_ref, p2_ref, p1_ref, g_ref, wgu_ref, cw_ref, cb_ref, wd_ref, o_ref, gate_ref):
    x = x_ref[...]
    xn = _rms(x, g_ref[...]).astype(BF16)
    acc = x
    for c in range(D_FF // FF_CHUNK):
        cs = slice(c * FF_CHUNK, (c + 1) * FF_CHUNK)
        gate = _dot(xn, wgu_ref[:, cs])
        up = _dot(xn, wgu_ref[:, D_FF + c * FF_CHUNK:D_FF + (c + 1) * FF_CHUNK])
        gate_ref[:, cs] = gate
        conv = cw_ref[0:1, cs] * p2_ref[:, cs] + cw_ref[1:2, cs] * p1_ref[:, cs] + cw_ref[2:3, cs] * gate + cb_ref[:, cs]
        act = _silu(conv) * up
        acc = acc + _dot(act.astype(BF16), wd_ref[cs, :])
    o_ref[...] = acc


def _ffn_step(x, p2, p1, g, wgu, cw, cb, wd):
    R, D = x.shape
    full = lambda a: pl.BlockSpec(a.shape, lambda i: (0,) * a.ndim)
    args = (x, p2, p1, g.reshape(1, D), wgu, cw, cb.reshape(1, D_FF), wd)
    return pl.pallas_call(
        _ffn_step_kernel,
        grid=(1,),
        in_specs=[full(a) for a in args],
        out_specs=[pl.BlockSpec((R, D), lambda i: (0, 0)), pl.BlockSpec((R, D_FF), lambda i: (0, 0))],
        out_shape=[jax.ShapeDtypeStruct((R, D), F32), jax.ShapeDtypeStruct((R, D_FF), F32)],
        compiler_params=_cparams("arbitrary"),
        name="conv_ffn_step",
    )(*args)


def _moba_query_tile(j, h, qt_ref, kb_ref, vt_ref, kmean_ref, bias_ref, ot_ref):
    blk = MOBA_BLOCK
    w = (j + 1) * blk
    qt = qt_ref[h]
    st = _dot(kb_ref[h, 0:w, :], (qt * (HD_C ** -0.5)).astype(BF16))
    if j > MOBA_TOPK:
        g = _dot(kmean_ref[h, 0:j, :], qt, HI)
        iota_n = lax.broadcasted_iota(jnp.int32, (j, blk), 0).astype(F32)
        sel = jnp.zeros((j, blk), F32)
        for _ in range(MOBA_TOPK):
            m = jnp.max(g, axis=0, keepdims=True)
            idx = jnp.min(jnp.where(g == m, iota_n, 1e9), axis=0, keepdims=True)
            hit = iota_n == idx
            sel = jnp.where(hit, 1.0, sel)
            g = jnp.where(hit, -jnp.inf, g)
    key_le_query = lax.broadcasted_iota(jnp.int32, (blk, blk), 0) <= lax.broadcasted_iota(jnp.int32, (blk, blk), 1)
    parts = []
    for n in range(j + 1):
        s = st[n * blk:(n + 1) * blk, :]
        if j - n < 2:
            s = s + bias_ref[h, j - n]
        if n == j:
            s = jnp.where(key_le_query, s, -jnp.inf)
        elif j > MOBA_TOPK:
            s = jnp.where(sel[n:n + 1, :] > 0.5, s, -jnp.inf)
        parts.append(s)
    m = jnp.max(parts[0], axis=0, keepdims=True)
    for s in parts[1:]:
        m = jnp.maximum(m, jnp.max(s, axis=0, keepdims=True))
    ps = [jnp.exp(s - m) for s in parts]
    l = jnp.sum(ps[0], axis=0, keepdims=True)
    for p in ps[1:]:
        l = l + jnp.sum(p, axis=0, keepdims=True)
    pcat = ps[0] if j == 0 else jnp.concatenate(ps, axis=0)
    ot_ref[h] = _dot(vt_ref[h, :, 0:w], pcat.astype(BF16)) / l


def _moba_seq_kernel(q_ref, k_ref, v_ref, bias_ref, o_ref, kmean_ref, kb_ref, vt_ref, qt_ref, ot_ref, *, nb):
    j = pl.program_id(1)
    blk = MOBA_BLOCK
    dc = H_C * HD_C

    @pl.when(j == 0)
    def _():
        for n in range(nb):
            km = jnp.mean(k_ref[n * blk:(n + 1) * blk, :], axis=0, keepdims=True)
            for h in range(H_C):
                kmean_ref[h, n:n + 1, :] = km[:, h * HD_C:(h + 1) * HD_C]

        def prep(n, _):
            rows = pl.ds(pl.multiple_of(n * blk, blk), blk)
            kblk = k_ref[rows, :]
            vt = v_ref[rows, :].T.astype(BF16)
            for h in range(H_C):
                sl = slice(h * HD_C, (h + 1) * HD_C)
                kb_ref[h, rows, :] = kblk[:, sl].astype(BF16)
                vt_ref[h, :, rows] = vt[sl, :]
            return 0
        lax.fori_loop(0, nb, prep, 0)

    qt_ref[...] = q_ref[...].T.reshape(H_C, HD_C, blk)
    for jj in range(nb):
        @pl.when(j == jj)
        def _(jj=jj):
            def head(h, _):
                _moba_query_tile(jj, h, qt_ref, kb_ref, vt_ref, kmean_ref, bias_ref, ot_ref)
                return 0
            lax.fori_loop(0, H_C, head, 0)
    o_ref[...] = ot_ref[...].reshape(dc, blk).T


def _moba_seq(qkv, bias_tiles):
    B, L, _ = qkv.shape
    nb = L // MOBA_BLOCK
    dc = H_C * HD_C
    return pl.pallas_call(
        functools.partial(_moba_seq_kernel, nb=nb),
        grid=(B, nb),
        in_specs=[
            pl.BlockSpec((None, MOBA_BLOCK, dc), lambda b, j: (b, j, 0)),
            pl.BlockSpec((None, L, dc), lambda b, j: (b, 0, 1)),
            pl.BlockSpec((None, L, dc), lambda b, j: (b, 0, 2)),
            pl.BlockSpec(bias_tiles.shape, lambda b, j: (0, 0, 0, 0), pipeline_mode=pl.Buffered(1)),
        ],
        out_specs=pl.BlockSpec((None, MOBA_BLOCK, dc), lambda b, j: (b, j, 0)),
        out_shape=jax.ShapeDtypeStruct((B, L, dc), F32),
        scratch_shapes=[pltpu.VMEM((H_C, nb, HD_C), F32), pltpu.VMEM((H_C, L, HD_C), BF16),
                        pltpu.VMEM((H_C, HD_C, L), BF16), pltpu.VMEM((H_C, HD_C, MOBA_BLOCK), F32),
                        pltpu.VMEM((H_C, HD_C, MOBA_BLOCK), F32)],
        compiler_params=_cparams("parallel", "arbitrary"),
        name="moba_seq",
    )(qkv, qkv, qkv, bias_tiles)


MOBA_PAGES_PER_BLOCK = MOBA_BLOCK // PAGE_SIZE
MOBA_STEP_PAGES = 8


def _moba_score_kernel(pt_ref, q_ref, *refs, n_pages):
    del pt_ref
    ppg, ppb = MOBA_STEP_PAGES, MOBA_PAGES_PER_BLOCK
    kp_refs = refs[:ppg]
    s_ref, sel_ref, gate_ref = refs[ppg:]
    step = pl.program_id(1)
    nbp = n_pages // ppb
    q = q_ref[...]
    inv = 1.0 / (MOBA_BLOCK * HD_C ** -0.5)
    gsum = None
    for r in range(ppg):
        sc = jnp.sum(kp_refs[r][...] * q, axis=1)
        s_ref[:, r * PAGE_SIZE:(r + 1) * PAGE_SIZE] = sc
        gs = jnp.sum(sc, axis=-1, keepdims=True)
        gsum = gs if r % ppb == 0 else gsum + gs
        if r % ppb == ppb - 1:
            gate_ref[step * (ppg // ppb) + r // ppb] = jnp.broadcast_to(gsum * inv, (H_C, LANES))

    @pl.when(step == n_pages // ppg - 1)
    def _():
        g = gate_ref[...]
        iota_b = lax.broadcasted_iota(jnp.int32, (nbp, H_C, LANES), 0).astype(F32)
        for i in range(MOBA_TOPK):
            m = jnp.max(g, axis=0, keepdims=True)
            idx = jnp.min(jnp.where(g == m, iota_b, 1e9), axis=0, keepdims=True)
            sel_ref[i] = idx[0]
            g = jnp.where(iota_b == idx, -jnp.inf, g)


def _moba_attend_kernel(pt_ref, sel_ref, q_ref, kn_ref, vn_ref, blast_ref, bsc_ref, *refs, nbp):
    del pt_ref
    nsel, ppb = MOBA_TOPK, MOBA_PAGES_PER_BLOCK
    s_refs, v_refs, o_ref = refs[:nsel], refs[nsel:nsel + nsel * ppb], refs[-1]
    b, h = pl.program_id(0), pl.program_id(1)
    far = bsc_ref[h, 0]
    s_new = jnp.sum(q_ref[...] * kn_ref[...], axis=-1, keepdims=True) * (HD_C ** -0.5) + bsc_ref[h, 1]
    ss = []
    for i in range(nsel):
        blk = sel_ref[(b * H_C + h) * nsel + i]
        ss.append(s_refs[i][...] + jnp.where(blk == nbp - 1, blast_ref[...], far))
    m = s_new
    for s in ss:
        m = jnp.maximum(m, jnp.max(s, axis=-1, keepdims=True))
    w_new = jnp.exp(s_new - m)
    den = w_new
    num = w_new * vn_ref[...]
    for i in range(nsel):
        e = jnp.exp(ss[i] - m)
        den = den + jnp.sum(e, axis=-1, keepdims=True)
        eb = jnp.broadcast_to(e, (SUBLANES, MOBA_BLOCK)).astype(BF16)
        for r in range(ppb):
            pv = _dot_nt(eb[:, r * PAGE_SIZE:(r + 1) * PAGE_SIZE], v_refs[i * ppb + r][...].astype(BF16))
            num = num + pv[0:1]
    o_ref[...] = num / den


def _moba_step(q, k_new, v_new, cache_k, cache_v, page_table, bias_last, bias_scalars):
    B = q.shape[0]
    n_pages = page_table.shape[1]
    ppg, ppb, nsel = MOBA_STEP_PAGES, MOBA_PAGES_PER_BLOCK, MOBA_TOPK
    assert n_pages % ppg == 0 and ppg % ppb == 0
    nbp = n_pages // ppb
    assert nbp >= nsel
    pt = page_table.reshape(-1)
    q_rep = jnp.broadcast_to((q * (HD_C ** -0.5))[..., None], (B, H_C, HD_C, LANES))
    kpage = lambda r: pl.BlockSpec((None, H_C, HD_C, PAGE_SIZE),
                                   lambda b, s, pt, r=r: (pt[b * n_pages + s * ppg + r], 0, 0, 0))
    scores, sel = pl.pallas_call(
        functools.partial(_moba_score_kernel, n_pages=n_pages),
        grid_spec=pltpu.PrefetchScalarGridSpec(
            num_scalar_prefetch=1,
            grid=(B, n_pages // ppg),
            in_specs=[pl.BlockSpec((None, H_C, HD_C, LANES), lambda b, s, pt: (b, 0, 0, 0))]
                     + [kpage(r) for r in range(ppg)],
            out_specs=[pl.BlockSpec((None, H_C, ppg * PAGE_SIZE), lambda b, s, pt: (b, 0, s)),
                       pl.BlockSpec((None, nsel, H_C, LANES), lambda b, s, pt: (b, 0, 0, 0))],
            scratch_shapes=[pltpu.VMEM((nbp, H_C, LANES), F32)],
        ),
        out_shape=[jax.ShapeDtypeStruct((B, H_C, n_pages * PAGE_SIZE), F32),
                   jax.ShapeDtypeStruct((B, nsel, H_C, LANES), F32)],
        compiler_params=_cparams("parallel", "arbitrary"),
        name="moba_step_score",
    )(pt, q_rep, *([cache_k] * ppg))
    sel_flat = jnp.swapaxes(sel[:, :, :, 0], 1, 2).astype(jnp.int32).reshape(-1)
    tok = pl.BlockSpec((None, None, 1, HD_C), lambda b, h, pt, sl: (b, h, 0, 0))
    srow = lambda i: pl.BlockSpec((None, None, None, 1, MOBA_BLOCK),
                                  lambda b, h, pt, sl, i=i: (b, h, sl[(b * H_C + h) * nsel + i], 0, 0))
    vpage = lambda i, r: pl.BlockSpec(
        (None, None, HD_C, PAGE_SIZE),
        lambda b, h, pt, sl, i=i, r=r: (pt[b * n_pages + sl[(b * H_C + h) * nsel + i] * ppb + r], h, 0, 0))
    row = lambda a: a.reshape(B, H_C, 1, HD_C)
    out = pl.pallas_call(
        functools.partial(_moba_attend_kernel, nbp=nbp),
        grid_spec=pltpu.PrefetchScalarGridSpec(
            num_scalar_prefetch=2,
            grid=(B, H_C),
            in_specs=[tok, tok, tok,
                      pl.BlockSpec((None, 1, MOBA_BLOCK), lambda b, h, pt, sl: (h, 0, 0)),
                      pl.BlockSpec(memory_space=pltpu.SMEM)]
                     + [srow(i) for i in range(nsel)]
                     + [vpage(i, r) for i in range(nsel) for r in range(ppb)],
            out_specs=tok,
        ),
        out_shape=jax.ShapeDtypeStruct((B, H_C, 1, HD_C), F32),
        compiler_params=_cparams("parallel", "arbitrary"),
        name="moba_step_attend",
    )(pt, sel_flat, row(q), row(k_new), row(v_new), bias_last, bias_scalars,
      *([scores.reshape(B, H_C, nbp, 1, MOBA_BLOCK)] * nsel), *([cache_v] * (nsel * ppb)))
    return out.reshape(B, H_C, HD_C)


def _s5_kernel(u_ref, h0r_ref, h0i_ref, wb_ref, ar_ref, ai_ref, wc_ref, d_ref, wglu_ref,
               o_ref, hr_ref, hi_ref, x_ref, *, tt, nb):
    ti = pl.program_id(0)
    ns = D_S5_STATE

    @pl.when(ti == 0)
    def _():
        hr_ref[...] = h0r_ref[...]
        hi_ref[...] = h0i_ref[...]

    u = u_ref[...]
    x_ref[...] = _dot(u.astype(BF16), wb_ref[...])
    ar = ar_ref[...]
    ai = ai_ref[...]

    def step(t, carry):
        hr, hi = carry
        rows = pl.ds(pl.multiple_of(t * nb, nb), nb)
        nr = ar * hr - ai * hi + x_ref[rows, 0:ns]
        ni = ar * hi + ai * hr + x_ref[rows, ns:2 * ns]
        x_ref[rows, 0:ns] = nr
        x_ref[rows, ns:2 * ns] = ni
        return nr, ni

    hr, hi = lax.fori_loop(0, tt, step, (hr_ref[...], hi_ref[...]))
    hr_ref[...] = hr
    hi_ref[...] = hi
    y = _dot(x_ref[...].astype(BF16), wc_ref[...]) + d_ref[...] * u
    y = jax.nn.gelu(y)
    gl = _dot(y.astype(BF16), wglu_ref[...])
    o_ref[...] = gl[:, 0:D_S5] * jax.nn.sigmoid(gl[:, D_S5:2 * D_S5])


def _s5(u_tm, h0r, h0i, wb, ar, ai, wc, d, wglu, *, tt, nb):
    R = u_tm.shape[0]
    L = R // nb
    const = lambda a: pl.BlockSpec(a.shape, lambda i: (0,) * a.ndim)
    rows = pl.BlockSpec((tt * nb, D_S5), lambda i: (i, 0))
    st = pl.BlockSpec((nb, D_S5_STATE), lambda i: (0, 0))
    args = (u_tm, h0r, h0i, wb, ar, ai, wc, d, wglu)
    return pl.pallas_call(
        functools.partial(_s5_kernel, tt=tt, nb=nb),
        grid=(L // tt,),
        in_specs=[rows] + [const(a) for a in args[1:]],
        out_specs=[rows, st, st],
        out_shape=[jax.ShapeDtypeStruct((R, D_S5), F32),
                   jax.ShapeDtypeStruct((nb, D_S5_STATE), F32), jax.ShapeDtypeStruct((nb, D_S5_STATE), F32)],
        scratch_shapes=[pltpu.VMEM((tt * nb, 2 * D_S5_STATE), F32)],
        compiler_params=_cparams("arbitrary"),
        name="s5",
    )(*args)


def _rel_bucket(dist):
    n = jnp.maximum(dist, 0)
    exact = N_BUCKETS // 2
    nf = jnp.maximum(n, exact).astype(F32)
    large = exact + (jnp.log(nf / exact) / math.log(MAX_DISTANCE / exact) * (N_BUCKETS - exact)).astype(jnp.int32)
    return jnp.where(n < exact, n, jnp.minimum(large, N_BUCKETS - 1))


def _block_diag(w):
    G, a, b = w.shape
    eye = jnp.eye(G, dtype=w.dtype)
    return (eye[:, None, :, None] * w[:, :, None, :]).reshape(G * a, G * b)


def _s5_weights(A_re, A_im, log_dt, B_re, B_im, C_re, C_im, D):
    dt = jnp.exp(log_dt.astype(F32))[:, None]
    lr, li = A_re.astype(F32), A_im.astype(F32)
    mag = jnp.exp(lr * dt)
    ar, ai = mag * jnp.cos(li * dt), mag * jnp.sin(li * dt)
    den = lr * lr + li * li
    nr = ar - 1.0
    cr, ci = (nr * lr + ai * li) / den, (ai * lr - nr * li) / den
    bbr = cr[..., None] * B_re - ci[..., None] * B_im
    bbi = cr[..., None] * B_im + ci[..., None] * B_re
    wb = jnp.concatenate([_block_diag(bbr.transpose(0, 2, 1)), _block_diag(bbi.transpose(0, 2, 1))], axis=1)
    wc = jnp.concatenate([_block_diag(C_re.transpose(0, 2, 1)), -_block_diag(C_im.transpose(0, 2, 1))], axis=0)
    return (wb.astype(BF16), ar.reshape(1, -1), ai.reshape(1, -1), wc.astype(BF16), D.reshape(1, -1).astype(F32))


def _pad_rows_front(a, rows):
    return jnp.pad(a, ((0, 0), (rows - a.shape[1], 0), (0, 0)))


def _trunk(x, st, W, *, decode):
    B, L, D = x.shape
    T = B * L
    tm = min(512, T)
    out = {}

    x2 = x.reshape(T, D)
    z = _norm_proj(x2, W['norm_mix'][0], W['ev_w_in'], tm=tm, tn=EV_IN_TILE, name="ev_in_proj")
    zc = z.shape[1]
    if decode:
        lp = SUBLANES
        z3 = jnp.pad(z.reshape(B, L, zc), ((0, 0), (0, lp - L), (0, 0)))
        tt = lp
    else:
        lp = L
        z3 = z.reshape(B, L, zc)
        tt = min(512, L)
    conv0 = _pad_rows_front(st['gdn_conv'], SUBLANES)
    mixed, tail, sg, sh_t = _even_mixer(z3, conv0, st['gdn'], jnp.swapaxes(st['hgrn'], -1, -2), W['gdn_conv_w'],
                                        W['gdn_neg_a'], W['gdn_dt_bias'], W['gdn_norm'], W['hgrn_norm'], W['hgrn_lb'],
                                        l_real=L, tt=tt)
    out['gdn_conv'] = tail[:, SUBLANES - (GDN_CONV - 1):, :]
    out['gdn'] = sg
    out['hgrn'] = jnp.swapaxes(sh_t, -1, -2)
    mixed2 = mixed[:, :L].reshape(T, 2 * D_A)
    x2 = _proj_res(x2, [(mixed2, W['ev_w_out'], pl.BlockSpec((tm, 2 * D_A), lambda i: (i, 0)))], tm=tm, name="ev_out_proj")
    x2, fs = _post_mixer(x2, st, W, 0, B, L, decode)
    out['ffn0'] = fs

    dc = H_C * HD_C
    gqk = jnp.concatenate([jnp.tile(W['moba_gq'], H_C), jnp.tile(W['moba_gk'], H_C), jnp.ones((dc,), F32)])
    qkv = _norm_proj(x2, W['norm_mix'][1], W['od_w_in'][:, :3 * dc], tm=tm, tn=dc, n_norm_tiles=2, group=HD_C,
                     gain=gqk, name="od_in_proj_qkv")
    tmu = min(128, L) if not decode else T
    u_tm = _norm_proj(x2, W['norm_mix'][1], W['od_w_in'][:, 3 * dc:], tm=tmu, tn=D_S5,
                      time_major=None if decode else (B, L), name="od_in_proj_u")
    u_tm = u_tm.reshape(L * B, D_S5)
    out['moba_k'] = qkv[:, dc:2 * dc].reshape(B, L, H_C, HD_C)
    out['moba_v'] = qkv[:, 2 * dc:].reshape(B, L, H_C, HD_C)
    if decode:
        heads = lambda a: a.reshape(B, H_C, HD_C)
        o_c = _moba_step(heads(qkv[:, :dc]), heads(qkv[:, dc:2 * dc]), heads(qkv[:, 2 * dc:]),
                         st['cache_k'], st['cache_v'], st['page_table'], *W['moba_bias_step'])
        o_c = o_c.reshape(T, dc)
        tts = 1
    else:
        o_c = _moba_seq(qkv.reshape(B, L, 3 * dc), W['moba_bias_tiles']).reshape(T, dc)
        tts = min(64, L)
    o_d, hr, hi = _s5(u_tm, st['s5_re'], st['s5_im'], *W['s5'], W['s5_glu_w'], tt=tts, nb=B)
    out['s5_re'] = hr.reshape(B, S5_GROUPS, S5_STATE)
    out['s5_im'] = hi.reshape(B, S5_GROUPS, S5_STATE)
    if decode:
        od_spec = pl.BlockSpec((tm, D_S5), lambda i: (i, 0))
        o_d2 = o_d
    else:
        nt = L // tm
        od_spec = pl.BlockSpec((tm, D_S5), lambda i: (i % nt, i // nt))
        o_d2 = o_d.reshape(L, B * D_S5)
    x2 = _proj_res(x2, [(o_c, W['od_w_out'][:dc], pl.BlockSpec((tm, dc), lambda i: (i, 0))),
                        (o_d2, W['od_w_out'][dc:], od_spec)], tm=tm, name="od_out_proj")
    x2, fs = _post_mixer(x2, st, W, 1, B, L, decode)
    out['ffn1'] = fs
    out['y'] = x2.reshape(B, L, D)
    return out


def _post_mixer(x2, st, W, l, B, L, decode):
    T, D = x2.shape
    x3 = _mem_attn(x2.reshape(B, L, D), st['mem_k'][l], st['mem_v'][l], W['norm_mem'][l], W['mem_wq'][l],
                   W['mem_gq'][l], W['mem_wo'][l], tm=min(512, L))
    buf = st['ffn_conv'][l]
    if decode:
        y, gate = _ffn_step(x3.reshape(T, D), buf[:, 0], buf[:, 1], W['norm_ffn'][l], W['ffn_w_gu'][l],
                            W['ffn_conv_w'][l], W['ffn_conv_b'][l], W['ffn_w_down'][l])
        fs = jnp.stack([buf[:, 1], gate], axis=1)
        return y, fs
    y, tail = _ffn_seq(x3, _pad_rows_front(buf, SUBLANES), W['norm_ffn'][l], W['ffn_w_gu'][l], W['ffn_conv_w'][l],
                       W['ffn_conv_b'][l], W['ffn_w_down'][l], tm=min(512, L))
    return y.reshape(T, D), tail[:, SUBLANES - (FFN_CONV - 1):, :]


def _mem_kv(mem, g_norm, w_kv, g_k):
    B, M, D = mem.shape
    dq = MEM_H * MEM_HD
    gain = jnp.concatenate([jnp.tile(g_k, MEM_H), jnp.ones((dq,), F32)])
    kv = _norm_proj(mem.reshape(B * M, D), g_norm, w_kv, tm=min(512, B * M), tn=dq, n_norm_tiles=1, group=MEM_HD,
                    gain=gain, name="mem_kv_proj")
    return kv[:, :dq].reshape(B, M, dq), kv[:, dq:].reshape(B, M, dq)


def kernel(x_prompt, x_sample, state_gdn_conv, state_gdn, state_hgrn, cache_moba_k, cache_moba_v, state_s5_re, state_s5_im, cache_mem_k, cache_mem_v, state_ffn_conv, page_table, mem_prompt, rel_bias, hgrn_lb_raw, norm_mix, norm_mem, norm_memkv, norm_ffn, mem_wq, mem_wkv, mem_gq, mem_gk, mem_wo, ffn_w_gu, ffn_conv_w, ffn_conv_b, ffn_w_down, ev_w_in, ev_w_out, gdn_conv_w, gdn_A_log, gdn_dt_bias, gdn_norm, hgrn_norm, od_w_in, od_w_out, moba_gq, moba_gk, s5_A_re, s5_A_im, s5_log_dt, s5_B_re, s5_B_im, s5_C_re, s5_C_im, s5_D, s5_glu_w):
    Bp, Lq, D = x_prompt.shape
    Bs = x_sample.shape[0]
    dc = H_C * HD_C
    dq = MEM_H * MEM_HD
    n_pages = page_table.shape[1]
    past = n_pages * PAGE_SIZE

    d3 = 3 * D_A
    w_in = ev_w_in[0]
    ba_cols = w_in[:, 4 * D_A:4 * D_A + 2 * H_A]
    w_in = jnp.concatenate([w_in[:, :4 * D_A], w_in[:, 4 * D_A + 2 * H_A:], ba_cols,
                            jnp.zeros((D, LANES - 2 * H_A), F32)], axis=1)
    lb_all = jnp.cumsum(jax.nn.softmax(hgrn_lb_raw.astype(F32), axis=0), axis=0)
    bias_t = rel_bias.T.astype(F32)
    nblk = MOBA_BLOCK
    by_rel = bias_t[:, _rel_bucket(jnp.arange(2 * nblk, dtype=jnp.int32))]
    mm = jnp.arange(2 * nblk, dtype=jnp.int32)

    def toeplitz(shift):
        y = by_rel[:, (shift + mm) % (2 * nblk)]
        flat = jnp.tile(y, (1, nblk))[:, :nblk * (2 * nblk - 1)]
        return flat.reshape(H_C, nblk, 2 * nblk - 1)[:, :, :nblk]

    far = bias_t[:, _rel_bucket(jnp.int32(2 * nblk))]
    tiles = jnp.stack([toeplitz(0), toeplitz(nblk)], axis=1) - far[:, None, None, None]
    bias_step = (by_rel[:, nblk - jnp.arange(nblk, dtype=jnp.int32)][:, None, :],
                 jnp.stack([far, bias_t[:, 0]], axis=1))
    W = dict(
        norm_mix=norm_mix, norm_mem=norm_mem, norm_ffn=norm_ffn,
        ev_w_in=w_in.astype(BF16), ev_w_out=ev_w_out[0].astype(BF16),
        gdn_conv_w=gdn_conv_w[0], gdn_neg_a=-jnp.exp(gdn_A_log[0].astype(F32)), gdn_dt_bias=gdn_dt_bias[0],
        gdn_norm=gdn_norm[0], hgrn_norm=hgrn_norm[0], hgrn_lb=lb_all[0],
        od_w_in=od_w_in[0].astype(BF16), od_w_out=od_w_out[0].astype(BF16),
        moba_gq=moba_gq[0], moba_gk=moba_gk[0], moba_bias_tiles=tiles, moba_bias_step=bias_step,
        s5=_s5_weights(s5_A_re[0], s5_A_im[0], s5_log_dt[0], s5_B_re[0], s5_B_im[0], s5_C_re[0], s5_C_im[0], s5_D[0]),
        s5_glu_w=s5_glu_w[0].astype(BF16),
        mem_wq=mem_wq.astype(BF16), mem_gq=mem_gq, mem_wo=mem_wo.astype(BF16),
        ffn_w_gu=ffn_w_gu.astype(BF16), ffn_conv_w=ffn_conv_w, ffn_conv_b=ffn_conv_b, ffn_w_down=ffn_w_down.astype(BF16),
    )

    mk_l, mv_l = [], []
    for l in range(2):
        mk, mv = _mem_kv(mem_prompt, norm_memkv[l], mem_wkv[l].astype(BF16), mem_gk[l])
        mk_l.append(mk)
        mv_l.append(mv)
    p_mem_k, p_mem_v = jnp.stack(mk_l), jnp.stack(mv_l)
    zeros = lambda *s: jnp.zeros(s, F32)
    st_p = dict(gdn_conv=zeros(Bp, GDN_CONV - 1, d3), gdn=zeros(Bp, H_A, HD_A, HD_A), hgrn=zeros(Bp, H_B, DK_B, DV_B),
                s5_re=zeros(Bp, D_S5_STATE), s5_im=zeros(Bp, D_S5_STATE), mem_k=p_mem_k, mem_v=p_mem_v,
                ffn_conv=zeros(2, Bp, FFN_CONV - 1, D_FF))
    op = _trunk(x_prompt, st_p, W, decode=False)

    st_s = dict(gdn_conv=state_gdn_conv[0], gdn=state_gdn[0], hgrn=state_hgrn[0],
                s5_re=state_s5_re[0].reshape(Bs, D_S5_STATE), s5_im=state_s5_im[0].reshape(Bs, D_S5_STATE),
                mem_k=cache_mem_k.reshape(2, Bs, -1, dq), mem_v=cache_mem_v.reshape(2, Bs, -1, dq),
                ffn_conv=state_ffn_conv,
                cache_k=jnp.transpose(cache_moba_k[0], (0, 2, 3, 1)), cache_v=jnp.transpose(cache_moba_v[0], (0, 2, 3, 1)),
                page_table=page_table)
    os_ = _trunk(x_sample, st_s, W, decode=True)

    M = mem_prompt.shape[1]
    return (op['y'], os_['y'],
            op['gdn_conv'][None], op['gdn'][None], op['hgrn'][None], op['moba_k'][None], op['moba_v'][None],
            op['s5_re'][None], op['s5_im'][None],
            p_mem_k.reshape(2, Bp, M, MEM_H, MEM_HD), p_mem_v.reshape(2, Bp, M, MEM_H, MEM_HD),
            jnp.stack([op['ffn0'], op['ffn1']]),
            os_['gdn_conv'][None], os_['gdn'][None], os_['hgrn'][None], os_['moba_k'][None], os_['moba_v'][None],
            os_['s5_re'][None], os_['s5_im'][None], jnp.stack([os_['ffn0'], os_['ffn1']]))
```

```python
import functools
import math

import jax
import jax.numpy as jnp
from jax import lax
from jax.experimental import pallas as pl
from jax.experimental.pallas import tpu as pltpu

F32 = jnp.float32
BF16 = jnp.bfloat16
HI = lax.Precision.HIGHEST
EPS = 1e-6

LANES = 128
SUBLANES = 8
VMEM_LIMIT_BYTES = 56 * 1024 * 1024

D_MODEL = 1024
H_A, HD_A, GDN_CONV = 8, 64, 4
H_B, DK_B, DV_B = 8, 64, 64
GDN_CHUNK, HGRN_CHUNK = 64, 32
GDN_GROUP, HGRN_GROUP = 2, 4
H_C, HD_C = 8, 64
MOBA_BLOCK, MOBA_TOPK = 256, 3
N_BUCKETS, MAX_DISTANCE = 32, 128
S5_GROUPS, S5_GROUP_CH, S5_STATE = 32, 16, 64
MEM_H, MEM_HD = 4, 128
D_FF, FFN_CONV = 2816, 3
PAGE_SIZE = 128
D_A = H_A * HD_A
D_S5 = S5_GROUPS * S5_GROUP_CH
D_S5_STATE = S5_GROUPS * S5_STATE
FF_CHUNK = D_FF // 2
EV_IN_COLS = 8 * D_A + LANES
EV_IN_TILE = EV_IN_COLS // 3


def _cparams(*sem):
    return pltpu.CompilerParams(dimension_semantics=sem, vmem_limit_bytes=VMEM_LIMIT_BYTES)


def _dot(a, b, precision=None):
    return jnp.dot(a, b, preferred_element_type=F32, precision=precision)


def _dot_nt(a, b, precision=None):
    return lax.dot_general(a, b, (((1,), (1,)), ((), ())), preferred_element_type=F32, precision=precision)


def _dot_tn(a, b, precision=None):
    return lax.dot_general(a, b, (((0,), (0,)), ((), ())), preferred_element_type=F32, precision=precision)


def _rms(x, g):
    return x * lax.rsqrt(jnp.mean(x * x, axis=-1, keepdims=True) + EPS) * g


def _silu(x):
    return x * jax.nn.sigmoid(x)


def _tri(n, strict=False):
    r = lax.broadcasted_iota(jnp.int32, (n, n), 0)
    c = lax.broadcasted_iota(jnp.int32, (n, n), 1)
    return (r > c) if strict else (r >= c)


def _norm_proj_kernel(x_ref, g_ref, w_ref, e_ref, gain_ref, o_ref, *, n_norm_tiles, group):
    xn = _rms(x_ref[...], g_ref[...])
    z = _dot(xn.astype(BF16), w_ref[...])
    if n_norm_tiles == 0:
        o_ref[...] = z
        return
    j = pl.program_id(1)

    @pl.when(j < n_norm_tiles)
    def _():
        zz = z * z
        hi = zz.astype(BF16)
        lo = (zz - hi.astype(F32)).astype(BF16)
        s = _dot(hi, e_ref[...]) + _dot(lo, e_ref[...])
        o_ref[...] = z * lax.rsqrt(s * (1.0 / group) + EPS) * gain_ref[...]

    @pl.when(j >= n_norm_tiles)
    def _():
        o_ref[...] = z


def _norm_proj(x, g, w, *, tm, tn, n_norm_tiles=0, group=1, gain=None, time_major=None, name):
    T, D = x.shape
    N = w.shape[1]
    nj = N // tn
    if gain is None:
        gain = jnp.ones((N,), F32)
    gi = lax.broadcasted_iota(jnp.int32, (tn, tn), 0) // group
    gj = lax.broadcasted_iota(jnp.int32, (tn, tn), 1) // group
    e = (gi == gj).astype(BF16)
    if time_major is None:
        out_shape = jax.ShapeDtypeStruct((T, N), F32)
        out_spec = pl.BlockSpec((tm, tn), lambda i, j: (i, j))
    else:
        B, L = time_major
        nt = L // tm
        out_shape = jax.ShapeDtypeStruct((L, B * N), F32)
        out_spec = pl.BlockSpec((tm, tn), lambda i, j: (i % nt, (i // nt) * nj + j))
    return pl.pallas_call(
        functools.partial(_norm_proj_kernel, n_norm_tiles=n_norm_tiles, group=group),
        grid=(T // tm, nj),
        in_specs=[
            pl.BlockSpec((tm, D), lambda i, j: (i, 0)),
            pl.BlockSpec((1, D), lambda i, j: (0, 0)),
            pl.BlockSpec((D, tn), lambda i, j: (0, j)),
            pl.BlockSpec((tn, tn), lambda i, j: (0, 0)),
            pl.BlockSpec((1, tn), lambda i, j: (0, j)),
        ],
        out_specs=out_spec,
        out_shape=out_shape,
        compiler_params=_cparams("parallel", "arbitrary"),
        name=name,
    )(x, g.reshape(1, D), w, e, gain.reshape(1, N))


def _proj_res_kernel(*refs, n_in):
    x_ref, o_ref = refs[0], refs[-1]
    acc = x_ref[...]
    for k in range(n_in):
        a_ref, w_ref = refs[1 + 2 * k], refs[2 + 2 * k]
        acc = acc + _dot(a_ref[...].astype(BF16), w_ref[...])
    o_ref[...] = acc


def _proj_res(x, parts, *, tm, name):
    T, N = x.shape
    in_specs = [pl.BlockSpec((tm, N), lambda i: (i, 0))]
    args = [x]
    for a, w, spec in parts:
        in_specs += [spec, pl.BlockSpec(w.shape, lambda i: (0, 0))]
        args += [a, w]
    return pl.pallas_call(
        functools.partial(_proj_res_kernel, n_in=len(parts)),
        grid=(T // tm,),
        in_specs=in_specs,
        out_specs=pl.BlockSpec((tm, N), lambda i: (i, 0)),
        out_shape=jax.ShapeDtypeStruct((T, N), F32),
        compiler_params=_cparams("parallel"),
        name=name,
    )(*args)


def _neumann_inverse(mats, n):
    eye = (lax.broadcasted_iota(jnp.int32, (n, n), 0) == lax.broadcasted_iota(jnp.int32, (n, n), 1)).astype(F32)
    ts = [eye - a for a in mats]
    ps = [a.astype(BF16) for a in mats]
    k = 2
    while k < n:
        ps = [_dot(p, p).astype(BF16) for p in ps]
        ts = [t + _dot(t.astype(BF16), p) for t, p in zip(ts, ps)]
        k *= 2
    return ts


def _even_mixer_kernel(qkv_ref, zg_ref, hq_ref, hf_ref, hi_ref, hg_ref, ba_ref,
                       conv0_ref, sg0_ref, sh0_ref,
                       convw_ref, nega_ref, dtb_ref, gnorm_ref, hnorm_ref, lb_ref,
                       mixed_ref, tail_ref, sg_ref, sh_ref,
                       xp_ref, qkvc_ref, beta_ref, g_ref,
                       *, tt, cg, ch, l_real):
    ti = pl.program_id(1)
    n_tiles = pl.num_programs(1)

    @pl.when(ti == 0)
    def _():
        xp_ref[0:SUBLANES, :] = conv0_ref[...]
        sg_ref[...] = sg0_ref[...]
        sh_ref[...] = sh0_ref[...]

    xp_ref[SUBLANES:SUBLANES + tt, :] = qkv_ref[...]
    conv = convw_ref[GDN_CONV - 1:GDN_CONV, :] * qkv_ref[...]
    for j in range(GDN_CONV - 1):
        off = SUBLANES - (GDN_CONV - 1) + j
        conv = conv + convw_ref[j:j + 1, :] * xp_ref[off:off + tt, :]
    qkvc_ref[...] = _silu(conv)
    lr_last = l_real - (l_real - 1) // tt * tt
    tail_ref[...] = xp_ref[lr_last:lr_last + SUBLANES, :]
    carry = xp_ref[tt:tt + SUBLANES, :]
    xp_ref[0:SUBLANES, :] = carry

    padded = (l_real % tt) != 0
    if padded:
        row = lax.broadcasted_iota(jnp.int32, (tt, 1), 0) + ti * tt
        valid = row < l_real
    ba = ba_ref[...]
    beta_all = jax.nn.sigmoid(ba)
    g_all = nega_ref[...] * jax.nn.softplus(ba + dtb_ref[...])
    if padded:
        beta_all = jnp.where(valid, beta_all, 0.0)
        g_all = jnp.where(valid, g_all, 0.0)
    beta_ref[...] = beta_all
    g_ref[...] = g_all

    ltri_g = _tri(cg).astype(F32)
    tril_g = _tri(cg)
    stril_g = _tri(cg, strict=True)
    gnorm = gnorm_ref[...]
    hnorm = hnorm_ref[...]

    ng = min(GDN_GROUP, tt // cg)

    def gdn_group(gi, _):
        heads = range(H_A)
        items = [(c, h) for c in range(ng) for h in heads]
        rows_c, gcum_c, gt_c, bet_c = [], [], [], []
        for c in range(ng):
            rows = pl.ds(pl.multiple_of((gi * ng + c) * cg, cg), cg)
            gcum = _dot(ltri_g, g_ref[rows, :], HI)
            gpad = gcum if cg == LANES else jnp.concatenate([gcum, jnp.zeros((LANES - cg, LANES), F32)], axis=0)
            rows_c.append(rows)
            gcum_c.append(gcum)
            gt_c.append(gpad.T)
            bet_c.append(beta_ref[rows, :])
        qs, ks, vs, betas, gcs, decays, glasts = {}, {}, {}, {}, {}, {}, {}
        for it in items:
            c, h = it
            rows = rows_c[c]
            q = qkvc_ref[rows, h * HD_A:(h + 1) * HD_A]
            k = qkvc_ref[rows, D_A + h * HD_A:D_A + (h + 1) * HD_A]
            qs[it] = q * lax.rsqrt(jnp.sum(q * q, axis=-1, keepdims=True) + EPS) * (HD_A ** -0.5)
            ks[it] = k * lax.rsqrt(jnp.sum(k * k, axis=-1, keepdims=True) + EPS)
            vs[it] = qkvc_ref[rows, 2 * D_A + h * HD_A:2 * D_A + (h + 1) * HD_A]
            betas[it] = bet_c[c][:, h:h + 1]
            gc = gcum_c[c][:, H_A + h:H_A + h + 1]
            gr = gt_c[c][H_A + h:H_A + h + 1, 0:cg]
            gcs[it] = gc
            glasts[it] = gcum_c[c][cg - 1:cg, H_A + h:H_A + h + 1]
            decays[it] = jnp.exp(jnp.where(tril_g, gc - gr, -jnp.inf))
        kqs = {it: _dot_nt(jnp.concatenate([ks[it], qs[it]], axis=0).astype(BF16), ks[it].astype(BF16))
               for it in items}
        ts = dict(zip(items, _neumann_inverse(
            [jnp.where(stril_g, betas[it] * kqs[it][0:cg] * decays[it], 0.0) for it in items], cg)))
        egs = {it: jnp.exp(gcs[it]) for it in items}
        sols = {it: _dot(ts[it].astype(BF16),
                         jnp.concatenate([betas[it] * vs[it], (betas[it] * egs[it]) * ks[it]], axis=-1).astype(BF16))
                for it in items}
        wqs = {it: jnp.concatenate([sols[it][:, HD_A:2 * HD_A], qs[it] * egs[it]], axis=0).astype(BF16)
               for it in items}
        qks = {it: (kqs[it][cg:2 * cg] * decays[it]).astype(BF16) for it in items}
        kouts = {it: (ks[it] * jnp.exp(glasts[it] - gcs[it])).astype(BF16) for it in items}
        ss = [sg_ref[h] for h in heads]
        for c in range(ng):
            wss = [_dot(wqs[(c, h)], ss[h].astype(BF16)) for h in heads]
            ubs = [(sols[(c, h)][:, 0:HD_A] - wss[h][0:cg]).astype(BF16) for h in heads]
            os_ = [wss[h][cg:2 * cg] + _dot(qks[(c, h)], ubs[h]) for h in heads]
            ss = [ss[h] * jnp.exp(glasts[(c, h)]) + _dot_tn(kouts[(c, h)], ubs[h]) for h in heads]
            outs = [_rms(os_[h], gnorm) * _silu(zg_ref[rows_c[c], h * HD_A:(h + 1) * HD_A]) for h in heads]
            for p in range(H_A // 2):
                mixed_ref[rows_c[c], p * LANES:(p + 1) * LANES] = jnp.concatenate(
                    [outs[2 * p], outs[2 * p + 1]], axis=-1)
        for h in heads:
            sg_ref[h] = ss[h]
        return 0

    lax.fori_loop(0, tt // (cg * ng), gdn_group, 0)

    nh = min(HGRN_GROUP, tt // ch)
    gr = nh * ch
    ri = lax.broadcasted_iota(jnp.int32, (gr, gr), 0)
    cj = lax.broadcasted_iota(jnp.int32, (gr, gr), 1)
    same_chunk_causal = (ri >= cj) & ((ri // ch) == (cj // ch))
    ltri_h = same_chunk_causal.astype(F32)
    lb = lb_ref[...]
    r_mid = (ch - 1) // 2

    def hgrn_group(gi, _):
        r0 = pl.multiple_of(gi * gr, gr)
        rows = pl.ds(r0, gr)
        f = lb + (1.0 - lb) * jax.nn.sigmoid(hf_ref[rows, :])
        logf = jnp.log(f)
        kk = 1.0 - f
        if padded:
            vrow = (lax.broadcasted_iota(jnp.int32, (gr, 1), 0) + r0 + ti * tt) < l_real
            logf = jnp.where(vrow, logf, 0.0)
            kk = jnp.where(vrow, kk, 0.0)
        qh = _silu(hq_ref[rows, :])
        bc = _dot(ltri_h, logf, HI).reshape(nh, ch, D_A)
        bref = bc[:, r_mid:r_mid + 1, :]
        blast = bc[:, ch - 1:ch, :]
        grp = lambda a: a.reshape(gr, D_A)
        qh3, kk3 = qh.reshape(nh, ch, D_A), kk.reshape(nh, ch, D_A)
        qa = grp(qh3 * jnp.exp(bc - bref)).astype(BF16)
        kb = grp(kk3 * jnp.exp(bref - bc)).astype(BF16)
        q_in = grp(qh3 * jnp.exp(bc)).astype(BF16)
        k_out = grp(kk3 * jnp.exp(blast - bc)).astype(BF16)
        fl = jnp.exp(blast)
        vv = hi_ref[rows, :].astype(BF16)
        heads = range(H_B)
        sls = [slice(h * DK_B, (h + 1) * DK_B) for h in heads]
        atts = [jnp.where(same_chunk_causal, _dot_nt(qa[:, sl], kb[:, sl]), 0.0).astype(BF16) for sl in sls]
        intra = [_dot(atts[h], vv[:, sls[h]]) for h in heads]
        sts = [sh_ref[h] for h in heads]
        inter = [[] for _ in heads]
        for c in range(nh):
            rc = slice(c * ch, (c + 1) * ch)
            for h in heads:
                inter[h].append(_dot_nt(q_in[rc, sls[h]], sts[h].astype(BF16)))
            sts = [sts[h] * fl[c][:, sls[h]] + _dot_tn(vv[rc, sls[h]], k_out[rc, sls[h]]) for h in heads]
        for h in heads:
            sh_ref[h] = sts[h]
        os_ = [intra[h] + (inter[h][0] if nh == 1 else jnp.concatenate(inter[h], axis=0)) for h in heads]
        outs = [_rms(os_[h], hnorm) * _silu(hg_ref[rows, h * DV_B:(h + 1) * DV_B]) for h in heads]
        for p in range(H_B // 2):
            c0 = D_A + p * LANES
            mixed_ref[rows, c0:c0 + LANES] = jnp.concatenate([outs[2 * p], outs[2 * p + 1]], axis=-1)
        return 0

    lax.fori_loop(0, tt // gr, hgrn_group, 0)


def _even_mixer(z, conv0, sg0, sh0_t, conv_w, neg_a, dt_bias, gnorm, hnorm, lb, *, l_real, tt):
    B, Lp, _ = z.shape
    cg = min(GDN_CHUNK, tt)
    ch = min(HGRN_CHUNK, tt)
    d3 = 3 * D_A
    nt = Lp // tt
    col = lambda w, idx: pl.BlockSpec((None, tt, w), lambda b, i, idx=idx: (b, i, idx))
    state = lambda: pl.BlockSpec((None, H_A, HD_A, HD_A), lambda b, i: (b, 0, 0, 0))
    vec = lambda n: pl.BlockSpec((1, n), lambda b, i: (0, 0))
    nega_row = jnp.zeros((1, LANES), F32).at[0, H_A:2 * H_A].set(neg_a)
    dtb_row = jnp.zeros((1, LANES), F32).at[0, H_A:2 * H_A].set(dt_bias)
    return pl.pallas_call(
        functools.partial(_even_mixer_kernel, tt=tt, cg=cg, ch=ch, l_real=l_real),
        grid=(B, nt),
        in_specs=[
            col(d3, 0), col(D_A, 3), col(D_A, 4), col(D_A, 5), col(D_A, 6), col(D_A, 7), col(LANES, 32),
            pl.BlockSpec((None, SUBLANES, d3), lambda b, i: (b, 0, 0)),
            state(), state(),
            pl.BlockSpec((GDN_CONV, d3), lambda b, i: (0, 0)),
            vec(LANES), vec(LANES), vec(HD_A), vec(DV_B), vec(D_A),
        ],
        out_specs=[
            pl.BlockSpec((None, tt, 2 * D_A), lambda b, i: (b, i, 0)),
            pl.BlockSpec((None, SUBLANES, d3), lambda b, i: (b, 0, 0)),
            state(), state(),
        ],
        out_shape=[
            jax.ShapeDtypeStruct((B, Lp, 2 * D_A), F32),
            jax.ShapeDtypeStruct((B, SUBLANES, d3), F32),
            jax.ShapeDtypeStruct((B, H_A, HD_A, HD_A), F32),
            jax.ShapeDtypeStruct((B, H_B, DV_B, DK_B), F32),
        ],
        scratch_shapes=[
            pltpu.VMEM((tt + SUBLANES, d3), F32),
            pltpu.VMEM((tt, d3), F32),
            pltpu.VMEM((tt, LANES), F32),
            pltpu.VMEM((tt, LANES), F32),
        ],
        compiler_params=_cparams("parallel", "arbitrary"),
        name="even_mixer",
    )(z, z, z, z, z, z, z, conv0, sg0, sh0_t, conv_w, nega_row, dtb_row,
      gnorm.reshape(1, HD_A), hnorm.reshape(1, DV_B), lb.reshape(1, D_A))


def _mem_attn_kernel(x_ref, mk_ref, mv_ref, g_ref, wq_ref, gq_ref, wo_ref, o_ref):
    x = x_ref[...]
    q = _dot(_rms(x, g_ref[...]).astype(BF16), wq_ref[...])
    scale = MEM_HD ** -0.5
    outs = []
    for h in range(MEM_H):
        sl = slice(h * MEM_HD, (h + 1) * MEM_HD)
        qh = _rms(q[:, sl], gq_ref[...])
        s = _dot_nt(qh.astype(BF16), mk_ref[:, sl].astype(BF16)) * scale
        s = s - jnp.max(s, axis=-1, keepdims=True)
        p = jnp.exp(s)
        p = p / jnp.sum(p, axis=-1, keepdims=True)
        outs.append(_dot(p.astype(BF16), mv_ref[:, sl].astype(BF16)))
    o = jnp.concatenate(outs, axis=-1)
    o_ref[...] = x + _dot(o.astype(BF16), wo_ref[...])


def _mem_attn(x, mk, mv, g, wq, gq, wo, *, tm):
    B, L, D = x.shape
    M = mk.shape[1]
    dq = MEM_H * MEM_HD
    return pl.pallas_call(
        _mem_attn_kernel,
        grid=(B, L // tm),
        in_specs=[
            pl.BlockSpec((None, tm, D), lambda b, i: (b, i, 0)),
            pl.BlockSpec((None, M, dq), lambda b, i: (b, 0, 0)),
            pl.BlockSpec((None, M, dq), lambda b, i: (b, 0, 0)),
            pl.BlockSpec((1, D), lambda b, i: (0, 0)),
            pl.BlockSpec((D, dq), lambda b, i: (0, 0)),
            pl.BlockSpec((1, MEM_HD), lambda b, i: (0, 0)),
            pl.BlockSpec((dq, D), lambda b, i: (0, 0)),
        ],
        out_specs=pl.BlockSpec((None, tm, D), lambda b, i: (b, i, 0)),
        out_shape=jax.ShapeDtypeStruct((B, L, D), F32),
        compiler_params=_cparams("parallel", "parallel"),
        name="mem_attn",
    )(x, mk, mv, g.reshape(1, D), wq, gq.reshape(1, MEM_HD), wo)


def _mem_attn_step_kernel(q_ref, mk_ref, mv_ref, o_ref):
    scale = MEM_HD ** -0.5
    outs = []
    for h in range(MEM_H):
        qh = jnp.broadcast_to(q_ref[:, h * MEM_HD:(h + 1) * MEM_HD], (SUBLANES, MEM_HD))
        s = _dot_nt(qh.astype(BF16), mk_ref[:, h, :].astype(BF16)) * scale
        s = s - jnp.max(s, axis=-1, keepdims=True)
        p = jnp.exp(s)
        p = p / jnp.sum(p, axis=-1, keepdims=True)
        outs.append(_dot(p.astype(BF16), mv_ref[:, h, :].astype(BF16))[0:1])
    o_ref[...] = jnp.concatenate(outs, axis=-1)


def _mem_attn_step(q, cache_k, cache_v, l):
    B = q.shape[0]
    M = cache_k.shape[2]
    dq = MEM_H * MEM_HD
    mem = pl.BlockSpec((None, None, M, MEM_H, MEM_HD), lambda b: (l, b, 0, 0, 0))
    row = pl.BlockSpec((None, 1, dq), lambda b: (b, 0, 0))
    return pl.pallas_call(
        _mem_attn_step_kernel,
        grid=(B,),
        in_specs=[row, mem, mem],
        out_specs=row,
        out_shape=jax.ShapeDtypeStruct((B, 1, dq), F32),
        compiler_params=_cparams("parallel"),
        name="mem_attn_step",
    )(q, cache_k, cache_v)


def _ffn_seq_kernel(x_ref, buf0_ref, g_ref, wgu_ref, cw_ref, cb_ref, wd_ref, o_ref, tail_ref, gp_ref, carry_ref, *, tm):
    ti = pl.program_id(1)

    @pl.when(ti == 0)
    def _():
        carry_ref[...] = buf0_ref[...]

    x = x_ref[...]
    xn = _rms(x, g_ref[...]).astype(BF16)
    acc = x
    for c in range(D_FF // FF_CHUNK):
        cs = slice(c * FF_CHUNK, (c + 1) * FF_CHUNK)
        gate = _dot(xn, wgu_ref[:, cs])
        up = _dot(xn, wgu_ref[:, D_FF + c * FF_CHUNK:D_FF + (c + 1) * FF_CHUNK])
        gp_ref[0:SUBLANES, :] = carry_ref[:, cs]
        gp_ref[SUBLANES:SUBLANES + tm, :] = gate
        conv = cw_ref[FFN_CONV - 1:FFN_CONV, cs] * gate + cb_ref[:, cs]
        for j in range(FFN_CONV - 1):
            off = SUBLANES - (FFN_CONV - 1) + j
            conv = conv + cw_ref[j:j + 1, cs] * gp_ref[off:off + tm, :]
        last = gp_ref[tm:tm + SUBLANES, :]
        carry_ref[:, cs] = last
        tail_ref[:, cs] = last
        act = _silu(conv) * up
        acc = acc + _dot(act.astype(BF16), wd_ref[cs, :])
    o_ref[...] = acc


def _ffn_seq(x, buf0, g, wgu, cw, cb, wd, *, tm):
    B, L, D = x.shape
    const = lambda shape: pl.BlockSpec(shape, lambda b, i: (0,) * len(shape), pipeline_mode=pl.Buffered(1))
    return pl.pallas_call(
        functools.partial(_ffn_seq_kernel, tm=tm),
        grid=(B, L // tm),
        in_specs=[
            pl.BlockSpec((None, tm, D), lambda b, i: (b, i, 0)),
            pl.BlockSpec((None, SUBLANES, D_FF), lambda b, i: (b, 0, 0)),
            const((1, D)), const((D, 2 * D_FF)), const((FFN_CONV, D_FF)), const((1, D_FF)), const((D_FF, D)),
        ],
        out_specs=[
            pl.BlockSpec((None, tm, D), lambda b, i: (b, i, 0)),
            pl.BlockSpec((None, SUBLANES, D_FF), lambda b, i: (b, 0, 0)),
        ],
        out_shape=[jax.ShapeDtypeStruct((B, L, D), F32), jax.ShapeDtypeStruct((B, SUBLANES, D_FF), F32)],
        scratch_shapes=[pltpu.VMEM((tm + SUBLANES, FF_CHUNK), F32), pltpu.VMEM((SUBLANES, D_FF), F32)],
        compiler_params=_cparams("parallel", "arbitrary"),
        name="conv_ffn_seq",
    )(x, buf0, g.reshape(1, D), wgu, cw, cb.reshape(1, D_FF), wd)


def _ffn_step_kernel(x_ref, p2_ref, p1_ref, g_ref, wgu_ref, cw_ref, cb_ref, wd_ref, o_ref, gate_ref):
    x = x_ref[...]
    xn = _rms(x, g_ref[...]).astype(BF16)
    acc = x
    for c in range(D_FF // FF_CHUNK):
        cs = slice(c * FF_CHUNK, (c + 1) * FF_CHUNK)
        gate = _dot(xn, wgu_ref[:, cs])
        up = _dot(xn, wgu_ref[:, D_FF + c * FF_CHUNK:D_FF + (c + 1) * FF_CHUNK])
        gate_ref[:, cs] = gate
        conv = cw_ref[0:1, cs] * p2_ref[:, cs] + cw_ref[1:2, cs] * p1_ref[:, cs] + cw_ref[2:3, cs] * gate + cb_ref[:, cs]
        act = _silu(conv) * up
        acc = acc + _dot(act.astype(BF16), wd_ref[cs, :])
    o_ref[...] = acc


def _ffn_step(x, p2, p1, g, wgu, cw, cb, wd):
    R, D = x.shape
    full = lambda a: pl.BlockSpec(a.shape, lambda i: (0,) * a.ndim)
    args = (x, p2, p1, g.reshape(1, D), wgu, cw, cb.reshape(1, D_FF), wd)
    return pl.pallas_call(
        _ffn_step_kernel,
        grid=(1,),
        in_specs=[full(a) for a in args],
        out_specs=[pl.BlockSpec((R, D), lambda i: (0, 0)), pl.BlockSpec((R, D_FF), lambda i: (0, 0))],
        out_shape=[jax.ShapeDtypeStruct((R, D), F32), jax.ShapeDtypeStruct((R, D_FF), F32)],
        compiler_params=_cparams("arbitrary"),
        name="conv_ffn_step",
    )(*args)


def _moba_query_tile(j, h, qt_ref, kb_ref, vt_ref, kmean_ref, bias_ref, ot_ref):
    blk = MOBA_BLOCK
    w = (j + 1) * blk
    qt = qt_ref[h]
    st = _dot(kb_ref[h, 0:w, :], (qt * (HD_C ** -0.5)).astype(BF16))
    if j > MOBA_TOPK:
        g = _dot(kmean_ref[h, 0:j, :], qt, HI)
        iota_n = lax.broadcasted_iota(jnp.int32, (j, blk), 0).astype(F32)
        sel = jnp.zeros((j, blk), F32)
        for _ in range(MOBA_TOPK):
            m = jnp.max(g, axis=0, keepdims=True)
            idx = jnp.min(jnp.where(g == m, iota_n, 1e9), axis=0, keepdims=True)
            hit = iota_n == idx
            sel = jnp.where(hit, 1.0, sel)
            g = jnp.where(hit, -jnp.inf, g)
    key_le_query = lax.broadcasted_iota(jnp.int32, (blk, blk), 0) <= lax.broadcasted_iota(jnp.int32, (blk, blk), 1)
    parts = []
    for n in range(j + 1):
        s = st[n * blk:(n + 1) * blk, :]
        if j - n < 2:
            s = s + bias_ref[h, j - n]
        if n == j:
            s = jnp.where(key_le_query, s, -jnp.inf)
        elif j > MOBA_TOPK:
            s = jnp.where(sel[n:n + 1, :] > 0.5, s, -jnp.inf)
        parts.append(s)
    m = jnp.max(parts[0], axis=0, keepdims=True)
    for s in parts[1:]:
        m = jnp.maximum(m, jnp.max(s, axis=0, keepdims=True))
    ps = [jnp.exp(s - m) for s in parts]
    l = jnp.sum(ps[0], axis=0, keepdims=True)
    for p in ps[1:]:
        l = l + jnp.sum(p, axis=0, keepdims=True)
    pcat = ps[0] if j == 0 else jnp.concatenate(ps, axis=0)
    ot_ref[h] = _dot(vt_ref[h, :, 0:w], pcat.astype(BF16)) / l


def _moba_seq_kernel(q_ref, k_ref, v_ref, bias_ref, o_ref, kmean_ref, kb_ref, vt_ref, qt_ref, ot_ref, *, nb):
    j = pl.program_id(1)
    blk = MOBA_BLOCK
    dc = H_C * HD_C

    @pl.when(j == 0)
    def _():
        for n in range(nb):
            km = jnp.mean(k_ref[n * blk:(n + 1) * blk, :], axis=0, keepdims=True)
            for h in range(H_C):
                kmean_ref[h, n:n + 1, :] = km[:, h * HD_C:(h + 1) * HD_C]

        def prep(n, _):
            rows = pl.ds(pl.multiple_of(n * blk, blk), blk)
            kblk = k_ref[rows, :]
            vt = v_ref[rows, :].T.astype(BF16)
            for h in range(H_C):
                sl = slice(h * HD_C, (h + 1) * HD_C)
                kb_ref[h, rows, :] = kblk[:, sl].astype(BF16)
                vt_ref[h, :, rows] = vt[sl, :]
            return 0
        lax.fori_loop(0, nb, prep, 0)

    qt_ref[...] = q_ref[...].T.reshape(H_C, HD_C, blk)
    for jj in range(nb):
        @pl.when(j == jj)
        def _(jj=jj):
            def head(h, _):
                _moba_query_tile(jj, h, qt_ref, kb_ref, vt_ref, kmean_ref, bias_ref, ot_ref)
                return 0
            lax.fori_loop(0, H_C, head, 0)
    o_ref[...] = ot_ref[...].reshape(dc, blk).T


def _moba_seq(qkv, bias_tiles):
    B, L, _ = qkv.shape
    nb = L // MOBA_BLOCK
    dc = H_C * HD_C
    return pl.pallas_call(
        functools.partial(_moba_seq_kernel, nb=nb),
        grid=(B, nb),
        in_specs=[
            pl.BlockSpec((None, MOBA_BLOCK, dc), lambda b, j: (b, j, 0)),
            pl.BlockSpec((None, L, dc), lambda b, j: (b, 0, 1)),
            pl.BlockSpec((None, L, dc), lambda b, j: (b, 0, 2)),
            pl.BlockSpec(bias_tiles.shape, lambda b, j: (0, 0, 0, 0), pipeline_mode=pl.Buffered(1)),
        ],
        out_specs=pl.BlockSpec((None, MOBA_BLOCK, dc), lambda b, j: (b, j, 0)),
        out_shape=jax.ShapeDtypeStruct((B, L, dc), F32),
        scratch_shapes=[pltpu.VMEM((H_C, nb, HD_C), F32), pltpu.VMEM((H_C, L, HD_C), BF16),
                        pltpu.VMEM((H_C, HD_C, L), BF16), pltpu.VMEM((H_C, HD_C, MOBA_BLOCK), F32),
                        pltpu.VMEM((H_C, HD_C, MOBA_BLOCK), F32)],
        compiler_params=_cparams("parallel", "arbitrary"),
        name="moba_seq",
    )(qkv, qkv, qkv, bias_tiles)


MOBA_PAGES_PER_BLOCK = MOBA_BLOCK // PAGE_SIZE
MOBA_STEP_PAGES = 8


def _moba_score_kernel(pt_ref, q_ref, *refs, n_pages):
    del pt_ref
    ppg, ppb = MOBA_STEP_PAGES, MOBA_PAGES_PER_BLOCK
    kp_refs = refs[:ppg]
    s_ref, sel_ref, gate_ref = refs[ppg:]
    step = pl.program_id(1)
    nbp = n_pages // ppb
    q = q_ref[...]
    inv = 1.0 / (MOBA_BLOCK * HD_C ** -0.5)
    gsum = None
    for r in range(ppg):
        sc = jnp.sum(kp_refs[r][...] * q, axis=1)
        s_ref[:, r * PAGE_SIZE:(r + 1) * PAGE_SIZE] = sc
        gs = jnp.sum(sc, axis=-1, keepdims=True)
        gsum = gs if r % ppb == 0 else gsum + gs
        if r % ppb == ppb - 1:
            gate_ref[step * (ppg // ppb) + r // ppb] = jnp.broadcast_to(gsum * inv, (H_C, LANES))

    @pl.when(step == n_pages // ppg - 1)
    def _():
        g = gate_ref[...]
        iota_b = lax.broadcasted_iota(jnp.int32, (nbp, H_C, LANES), 0).astype(F32)
        for i in range(MOBA_TOPK):
            m = jnp.max(g, axis=0, keepdims=True)
            idx = jnp.min(jnp.where(g == m, iota_b, 1e9), axis=0, keepdims=True)
            sel_ref[i] = idx[0]
            g = jnp.where(iota_b == idx, -jnp.inf, g)


def _moba_attend_kernel(pt_ref, sel_ref, q_ref, kn_ref, vn_ref, blast_ref, bsc_ref, *refs, nbp):
    del pt_ref
    nsel, ppb = MOBA_TOPK, MOBA_PAGES_PER_BLOCK
    s_refs, v_refs, o_ref = refs[:nsel], refs[nsel:nsel + nsel * ppb], refs[-1]
    b, h = pl.program_id(0), pl.program_id(1)
    far = bsc_ref[h, 0]
    s_new = jnp.sum(q_ref[...] * kn_ref[...], axis=-1, keepdims=True) * (HD_C ** -0.5) + bsc_ref[h, 1]
    ss = []
    for i in range(nsel):
        blk = sel_ref[(b * H_C + h) * nsel + i]
        ss.append(s_refs[i][...] + jnp.where(blk == nbp - 1, blast_ref[...], far))
    m = s_new
    for s in ss:
        m = jnp.maximum(m, jnp.max(s, axis=-1, keepdims=True))
    w_new = jnp.exp(s_new - m)
    den = w_new
    num = w_new * vn_ref[...]
    for i in range(nsel):
        e = jnp.exp(ss[i] - m)
        den = den + jnp.sum(e, axis=-1, keepdims=True)
        eb = jnp.broadcast_to(e, (SUBLANES, MOBA_BLOCK)).astype(BF16)
        for r in range(ppb):
            pv = _dot_nt(eb[:, r * PAGE_SIZE:(r + 1) * PAGE_SIZE], v_refs[i * ppb + r][...].astype(BF16))
            num = num + pv[0:1]
    o_ref[...] = num / den


def _moba_step(q, k_new, v_new, cache_k, cache_v, page_table, bias_last, bias_scalars):
    B = q.shape[0]
    n_pages = page_table.shape[1]
    ppg, ppb, nsel = MOBA_STEP_PAGES, MOBA_PAGES_PER_BLOCK, MOBA_TOPK
    assert n_pages % ppg == 0 and ppg % ppb == 0
    nbp = n_pages // ppb
    assert nbp >= nsel
    pt = page_table.reshape(-1)
    q_rep = jnp.broadcast_to((q * (HD_C ** -0.5))[..., None], (B, H_C, HD_C, LANES))
    kpage = lambda r: pl.BlockSpec((None, H_C, HD_C, PAGE_SIZE),
                                   lambda b, s, pt, r=r: (pt[b * n_pages + s * ppg + r], 0, 0, 0))
    scores, sel = pl.pallas_call(
        functools.partial(_moba_score_kernel, n_pages=n_pages),
        grid_spec=pltpu.PrefetchScalarGridSpec(
            num_scalar_prefetch=1,
            grid=(B, n_pages // ppg),
            in_specs=[pl.BlockSpec((None, H_C, HD_C, LANES), lambda b, s, pt: (b, 0, 0, 0))]
                     + [kpage(r) for r in range(ppg)],
            out_specs=[pl.BlockSpec((None, H_C, ppg * PAGE_SIZE), lambda b, s, pt: (b, 0, s)),
                       pl.BlockSpec((None, nsel, H_C, LANES), lambda b, s, pt: (b, 0, 0, 0))],
            scratch_shapes=[pltpu.VMEM((nbp, H_C, LANES), F32)],
        ),
        out_shape=[jax.ShapeDtypeStruct((B, H_C, n_pages * PAGE_SIZE), F32),
                   jax.ShapeDtypeStruct((B, nsel, H_C, LANES), F32)],
        compiler_params=_cparams("parallel", "arbitrary"),
        name="moba_step_score",
    )(pt, q_rep, *([cache_k] * ppg))
    sel_flat = jnp.swapaxes(sel[:, :, :, 0], 1, 2).astype(jnp.int32).reshape(-1)
    tok = pl.BlockSpec((None, None, 1, HD_C), lambda b, h, pt, sl: (b, h, 0, 0))
    srow = lambda i: pl.BlockSpec((None, None, None, 1, MOBA_BLOCK),
                                  lambda b, h, pt, sl, i=i: (b, h, sl[(b * H_C + h) * nsel + i], 0, 0))
    vpage = lambda i, r: pl.BlockSpec(
        (None, None, HD_C, PAGE_SIZE),
        lambda b, h, pt, sl, i=i, r=r: (pt[b * n_pages + sl[(b * H_C + h) * nsel + i] * ppb + r], h, 0, 0))
    row = lambda a: a.reshape(B, H_C, 1, HD_C)
    out = pl.pallas_call(
        functools.partial(_moba_attend_kernel, nbp=nbp),
        grid_spec=pltpu.PrefetchScalarGridSpec(
            num_scalar_prefetch=2,
            grid=(B, H_C),
            in_specs=[tok, tok, tok,
                      pl.BlockSpec((None, 1, MOBA_BLOCK), lambda b, h, pt, sl: (h, 0, 0)),
                      pl.BlockSpec(memory_space=pltpu.SMEM)]
                     + [srow(i) for i in range(nsel)]
                     + [vpage(i, r) for i in range(nsel) for r in range(ppb)],
            out_specs=tok,
        ),
        out_shape=jax.ShapeDtypeStruct((B, H_C, 1, HD_C), F32),
        compiler_params=_cparams("parallel", "arbitrary"),
        name="moba_step_attend",
    )(pt, sel_flat, row(q), row(k_new), row(v_new), bias_last, bias_scalars,
      *([scores.reshape(B, H_C, nbp, 1, MOBA_BLOCK)] * nsel), *([cache_v] * (nsel * ppb)))
    return out.reshape(B, H_C, HD_C)


def _s5_kernel(u_ref, h0r_ref, h0i_ref, wb_ref, ar_ref, ai_ref, wc_ref, d_ref, wglu_ref,
               o_ref, hr_ref, hi_ref, x_ref, *, tt, nb):
    ti = pl.program_id(0)
    ns = D_S5_STATE

    @pl.when(ti == 0)
    def _():
        hr_ref[...] = h0r_ref[...]
        hi_ref[...] = h0i_ref[...]

    nslab = ns // LANES
    if tt == 1:
        seqs, seq_rows, mat_rows, cols = [0], lambda b: slice(None), lambda b: slice(None), lambda b: slice(None)
        u = u_ref[...]
    else:
        seqs = range(nb)
        seq_rows = lambda b: pl.ds(b, tt, stride=nb)
        mat_rows = lambda b: slice(b * tt, (b + 1) * tt)
        cols = lambda b: slice(b * D_S5, (b + 1) * D_S5)
        u = jnp.concatenate([u_ref[:, cols(b)] for b in seqs], axis=0)
    x = _dot(u.astype(BF16), wb_ref[...])
    for b in seqs:
        for c in range(2 * nslab):
            x_ref[c, seq_rows(b), :] = x[mat_rows(b), c * LANES:(c + 1) * LANES]
    slabs = lambda ref: jnp.stack([ref[:, c * LANES:(c + 1) * LANES] for c in range(nslab)])
    ar = slabs(ar_ref)
    ai = slabs(ai_ref)

    def step(t, carry):
        hr, hi = carry
        rows = pl.ds(pl.multiple_of(t * nb, nb), nb)
        nr = ar * hr - ai * hi + x_ref[0:nslab, rows, :]
        ni = ar * hi + ai * hr + x_ref[nslab:2 * nslab, rows, :]
        x_ref[0:nslab, rows, :] = nr
        x_ref[nslab:2 * nslab, rows, :] = ni
        return nr, ni

    hr, hi = lax.fori_loop(0, tt, step, (slabs(hr_ref), slabs(hi_ref)))
    for c in range(nslab):
        hr_ref[:, c * LANES:(c + 1) * LANES] = hr[c]
        hi_ref[:, c * LANES:(c + 1) * LANES] = hi[c]

    hs = [jnp.concatenate([x_ref[c, seq_rows(b), :] for c in range(2 * nslab)], axis=-1) for b in seqs]
    h = hs[0] if len(hs) == 1 else jnp.concatenate(hs, axis=0)
    y = jax.nn.gelu(_dot(h.astype(BF16), wc_ref[...]) + d_ref[...] * u)
    gl = _dot(y.astype(BF16), wglu_ref[...])
    o = gl[:, 0:D_S5] * jax.nn.sigmoid(gl[:, D_S5:2 * D_S5])
    for b in seqs:
        o_ref[:, cols(b)] = o[mat_rows(b), :]


def _s5(u, h0r, h0i, wb, ar, ai, wc, d, wglu, *, tt, nb):
    L = u.shape[0]
    const = lambda a: pl.BlockSpec(a.shape, lambda i: (0,) * a.ndim)
    rows = pl.BlockSpec((tt, nb * D_S5), lambda i: (i, 0))
    st = pl.BlockSpec((nb, D_S5_STATE), lambda i: (0, 0))
    args = (u, h0r, h0i, wb, ar, ai, wc, d, wglu)
    if tt == 1:
        assert L == 1
        rows = pl.BlockSpec((nb, D_S5), lambda i: (0, 0))
        args = (u.reshape(nb, D_S5),) + args[1:]
    return pl.pallas_call(
        functools.partial(_s5_kernel, tt=tt, nb=nb),
        grid=(L // tt,),
        in_specs=[rows] + [const(a) for a in args[1:]],
        out_specs=[rows, st, st],
        out_shape=[jax.ShapeDtypeStruct(args[0].shape, F32),
                   jax.ShapeDtypeStruct((nb, D_S5_STATE), F32), jax.ShapeDtypeStruct((nb, D_S5_STATE), F32)],
        scratch_shapes=[pltpu.VMEM((2 * D_S5_STATE // LANES, tt * nb, LANES), F32)],
        compiler_params=_cparams("arbitrary"),
        name="s5",
    )(*args)


def _rel_bucket(dist):
    n = jnp.maximum(dist, 0)
    exact = N_BUCKETS // 2
    nf = jnp.maximum(n, exact).astype(F32)
    large = exact + (jnp.log(nf / exact) / math.log(MAX_DISTANCE / exact) * (N_BUCKETS - exact)).astype(jnp.int32)
    return jnp.where(n < exact, n, jnp.minimum(large, N_BUCKETS - 1))


def _block_diag(w):
    G, a, b = w.shape
    eye = jnp.eye(G, dtype=w.dtype)
    return (eye[:, None, :, None] * w[:, :, None, :]).reshape(G * a, G * b)


def _s5_weights(A_re, A_im, log_dt, B_re, B_im, C_re, C_im, D):
    dt = jnp.exp(log_dt.astype(F32))[:, None]
    lr, li = A_re.astype(F32), A_im.astype(F32)
    mag = jnp.exp(lr * dt)
    ar, ai = mag * jnp.cos(li * dt), mag * jnp.sin(li * dt)
    den = lr * lr + li * li
    nr = ar - 1.0
    cr, ci = (nr * lr + ai * li) / den, (ai * lr - nr * li) / den
    bbr = cr[..., None] * B_re - ci[..., None] * B_im
    bbi = cr[..., None] * B_im + ci[..., None] * B_re
    wb = jnp.concatenate([_block_diag(bbr.transpose(0, 2, 1)), _block_diag(bbi.transpose(0, 2, 1))], axis=1)
    wc = jnp.concatenate([_block_diag(C_re.transpose(0, 2, 1)), -_block_diag(C_im.transpose(0, 2, 1))], axis=0)
    return (wb.astype(BF16), ar.reshape(1, -1), ai.reshape(1, -1), wc.astype(BF16), D.reshape(1, -1).astype(F32))


def _pad_rows_front(a, rows):
    return jnp.pad(a, ((0, 0), (rows - a.shape[1], 0), (0, 0)))


def _trunk(x, st, W, *, decode):
    B, L, D = x.shape
    T = B * L
    tm = min(512, T)
    out = {}

    x2 = x.reshape(T, D)
    z = _norm_proj(x2, W['norm_mix'][0], W['ev_w_in'], tm=tm, tn=EV_IN_TILE, name="ev_in_proj")
    zc = z.shape[1]
    if decode:
        lp = SUBLANES
        z3 = jnp.pad(z.reshape(B, L, zc), ((0, 0), (0, lp - L), (0, 0)))
        tt = lp
    else:
        lp = L
        z3 = z.reshape(B, L, zc)
        tt = min(512, L)
    conv0 = _pad_rows_front(st['gdn_conv'], SUBLANES)
    mixed, tail, sg, sh_t = _even_mixer(z3, conv0, st['gdn'], jnp.swapaxes(st['hgrn'], -1, -2), W['gdn_conv_w'],
                                        W['gdn_neg_a'], W['gdn_dt_bias'], W['gdn_norm'], W['hgrn_norm'], W['hgrn_lb'],
                                        l_real=L, tt=tt)
    out['gdn_conv'] = tail[:, SUBLANES - (GDN_CONV - 1):, :]
    out['gdn'] = sg
    out['hgrn'] = jnp.swapaxes(sh_t, -1, -2)
    mixed2 = mixed[:, :L].reshape(T, 2 * D_A)
    x2 = _proj_res(x2, [(mixed2, W['ev_w_out'], pl.BlockSpec((tm, 2 * D_A), lambda i: (i, 0)))], tm=tm, name="ev_out_proj")
    x2, fs = _post_mixer(x2, st, W, 0, B, L, decode)
    out['ffn0'] = fs

    dc = H_C * HD_C
    gqk = jnp.concatenate([jnp.tile(W['moba_gq'], H_C), jnp.tile(W['moba_gk'], H_C), jnp.ones((dc,), F32)])
    qkv = _norm_proj(x2, W['norm_mix'][1], W['od_w_in'][:, :3 * dc], tm=tm, tn=dc, n_norm_tiles=2, group=HD_C,
                     gain=gqk, name="od_in_proj_qkv")
    tmu = min(128, L) if not decode else T
    u_tm = _norm_proj(x2, W['norm_mix'][1], W['od_w_in'][:, 3 * dc:], tm=tmu, tn=D_S5,
                      time_major=None if decode else (B, L), name="od_in_proj_u")
    u_tm = u_tm.reshape(L, B * D_S5)
    out['moba_k'] = qkv[:, dc:2 * dc].reshape(B, L, H_C, HD_C)
    out['moba_v'] = qkv[:, 2 * dc:].reshape(B, L, H_C, HD_C)
    if decode:
        heads = lambda a: a.reshape(B, H_C, HD_C)
        o_c = _moba_step(heads(qkv[:, :dc]), heads(qkv[:, dc:2 * dc]), heads(qkv[:, 2 * dc:]),
                         st['cache_k'], st['cache_v'], st['page_table'], *W['moba_bias_step'])
        o_c = o_c.reshape(T, dc)
        tts = 1
    else:
        o_c = _moba_seq(qkv.reshape(B, L, 3 * dc), W['moba_bias_tiles']).reshape(T, dc)
        tts = min(64, L)
    o_d, hr, hi = _s5(u_tm, st['s5_re'], st['s5_im'], *W['s5'], W['s5_glu_w'], tt=tts, nb=B)
    out['s5_re'] = hr.reshape(B, S5_GROUPS, S5_STATE)
    out['s5_im'] = hi.reshape(B, S5_GROUPS, S5_STATE)
    if decode:
        od_spec = pl.BlockSpec((tm, D_S5), lambda i: (i, 0))
        o_d2 = o_d
    else:
        nt = L // tm
        od_spec = pl.BlockSpec((tm, D_S5), lambda i: (i % nt, i // nt))
        o_d2 = o_d
    x2 = _proj_res(x2, [(o_c, W['od_w_out'][:dc], pl.BlockSpec((tm, dc), lambda i: (i, 0))),
                        (o_d2, W['od_w_out'][dc:], od_spec)], tm=tm, name="od_out_proj")
    x2, fs = _post_mixer(x2, st, W, 1, B, L, decode)
    out['ffn1'] = fs
    out['y'] = x2.reshape(B, L, D)
    return out


def _post_mixer(x2, st, W, l, B, L, decode):
    T, D = x2.shape
    dq = MEM_H * MEM_HD
    buf = st['ffn_conv'][l]
    if decode:
        q = _norm_proj(x2, W['norm_mem'][l], W['mem_wq'][l], tm=T, tn=dq, n_norm_tiles=1, group=MEM_HD,
                       gain=jnp.tile(W['mem_gq'][l], MEM_H), name="mem_q_proj")
        o = _mem_attn_step(q.reshape(B, 1, dq), st['mem_k'], st['mem_v'], l).reshape(T, dq)
        x3 = _proj_res(x2, [(o, W['mem_wo'][l], pl.BlockSpec((T, dq), lambda i: (i, 0)))], tm=T, name="mem_out_proj")
        y, gate = _ffn_step(x3, buf[:, 0], buf[:, 1], W['norm_ffn'][l], W['ffn_w_gu'][l],
                            W['ffn_conv_w'][l], W['ffn_conv_b'][l], W['ffn_w_down'][l])
        fs = jnp.stack([buf[:, 1], gate], axis=1)
        return y, fs
    x3 = _mem_attn(x2.reshape(B, L, D), st['mem_k'][l], st['mem_v'][l], W['norm_mem'][l], W['mem_wq'][l],
                   W['mem_gq'][l], W['mem_wo'][l], tm=min(512, L))
    y, tail = _ffn_seq(x3, _pad_rows_front(buf, SUBLANES), W['norm_ffn'][l], W['ffn_w_gu'][l], W['ffn_conv_w'][l],
                       W['ffn_conv_b'][l], W['ffn_w_down'][l], tm=min(512, L))
    return y.reshape(T, D), tail[:, SUBLANES - (FFN_CONV - 1):, :]


def _mem_kv(mem, g_norm, w_kv, g_k):
    B, M, D = mem.shape
    dq = MEM_H * MEM_HD
    gain = jnp.concatenate([jnp.tile(g_k, MEM_H), jnp.ones((dq,), F32)])
    kv = _norm_proj(mem.reshape(B * M, D), g_norm, w_kv, tm=min(512, B * M), tn=dq, n_norm_tiles=1, group=MEM_HD,
                    gain=gain, name="mem_kv_proj")
    return kv[:, :dq].reshape(B, M, dq), kv[:, dq:].reshape(B, M, dq)


def kernel(x_prompt, x_sample, state_gdn_conv, state_gdn, state_hgrn, cache_moba_k, cache_moba_v, state_s5_re, state_s5_im, cache_mem_k, cache_mem_v, state_ffn_conv, page_table, mem_prompt, rel_bias, hgrn_lb_raw, norm_mix, norm_mem, norm_memkv, norm_ffn, mem_wq, mem_wkv, mem_gq, mem_gk, mem_wo, ffn_w_gu, ffn_conv_w, ffn_conv_b, ffn_w_down, ev_w_in, ev_w_out, gdn_conv_w, gdn_A_log, gdn_dt_bias, gdn_norm, hgrn_norm, od_w_in, od_w_out, moba_gq, moba_gk, s5_A_re, s5_A_im, s5_log_dt, s5_B_re, s5_B_im, s5_C_re, s5_C_im, s5_D, s5_glu_w):
    Bp, Lq, D = x_prompt.shape
    Bs = x_sample.shape[0]
    dc = H_C * HD_C
    dq = MEM_H * MEM_HD
    n_pages = page_table.shape[1]
    past = n_pages * PAGE_SIZE

    d3 = 3 * D_A
    w_in = ev_w_in[0]
    ba_cols = w_in[:, 4 * D_A:4 * D_A + 2 * H_A]
    w_in = jnp.concatenate([w_in[:, :4 * D_A], w_in[:, 4 * D_A + 2 * H_A:], ba_cols,
                            jnp.zeros((D, LANES - 2 * H_A), F32)], axis=1)
    lb_all = jnp.cumsum(jax.nn.softmax(hgrn_lb_raw.astype(F32), axis=0), axis=0)
    bias_t = rel_bias.T.astype(F32)
    nblk = MOBA_BLOCK
    by_rel = bias_t[:, _rel_bucket(jnp.arange(2 * nblk, dtype=jnp.int32))]
    mm = jnp.arange(2 * nblk, dtype=jnp.int32)

    def toeplitz(shift):
        y = by_rel[:, (shift + mm) % (2 * nblk)]
        flat = jnp.tile(y, (1, nblk))[:, :nblk * (2 * nblk - 1)]
        return flat.reshape(H_C, nblk, 2 * nblk - 1)[:, :, :nblk]

    far = bias_t[:, _rel_bucket(jnp.int32(2 * nblk))]
    tiles = jnp.stack([toeplitz(0), toeplitz(nblk)], axis=1) - far[:, None, None, None]
    bias_step = (by_rel[:, nblk - jnp.arange(nblk, dtype=jnp.int32)][:, None, :],
                 jnp.stack([far, bias_t[:, 0]], axis=1))
    W = dict(
        norm_mix=norm_mix, norm_mem=norm_mem, norm_ffn=norm_ffn,
        ev_w_in=w_in.astype(BF16), ev_w_out=ev_w_out[0].astype(BF16),
        gdn_conv_w=gdn_conv_w[0], gdn_neg_a=-jnp.exp(gdn_A_log[0].astype(F32)), gdn_dt_bias=gdn_dt_bias[0],
        gdn_norm=gdn_norm[0], hgrn_norm=hgrn_norm[0], hgrn_lb=lb_all[0],
        od_w_in=od_w_in[0].astype(BF16), od_w_out=od_w_out[0].astype(BF16),
        moba_gq=moba_gq[0], moba_gk=moba_gk[0], moba_bias_tiles=tiles, moba_bias_step=bias_step,
        s5=_s5_weights(s5_A_re[0], s5_A_im[0], s5_log_dt[0], s5_B_re[0], s5_B_im[0], s5_C_re[0], s5_C_im[0], s5_D[0]),
        s5_glu_w=s5_glu_w[0].astype(BF16),
        mem_wq=mem_wq.astype(BF16), mem_gq=mem_gq, mem_wo=mem_wo.astype(BF16),
        ffn_w_gu=ffn_w_gu.astype(BF16), ffn_conv_w=ffn_conv_w, ffn_conv_b=ffn_conv_b, ffn_w_down=ffn_w_down.astype(BF16),
    )

    mk_l, mv_l = [], []
    for l in range(2):
        mk, mv = _mem_kv(mem_prompt, norm_memkv[l], mem_wkv[l].astype(BF16), mem_gk[l])
        mk_l.append(mk)
        mv_l.append(mv)
    p_mem_k, p_mem_v = jnp.stack(mk_l), jnp.stack(mv_l)
    zeros = lambda *s: jnp.zeros(s, F32)
    st_p = dict(gdn_conv=zeros(Bp, GDN_CONV - 1, d3), gdn=zeros(Bp, H_A, HD_A, HD_A), hgrn=zeros(Bp, H_B, DK_B, DV_B),
                s5_re=zeros(Bp, D_S5_STATE), s5_im=zeros(Bp, D_S5_STATE), mem_k=p_mem_k, mem_v=p_mem_v,
                ffn_conv=zeros(2, Bp, FFN_CONV - 1, D_FF))
    op = _trunk(x_prompt, st_p, W, decode=False)

    st_s = dict(gdn_conv=state_gdn_conv[0], gdn=state_gdn[0], hgrn=state_hgrn[0],
                s5_re=state_s5_re[0].reshape(Bs, D_S5_STATE), s5_im=state_s5_im[0].reshape(Bs, D_S5_STATE),
                mem_k=cache_mem_k, mem_v=cache_mem_v,
                ffn_conv=state_ffn_conv,
                cache_k=jnp.transpose(cache_moba_k[0], (0, 2, 3, 1)), cache_v=jnp.transpose(cache_moba_v[0], (0, 2, 3, 1)),
                page_table=page_table)
    os_ = _trunk(x_sample, st_s, W, decode=True)

    M = mem_prompt.shape[1]
    return (op['y'], os_['y'],
            op['gdn_conv'][None], op['gdn'][None], op['hgrn'][None], op['moba_k'][None], op['moba_v'][None],
            op['s5_re'][None], op['s5_im'][None],
            p_mem_k.reshape(2, Bp, M, MEM_H, MEM_HD), p_mem_v.reshape(2, Bp, M, MEM_H, MEM_HD),
            jnp.stack([op['ffn0'], op['ffn1']]),
            os_['gdn_conv'][None], os_['gdn'][None], os_['hgrn'][None], os_['moba_k'][None], os_['moba_v'][None],
            os_['s5_re'][None], os_['s5_im'][None], jnp.stack([os_['ffn0'], os_['ffn1']]))
```

```python
import functools
import math

import jax
import jax.numpy as jnp
from jax import lax
from jax.experimental import pallas as pl
from jax.experimental.pallas import tpu as pltpu

F32 = jnp.float32
BF16 = jnp.bfloat16
HI = lax.Precision.HIGHEST
EPS = 1e-6

LANES = 128
SUBLANES = 8
VMEM_LIMIT_BYTES = 56 * 1024 * 1024

D_MODEL = 1024
H_A, HD_A, GDN_CONV = 8, 64, 4
H_B, DK_B, DV_B = 8, 64, 64
GDN_CHUNK, HGRN_CHUNK = 64, 32
GDN_GROUP, HGRN_GROUP = 2, 4
H_C, HD_C = 8, 64
MOBA_BLOCK, MOBA_TOPK = 256, 3
N_BUCKETS, MAX_DISTANCE = 32, 128
S5_GROUPS, S5_GROUP_CH, S5_STATE = 32, 16, 64
MEM_H, MEM_HD = 4, 128
D_FF, FFN_CONV = 2816, 3
PAGE_SIZE = 128
D_A = H_A * HD_A
D_S5 = S5_GROUPS * S5_GROUP_CH
D_S5_STATE = S5_GROUPS * S5_STATE
FF_CHUNK = D_FF // 2
EV_IN_COLS = 8 * D_A + LANES
EV_IN_TILE = EV_IN_COLS // 3


def _cparams(*sem):
    return pltpu.CompilerParams(dimension_semantics=sem, vmem_limit_bytes=VMEM_LIMIT_BYTES)


def _dot(a, b, precision=None):
    return jnp.dot(a, b, preferred_element_type=F32, precision=precision)


def _dot_nt(a, b, precision=None):
    return lax.dot_general(a, b, (((1,), (1,)), ((), ())), preferred_element_type=F32, precision=precision)


def _dot_tn(a, b, precision=None):
    return lax.dot_general(a, b, (((0,), (0,)), ((), ())), preferred_element_type=F32, precision=precision)


def _rms(x, g):
    return x * lax.rsqrt(jnp.mean(x * x, axis=-1, keepdims=True) + EPS) * g


def _silu(x):
    return x * jax.nn.sigmoid(x)


def _tri(n, strict=False):
    r = lax.broadcasted_iota(jnp.int32, (n, n), 0)
    c = lax.broadcasted_iota(jnp.int32, (n, n), 1)
    return (r > c) if strict else (r >= c)


def _norm_proj_kernel(x_ref, g_ref, w_ref, e_ref, gain_ref, o_ref, *, n_norm_tiles, group):
    xn = _rms(x_ref[...], g_ref[...])
    z = _dot(xn.astype(BF16), w_ref[...])
    if n_norm_tiles == 0:
        o_ref[...] = z
        return
    j = pl.program_id(1)

    @pl.when(j < n_norm_tiles)
    def _():
        zz = z * z
        hi = zz.astype(BF16)
        lo = (zz - hi.astype(F32)).astype(BF16)
        s = _dot(hi, e_ref[...]) + _dot(lo, e_ref[...])
        o_ref[...] = z * lax.rsqrt(s * (1.0 / group) + EPS) * gain_ref[...]

    @pl.when(j >= n_norm_tiles)
    def _():
        o_ref[...] = z


def _norm_proj(x, g, w, *, tm, tn, n_norm_tiles=0, group=1, gain=None, time_major=None, name):
    T, D = x.shape
    N = w.shape[1]
    nj = N // tn
    if gain is None:
        gain = jnp.ones((N,), F32)
    gi = lax.broadcasted_iota(jnp.int32, (tn, tn), 0) // group
    gj = lax.broadcasted_iota(jnp.int32, (tn, tn), 1) // group
    e = (gi == gj).astype(BF16)
    if time_major is None:
        out_shape = jax.ShapeDtypeStruct((T, N), F32)
        out_spec = pl.BlockSpec((tm, tn), lambda i, j: (i, j))
    else:
        B, L = time_major
        nt = L // tm
        out_shape = jax.ShapeDtypeStruct((L, B * N), F32)
        out_spec = pl.BlockSpec((tm, tn), lambda i, j: (i % nt, (i // nt) * nj + j))
    return pl.pallas_call(
        functools.partial(_norm_proj_kernel, n_norm_tiles=n_norm_tiles, group=group),
        grid=(T // tm, nj),
        in_specs=[
            pl.BlockSpec((tm, D), lambda i, j: (i, 0)),
            pl.BlockSpec((1, D), lambda i, j: (0, 0)),
            pl.BlockSpec((D, tn), lambda i, j: (0, j)),
            pl.BlockSpec((tn, tn), lambda i, j: (0, 0)),
            pl.BlockSpec((1, tn), lambda i, j: (0, j)),
        ],
        out_specs=out_spec,
        out_shape=out_shape,
        compiler_params=_cparams("parallel", "arbitrary"),
        name=name,
    )(x, g.reshape(1, D), w, e, gain.reshape(1, N))


def _od_qkv_kernel(x_ref, g_ref, w_ref, e_ref, gain_ref, o_ref, *t_refs):
    dc = H_C * HD_C
    xn = _rms(x_ref[...], g_ref[...]).astype(BF16)
    for c in range(3):
        cs = slice(c * dc, (c + 1) * dc)
        z = _dot(xn, w_ref[:, cs])
        if c < 2:
            zz = z * z
            hi = zz.astype(BF16)
            lo = (zz - hi.astype(F32)).astype(BF16)
            s = _dot(hi, e_ref[...]) + _dot(lo, e_ref[...])
            z = z * lax.rsqrt(s * (1.0 / HD_C) + EPS) * gain_ref[:, cs]
        o_ref[:, cs] = z
        if t_refs and c > 0:
            t_refs[c - 1][...] = z.T


def _od_qkv_proj(x, g, w, gain, *, tm, seq=None):
    T, D = x.shape
    dc = H_C * HD_C
    gi = lax.broadcasted_iota(jnp.int32, (dc, dc), 0) // HD_C
    gj = lax.broadcasted_iota(jnp.int32, (dc, dc), 1) // HD_C
    e = (gi == gj).astype(BF16)
    const = lambda shape: pl.BlockSpec(shape, lambda i: (0, 0), pipeline_mode=pl.Buffered(1))
    out_specs = [pl.BlockSpec((tm, 3 * dc), lambda i: (i, 0))]
    out_shape = [jax.ShapeDtypeStruct((T, 3 * dc), F32)]
    if seq is not None:
        B, L = seq
        nt = L // tm
        tspec = pl.BlockSpec((None, dc, tm), lambda i: (i // nt, 0, i % nt))
        out_specs += [tspec, tspec]
        out_shape += [jax.ShapeDtypeStruct((B, dc, L), F32)] * 2
    return pl.pallas_call(
        _od_qkv_kernel,
        grid=(T // tm,),
        in_specs=[pl.BlockSpec((tm, D), lambda i: (i, 0)), const((1, D)), const((D, 3 * dc)), const((dc, dc)),
                  const((1, 2 * dc))],
        out_specs=out_specs,
        out_shape=out_shape,
        compiler_params=_cparams("parallel"),
        name="od_in_proj_qkv",
    )(x, g.reshape(1, D), w, e, gain.reshape(1, 2 * dc))


def _proj_res_kernel(*refs, n_in):
    x_ref, o_ref = refs[0], refs[-1]
    acc = x_ref[...]
    for k in range(n_in):
        a_ref, w_ref = refs[1 + 2 * k], refs[2 + 2 * k]
        acc = acc + _dot(a_ref[...].astype(BF16), w_ref[...])
    o_ref[...] = acc


def _proj_res(x, parts, *, tm, name):
    T, N = x.shape
    in_specs = [pl.BlockSpec((tm, N), lambda i: (i, 0))]
    args = [x]
    for a, w, spec in parts:
        in_specs += [spec, pl.BlockSpec(w.shape, lambda i: (0, 0))]
        args += [a, w]
    return pl.pallas_call(
        functools.partial(_proj_res_kernel, n_in=len(parts)),
        grid=(T // tm,),
        in_specs=in_specs,
        out_specs=pl.BlockSpec((tm, N), lambda i: (i, 0)),
        out_shape=jax.ShapeDtypeStruct((T, N), F32),
        compiler_params=_cparams("parallel"),
        name=name,
    )(*args)


def _neumann_inverse(mats, n):
    eye = (lax.broadcasted_iota(jnp.int32, (n, n), 0) == lax.broadcasted_iota(jnp.int32, (n, n), 1)).astype(F32)
    ts = [eye - a for a in mats]
    ps = [a.astype(BF16) for a in mats]
    k = 2
    while k < n:
        ps = [_dot(p, p).astype(BF16) for p in ps]
        ts = [t + _dot(t.astype(BF16), p) for t, p in zip(ts, ps)]
        k *= 2
    return ts


def _even_mixer_kernel(qkv_ref, zg_ref, hq_ref, hf_ref, hi_ref, hg_ref, ba_ref,
                       conv0_ref, sg0_ref, sh0_ref,
                       convw_ref, nega_ref, dtb_ref, gnorm_ref, hnorm_ref, lb_ref,
                       mixed_ref, tail_ref, sg_ref, sh_ref,
                       xp_ref, qkvc_ref, beta_ref, g_ref,
                       *, tt, cg, ch, l_real):
    ti = pl.program_id(1)
    n_tiles = pl.num_programs(1)

    @pl.when(ti == 0)
    def _():
        xp_ref[0:SUBLANES, :] = conv0_ref[...]
        sg_ref[...] = sg0_ref[...]
        sh_ref[...] = sh0_ref[...]

    xp_ref[SUBLANES:SUBLANES + tt, :] = qkv_ref[...]
    conv = convw_ref[GDN_CONV - 1:GDN_CONV, :] * qkv_ref[...]
    for j in range(GDN_CONV - 1):
        off = SUBLANES - (GDN_CONV - 1) + j
        conv = conv + convw_ref[j:j + 1, :] * xp_ref[off:off + tt, :]
    qkvc_ref[...] = _silu(conv)
    lr_last = l_real - (l_real - 1) // tt * tt
    tail_ref[...] = xp_ref[lr_last:lr_last + SUBLANES, :]
    carry = xp_ref[tt:tt + SUBLANES, :]
    xp_ref[0:SUBLANES, :] = carry

    padded = (l_real % tt) != 0
    if padded:
        row = lax.broadcasted_iota(jnp.int32, (tt, 1), 0) + ti * tt
        valid = row < l_real
    ba = ba_ref[...]
    beta_all = jax.nn.sigmoid(ba)
    g_all = nega_ref[...] * jax.nn.softplus(ba + dtb_ref[...])
    if padded:
        beta_all = jnp.where(valid, beta_all, 0.0)
        g_all = jnp.where(valid, g_all, 0.0)
    beta_ref[...] = beta_all
    g_ref[...] = g_all

    ltri_g = _tri(cg).astype(F32)
    tril_g = _tri(cg)
    stril_g = _tri(cg, strict=True)
    gnorm = gnorm_ref[...]
    hnorm = hnorm_ref[...]

    ng = min(GDN_GROUP, tt // cg)

    def gdn_group(gi, _):
        heads = range(H_A)
        items = [(c, h) for c in range(ng) for h in heads]
        rows_c, gcum_c, gt_c, bet_c = [], [], [], []
        for c in range(ng):
            rows = pl.ds(pl.multiple_of((gi * ng + c) * cg, cg), cg)
            gcum = _dot(ltri_g, g_ref[rows, :], HI)
            gpad = gcum if cg == LANES else jnp.concatenate([gcum, jnp.zeros((LANES - cg, LANES), F32)], axis=0)
            rows_c.append(rows)
            gcum_c.append(gcum)
            gt_c.append(gpad.T)
            bet_c.append(beta_ref[rows, :])
        qs, ks, vs, betas, gcs, decays, glasts = {}, {}, {}, {}, {}, {}, {}
        for it in items:
            c, h = it
            rows = rows_c[c]
            q = qkvc_ref[rows, h * HD_A:(h + 1) * HD_A]
            k = qkvc_ref[rows, D_A + h * HD_A:D_A + (h + 1) * HD_A]
            qs[it] = q * lax.rsqrt(jnp.sum(q * q, axis=-1, keepdims=True) + EPS) * (HD_A ** -0.5)
            ks[it] = k * lax.rsqrt(jnp.sum(k * k, axis=-1, keepdims=True) + EPS)
            vs[it] = qkvc_ref[rows, 2 * D_A + h * HD_A:2 * D_A + (h + 1) * HD_A]
            betas[it] = bet_c[c][:, h:h + 1]
            gc = gcum_c[c][:, H_A + h:H_A + h + 1]
            gr = gt_c[c][H_A + h:H_A + h + 1, 0:cg]
            gcs[it] = gc
            glasts[it] = gcum_c[c][cg - 1:cg, H_A + h:H_A + h + 1]
            decays[it] = jnp.exp(jnp.where(tril_g, gc - gr, -jnp.inf))
        kqs = {it: _dot_nt(jnp.concatenate([ks[it], qs[it]], axis=0).astype(BF16), ks[it].astype(BF16))
               for it in items}
        ts = dict(zip(items, _neumann_inverse(
            [jnp.where(stril_g, betas[it] * kqs[it][0:cg] * decays[it], 0.0) for it in items], cg)))
        egs = {it: jnp.exp(gcs[it]) for it in items}
        sols = {it: _dot(ts[it].astype(BF16),
                         jnp.concatenate([betas[it] * vs[it], (betas[it] * egs[it]) * ks[it]], axis=-1).astype(BF16))
                for it in items}
        wqs = {it: jnp.concatenate([sols[it][:, HD_A:2 * HD_A], qs[it] * egs[it]], axis=0).astype(BF16)
               for it in items}
        qks = {it: (kqs[it][cg:2 * cg] * decays[it]).astype(BF16) for it in items}
        kouts = {it: (ks[it] * jnp.exp(glasts[it] - gcs[it])).astype(BF16) for it in items}
        ss = [sg_ref[h] for h in heads]
        for c in range(ng):
            wss = [_dot(wqs[(c, h)], ss[h].astype(BF16)) for h in heads]
            ubs = [(sols[(c, h)][:, 0:HD_A] - wss[h][0:cg]).astype(BF16) for h in heads]
            os_ = [wss[h][cg:2 * cg] + _dot(qks[(c, h)], ubs[h]) for h in heads]
            ss = [ss[h] * jnp.exp(glasts[(c, h)]) + _dot_tn(kouts[(c, h)], ubs[h]) for h in heads]
            outs = [_rms(os_[h], gnorm) * _silu(zg_ref[rows_c[c], h * HD_A:(h + 1) * HD_A]) for h in heads]
            for p in range(H_A // 2):
                mixed_ref[rows_c[c], p * LANES:(p + 1) * LANES] = jnp.concatenate(
                    [outs[2 * p], outs[2 * p + 1]], axis=-1)
        for h in heads:
            sg_ref[h] = ss[h]
        return 0

    lax.fori_loop(0, tt // (cg * ng), gdn_group, 0)

    nh = min(HGRN_GROUP, tt // ch)
    gr = nh * ch
    ri = lax.broadcasted_iota(jnp.int32, (gr, gr), 0)
    cj = lax.broadcasted_iota(jnp.int32, (gr, gr), 1)
    same_chunk_causal = (ri >= cj) & ((ri // ch) == (cj // ch))
    ltri_h = same_chunk_causal.astype(F32)
    lb = lb_ref[...]
    r_mid = (ch - 1) // 2

    def hgrn_group(gi, _):
        r0 = pl.multiple_of(gi * gr, gr)
        rows = pl.ds(r0, gr)
        f = lb + (1.0 - lb) * jax.nn.sigmoid(hf_ref[rows, :])
        logf = jnp.log(f)
        kk = 1.0 - f
        if padded:
            vrow = (lax.broadcasted_iota(jnp.int32, (gr, 1), 0) + r0 + ti * tt) < l_real
            logf = jnp.where(vrow, logf, 0.0)
            kk = jnp.where(vrow, kk, 0.0)
        qh = _silu(hq_ref[rows, :])
        bc = _dot(ltri_h, logf, HI).reshape(nh, ch, D_A)
        bref = bc[:, r_mid:r_mid + 1, :]
        blast = bc[:, ch - 1:ch, :]
        grp = lambda a: a.reshape(gr, D_A)
        qh3, kk3 = qh.reshape(nh, ch, D_A), kk.reshape(nh, ch, D_A)
        qa = grp(qh3 * jnp.exp(bc - bref)).astype(BF16)
        kb = grp(kk3 * jnp.exp(bref - bc)).astype(BF16)
        q_in = grp(qh3 * jnp.exp(bc)).astype(BF16)
        k_out = grp(kk3 * jnp.exp(blast - bc)).astype(BF16)
        fl = jnp.exp(blast)
        vv = hi_ref[rows, :].astype(BF16)
        heads = range(H_B)
        sls = [slice(h * DK_B, (h + 1) * DK_B) for h in heads]
        atts = [jnp.where(same_chunk_causal, _dot_nt(qa[:, sl], kb[:, sl]), 0.0).astype(BF16) for sl in sls]
        intra = [_dot(atts[h], vv[:, sls[h]]) for h in heads]
        sts = [sh_ref[h] for h in heads]
        inter = [[] for _ in heads]
        for c in range(nh):
            rc = slice(c * ch, (c + 1) * ch)
            for h in heads:
                inter[h].append(_dot_nt(q_in[rc, sls[h]], sts[h].astype(BF16)))
            sts = [sts[h] * fl[c][:, sls[h]] + _dot_tn(vv[rc, sls[h]], k_out[rc, sls[h]]) for h in heads]
        for h in heads:
            sh_ref[h] = sts[h]
        os_ = [intra[h] + (inter[h][0] if nh == 1 else jnp.concatenate(inter[h], axis=0)) for h in heads]
        outs = [_rms(os_[h], hnorm) * _silu(hg_ref[rows, h * DV_B:(h + 1) * DV_B]) for h in heads]
        for p in range(H_B // 2):
            c0 = D_A + p * LANES
            mixed_ref[rows, c0:c0 + LANES] = jnp.concatenate([outs[2 * p], outs[2 * p + 1]], axis=-1)
        return 0

    lax.fori_loop(0, tt // gr, hgrn_group, 0)


def _even_mixer(z, conv0, sg0, sh0_t, conv_w, neg_a, dt_bias, gnorm, hnorm, lb, *, l_real, tt):
    B, Lp, _ = z.shape
    cg = min(GDN_CHUNK, tt)
    ch = min(HGRN_CHUNK, tt)
    d3 = 3 * D_A
    nt = Lp // tt
    col = lambda w, idx: pl.BlockSpec((None, tt, w), lambda b, i, idx=idx: (b, i, idx))
    state = lambda: pl.BlockSpec((None, H_A, HD_A, HD_A), lambda b, i: (b, 0, 0, 0))
    vec = lambda n: pl.BlockSpec((1, n), lambda b, i: (0, 0))
    nega_row = jnp.zeros((1, LANES), F32).at[0, H_A:2 * H_A].set(neg_a)
    dtb_row = jnp.zeros((1, LANES), F32).at[0, H_A:2 * H_A].set(dt_bias)
    return pl.pallas_call(
        functools.partial(_even_mixer_kernel, tt=tt, cg=cg, ch=ch, l_real=l_real),
        grid=(B, nt),
        in_specs=[
            col(d3, 0), col(D_A, 3), col(D_A, 4), col(D_A, 5), col(D_A, 6), col(D_A, 7), col(LANES, 32),
            pl.BlockSpec((None, SUBLANES, d3), lambda b, i: (b, 0, 0)),
            state(), state(),
            pl.BlockSpec((GDN_CONV, d3), lambda b, i: (0, 0)),
            vec(LANES), vec(LANES), vec(HD_A), vec(DV_B), vec(D_A),
        ],
        out_specs=[
            pl.BlockSpec((None, tt, 2 * D_A), lambda b, i: (b, i, 0)),
            pl.BlockSpec((None, SUBLANES, d3), lambda b, i: (b, 0, 0)),
            state(), state(),
        ],
        out_shape=[
            jax.ShapeDtypeStruct((B, Lp, 2 * D_A), F32),
            jax.ShapeDtypeStruct((B, SUBLANES, d3), F32),
            jax.ShapeDtypeStruct((B, H_A, HD_A, HD_A), F32),
            jax.ShapeDtypeStruct((B, H_B, DV_B, DK_B), F32),
        ],
        scratch_shapes=[
            pltpu.VMEM((tt + SUBLANES, d3), F32),
            pltpu.VMEM((tt, d3), F32),
            pltpu.VMEM((tt, LANES), F32),
            pltpu.VMEM((tt, LANES), F32),
        ],
        compiler_params=_cparams("parallel", "arbitrary"),
        name="even_mixer",
    )(z, z, z, z, z, z, z, conv0, sg0, sh0_t, conv_w, nega_row, dtb_row,
      gnorm.reshape(1, HD_A), hnorm.reshape(1, DV_B), lb.reshape(1, D_A))


def _mem_attn_kernel(x_ref, mk_ref, mv_ref, g_ref, wq_ref, gq_ref, wo_ref, o_ref):
    x = x_ref[...]
    q = _dot(_rms(x, g_ref[...]).astype(BF16), wq_ref[...])
    scale = MEM_HD ** -0.5
    outs = []
    for h in range(MEM_H):
        sl = slice(h * MEM_HD, (h + 1) * MEM_HD)
        qh = _rms(q[:, sl], gq_ref[...])
        s = _dot_nt(qh.astype(BF16), mk_ref[:, sl].astype(BF16)) * scale
        s = s - jnp.max(s, axis=-1, keepdims=True)
        p = jnp.exp(s)
        p = p / jnp.sum(p, axis=-1, keepdims=True)
        outs.append(_dot(p.astype(BF16), mv_ref[:, sl].astype(BF16)))
    o = jnp.concatenate(outs, axis=-1)
    o_ref[...] = x + _dot(o.astype(BF16), wo_ref[...])


def _mem_attn(x, mk, mv, g, wq, gq, wo, *, tm):
    B, L, D = x.shape
    M = mk.shape[1]
    dq = MEM_H * MEM_HD
    return pl.pallas_call(
        _mem_attn_kernel,
        grid=(B, L // tm),
        in_specs=[
            pl.BlockSpec((None, tm, D), lambda b, i: (b, i, 0)),
            pl.BlockSpec((None, M, dq), lambda b, i: (b, 0, 0)),
            pl.BlockSpec((None, M, dq), lambda b, i: (b, 0, 0)),
            pl.BlockSpec((1, D), lambda b, i: (0, 0)),
            pl.BlockSpec((D, dq), lambda b, i: (0, 0)),
            pl.BlockSpec((1, MEM_HD), lambda b, i: (0, 0)),
            pl.BlockSpec((dq, D), lambda b, i: (0, 0)),
        ],
        out_specs=pl.BlockSpec((None, tm, D), lambda b, i: (b, i, 0)),
        out_shape=jax.ShapeDtypeStruct((B, L, D), F32),
        compiler_params=_cparams("parallel", "parallel"),
        name="mem_attn",
    )(x, mk, mv, g.reshape(1, D), wq, gq.reshape(1, MEM_HD), wo)


def _mem_attn_step_kernel(q_ref, mk_ref, mv_ref, o_ref):
    scale = MEM_HD ** -0.5
    outs = []
    for h in range(MEM_H):
        qh = jnp.broadcast_to(q_ref[:, h * MEM_HD:(h + 1) * MEM_HD], (SUBLANES, MEM_HD))
        s = _dot_nt(qh.astype(BF16), mk_ref[:, h, :].astype(BF16)) * scale
        s = s - jnp.max(s, axis=-1, keepdims=True)
        p = jnp.exp(s)
        p = p / jnp.sum(p, axis=-1, keepdims=True)
        outs.append(_dot(p.astype(BF16), mv_ref[:, h, :].astype(BF16))[0:1])
    o_ref[...] = jnp.concatenate(outs, axis=-1)


def _mem_attn_step(q, cache_k, cache_v, l):
    B = q.shape[0]
    M = cache_k.shape[2]
    dq = MEM_H * MEM_HD
    mem = pl.BlockSpec((None, None, M, MEM_H, MEM_HD), lambda b: (l, b, 0, 0, 0))
    row = pl.BlockSpec((None, 1, dq), lambda b: (b, 0, 0))
    return pl.pallas_call(
        _mem_attn_step_kernel,
        grid=(B,),
        in_specs=[row, mem, mem],
        out_specs=row,
        out_shape=jax.ShapeDtypeStruct((B, 1, dq), F32),
        compiler_params=_cparams("parallel"),
        name="mem_attn_step",
    )(q, cache_k, cache_v)


def _ffn_seq_kernel(x_ref, buf0_ref, g_ref, wgu_ref, cw_ref, cb_ref, wd_ref, o_ref, tail_ref, gp_ref, carry_ref, *, tm):
    ti = pl.program_id(1)

    @pl.when(ti == 0)
    def _():
        carry_ref[...] = buf0_ref[...]

    x = x_ref[...]
    xn = _rms(x, g_ref[...]).astype(BF16)
    acc = x
    for c in range(D_FF // FF_CHUNK):
        cs = slice(c * FF_CHUNK, (c + 1) * FF_CHUNK)
        gate = _dot(xn, wgu_ref[:, cs])
        up = _dot(xn, wgu_ref[:, D_FF + c * FF_CHUNK:D_FF + (c + 1) * FF_CHUNK])
        gp_ref[0:SUBLANES, :] = carry_ref[:, cs]
        gp_ref[SUBLANES:SUBLANES + tm, :] = gate
        conv = cw_ref[FFN_CONV - 1:FFN_CONV, cs] * gate + cb_ref[:, cs]
        for j in range(FFN_CONV - 1):
            off = SUBLANES - (FFN_CONV - 1) + j
            conv = conv + cw_ref[j:j + 1, cs] * gp_ref[off:off + tm, :]
        last = gp_ref[tm:tm + SUBLANES, :]
        carry_ref[:, cs] = last
        tail_ref[:, cs] = last
        act = _silu(conv) * up
        acc = acc + _dot(act.astype(BF16), wd_ref[cs, :])
    o_ref[...] = acc


def _ffn_seq(x, buf0, g, wgu, cw, cb, wd, *, tm):
    B, L, D = x.shape
    const = lambda shape: pl.BlockSpec(shape, lambda b, i: (0,) * len(shape), pipeline_mode=pl.Buffered(1))
    return pl.pallas_call(
        functools.partial(_ffn_seq_kernel, tm=tm),
        grid=(B, L // tm),
        in_specs=[
            pl.BlockSpec((None, tm, D), lambda b, i: (b, i, 0)),
            pl.BlockSpec((None, SUBLANES, D_FF), lambda b, i: (b, 0, 0)),
            const((1, D)), const((D, 2 * D_FF)), const((FFN_CONV, D_FF)), const((1, D_FF)), const((D_FF, D)),
        ],
        out_specs=[
            pl.BlockSpec((None, tm, D), lambda b, i: (b, i, 0)),
            pl.BlockSpec((None, SUBLANES, D_FF), lambda b, i: (b, 0, 0)),
        ],
        out_shape=[jax.ShapeDtypeStruct((B, L, D), F32), jax.ShapeDtypeStruct((B, SUBLANES, D_FF), F32)],
        scratch_shapes=[pltpu.VMEM((tm + SUBLANES, FF_CHUNK), F32), pltpu.VMEM((SUBLANES, D_FF), F32)],
        compiler_params=_cparams("parallel", "arbitrary"),
        name="conv_ffn_seq",
    )(x, buf0, g.reshape(1, D), wgu, cw, cb.reshape(1, D_FF), wd)


def _ffn_step_kernel(x_ref, p2_ref, p1_ref, g_ref, wgu_ref, cw_ref, cb_ref, wd_ref, o_ref, gate_ref):
    x = x_ref[...]
    xn = _rms(x, g_ref[...]).astype(BF16)
    acc = x
    for c in range(D_FF // FF_CHUNK):
        cs = slice(c * FF_CHUNK, (c + 1) * FF_CHUNK)
        gate = _dot(xn, wgu_ref[:, cs])
        up = _dot(xn, wgu_ref[:, D_FF + c * FF_CHUNK:D_FF + (c + 1) * FF_CHUNK])
        gate_ref[:, cs] = gate
        conv = cw_ref[0:1, cs] * p2_ref[:, cs] + cw_ref[1:2, cs] * p1_ref[:, cs] + cw_ref[2:3, cs] * gate + cb_ref[:, cs]
        act = _silu(conv) * up
        acc = acc + _dot(act.astype(BF16), wd_ref[cs, :])
    o_ref[...] = acc


def _ffn_step(x, p2, p1, g, wgu, cw, cb, wd):
    R, D = x.shape
    full = lambda a: pl.BlockSpec(a.shape, lambda i: (0,) * a.ndim)
    args = (x, p2, p1, g.reshape(1, D), wgu, cw, cb.reshape(1, D_FF), wd)
    return pl.pallas_call(
        _ffn_step_kernel,
        grid=(1,),
        in_specs=[full(a) for a in args],
        out_specs=[pl.BlockSpec((R, D), lambda i: (0, 0)), pl.BlockSpec((R, D_FF), lambda i: (0, 0))],
        out_shape=[jax.ShapeDtypeStruct((R, D), F32), jax.ShapeDtypeStruct((R, D_FF), F32)],
        compiler_params=_cparams("arbitrary"),
        name="conv_ffn_step",
    )(*args)


def _moba_query_tile(j, h, qt_ref, kb_ref, vt_ref, kmean_ref, bias_ref, ot_ref):
    blk = MOBA_BLOCK
    w = (j + 1) * blk
    qt = qt_ref[h]
    st = _dot(kb_ref[h, 0:w, :], (qt * (HD_C ** -0.5)).astype(BF16))
    if j > MOBA_TOPK:
        g = _dot(kmean_ref[h, 0:j, :], qt, HI)
        iota_n = lax.broadcasted_iota(jnp.int32, (j, blk), 0).astype(F32)
        sel = jnp.zeros((j, blk), F32)
        for _ in range(MOBA_TOPK):
            m = jnp.max(g, axis=0, keepdims=True)
            idx = jnp.min(jnp.where(g == m, iota_n, 1e9), axis=0, keepdims=True)
            hit = iota_n == idx
            sel = jnp.where(hit, 1.0, sel)
            g = jnp.where(hit, -jnp.inf, g)
    key_le_query = lax.broadcasted_iota(jnp.int32, (blk, blk), 0) <= lax.broadcasted_iota(jnp.int32, (blk, blk), 1)
    parts = []
    for n in range(j + 1):
        s = st[n * blk:(n + 1) * blk, :]
        if j - n < 2:
            s = s + bias_ref[h, j - n]
        if n == j:
            s = jnp.where(key_le_query, s, -jnp.inf)
        elif j > MOBA_TOPK:
            s = jnp.where(sel[n:n + 1, :] > 0.5, s, -jnp.inf)
        parts.append(s)
    m = jnp.max(parts[0], axis=0, keepdims=True)
    for s in parts[1:]:
        m = jnp.maximum(m, jnp.max(s, axis=0, keepdims=True))
    ps = [jnp.exp(s - m) for s in parts]
    l = jnp.sum(ps[0], axis=0, keepdims=True)
    for p in ps[1:]:
        l = l + jnp.sum(p, axis=0, keepdims=True)
    pcat = ps[0] if j == 0 else jnp.concatenate(ps, axis=0)
    ot_ref[h] = _dot(vt_ref[h, :, 0:w], pcat.astype(BF16)) / l


def _moba_seq_kernel(q_ref, k_ref, vtin_ref, bias_ref, o_ref, kmean_ref, kb_ref, vt_ref, qt_ref, ot_ref, *, nb):
    j = pl.program_id(1)
    blk = MOBA_BLOCK
    dc = H_C * HD_C

    @pl.when(j == 0)
    def _():
        for n in range(nb):
            km = jnp.mean(k_ref[n * blk:(n + 1) * blk, :], axis=0, keepdims=True)
            for h in range(H_C):
                kmean_ref[h, n:n + 1, :] = km[:, h * HD_C:(h + 1) * HD_C]

        def prep(n, _):
            rows = pl.ds(pl.multiple_of(n * blk, blk), blk)
            kblk = k_ref[rows, :]
            vt_ref[:, :, rows] = vtin_ref[:, rows].astype(BF16).reshape(H_C, HD_C, blk)
            for h in range(H_C):
                kb_ref[h, rows, :] = kblk[:, h * HD_C:(h + 1) * HD_C].astype(BF16)
            return 0
        lax.fori_loop(0, nb, prep, 0)

    qt_ref[...] = q_ref[...].T.reshape(H_C, HD_C, blk)
    for jj in range(nb):
        @pl.when(j == jj)
        def _(jj=jj):
            def head(h, _):
                _moba_query_tile(jj, h, qt_ref, kb_ref, vt_ref, kmean_ref, bias_ref, ot_ref)
                return 0
            lax.fori_loop(0, H_C, head, 0)
    o_ref[...] = ot_ref[...].reshape(dc, blk).T


def _moba_seq(qkv, vt, bias_tiles):
    B, L, _ = qkv.shape
    nb = L // MOBA_BLOCK
    dc = H_C * HD_C
    return pl.pallas_call(
        functools.partial(_moba_seq_kernel, nb=nb),
        grid=(B, nb),
        in_specs=[
            pl.BlockSpec((None, MOBA_BLOCK, dc), lambda b, j: (b, j, 0)),
            pl.BlockSpec((None, L, dc), lambda b, j: (b, 0, 1)),
            pl.BlockSpec((None, dc, L), lambda b, j: (b, 0, 0)),
            pl.BlockSpec(bias_tiles.shape, lambda b, j: (0, 0, 0, 0), pipeline_mode=pl.Buffered(1)),
        ],
        out_specs=pl.BlockSpec((None, MOBA_BLOCK, dc), lambda b, j: (b, j, 0)),
        out_shape=jax.ShapeDtypeStruct((B, L, dc), F32),
        scratch_shapes=[pltpu.VMEM((H_C, nb, HD_C), F32), pltpu.VMEM((H_C, L, HD_C), BF16),
                        pltpu.VMEM((H_C, HD_C, L), BF16), pltpu.VMEM((H_C, HD_C, MOBA_BLOCK), F32),
                        pltpu.VMEM((H_C, HD_C, MOBA_BLOCK), F32)],
        compiler_params=_cparams("parallel", "arbitrary"),
        name="moba_seq",
    )(qkv, qkv, vt, bias_tiles)


MOBA_PAGES_PER_BLOCK = MOBA_BLOCK // PAGE_SIZE
MOBA_STEP_PAGES = 8


def _moba_score_kernel(pt_ref, q_ref, *refs, n_pages):
    del pt_ref
    ppg, ppb = MOBA_STEP_PAGES, MOBA_PAGES_PER_BLOCK
    kp_refs = refs[:ppg]
    s_ref, sel_ref, gate_ref = refs[ppg:]
    step = pl.program_id(1)
    nbp = n_pages // ppb
    q = q_ref[...]
    inv = 1.0 / (MOBA_BLOCK * HD_C ** -0.5)
    gsum = None
    for r in range(ppg):
        sc = jnp.sum(kp_refs[r][...] * q, axis=1)
        s_ref[:, r * PAGE_SIZE:(r + 1) * PAGE_SIZE] = sc
        gs = jnp.sum(sc, axis=-1, keepdims=True)
        gsum = gs if r % ppb == 0 else gsum + gs
        if r % ppb == ppb - 1:
            gate_ref[step * (ppg // ppb) + r // ppb] = jnp.broadcast_to(gsum * inv, (H_C, LANES))

    @pl.when(step == n_pages // ppg - 1)
    def _():
        g = gate_ref[...]
        iota_b = lax.broadcasted_iota(jnp.int32, (nbp, H_C, LANES), 0).astype(F32)
        for i in range(MOBA_TOPK):
            m = jnp.max(g, axis=0, keepdims=True)
            idx = jnp.min(jnp.where(g == m, iota_b, 1e9), axis=0, keepdims=True)
            sel_ref[i] = idx[0]
            g = jnp.where(iota_b == idx, -jnp.inf, g)


def _moba_attend_kernel(pt_ref, sel_ref, q_ref, kn_ref, vn_ref, blast_ref, bsc_ref, *refs, nbp):
    del pt_ref
    nsel, ppb = MOBA_TOPK, MOBA_PAGES_PER_BLOCK
    s_refs, v_refs, o_ref = refs[:nsel], refs[nsel:nsel + nsel * ppb], refs[-1]
    b, h = pl.program_id(0), pl.program_id(1)
    far = bsc_ref[h, 0]
    s_new = jnp.sum(q_ref[...] * kn_ref[...], axis=-1, keepdims=True) * (HD_C ** -0.5) + bsc_ref[h, 1]
    ss = []
    for i in range(nsel):
        blk = sel_ref[(b * H_C + h) * nsel + i]
        ss.append(s_refs[i][...] + jnp.where(blk == nbp - 1, blast_ref[...], far))
    m = s_new
    for s in ss:
        m = jnp.maximum(m, jnp.max(s, axis=-1, keepdims=True))
    w_new = jnp.exp(s_new - m)
    den = w_new
    num = w_new * vn_ref[...]
    for i in range(nsel):
        e = jnp.exp(ss[i] - m)
        den = den + jnp.sum(e, axis=-1, keepdims=True)
        eb = jnp.broadcast_to(e, (SUBLANES, MOBA_BLOCK)).astype(BF16)
        for r in range(ppb):
            pv = _dot_nt(eb[:, r * PAGE_SIZE:(r + 1) * PAGE_SIZE], v_refs[i * ppb + r][...].astype(BF16))
            num = num + pv[0:1]
    o_ref[...] = num / den


def _moba_step(q, k_new, v_new, cache_k, cache_v, page_table, bias_last, bias_scalars):
    B = q.shape[0]
    n_pages = page_table.shape[1]
    ppg, ppb, nsel = MOBA_STEP_PAGES, MOBA_PAGES_PER_BLOCK, MOBA_TOPK
    assert n_pages % ppg == 0 and ppg % ppb == 0
    nbp = n_pages // ppb
    assert nbp >= nsel
    pt = page_table.reshape(-1)
    q_rep = jnp.broadcast_to((q * (HD_C ** -0.5))[..., None], (B, H_C, HD_C, LANES))
    kpage = lambda r: pl.BlockSpec((None, H_C, HD_C, PAGE_SIZE),
                                   lambda b, s, pt, r=r: (pt[b * n_pages + s * ppg + r], 0, 0, 0))
    scores, sel = pl.pallas_call(
        functools.partial(_moba_score_kernel, n_pages=n_pages),
        grid_spec=pltpu.PrefetchScalarGridSpec(
            num_scalar_prefetch=1,
            grid=(B, n_pages // ppg),
            in_specs=[pl.BlockSpec((None, H_C, HD_C, LANES), lambda b, s, pt: (b, 0, 0, 0))]
                     + [kpage(r) for r in range(ppg)],
            out_specs=[pl.BlockSpec((None, H_C, ppg * PAGE_SIZE), lambda b, s, pt: (b, 0, s)),
                       pl.BlockSpec((None, nsel, H_C, LANES), lambda b, s, pt: (b, 0, 0, 0))],
            scratch_shapes=[pltpu.VMEM((nbp, H_C, LANES), F32)],
        ),
        out_shape=[jax.ShapeDtypeStruct((B, H_C, n_pages * PAGE_SIZE), F32),
                   jax.ShapeDtypeStruct((B, nsel, H_C, LANES), F32)],
        compiler_params=_cparams("parallel", "arbitrary"),
        name="moba_step_score",
    )(pt, q_rep, *([cache_k] * ppg))
    sel_flat = jnp.swapaxes(sel[:, :, :, 0], 1, 2).astype(jnp.int32).reshape(-1)
    tok = pl.BlockSpec((None, None, 1, HD_C), lambda b, h, pt, sl: (b, h, 0, 0))
    srow = lambda i: pl.BlockSpec((None, None, None, 1, MOBA_BLOCK),
                                  lambda b, h, pt, sl, i=i: (b, h, sl[(b * H_C + h) * nsel + i], 0, 0))
    vpage = lambda i, r: pl.BlockSpec(
        (None, None, HD_C, PAGE_SIZE),
        lambda b, h, pt, sl, i=i, r=r: (pt[b * n_pages + sl[(b * H_C + h) * nsel + i] * ppb + r], h, 0, 0))
    row = lambda a: a.reshape(B, H_C, 1, HD_C)
    out = pl.pallas_call(
        functools.partial(_moba_attend_kernel, nbp=nbp),
        grid_spec=pltpu.PrefetchScalarGridSpec(
            num_scalar_prefetch=2,
            grid=(B, H_C),
            in_specs=[tok, tok, tok,
                      pl.BlockSpec((None, 1, MOBA_BLOCK), lambda b, h, pt, sl: (h, 0, 0)),
                      pl.BlockSpec(memory_space=pltpu.SMEM)]
                     + [srow(i) for i in range(nsel)]
                     + [vpage(i, r) for i in range(nsel) for r in range(ppb)],
            out_specs=tok,
        ),
        out_shape=jax.ShapeDtypeStruct((B, H_C, 1, HD_C), F32),
        compiler_params=_cparams("parallel", "arbitrary"),
        name="moba_step_attend",
    )(pt, sel_flat, row(q), row(k_new), row(v_new), bias_last, bias_scalars,
      *([scores.reshape(B, H_C, nbp, 1, MOBA_BLOCK)] * nsel), *([cache_v] * (nsel * ppb)))
    return out.reshape(B, H_C, HD_C)


def _s5_kernel(u_ref, h0r_ref, h0i_ref, wb_ref, ar_ref, ai_ref, wc_ref, d_ref, wglu_ref,
               o_ref, hr_ref, hi_ref, x_ref, *, tt, nb):
    ti = pl.program_id(0)
    ns = D_S5_STATE

    @pl.when(ti == 0)
    def _():
        hr_ref[...] = h0r_ref[...]
        hi_ref[...] = h0i_ref[...]

    nslab = ns // LANES
    if tt == 1:
        seqs, seq_rows, mat_rows, cols = [0], lambda b: slice(None), lambda b: slice(None), lambda b: slice(None)
        u = u_ref[...]
    else:
        seqs = range(nb)
        seq_rows = lambda b: pl.ds(b, tt, stride=nb)
        mat_rows = lambda b: slice(b * tt, (b + 1) * tt)
        cols = lambda b: slice(b * D_S5, (b + 1) * D_S5)
        u = jnp.concatenate([u_ref[:, cols(b)] for b in seqs], axis=0)
    x = _dot(u.astype(BF16), wb_ref[...])
    for b in seqs:
        for c in range(2 * nslab):
            x_ref[c, seq_rows(b), :] = x[mat_rows(b), c * LANES:(c + 1) * LANES]
    slabs = lambda ref: jnp.stack([ref[:, c * LANES:(c + 1) * LANES] for c in range(nslab)])
    ar = slabs(ar_ref)
    ai = slabs(ai_ref)

    def step(t, carry):
        hr, hi = carry
        rows = pl.ds(pl.multiple_of(t * nb, nb), nb)
        nr = ar * hr - ai * hi + x_ref[0:nslab, rows, :]
        ni = ar * hi + ai * hr + x_ref[nslab:2 * nslab, rows, :]
        x_ref[0:nslab, rows, :] = nr
        x_ref[nslab:2 * nslab, rows, :] = ni
        return nr, ni

    hr, hi = lax.fori_loop(0, tt, step, (slabs(hr_ref), slabs(hi_ref)))
    for c in range(nslab):
        hr_ref[:, c * LANES:(c + 1) * LANES] = hr[c]
        hi_ref[:, c * LANES:(c + 1) * LANES] = hi[c]

    hs = [jnp.concatenate([x_ref[c, seq_rows(b), :] for c in range(2 * nslab)], axis=-1) for b in seqs]
    h = hs[0] if len(hs) == 1 else jnp.concatenate(hs, axis=0)
    y = jax.nn.gelu(_dot(h.astype(BF16), wc_ref[...]) + d_ref[...] * u)
    gl = _dot(y.astype(BF16), wglu_ref[...])
    o = gl[:, 0:D_S5] * jax.nn.sigmoid(gl[:, D_S5:2 * D_S5])
    for b in seqs:
        o_ref[:, cols(b)] = o[mat_rows(b), :]


def _s5(u, h0r, h0i, wb, ar, ai, wc, d, wglu, *, tt, nb):
    L = u.shape[0]
    const = lambda a: pl.BlockSpec(a.shape, lambda i: (0,) * a.ndim)
    rows = pl.BlockSpec((tt, nb * D_S5), lambda i: (i, 0))
    st = pl.BlockSpec((nb, D_S5_STATE), lambda i: (0, 0))
    args = (u, h0r, h0i, wb, ar, ai, wc, d, wglu)
    if tt == 1:
        assert L == 1
        rows = pl.BlockSpec((nb, D_S5), lambda i: (0, 0))
        args = (u.reshape(nb, D_S5),) + args[1:]
    return pl.pallas_call(
        functools.partial(_s5_kernel, tt=tt, nb=nb),
        grid=(L // tt,),
        in_specs=[rows] + [const(a) for a in args[1:]],
        out_specs=[rows, st, st],
        out_shape=[jax.ShapeDtypeStruct(args[0].shape, F32),
                   jax.ShapeDtypeStruct((nb, D_S5_STATE), F32), jax.ShapeDtypeStruct((nb, D_S5_STATE), F32)],
        scratch_shapes=[pltpu.VMEM((2 * D_S5_STATE // LANES, tt * nb, LANES), F32)],
        compiler_params=_cparams("arbitrary"),
        name="s5",
    )(*args)


def _rel_bucket(dist):
    n = jnp.maximum(dist, 0)
    exact = N_BUCKETS // 2
    nf = jnp.maximum(n, exact).astype(F32)
    large = exact + (jnp.log(nf / exact) / math.log(MAX_DISTANCE / exact) * (N_BUCKETS - exact)).astype(jnp.int32)
    return jnp.where(n < exact, n, jnp.minimum(large, N_BUCKETS - 1))


def _block_diag(w):
    G, a, b = w.shape
    eye = jnp.eye(G, dtype=w.dtype)
    return (eye[:, None, :, None] * w[:, :, None, :]).reshape(G * a, G * b)


def _s5_weights(A_re, A_im, log_dt, B_re, B_im, C_re, C_im, D):
    dt = jnp.exp(log_dt.astype(F32))[:, None]
    lr, li = A_re.astype(F32), A_im.astype(F32)
    mag = jnp.exp(lr * dt)
    ar, ai = mag * jnp.cos(li * dt), mag * jnp.sin(li * dt)
    den = lr * lr + li * li
    nr = ar - 1.0
    cr, ci = (nr * lr + ai * li) / den, (ai * lr - nr * li) / den
    bbr = cr[..., None] * B_re - ci[..., None] * B_im
    bbi = cr[..., None] * B_im + ci[..., None] * B_re
    wb = jnp.concatenate([_block_diag(bbr.transpose(0, 2, 1)), _block_diag(bbi.transpose(0, 2, 1))], axis=1)
    wc = jnp.concatenate([_block_diag(C_re.transpose(0, 2, 1)), -_block_diag(C_im.transpose(0, 2, 1))], axis=0)
    return (wb.astype(BF16), ar.reshape(1, -1), ai.reshape(1, -1), wc.astype(BF16), D.reshape(1, -1).astype(F32))


def _pad_rows_front(a, rows):
    return jnp.pad(a, ((0, 0), (rows - a.shape[1], 0), (0, 0)))


def _trunk(x, st, W, *, decode):
    B, L, D = x.shape
    T = B * L
    tm = min(512, T)
    out = {}

    x2 = x.reshape(T, D)
    z = _norm_proj(x2, W['norm_mix'][0], W['ev_w_in'], tm=tm, tn=EV_IN_TILE, name="ev_in_proj")
    zc = z.shape[1]
    if decode:
        lp = SUBLANES
        z3 = jnp.pad(z.reshape(B, L, zc), ((0, 0), (0, lp - L), (0, 0)))
        tt = lp
    else:
        lp = L
        z3 = z.reshape(B, L, zc)
        tt = min(512, L)
    conv0 = _pad_rows_front(st['gdn_conv'], SUBLANES)
    mixed, tail, sg, sh_t = _even_mixer(z3, conv0, st['gdn'], jnp.swapaxes(st['hgrn'], -1, -2), W['gdn_conv_w'],
                                        W['gdn_neg_a'], W['gdn_dt_bias'], W['gdn_norm'], W['hgrn_norm'], W['hgrn_lb'],
                                        l_real=L, tt=tt)
    out['gdn_conv'] = tail[:, SUBLANES - (GDN_CONV - 1):, :]
    out['gdn'] = sg
    out['hgrn'] = jnp.swapaxes(sh_t, -1, -2)
    mixed2 = mixed[:, :L].reshape(T, 2 * D_A)
    x2 = _proj_res(x2, [(mixed2, W['ev_w_out'], pl.BlockSpec((tm, 2 * D_A), lambda i: (i, 0)))], tm=tm, name="ev_out_proj")
    x2, fs = _post_mixer(x2, st, W, 0, B, L, decode)
    out['ffn0'] = fs

    dc = H_C * HD_C
    gqk = jnp.concatenate([jnp.tile(W['moba_gq'], H_C), jnp.tile(W['moba_gk'], H_C)])
    u_tm = _norm_proj(x2, W['norm_mix'][1], W['od_w_in'][:, 3 * dc:], tm=tm, tn=D_S5,
                      time_major=None if decode else (B, L), name="od_in_proj_u")
    u_tm = u_tm.reshape(L, B * D_S5)
    if decode:
        (qkv,) = _od_qkv_proj(x2, W['norm_mix'][1], W['od_w_in'][:, :3 * dc], gqk, tm=tm)
        out['moba_k'] = qkv[:, dc:2 * dc].reshape(B, L, H_C, HD_C)
        out['moba_v'] = qkv[:, 2 * dc:].reshape(B, L, H_C, HD_C)
        heads = lambda a: a.reshape(B, H_C, HD_C)
        o_c = _moba_step(heads(qkv[:, :dc]), heads(qkv[:, dc:2 * dc]), heads(qkv[:, 2 * dc:]),
                         st['cache_k'], st['cache_v'], st['page_table'], *W['moba_bias_step'])
        o_c = o_c.reshape(T, dc)
        tts = 1
    else:
        qkv, kt, vt = _od_qkv_proj(x2, W['norm_mix'][1], W['od_w_in'][:, :3 * dc], gqk, tm=tm, seq=(B, L))
        out['moba_k'] = jnp.transpose(kt.reshape(B, H_C, HD_C, L), (0, 3, 1, 2))
        out['moba_v'] = jnp.transpose(vt.reshape(B, H_C, HD_C, L), (0, 3, 1, 2))
        o_c = _moba_seq(qkv.reshape(B, L, 3 * dc), vt, W['moba_bias_tiles']).reshape(T, dc)
        tts = min(64, L)
    o_d, hr, hi = _s5(u_tm, st['s5_re'], st['s5_im'], *W['s5'], W['s5_glu_w'], tt=tts, nb=B)
    out['s5_re'] = hr.reshape(B, S5_GROUPS, S5_STATE)
    out['s5_im'] = hi.reshape(B, S5_GROUPS, S5_STATE)
    if decode:
        od_spec = pl.BlockSpec((tm, D_S5), lambda i: (i, 0))
        o_d2 = o_d
    else:
        nt = L // tm
        od_spec = pl.BlockSpec((tm, D_S5), lambda i: (i % nt, i // nt))
        o_d2 = o_d
    x2 = _proj_res(x2, [(o_c, W['od_w_out'][:dc], pl.BlockSpec((tm, dc), lambda i: (i, 0))),
                        (o_d2, W['od_w_out'][dc:], od_spec)], tm=tm, name="od_out_proj")
    x2, fs = _post_mixer(x2, st, W, 1, B, L, decode)
    out['ffn1'] = fs
    out['y'] = x2.reshape(B, L, D)
    return out


def _post_mixer(x2, st, W, l, B, L, decode):
    T, D = x2.shape
    dq = MEM_H * MEM_HD
    buf = st['ffn_conv'][l]
    if decode:
        q = _norm_proj(x2, W['norm_mem'][l], W['mem_wq'][l], tm=T, tn=dq, n_norm_tiles=1, group=MEM_HD,
                       gain=jnp.tile(W['mem_gq'][l], MEM_H), name="mem_q_proj")
        o = _mem_attn_step(q.reshape(B, 1, dq), st['mem_k'], st['mem_v'], l).reshape(T, dq)
        x3 = _proj_res(x2, [(o, W['mem_wo'][l], pl.BlockSpec((T, dq), lambda i: (i, 0)))], tm=T, name="mem_out_proj")
        y, gate = _ffn_step(x3, buf[:, 0], buf[:, 1], W['norm_ffn'][l], W['ffn_w_gu'][l],
                            W['ffn_conv_w'][l], W['ffn_conv_b'][l], W['ffn_w_down'][l])
        fs = jnp.stack([buf[:, 1], gate], axis=1)
        return y, fs
    x3 = _mem_attn(x2.reshape(B, L, D), st['mem_k'][l], st['mem_v'][l], W['norm_mem'][l], W['mem_wq'][l],
                   W['mem_gq'][l], W['mem_wo'][l], tm=min(512, L))
    y, tail = _ffn_seq(x3, _pad_rows_front(buf, SUBLANES), W['norm_ffn'][l], W['ffn_w_gu'][l], W['ffn_conv_w'][l],
                       W['ffn_conv_b'][l], W['ffn_w_down'][l], tm=min(512, L))
    return y.reshape(T, D), tail[:, SUBLANES - (FFN_CONV - 1):, :]


def _mem_kv(mem, g_norm, w_kv, g_k):
    B, M, D = mem.shape
    dq = MEM_H * MEM_HD
    gain = jnp.concatenate([jnp.tile(g_k, MEM_H), jnp.ones((dq,), F32)])
    kv = _norm_proj(mem.reshape(B * M, D), g_norm, w_kv, tm=min(512, B * M), tn=dq, n_norm_tiles=1, group=MEM_HD,
                    gain=gain, name="mem_kv_proj")
    return kv[:, :dq].reshape(B, M, dq), kv[:, dq:].reshape(B, M, dq)


def kernel(x_prompt, x_sample, state_gdn_conv, state_gdn, state_hgrn, cache_moba_k, cache_moba_v, state_s5_re, state_s5_im, cache_mem_k, cache_mem_v, state_ffn_conv, page_table, mem_prompt, rel_bias, hgrn_lb_raw, norm_mix, norm_mem, norm_memkv, norm_ffn, mem_wq, mem_wkv, mem_gq, mem_gk, mem_wo, ffn_w_gu, ffn_conv_w, ffn_conv_b, ffn_w_down, ev_w_in, ev_w_out, gdn_conv_w, gdn_A_log, gdn_dt_bias, gdn_norm, hgrn_norm, od_w_in, od_w_out, moba_gq, moba_gk, s5_A_re, s5_A_im, s5_log_dt, s5_B_re, s5_B_im, s5_C_re, s5_C_im, s5_D, s5_glu_w):
    Bp, Lq, D = x_prompt.shape
    Bs = x_sample.shape[0]
    dc = H_C * HD_C
    dq = MEM_H * MEM_HD
    n_pages = page_table.shape[1]
    past = n_pages * PAGE_SIZE

    d3 = 3 * D_A
    w_in = ev_w_in[0]
    ba_cols = w_in[:, 4 * D_A:4 * D_A + 2 * H_A]
    w_in = jnp.concatenate([w_in[:, :4 * D_A], w_in[:, 4 * D_A + 2 * H_A:], ba_cols,
                            jnp.zeros((D, LANES - 2 * H_A), F32)], axis=1)
    lb_all = jnp.cumsum(jax.nn.softmax(hgrn_lb_raw.astype(F32), axis=0), axis=0)
    bias_t = rel_bias.T.astype(F32)
    nblk = MOBA_BLOCK
    by_rel = bias_t[:, _rel_bucket(jnp.arange(2 * nblk, dtype=jnp.int32))]
    mm = jnp.arange(2 * nblk, dtype=jnp.int32)

    def toeplitz(shift):
        y = by_rel[:, (shift + mm) % (2 * nblk)]
        flat = jnp.tile(y, (1, nblk))[:, :nblk * (2 * nblk - 1)]
        return flat.reshape(H_C, nblk, 2 * nblk - 1)[:, :, :nblk]

    far = bias_t[:, _rel_bucket(jnp.int32(2 * nblk))]
    tiles = jnp.stack([toeplitz(0), toeplitz(nblk)], axis=1) - far[:, None, None, None]
    bias_step = (by_rel[:, nblk - jnp.arange(nblk, dtype=jnp.int32)][:, None, :],
                 jnp.stack([far, bias_t[:, 0]], axis=1))
    W = dict(
        norm_mix=norm_mix, norm_mem=norm_mem, norm_ffn=norm_ffn,
        ev_w_in=w_in.astype(BF16), ev_w_out=ev_w_out[0].astype(BF16),
        gdn_conv_w=gdn_conv_w[0], gdn_neg_a=-jnp.exp(gdn_A_log[0].astype(F32)), gdn_dt_bias=gdn_dt_bias[0],
        gdn_norm=gdn_norm[0], hgrn_norm=hgrn_norm[0], hgrn_lb=lb_all[0],
        od_w_in=od_w_in[0].astype(BF16), od_w_out=od_w_out[0].astype(BF16),
        moba_gq=moba_gq[0], moba_gk=moba_gk[0], moba_bias_tiles=tiles, moba_bias_step=bias_step,
        s5=_s5_weights(s5_A_re[0], s5_A_im[0], s5_log_dt[0], s5_B_re[0], s5_B_im[0], s5_C_re[0], s5_C_im[0], s5_D[0]),
        s5_glu_w=s5_glu_w[0].astype(BF16),
        mem_wq=mem_wq.astype(BF16), mem_gq=mem_gq, mem_wo=mem_wo.astype(BF16),
        ffn_w_gu=ffn_w_gu.astype(BF16), ffn_conv_w=ffn_conv_w, ffn_conv_b=ffn_conv_b, ffn_w_down=ffn_w_down.astype(BF16),
    )

    mk_l, mv_l = [], []
    for l in range(2):
        mk, mv = _mem_kv(mem_prompt, norm_memkv[l], mem_wkv[l].astype(BF16), mem_gk[l])
        mk_l.append(mk)
        mv_l.append(mv)
    p_mem_k, p_mem_v = jnp.stack(mk_l), jnp.stack(mv_l)
    zeros = lambda *s: jnp.zeros(s, F32)
    st_p = dict(gdn_conv=zeros(Bp, GDN_CONV - 1, d3), gdn=zeros(Bp, H_A, HD_A, HD_A), hgrn=zeros(Bp, H_B, DK_B, DV_B),
                s5_re=zeros(Bp, D_S5_STATE), s5_im=zeros(Bp, D_S5_STATE), mem_k=p_mem_k, mem_v=p_mem_v,
                ffn_conv=zeros(2, Bp, FFN_CONV - 1, D_FF))
    op = _trunk(x_prompt, st_p, W, decode=False)

    st_s = dict(gdn_conv=state_gdn_conv[0], gdn=state_gdn[0], hgrn=state_hgrn[0],
                s5_re=state_s5_re[0].reshape(Bs, D_S5_STATE), s5_im=state_s5_im[0].reshape(Bs, D_S5_STATE),
                mem_k=cache_mem_k, mem_v=cache_mem_v,
                ffn_conv=state_ffn_conv,
                cache_k=jnp.transpose(cache_moba_k[0], (0, 2, 3, 1)), cache_v=jnp.transpose(cache_moba_v[0], (0, 2, 3, 1)),
                page_table=page_table)
    os_ = _trunk(x_sample, st_s, W, decode=True)

    M = mem_prompt.shape[1]
    return (op['y'], os_['y'],
            op['gdn_conv'][None], op['gdn'][None], op['hgrn'][None], op['moba_k'][None], op['moba_v'][None],
            op['s5_re'][None], op['s5_im'][None],
            p_mem_k.reshape(2, Bp, M, MEM_H, MEM_HD), p_mem_v.reshape(2, Bp, M, MEM_H, MEM_HD),
            jnp.stack([op['ffn0'], op['ffn1']]),
            os_['gdn_conv'][None], os_['gdn'][None], os_['hgrn'][None], os_['moba_k'][None], os_['moba_v'][None],
            os_['s5_re'][None], os_['s5_im'][None], jnp.stack([os_['ffn0'], os_['ffn1']]))
```

```python
import functools
import math

import jax
import jax.numpy as jnp
from jax import lax
from jax.experimental import pallas as pl
from jax.experimental.pallas import tpu as pltpu

F32 = jnp.float32
BF16 = jnp.bfloat16
HI = lax.Precision.HIGHEST
EPS = 1e-6

LANES = 128
SUBLANES = 8
VMEM_LIMIT_BYTES = 56 * 1024 * 1024

D_MODEL = 1024
H_A, HD_A, GDN_CONV = 8, 64, 4
H_B, DK_B, DV_B = 8, 64, 64
GDN_CHUNK, HGRN_CHUNK = 64, 32
GDN_GROUP, HGRN_GROUP = 2, 4
H_C, HD_C = 8, 64
MOBA_BLOCK, MOBA_TOPK = 256, 3
N_BUCKETS, MAX_DISTANCE = 32, 128
S5_GROUPS, S5_GROUP_CH, S5_STATE = 32, 16, 64
MEM_H, MEM_HD = 4, 128
D_FF, FFN_CONV = 2816, 3
PAGE_SIZE = 128
D_A = H_A * HD_A
D_S5 = S5_GROUPS * S5_GROUP_CH
D_S5_STATE = S5_GROUPS * S5_STATE
FF_CHUNK = D_FF // 2
EV_IN_COLS = 8 * D_A + LANES
EV_IN_TILE = EV_IN_COLS // 3


def _cparams(*sem):
    return pltpu.CompilerParams(dimension_semantics=sem, vmem_limit_bytes=VMEM_LIMIT_BYTES)


def _dot(a, b, precision=None):
    return jnp.dot(a, b, preferred_element_type=F32, precision=precision)


def _dot_nt(a, b, precision=None):
    return lax.dot_general(a, b, (((1,), (1,)), ((), ())), preferred_element_type=F32, precision=precision)


def _dot_tn(a, b, precision=None):
    return lax.dot_general(a, b, (((0,), (0,)), ((), ())), preferred_element_type=F32, precision=precision)


def _rms(x, g):
    return x * lax.rsqrt(jnp.mean(x * x, axis=-1, keepdims=True) + EPS) * g


def _silu(x):
    return x * jax.nn.sigmoid(x)


def _tri(n, strict=False):
    r = lax.broadcasted_iota(jnp.int32, (n, n), 0)
    c = lax.broadcasted_iota(jnp.int32, (n, n), 1)
    return (r > c) if strict else (r >= c)


def _norm_proj_kernel(x_ref, g_ref, w_ref, e_ref, gain_ref, o_ref, *, n_norm_tiles, group):
    xn = _rms(x_ref[...], g_ref[...])
    z = _dot(xn.astype(BF16), w_ref[...])
    if n_norm_tiles == 0:
        o_ref[...] = z
        return
    j = pl.program_id(1)

    @pl.when(j < n_norm_tiles)
    def _():
        zz = z * z
        hi = zz.astype(BF16)
        lo = (zz - hi.astype(F32)).astype(BF16)
        s = _dot(hi, e_ref[...]) + _dot(lo, e_ref[...])
        o_ref[...] = z * lax.rsqrt(s * (1.0 / group) + EPS) * gain_ref[...]

    @pl.when(j >= n_norm_tiles)
    def _():
        o_ref[...] = z


def _norm_proj(x, g, w, *, tm, tn, n_norm_tiles=0, group=1, gain=None, time_major=None, name):
    T, D = x.shape
    N = w.shape[1]
    nj = N // tn
    if gain is None:
        gain = jnp.ones((N,), F32)
    gi = lax.broadcasted_iota(jnp.int32, (tn, tn), 0) // group
    gj = lax.broadcasted_iota(jnp.int32, (tn, tn), 1) // group
    e = (gi == gj).astype(BF16)
    if time_major is None:
        out_shape = jax.ShapeDtypeStruct((T, N), F32)
        out_spec = pl.BlockSpec((tm, tn), lambda i, j: (i, j))
    else:
        B, L = time_major
        nt = L // tm
        out_shape = jax.ShapeDtypeStruct((L, B * N), F32)
        out_spec = pl.BlockSpec((tm, tn), lambda i, j: (i % nt, (i // nt) * nj + j))
    return pl.pallas_call(
        functools.partial(_norm_proj_kernel, n_norm_tiles=n_norm_tiles, group=group),
        grid=(T // tm, nj),
        in_specs=[
            pl.BlockSpec((tm, D), lambda i, j: (i, 0)),
            pl.BlockSpec((1, D), lambda i, j: (0, 0)),
            pl.BlockSpec((D, tn), lambda i, j: (0, j)),
            pl.BlockSpec((tn, tn), lambda i, j: (0, 0)),
            pl.BlockSpec((1, tn), lambda i, j: (0, j)),
        ],
        out_specs=out_spec,
        out_shape=out_shape,
        compiler_params=_cparams("parallel", "arbitrary"),
        name=name,
    )(x, g.reshape(1, D), w, e, gain.reshape(1, N))


def _od_qkv_kernel(x_ref, g_ref, w_ref, e_ref, gain_ref, o_ref, *t_refs):
    dc = H_C * HD_C
    xn = _rms(x_ref[...], g_ref[...]).astype(BF16)
    for c in range(3):
        cs = slice(c * dc, (c + 1) * dc)
        z = _dot(xn, w_ref[:, cs])
        if c < 2:
            zz = z * z
            hi = zz.astype(BF16)
            lo = (zz - hi.astype(F32)).astype(BF16)
            s = _dot(hi, e_ref[...]) + _dot(lo, e_ref[...])
            z = z * lax.rsqrt(s * (1.0 / HD_C) + EPS) * gain_ref[:, cs]
        o_ref[:, cs] = z
        if t_refs and c > 0:
            t_refs[c - 1][...] = z.T


def _od_qkv_proj(x, g, w, gain, *, tm, seq=None):
    T, D = x.shape
    dc = H_C * HD_C
    gi = lax.broadcasted_iota(jnp.int32, (dc, dc), 0) // HD_C
    gj = lax.broadcasted_iota(jnp.int32, (dc, dc), 1) // HD_C
    e = (gi == gj).astype(BF16)
    const = lambda shape: pl.BlockSpec(shape, lambda i: (0, 0), pipeline_mode=pl.Buffered(1))
    out_specs = [pl.BlockSpec((tm, 3 * dc), lambda i: (i, 0))]
    out_shape = [jax.ShapeDtypeStruct((T, 3 * dc), F32)]
    if seq is not None:
        B, L = seq
        nt = L // tm
        tspec = pl.BlockSpec((None, dc, tm), lambda i: (i // nt, 0, i % nt))
        out_specs += [tspec, tspec]
        out_shape += [jax.ShapeDtypeStruct((B, dc, L), F32)] * 2
    return pl.pallas_call(
        _od_qkv_kernel,
        grid=(T // tm,),
        in_specs=[pl.BlockSpec((tm, D), lambda i: (i, 0)), const((1, D)), const((D, 3 * dc)), const((dc, dc)),
                  const((1, 2 * dc))],
        out_specs=out_specs,
        out_shape=out_shape,
        compiler_params=_cparams("parallel"),
        name="od_in_proj_qkv",
    )(x, g.reshape(1, D), w, e, gain.reshape(1, 2 * dc))


def _proj_res_kernel(*refs, n_in):
    x_ref, o_ref = refs[0], refs[-1]
    acc = x_ref[...]
    for k in range(n_in):
        a_ref, w_ref = refs[1 + 2 * k], refs[2 + 2 * k]
        acc = acc + _dot(a_ref[...].astype(BF16), w_ref[...])
    o_ref[...] = acc


def _proj_res(x, parts, *, tm, name):
    T, N = x.shape
    in_specs = [pl.BlockSpec((tm, N), lambda i: (i, 0))]
    args = [x]
    for a, w, spec in parts:
        in_specs += [spec, pl.BlockSpec(w.shape, lambda i: (0, 0))]
        args += [a, w]
    return pl.pallas_call(
        functools.partial(_proj_res_kernel, n_in=len(parts)),
        grid=(T // tm,),
        in_specs=in_specs,
        out_specs=pl.BlockSpec((tm, N), lambda i: (i, 0)),
        out_shape=jax.ShapeDtypeStruct((T, N), F32),
        compiler_params=_cparams("parallel"),
        name=name,
    )(*args)


def _neumann_inverse(mats, n):
    eye = (lax.broadcasted_iota(jnp.int32, (n, n), 0) == lax.broadcasted_iota(jnp.int32, (n, n), 1)).astype(F32)
    ts = [eye - a for a in mats]
    ps = [a.astype(BF16) for a in mats]
    k = 2
    while k < n:
        ps = [_dot(p, p).astype(BF16) for p in ps]
        ts = [t + _dot(t.astype(BF16), p) for t, p in zip(ts, ps)]
        k *= 2
    return ts


def _even_mixer_kernel(qkv_ref, zg_ref, hq_ref, hf_ref, hi_ref, hg_ref, ba_ref,
                       conv0_ref, sg0_ref, sh0_ref,
                       convw_ref, nega_ref, dtb_ref, gnorm_ref, hnorm_ref, lb_ref,
                       mixed_ref, tail_ref, sg_ref, sh_ref,
                       xp_ref, qkvc_ref, beta_ref, g_ref,
                       *, tt, cg, ch, l_real):
    ti = pl.program_id(1)
    n_tiles = pl.num_programs(1)

    @pl.when(ti == 0)
    def _():
        xp_ref[0:SUBLANES, :] = conv0_ref[...]
        sg_ref[...] = sg0_ref[...]
        sh_ref[...] = sh0_ref[...]

    xp_ref[SUBLANES:SUBLANES + tt, :] = qkv_ref[...]
    conv = convw_ref[GDN_CONV - 1:GDN_CONV, :] * qkv_ref[...]
    for j in range(GDN_CONV - 1):
        off = SUBLANES - (GDN_CONV - 1) + j
        conv = conv + convw_ref[j:j + 1, :] * xp_ref[off:off + tt, :]
    qkvc_ref[...] = _silu(conv)
    lr_last = l_real - (l_real - 1) // tt * tt
    tail_ref[...] = xp_ref[lr_last:lr_last + SUBLANES, :]
    carry = xp_ref[tt:tt + SUBLANES, :]
    xp_ref[0:SUBLANES, :] = carry

    padded = (l_real % tt) != 0
    if padded:
        row = lax.broadcasted_iota(jnp.int32, (tt, 1), 0) + ti * tt
        valid = row < l_real
    ba = ba_ref[...]
    beta_all = jax.nn.sigmoid(ba)
    g_all = nega_ref[...] * jax.nn.softplus(ba + dtb_ref[...])
    if padded:
        beta_all = jnp.where(valid, beta_all, 0.0)
        g_all = jnp.where(valid, g_all, 0.0)
    beta_ref[...] = beta_all
    g_ref[...] = g_all

    ltri_g = _tri(cg).astype(F32)
    tril_g = _tri(cg)
    stril_g = _tri(cg, strict=True)
    gnorm = gnorm_ref[...]
    hnorm = hnorm_ref[...]

    ng = min(GDN_GROUP, tt // cg)

    def gdn_group(gi, _):
        heads = range(H_A)
        items = [(c, h) for c in range(ng) for h in heads]
        rows_c, gcum_c, gt_c, bet_c = [], [], [], []
        for c in range(ng):
            rows = pl.ds(pl.multiple_of((gi * ng + c) * cg, cg), cg)
            gcum = _dot(ltri_g, g_ref[rows, :], HI)
            gpad = gcum if cg == LANES else jnp.concatenate([gcum, jnp.zeros((LANES - cg, LANES), F32)], axis=0)
            rows_c.append(rows)
            gcum_c.append(gcum)
            gt_c.append(gpad.T)
            bet_c.append(beta_ref[rows, :])
        qs, ks, vs, betas, gcs, decays, glasts = {}, {}, {}, {}, {}, {}, {}
        for it in items:
            c, h = it
            rows = rows_c[c]
            q = qkvc_ref[rows, h * HD_A:(h + 1) * HD_A]
            k = qkvc_ref[rows, D_A + h * HD_A:D_A + (h + 1) * HD_A]
            qs[it] = q * lax.rsqrt(jnp.sum(q * q, axis=-1, keepdims=True) + EPS) * (HD_A ** -0.5)
            ks[it] = k * lax.rsqrt(jnp.sum(k * k, axis=-1, keepdims=True) + EPS)
            vs[it] = qkvc_ref[rows, 2 * D_A + h * HD_A:2 * D_A + (h + 1) * HD_A]
            betas[it] = bet_c[c][:, h:h + 1]
            gc = gcum_c[c][:, H_A + h:H_A + h + 1]
            gr = gt_c[c][H_A + h:H_A + h + 1, 0:cg]
            gcs[it] = gc
            glasts[it] = gcum_c[c][cg - 1:cg, H_A + h:H_A + h + 1]
            decays[it] = jnp.exp(jnp.where(tril_g, gc - gr, -jnp.inf))
        kqs = {it: _dot_nt(jnp.concatenate([ks[it], qs[it]], axis=0).astype(BF16), ks[it].astype(BF16))
               for it in items}
        ts = dict(zip(items, _neumann_inverse(
            [jnp.where(stril_g, betas[it] * kqs[it][0:cg] * decays[it], 0.0) for it in items], cg)))
        egs = {it: jnp.exp(gcs[it]) for it in items}
        sols = {it: _dot(ts[it].astype(BF16),
                         jnp.concatenate([betas[it] * vs[it], (betas[it] * egs[it]) * ks[it]], axis=-1).astype(BF16))
                for it in items}
        wqs = {it: jnp.concatenate([sols[it][:, HD_A:2 * HD_A], qs[it] * egs[it]], axis=0).astype(BF16)
               for it in items}
        qks = {it: (kqs[it][cg:2 * cg] * decays[it]).astype(BF16) for it in items}
        kouts = {it: (ks[it] * jnp.exp(glasts[it] - gcs[it])).astype(BF16) for it in items}
        ss = [sg_ref[h] for h in heads]
        for c in range(ng):
            wss = [_dot(wqs[(c, h)], ss[h].astype(BF16)) for h in heads]
            ubs = [(sols[(c, h)][:, 0:HD_A] - wss[h][0:cg]).astype(BF16) for h in heads]
            os_ = [wss[h][cg:2 * cg] + _dot(qks[(c, h)], ubs[h]) for h in heads]
            ss = [ss[h] * jnp.exp(glasts[(c, h)]) + _dot_tn(kouts[(c, h)], ubs[h]) for h in heads]
            outs = [_rms(os_[h], gnorm) * _silu(zg_ref[rows_c[c], h * HD_A:(h + 1) * HD_A]) for h in heads]
            for p in range(H_A // 2):
                mixed_ref[rows_c[c], p * LANES:(p + 1) * LANES] = jnp.concatenate(
                    [outs[2 * p], outs[2 * p + 1]], axis=-1)
        for h in heads:
            sg_ref[h] = ss[h]
        return 0

    lax.fori_loop(0, tt // (cg * ng), gdn_group, 0)

    nh = min(HGRN_GROUP, tt // ch)
    gr = nh * ch
    ri = lax.broadcasted_iota(jnp.int32, (gr, gr), 0)
    cj = lax.broadcasted_iota(jnp.int32, (gr, gr), 1)
    same_chunk_causal = (ri >= cj) & ((ri // ch) == (cj // ch))
    ltri_h = same_chunk_causal.astype(F32)
    lb = lb_ref[...]
    r_mid = (ch - 1) // 2

    def hgrn_group(gi, _):
        r0 = pl.multiple_of(gi * gr, gr)
        rows = pl.ds(r0, gr)
        f = lb + (1.0 - lb) * jax.nn.sigmoid(hf_ref[rows, :])
        logf = jnp.log(f)
        kk = 1.0 - f
        if padded:
            vrow = (lax.broadcasted_iota(jnp.int32, (gr, 1), 0) + r0 + ti * tt) < l_real
            logf = jnp.where(vrow, logf, 0.0)
            kk = jnp.where(vrow, kk, 0.0)
        qh = _silu(hq_ref[rows, :])
        bc = _dot(ltri_h, logf, HI).reshape(nh, ch, D_A)
        bref = bc[:, r_mid:r_mid + 1, :]
        blast = bc[:, ch - 1:ch, :]
        grp = lambda a: a.reshape(gr, D_A)
        qh3, kk3 = qh.reshape(nh, ch, D_A), kk.reshape(nh, ch, D_A)
        qa = grp(qh3 * jnp.exp(bc - bref)).astype(BF16)
        kb = grp(kk3 * jnp.exp(bref - bc)).astype(BF16)
        q_in = grp(qh3 * jnp.exp(bc)).astype(BF16)
        k_out = grp(kk3 * jnp.exp(blast - bc)).astype(BF16)
        fl = jnp.exp(blast)
        vv = hi_ref[rows, :].astype(BF16)
        heads = range(H_B)
        sls = [slice(h * DK_B, (h + 1) * DK_B) for h in heads]
        atts = [jnp.where(same_chunk_causal, _dot_nt(qa[:, sl], kb[:, sl]), 0.0).astype(BF16) for sl in sls]
        intra = [_dot(atts[h], vv[:, sls[h]]) for h in heads]
        sts = [sh_ref[h] for h in heads]
        inter = [[] for _ in heads]
        for c in range(nh):
            rc = slice(c * ch, (c + 1) * ch)
            for h in heads:
                inter[h].append(_dot_nt(q_in[rc, sls[h]], sts[h].astype(BF16)))
            sts = [sts[h] * fl[c][:, sls[h]] + _dot_tn(vv[rc, sls[h]], k_out[rc, sls[h]]) for h in heads]
        for h in heads:
            sh_ref[h] = sts[h]
        os_ = [intra[h] + (inter[h][0] if nh == 1 else jnp.concatenate(inter[h], axis=0)) for h in heads]
        outs = [_rms(os_[h], hnorm) * _silu(hg_ref[rows, h * DV_B:(h + 1) * DV_B]) for h in heads]
        for p in range(H_B // 2):
            c0 = D_A + p * LANES
            mixed_ref[rows, c0:c0 + LANES] = jnp.concatenate([outs[2 * p], outs[2 * p + 1]], axis=-1)
        return 0

    lax.fori_loop(0, tt // gr, hgrn_group, 0)


def _even_mixer(z, conv0, sg0, sh0_t, conv_w, neg_a, dt_bias, gnorm, hnorm, lb, *, l_real, tt):
    B, Lp, _ = z.shape
    cg = min(GDN_CHUNK, tt)
    ch = min(HGRN_CHUNK, tt)
    d3 = 3 * D_A
    nt = Lp // tt
    col = lambda w, idx: pl.BlockSpec((None, tt, w), lambda b, i, idx=idx: (b, i, idx))
    state = lambda: pl.BlockSpec((None, H_A, HD_A, HD_A), lambda b, i: (b, 0, 0, 0))
    vec = lambda n: pl.BlockSpec((1, n), lambda b, i: (0, 0))
    nega_row = jnp.zeros((1, LANES), F32).at[0, H_A:2 * H_A].set(neg_a)
    dtb_row = jnp.zeros((1, LANES), F32).at[0, H_A:2 * H_A].set(dt_bias)
    return pl.pallas_call(
        functools.partial(_even_mixer_kernel, tt=tt, cg=cg, ch=ch, l_real=l_real),
        grid=(B, nt),
        in_specs=[
            col(d3, 0), col(D_A, 3), col(D_A, 4), col(D_A, 5), col(D_A, 6), col(D_A, 7), col(LANES, 32),
            pl.BlockSpec((None, SUBLANES, d3), lambda b, i: (b, 0, 0)),
            state(), state(),
            pl.BlockSpec((GDN_CONV, d3), lambda b, i: (0, 0)),
            vec(LANES), vec(LANES), vec(HD_A), vec(DV_B), vec(D_A),
        ],
        out_specs=[
            pl.BlockSpec((None, tt, 2 * D_A), lambda b, i: (b, i, 0)),
            pl.BlockSpec((None, SUBLANES, d3), lambda b, i: (b, 0, 0)),
            state(), state(),
        ],
        out_shape=[
            jax.ShapeDtypeStruct((B, Lp, 2 * D_A), F32),
            jax.ShapeDtypeStruct((B, SUBLANES, d3), F32),
            jax.ShapeDtypeStruct((B, H_A, HD_A, HD_A), F32),
            jax.ShapeDtypeStruct((B, H_B, DV_B, DK_B), F32),
        ],
        scratch_shapes=[
            pltpu.VMEM((tt + SUBLANES, d3), F32),
            pltpu.VMEM((tt, d3), F32),
            pltpu.VMEM((tt, LANES), F32),
            pltpu.VMEM((tt, LANES), F32),
        ],
        compiler_params=_cparams("parallel", "arbitrary"),
        name="even_mixer",
    )(z, z, z, z, z, z, z, conv0, sg0, sh0_t, conv_w, nega_row, dtb_row,
      gnorm.reshape(1, HD_A), hnorm.reshape(1, DV_B), lb.reshape(1, D_A))


def _mem_attn_rows(x, mk_ref, mv_ref, g_ref, wq_ref, gq_ref, wo_ref):
    q = _dot(_rms(x, g_ref[...]).astype(BF16), wq_ref[...])
    scale = MEM_HD ** -0.5
    outs = []
    for h in range(MEM_H):
        sl = slice(h * MEM_HD, (h + 1) * MEM_HD)
        qh = _rms(q[:, sl], gq_ref[...])
        s = _dot_nt(qh.astype(BF16), mk_ref[:, sl].astype(BF16)) * scale
        s = s - jnp.max(s, axis=-1, keepdims=True)
        p = jnp.exp(s)
        p = p / jnp.sum(p, axis=-1, keepdims=True)
        outs.append(_dot(p.astype(BF16), mv_ref[:, sl].astype(BF16)))
    o = jnp.concatenate(outs, axis=-1)
    return x + _dot(o.astype(BF16), wo_ref[...])


def _mem_attn_step_kernel(q_ref, mk_ref, mv_ref, o_ref):
    scale = MEM_HD ** -0.5
    outs = []
    for h in range(MEM_H):
        qh = jnp.broadcast_to(q_ref[:, h * MEM_HD:(h + 1) * MEM_HD], (SUBLANES, MEM_HD))
        s = _dot_nt(qh.astype(BF16), mk_ref[:, h, :].astype(BF16)) * scale
        s = s - jnp.max(s, axis=-1, keepdims=True)
        p = jnp.exp(s)
        p = p / jnp.sum(p, axis=-1, keepdims=True)
        outs.append(_dot(p.astype(BF16), mv_ref[:, h, :].astype(BF16))[0:1])
    o_ref[...] = jnp.concatenate(outs, axis=-1)


def _mem_attn_step(q, cache_k, cache_v, l):
    B = q.shape[0]
    M = cache_k.shape[2]
    dq = MEM_H * MEM_HD
    mem = pl.BlockSpec((None, None, M, MEM_H, MEM_HD), lambda b: (l, b, 0, 0, 0))
    row = pl.BlockSpec((None, 1, dq), lambda b: (b, 0, 0))
    return pl.pallas_call(
        _mem_attn_step_kernel,
        grid=(B,),
        in_specs=[row, mem, mem],
        out_specs=row,
        out_shape=jax.ShapeDtypeStruct((B, 1, dq), F32),
        compiler_params=_cparams("parallel"),
        name="mem_attn_step",
    )(q, cache_k, cache_v)


def _post_seq_kernel(*refs, n_parts, tm):
    x_ref = refs[0]
    parts = refs[1:1 + 2 * n_parts]
    (mk_ref, mv_ref, gmem_ref, wq_ref, gq_ref, wo_ref, buf0_ref, g_ref, wgu_ref, cw_ref, cb_ref, wd_ref,
     o_ref, tail_ref, gp_ref, carry_ref) = refs[1 + 2 * n_parts:]
    ti = pl.program_id(1)

    @pl.when(ti == 0)
    def _():
        carry_ref[...] = buf0_ref[...]

    x = x_ref[...]
    for k in range(n_parts):
        x = x + _dot(parts[2 * k][...].astype(BF16), parts[2 * k + 1][...])
    x = _mem_attn_rows(x, mk_ref, mv_ref, gmem_ref, wq_ref, gq_ref, wo_ref)
    xn = _rms(x, g_ref[...]).astype(BF16)
    acc = x
    for c in range(D_FF // FF_CHUNK):
        cs = slice(c * FF_CHUNK, (c + 1) * FF_CHUNK)
        gate = _dot(xn, wgu_ref[:, cs])
        up = _dot(xn, wgu_ref[:, D_FF + c * FF_CHUNK:D_FF + (c + 1) * FF_CHUNK])
        gp_ref[0:SUBLANES, :] = carry_ref[:, cs]
        gp_ref[SUBLANES:SUBLANES + tm, :] = gate
        conv = cw_ref[FFN_CONV - 1:FFN_CONV, cs] * gate + cb_ref[:, cs]
        for j in range(FFN_CONV - 1):
            off = SUBLANES - (FFN_CONV - 1) + j
            conv = conv + cw_ref[j:j + 1, cs] * gp_ref[off:off + tm, :]
        last = gp_ref[tm:tm + SUBLANES, :]
        carry_ref[:, cs] = last
        tail_ref[:, cs] = last
        act = _silu(conv) * up
        acc = acc + _dot(act.astype(BF16), wd_ref[cs, :])
    o_ref[...] = acc


def _post_seq(x, parts, mk, mv, gmem, wq, gq, wo, buf0, g, wgu, cw, cb, wd, *, tm, name):
    B, L, D = x.shape
    M = mk.shape[1]
    dq = MEM_H * MEM_HD
    const = lambda shape: pl.BlockSpec(shape, lambda b, i: (0,) * len(shape), pipeline_mode=pl.Buffered(1))
    row_tile = pl.BlockSpec((None, tm, D), lambda b, i: (b, i, 0))
    mem = pl.BlockSpec((None, M, dq), lambda b, i: (b, 0, 0))
    in_specs, args = [row_tile], [x]
    for a, w, spec in parts:
        in_specs += [spec, const(w.shape)]
        args += [a, w]
    in_specs += [mem, mem, const((1, D)), const((D, dq)), const((1, MEM_HD)), const((dq, D)),
                 pl.BlockSpec((None, SUBLANES, D_FF), lambda b, i: (b, 0, 0)),
                 const((1, D)), const((D, 2 * D_FF)), const((FFN_CONV, D_FF)), const((1, D_FF)), const((D_FF, D))]
    args += [mk, mv, gmem.reshape(1, D), wq, gq.reshape(1, MEM_HD), wo, buf0, g.reshape(1, D), wgu, cw,
             cb.reshape(1, D_FF), wd]
    return pl.pallas_call(
        functools.partial(_post_seq_kernel, n_parts=len(parts), tm=tm),
        grid=(B, L // tm),
        in_specs=in_specs,
        out_specs=[row_tile, pl.BlockSpec((None, SUBLANES, D_FF), lambda b, i: (b, 0, 0))],
        out_shape=[jax.ShapeDtypeStruct((B, L, D), F32), jax.ShapeDtypeStruct((B, SUBLANES, D_FF), F32)],
        scratch_shapes=[pltpu.VMEM((tm + SUBLANES, FF_CHUNK), F32), pltpu.VMEM((SUBLANES, D_FF), F32)],
        compiler_params=_cparams("parallel", "arbitrary"),
        name=name,
    )(*args)


def _ffn_step_kernel(x_ref, p2_ref, p1_ref, g_ref, wgu_ref, cw_ref, cb_ref, wd_ref, o_ref, gate_ref):
    x = x_ref[...]
    xn = _rms(x, g_ref[...]).astype(BF16)
    acc = x
    for c in range(D_FF // FF_CHUNK):
        cs = slice(c * FF_CHUNK, (c + 1) * FF_CHUNK)
        gate = _dot(xn, wgu_ref[:, cs])
        up = _dot(xn, wgu_ref[:, D_FF + c * FF_CHUNK:D_FF + (c + 1) * FF_CHUNK])
        gate_ref[:, cs] = gate
        conv = cw_ref[0:1, cs] * p2_ref[:, cs] + cw_ref[1:2, cs] * p1_ref[:, cs] + cw_ref[2:3, cs] * gate + cb_ref[:, cs]
        act = _silu(conv) * up
        acc = acc + _dot(act.astype(BF16), wd_ref[cs, :])
    o_ref[...] = acc


def _ffn_step(x, p2, p1, g, wgu, cw, cb, wd):
    R, D = x.shape
    full = lambda a: pl.BlockSpec(a.shape, lambda i: (0,) * a.ndim)
    args = (x, p2, p1, g.reshape(1, D), wgu, cw, cb.reshape(1, D_FF), wd)
    return pl.pallas_call(
        _ffn_step_kernel,
        grid=(1,),
        in_specs=[full(a) for a in args],
        out_specs=[pl.BlockSpec((R, D), lambda i: (0, 0)), pl.BlockSpec((R, D_FF), lambda i: (0, 0))],
        out_shape=[jax.ShapeDtypeStruct((R, D), F32), jax.ShapeDtypeStruct((R, D_FF), F32)],
        compiler_params=_cparams("arbitrary"),
        name="conv_ffn_step",
    )(*args)


def _moba_query_tile(j, h, qt_ref, kb_ref, vt_ref, kmean_ref, bias_ref, ot_ref):
    blk = MOBA_BLOCK
    w = (j + 1) * blk
    qt = qt_ref[h]
    st = _dot(kb_ref[h, 0:w, :], (qt * (HD_C ** -0.5)).astype(BF16))
    if j > MOBA_TOPK:
        g = _dot(kmean_ref[h, 0:j, :], qt, HI)
        iota_n = lax.broadcasted_iota(jnp.int32, (j, blk), 0).astype(F32)
        sel = jnp.zeros((j, blk), F32)
        for _ in range(MOBA_TOPK):
            m = jnp.max(g, axis=0, keepdims=True)
            idx = jnp.min(jnp.where(g == m, iota_n, 1e9), axis=0, keepdims=True)
            hit = iota_n == idx
            sel = jnp.where(hit, 1.0, sel)
            g = jnp.where(hit, -jnp.inf, g)
    key_le_query = lax.broadcasted_iota(jnp.int32, (blk, blk), 0) <= lax.broadcasted_iota(jnp.int32, (blk, blk), 1)
    parts = []
    for n in range(j + 1):
        s = st[n * blk:(n + 1) * blk, :]
        if j - n < 2:
            s = s + bias_ref[h, j - n]
        if n == j:
            s = jnp.where(key_le_query, s, -jnp.inf)
        elif j > MOBA_TOPK:
            s = jnp.where(sel[n:n + 1, :] > 0.5, s, -jnp.inf)
        parts.append(s)
    m = jnp.max(parts[0], axis=0, keepdims=True)
    for s in parts[1:]:
        m = jnp.maximum(m, jnp.max(s, axis=0, keepdims=True))
    ps = [jnp.exp(s - m) for s in parts]
    l = jnp.sum(ps[0], axis=0, keepdims=True)
    for p in ps[1:]:
        l = l + jnp.sum(p, axis=0, keepdims=True)
    pcat = ps[0] if j == 0 else jnp.concatenate(ps, axis=0)
    ot_ref[h] = _dot(vt_ref[h, :, 0:w], pcat.astype(BF16)) / l


def _moba_seq_kernel(q_ref, k_ref, vtin_ref, bias_ref, o_ref, kmean_ref, kb_ref, vt_ref, qt_ref, ot_ref, *, nb):
    j = pl.program_id(1)
    blk = MOBA_BLOCK
    dc = H_C * HD_C

    @pl.when(j == 0)
    def _():
        for n in range(nb):
            km = jnp.mean(k_ref[n * blk:(n + 1) * blk, :], axis=0, keepdims=True)
            for h in range(H_C):
                kmean_ref[h, n:n + 1, :] = km[:, h * HD_C:(h + 1) * HD_C]

        def prep(n, _):
            rows = pl.ds(pl.multiple_of(n * blk, blk), blk)
            kblk = k_ref[rows, :]
            vt_ref[:, :, rows] = vtin_ref[:, rows].astype(BF16).reshape(H_C, HD_C, blk)
            for h in range(H_C):
                kb_ref[h, rows, :] = kblk[:, h * HD_C:(h + 1) * HD_C].astype(BF16)
            return 0
        lax.fori_loop(0, nb, prep, 0)

    qt_ref[...] = q_ref[...].T.reshape(H_C, HD_C, blk)
    for jj in range(nb):
        @pl.when(j == jj)
        def _(jj=jj):
            def head(h, _):
                _moba_query_tile(jj, h, qt_ref, kb_ref, vt_ref, kmean_ref, bias_ref, ot_ref)
                return 0
            lax.fori_loop(0, H_C, head, 0)
    o_ref[...] = ot_ref[...].reshape(dc, blk).T


def _moba_seq(qkv, vt, bias_tiles):
    B, L, _ = qkv.shape
    nb = L // MOBA_BLOCK
    dc = H_C * HD_C
    return pl.pallas_call(
        functools.partial(_moba_seq_kernel, nb=nb),
        grid=(B, nb),
        in_specs=[
            pl.BlockSpec((None, MOBA_BLOCK, dc), lambda b, j: (b, j, 0)),
            pl.BlockSpec((None, L, dc), lambda b, j: (b, 0, 1)),
            pl.BlockSpec((None, dc, L), lambda b, j: (b, 0, 0)),
            pl.BlockSpec(bias_tiles.shape, lambda b, j: (0, 0, 0, 0), pipeline_mode=pl.Buffered(1)),
        ],
        out_specs=pl.BlockSpec((None, MOBA_BLOCK, dc), lambda b, j: (b, j, 0)),
        out_shape=jax.ShapeDtypeStruct((B, L, dc), F32),
        scratch_shapes=[pltpu.VMEM((H_C, nb, HD_C), F32), pltpu.VMEM((H_C, L, HD_C), BF16),
                        pltpu.VMEM((H_C, HD_C, L), BF16), pltpu.VMEM((H_C, HD_C, MOBA_BLOCK), F32),
                        pltpu.VMEM((H_C, HD_C, MOBA_BLOCK), F32)],
        compiler_params=_cparams("parallel", "arbitrary"),
        name="moba_seq",
    )(qkv, qkv, vt, bias_tiles)


MOBA_PAGES_PER_BLOCK = MOBA_BLOCK // PAGE_SIZE
MOBA_STEP_PAGES = 8


def _moba_score_kernel(pt_ref, q_ref, *refs, n_pages):
    del pt_ref
    ppg, ppb = MOBA_STEP_PAGES, MOBA_PAGES_PER_BLOCK
    kp_refs = refs[:ppg]
    s_ref, sel_ref, gate_ref = refs[ppg:]
    step = pl.program_id(1)
    nbp = n_pages // ppb
    q = q_ref[...]
    inv = 1.0 / (MOBA_BLOCK * HD_C ** -0.5)
    gsum = None
    for r in range(ppg):
        sc = jnp.sum(kp_refs[r][...] * q, axis=1)
        s_ref[:, r * PAGE_SIZE:(r + 1) * PAGE_SIZE] = sc
        gs = jnp.sum(sc, axis=-1, keepdims=True)
        gsum = gs if r % ppb == 0 else gsum + gs
        if r % ppb == ppb - 1:
            gate_ref[step * (ppg // ppb) + r // ppb] = jnp.broadcast_to(gsum * inv, (H_C, LANES))

    @pl.when(step == n_pages // ppg - 1)
    def _():
        g = gate_ref[...]
        iota_b = lax.broadcasted_iota(jnp.int32, (nbp, H_C, LANES), 0).astype(F32)
        for i in range(MOBA_TOPK):
            m = jnp.max(g, axis=0, keepdims=True)
            idx = jnp.min(jnp.where(g == m, iota_b, 1e9), axis=0, keepdims=True)
            sel_ref[i] = idx[0]
            g = jnp.where(iota_b == idx, -jnp.inf, g)


def _moba_attend_kernel(pt_ref, sel_ref, q_ref, kn_ref, vn_ref, blast_ref, bsc_ref, *refs, nbp):
    del pt_ref
    nsel, ppb = MOBA_TOPK, MOBA_PAGES_PER_BLOCK
    s_refs, v_refs, o_ref = refs[:nsel], refs[nsel:nsel + nsel * ppb], refs[-1]
    b, h = pl.program_id(0), pl.program_id(1)
    far = bsc_ref[h, 0]
    s_new = jnp.sum(q_ref[...] * kn_ref[...], axis=-1, keepdims=True) * (HD_C ** -0.5) + bsc_ref[h, 1]
    ss = []
    for i in range(nsel):
        blk = sel_ref[(b * H_C + h) * nsel + i]
        ss.append(s_refs[i][...] + jnp.where(blk == nbp - 1, blast_ref[...], far))
    m = s_new
    for s in ss:
        m = jnp.maximum(m, jnp.max(s, axis=-1, keepdims=True))
    w_new = jnp.exp(s_new - m)
    den = w_new
    num = w_new * vn_ref[...]
    for i in range(nsel):
        e = jnp.exp(ss[i] - m)
        den = den + jnp.sum(e, axis=-1, keepdims=True)
        eb = jnp.broadcast_to(e, (SUBLANES, MOBA_BLOCK)).astype(BF16)
        for r in range(ppb):
            pv = _dot_nt(eb[:, r * PAGE_SIZE:(r + 1) * PAGE_SIZE], v_refs[i * ppb + r][...].astype(BF16))
            num = num + pv[0:1]
    o_ref[...] = num / den


def _moba_step(q, k_new, v_new, cache_k, cache_v, page_table, bias_last, bias_scalars):
    B = q.shape[0]
    n_pages = page_table.shape[1]
    ppg, ppb, nsel = MOBA_STEP_PAGES, MOBA_PAGES_PER_BLOCK, MOBA_TOPK
    assert n_pages % ppg == 0 and ppg % ppb == 0
    nbp = n_pages // ppb
    assert nbp >= nsel
    pt = page_table.reshape(-1)
    q_rep = jnp.broadcast_to((q * (HD_C ** -0.5))[..., None], (B, H_C, HD_C, LANES))
    kpage = lambda r: pl.BlockSpec((None, H_C, HD_C, PAGE_SIZE),
                                   lambda b, s, pt, r=r: (pt[b * n_pages + s * ppg + r], 0, 0, 0))
    scores, sel = pl.pallas_call(
        functools.partial(_moba_score_kernel, n_pages=n_pages),
        grid_spec=pltpu.PrefetchScalarGridSpec(
            num_scalar_prefetch=1,
            grid=(B, n_pages // ppg),
            in_specs=[pl.BlockSpec((None, H_C, HD_C, LANES), lambda b, s, pt: (b, 0, 0, 0))]
                     + [kpage(r) for r in range(ppg)],
            out_specs=[pl.BlockSpec((None, H_C, ppg * PAGE_SIZE), lambda b, s, pt: (b, 0, s)),
                       pl.BlockSpec((None, nsel, H_C, LANES), lambda b, s, pt: (b, 0, 0, 0))],
            scratch_shapes=[pltpu.VMEM((nbp, H_C, LANES), F32)],
        ),
        out_shape=[jax.ShapeDtypeStruct((B, H_C, n_pages * PAGE_SIZE), F32),
                   jax.ShapeDtypeStruct((B, nsel, H_C, LANES), F32)],
        compiler_params=_cparams("parallel", "arbitrary"),
        name="moba_step_score",
    )(pt, q_rep, *([cache_k] * ppg))
    sel_flat = jnp.swapaxes(sel[:, :, :, 0], 1, 2).astype(jnp.int32).reshape(-1)
    tok = pl.BlockSpec((None, None, 1, HD_C), lambda b, h, pt, sl: (b, h, 0, 0))
    srow = lambda i: pl.BlockSpec((None, None, None, 1, MOBA_BLOCK),
                                  lambda b, h, pt, sl, i=i: (b, h, sl[(b * H_C + h) * nsel + i], 0, 0))
    vpage = lambda i, r: pl.BlockSpec(
        (None, None, HD_C, PAGE_SIZE),
        lambda b, h, pt, sl, i=i, r=r: (pt[b * n_pages + sl[(b * H_C + h) * nsel + i] * ppb + r], h, 0, 0))
    row = lambda a: a.reshape(B, H_C, 1, HD_C)
    out = pl.pallas_call(
        functools.partial(_moba_attend_kernel, nbp=nbp),
        grid_spec=pltpu.PrefetchScalarGridSpec(
            num_scalar_prefetch=2,
            grid=(B, H_C),
            in_specs=[tok, tok, tok,
                      pl.BlockSpec((None, 1, MOBA_BLOCK), lambda b, h, pt, sl: (h, 0, 0)),
                      pl.BlockSpec(memory_space=pltpu.SMEM)]
                     + [srow(i) for i in range(nsel)]
                     + [vpage(i, r) for i in range(nsel) for r in range(ppb)],
            out_specs=tok,
        ),
        out_shape=jax.ShapeDtypeStruct((B, H_C, 1, HD_C), F32),
        compiler_params=_cparams("parallel", "arbitrary"),
        name="moba_step_attend",
    )(pt, sel_flat, row(q), row(k_new), row(v_new), bias_last, bias_scalars,
      *([scores.reshape(B, H_C, nbp, 1, MOBA_BLOCK)] * nsel), *([cache_v] * (nsel * ppb)))
    return out.reshape(B, H_C, HD_C)


def _s5_kernel(u_ref, h0r_ref, h0i_ref, wb_ref, ar_ref, ai_ref, wc_ref, d_ref, wglu_ref,
               o_ref, hr_ref, hi_ref, x_ref, *, tt, nb):
    ti = pl.program_id(0)
    ns = D_S5_STATE

    @pl.when(ti == 0)
    def _():
        hr_ref[...] = h0r_ref[...]
        hi_ref[...] = h0i_ref[...]

    nslab = ns // LANES
    if tt == 1:
        seqs, seq_rows, mat_rows, cols = [0], lambda b: slice(None), lambda b: slice(None), lambda b: slice(None)
        u = u_ref[...]
    else:
        seqs = range(nb)
        seq_rows = lambda b: pl.ds(b, tt, stride=nb)
        mat_rows = lambda b: slice(b * tt, (b + 1) * tt)
        cols = lambda b: slice(b * D_S5, (b + 1) * D_S5)
        u = jnp.concatenate([u_ref[:, cols(b)] for b in seqs], axis=0)
    x = _dot(u.astype(BF16), wb_ref[...])
    for b in seqs:
        for c in range(2 * nslab):
            x_ref[c, seq_rows(b), :] = x[mat_rows(b), c * LANES:(c + 1) * LANES]
    slabs = lambda ref: jnp.stack([ref[:, c * LANES:(c + 1) * LANES] for c in range(nslab)])
    ar = slabs(ar_ref)
    ai = slabs(ai_ref)

    def step(t, carry):
        hr, hi = carry
        rows = pl.ds(pl.multiple_of(t * nb, nb), nb)
        nr = ar * hr - ai * hi + x_ref[0:nslab, rows, :]
        ni = ar * hi + ai * hr + x_ref[nslab:2 * nslab, rows, :]
        x_ref[0:nslab, rows, :] = nr
        x_ref[nslab:2 * nslab, rows, :] = ni
        return nr, ni

    hr, hi = lax.fori_loop(0, tt, step, (slabs(hr_ref), slabs(hi_ref)))
    for c in range(nslab):
        hr_ref[:, c * LANES:(c + 1) * LANES] = hr[c]
        hi_ref[:, c * LANES:(c + 1) * LANES] = hi[c]

    hs = [jnp.concatenate([x_ref[c, seq_rows(b), :] for c in range(2 * nslab)], axis=-1) for b in seqs]
    h = hs[0] if len(hs) == 1 else jnp.concatenate(hs, axis=0)
    y = jax.nn.gelu(_dot(h.astype(BF16), wc_ref[...]) + d_ref[...] * u)
    gl = _dot(y.astype(BF16), wglu_ref[...])
    o = gl[:, 0:D_S5] * jax.nn.sigmoid(gl[:, D_S5:2 * D_S5])
    for b in seqs:
        o_ref[:, cols(b)] = o[mat_rows(b), :]


def _s5(u, h0r, h0i, wb, ar, ai, wc, d, wglu, *, tt, nb):
    L = u.shape[0]
    const = lambda a: pl.BlockSpec(a.shape, lambda i: (0,) * a.ndim)
    rows = pl.BlockSpec((tt, nb * D_S5), lambda i: (i, 0))
    st = pl.BlockSpec((nb, D_S5_STATE), lambda i: (0, 0))
    args = (u, h0r, h0i, wb, ar, ai, wc, d, wglu)
    if tt == 1:
        assert L == 1
        rows = pl.BlockSpec((nb, D_S5), lambda i: (0, 0))
        args = (u.reshape(nb, D_S5),) + args[1:]
    return pl.pallas_call(
        functools.partial(_s5_kernel, tt=tt, nb=nb),
        grid=(L // tt,),
        in_specs=[rows] + [const(a) for a in args[1:]],
        out_specs=[rows, st, st],
        out_shape=[jax.ShapeDtypeStruct(args[0].shape, F32),
                   jax.ShapeDtypeStruct((nb, D_S5_STATE), F32), jax.ShapeDtypeStruct((nb, D_S5_STATE), F32)],
        scratch_shapes=[pltpu.VMEM((2 * D_S5_STATE // LANES, tt * nb, LANES), F32)],
        compiler_params=_cparams("arbitrary"),
        name="s5",
    )(*args)


def _rel_bucket(dist):
    n = jnp.maximum(dist, 0)
    exact = N_BUCKETS // 2
    nf = jnp.maximum(n, exact).astype(F32)
    large = exact + (jnp.log(nf / exact) / math.log(MAX_DISTANCE / exact) * (N_BUCKETS - exact)).astype(jnp.int32)
    return jnp.where(n < exact, n, jnp.minimum(large, N_BUCKETS - 1))


def _block_diag(w):
    G, a, b = w.shape
    eye = jnp.eye(G, dtype=w.dtype)
    return (eye[:, None, :, None] * w[:, :, None, :]).reshape(G * a, G * b)


def _s5_weights(A_re, A_im, log_dt, B_re, B_im, C_re, C_im, D):
    dt = jnp.exp(log_dt.astype(F32))[:, None]
    lr, li = A_re.astype(F32), A_im.astype(F32)
    mag = jnp.exp(lr * dt)
    ar, ai = mag * jnp.cos(li * dt), mag * jnp.sin(li * dt)
    den = lr * lr + li * li
    nr = ar - 1.0
    cr, ci = (nr * lr + ai * li) / den, (ai * lr - nr * li) / den
    bbr = cr[..., None] * B_re - ci[..., None] * B_im
    bbi = cr[..., None] * B_im + ci[..., None] * B_re
    wb = jnp.concatenate([_block_diag(bbr.transpose(0, 2, 1)), _block_diag(bbi.transpose(0, 2, 1))], axis=1)
    wc = jnp.concatenate([_block_diag(C_re.transpose(0, 2, 1)), -_block_diag(C_im.transpose(0, 2, 1))], axis=0)
    return (wb.astype(BF16), ar.reshape(1, -1), ai.reshape(1, -1), wc.astype(BF16), D.reshape(1, -1).astype(F32))


def _pad_rows_front(a, rows):
    return jnp.pad(a, ((0, 0), (rows - a.shape[1], 0), (0, 0)))


def _trunk(x, st, W, *, decode):
    B, L, D = x.shape
    T = B * L
    tm = min(512, T)
    out = {}

    x2 = x.reshape(T, D)
    z = _norm_proj(x2, W['norm_mix'][0], W['ev_w_in'], tm=tm, tn=EV_IN_TILE, name="ev_in_proj")
    zc = z.shape[1]
    if decode:
        lp = SUBLANES
        z3 = jnp.pad(z.reshape(B, L, zc), ((0, 0), (0, lp - L), (0, 0)))
        tt = lp
    else:
        lp = L
        z3 = z.reshape(B, L, zc)
        tt = min(512, L)
    conv0 = _pad_rows_front(st['gdn_conv'], SUBLANES)
    mixed, tail, sg, sh_t = _even_mixer(z3, conv0, st['gdn'], jnp.swapaxes(st['hgrn'], -1, -2), W['gdn_conv_w'],
                                        W['gdn_neg_a'], W['gdn_dt_bias'], W['gdn_norm'], W['hgrn_norm'], W['hgrn_lb'],
                                        l_real=L, tt=tt)
    out['gdn_conv'] = tail[:, SUBLANES - (GDN_CONV - 1):, :]
    out['gdn'] = sg
    out['hgrn'] = jnp.swapaxes(sh_t, -1, -2)
    if decode:
        parts = [(mixed[:, :L].reshape(T, 2 * D_A), W['ev_w_out'], pl.BlockSpec((tm, 2 * D_A), lambda i: (i, 0)))]
    else:
        parts = [(mixed, W['ev_w_out'], pl.BlockSpec((None, tm, 2 * D_A), lambda b, i: (b, i, 0)))]
    x2, fs = _post_mixer(x2, parts, st, W, 0, B, L, decode)
    out['ffn0'] = fs

    dc = H_C * HD_C
    gqk = jnp.concatenate([jnp.tile(W['moba_gq'], H_C), jnp.tile(W['moba_gk'], H_C)])
    u_tm = _norm_proj(x2, W['norm_mix'][1], W['od_w_in'][:, 3 * dc:], tm=tm, tn=D_S5,
                      time_major=None if decode else (B, L), name="od_in_proj_u")
    u_tm = u_tm.reshape(L, B * D_S5)
    if decode:
        (qkv,) = _od_qkv_proj(x2, W['norm_mix'][1], W['od_w_in'][:, :3 * dc], gqk, tm=tm)
        out['moba_k'] = qkv[:, dc:2 * dc].reshape(B, L, H_C, HD_C)
        out['moba_v'] = qkv[:, 2 * dc:].reshape(B, L, H_C, HD_C)
        heads = lambda a: a.reshape(B, H_C, HD_C)
        o_c = _moba_step(heads(qkv[:, :dc]), heads(qkv[:, dc:2 * dc]), heads(qkv[:, 2 * dc:]),
                         st['cache_k'], st['cache_v'], st['page_table'], *W['moba_bias_step'])
        o_c = o_c.reshape(T, dc)
        tts = 1
    else:
        qkv, kt, vt = _od_qkv_proj(x2, W['norm_mix'][1], W['od_w_in'][:, :3 * dc], gqk, tm=tm, seq=(B, L))
        out['moba_k'] = jnp.transpose(kt.reshape(B, H_C, HD_C, L), (0, 3, 1, 2))
        out['moba_v'] = jnp.transpose(vt.reshape(B, H_C, HD_C, L), (0, 3, 1, 2))
        o_c = _moba_seq(qkv.reshape(B, L, 3 * dc), vt, W['moba_bias_tiles']).reshape(T, dc)
        tts = min(64, L)
    o_d, hr, hi = _s5(u_tm, st['s5_re'], st['s5_im'], *W['s5'], W['s5_glu_w'], tt=tts, nb=B)
    out['s5_re'] = hr.reshape(B, S5_GROUPS, S5_STATE)
    out['s5_im'] = hi.reshape(B, S5_GROUPS, S5_STATE)
    if decode:
        parts = [(o_c, W['od_w_out'][:dc], pl.BlockSpec((tm, dc), lambda i: (i, 0))),
                 (o_d, W['od_w_out'][dc:], pl.BlockSpec((tm, D_S5), lambda i: (i, 0)))]
    else:
        parts = [(o_c.reshape(B, L, dc), W['od_w_out'][:dc], pl.BlockSpec((None, tm, dc), lambda b, i: (b, i, 0))),
                 (o_d, W['od_w_out'][dc:], pl.BlockSpec((tm, D_S5), lambda b, i: (i, b)))]
    x2, fs = _post_mixer(x2, parts, st, W, 1, B, L, decode)
    out['ffn1'] = fs
    out['y'] = x2.reshape(B, L, D)
    return out


def _post_mixer(x2, parts, st, W, l, B, L, decode):
    T, D = x2.shape
    dq = MEM_H * MEM_HD
    buf = st['ffn_conv'][l]
    if decode:
        x2 = _proj_res(x2, parts, tm=T, name="mixer_out_proj")
        q = _norm_proj(x2, W['norm_mem'][l], W['mem_wq'][l], tm=T, tn=dq, n_norm_tiles=1, group=MEM_HD,
                       gain=jnp.tile(W['mem_gq'][l], MEM_H), name="mem_q_proj")
        o = _mem_attn_step(q.reshape(B, 1, dq), st['mem_k'], st['mem_v'], l).reshape(T, dq)
        x3 = _proj_res(x2, [(o, W['mem_wo'][l], pl.BlockSpec((T, dq), lambda i: (i, 0)))], tm=T, name="mem_out_proj")
        y, gate = _ffn_step(x3, buf[:, 0], buf[:, 1], W['norm_ffn'][l], W['ffn_w_gu'][l],
                            W['ffn_conv_w'][l], W['ffn_conv_b'][l], W['ffn_w_down'][l])
        fs = jnp.stack([buf[:, 1], gate], axis=1)
        return y, fs
    y, tail = _post_seq(x2.reshape(B, L, D), parts, st['mem_k'][l], st['mem_v'][l], W['norm_mem'][l], W['mem_wq'][l],
                        W['mem_gq'][l], W['mem_wo'][l], _pad_rows_front(buf, SUBLANES), W['norm_ffn'][l],
                        W['ffn_w_gu'][l], W['ffn_conv_w'][l], W['ffn_conv_b'][l], W['ffn_w_down'][l],
                        tm=min(512, L), name="post_mixer_seq")
    return y.reshape(T, D), tail[:, SUBLANES - (FFN_CONV - 1):, :]


def _mem_kv(mem, g_norm, w_kv, g_k):
    B, M, D = mem.shape
    dq = MEM_H * MEM_HD
    gain = jnp.concatenate([jnp.tile(g_k, MEM_H), jnp.ones((dq,), F32)])
    kv = _norm_proj(mem.reshape(B * M, D), g_norm, w_kv, tm=min(512, B * M), tn=dq, n_norm_tiles=1, group=MEM_HD,
                    gain=gain, name="mem_kv_proj")
    return kv[:, :dq].reshape(B, M, dq), kv[:, dq:].reshape(B, M, dq)


def kernel(x_prompt, x_sample, state_gdn_conv, state_gdn, state_hgrn, cache_moba_k, cache_moba_v, state_s5_re, state_s5_im, cache_mem_k, cache_mem_v, state_ffn_conv, page_table, mem_prompt, rel_bias, hgrn_lb_raw, norm_mix, norm_mem, norm_memkv, norm_ffn, mem_wq, mem_wkv, mem_gq, mem_gk, mem_wo, ffn_w_gu, ffn_conv_w, ffn_conv_b, ffn_w_down, ev_w_in, ev_w_out, gdn_conv_w, gdn_A_log, gdn_dt_bias, gdn_norm, hgrn_norm, od_w_in, od_w_out, moba_gq, moba_gk, s5_A_re, s5_A_im, s5_log_dt, s5_B_re, s5_B_im, s5_C_re, s5_C_im, s5_D, s5_glu_w):
    Bp, Lq, D = x_prompt.shape
    Bs = x_sample.shape[0]
    dc = H_C * HD_C
    dq = MEM_H * MEM_HD
    n_pages = page_table.shape[1]
    past = n_pages * PAGE_SIZE

    d3 = 3 * D_A
    w_in = ev_w_in[0]
    ba_cols = w_in[:, 4 * D_A:4 * D_A + 2 * H_A]
    w_in = jnp.concatenate([w_in[:, :4 * D_A], w_in[:, 4 * D_A + 2 * H_A:], ba_cols,
                            jnp.zeros((D, LANES - 2 * H_A), F32)], axis=1)
    lb_all = jnp.cumsum(jax.nn.softmax(hgrn_lb_raw.astype(F32), axis=0), axis=0)
    bias_t = rel_bias.T.astype(F32)
    nblk = MOBA_BLOCK
    by_rel = bias_t[:, _rel_bucket(jnp.arange(2 * nblk, dtype=jnp.int32))]
    mm = jnp.arange(2 * nblk, dtype=jnp.int32)

    def toeplitz(shift):
        y = by_rel[:, (shift + mm) % (2 * nblk)]
        flat = jnp.tile(y, (1, nblk))[:, :nblk * (2 * nblk - 1)]
        return flat.reshape(H_C, nblk, 2 * nblk - 1)[:, :, :nblk]

    far = bias_t[:, _rel_bucket(jnp.int32(2 * nblk))]
    tiles = jnp.stack([toeplitz(0), toeplitz(nblk)], axis=1) - far[:, None, None, None]
    bias_step = (by_rel[:, nblk - jnp.arange(nblk, dtype=jnp.int32)][:, None, :],
                 jnp.stack([far, bias_t[:, 0]], axis=1))
    W = dict(
        norm_mix=norm_mix, norm_mem=norm_mem, norm_ffn=norm_ffn,
        ev_w_in=w_in.astype(BF16), ev_w_out=ev_w_out[0].astype(BF16),
        gdn_conv_w=gdn_conv_w[0], gdn_neg_a=-jnp.exp(gdn_A_log[0].astype(F32)), gdn_dt_bias=gdn_dt_bias[0],
        gdn_norm=gdn_norm[0], hgrn_norm=hgrn_norm[0], hgrn_lb=lb_all[0],
        od_w_in=od_w_in[0].astype(BF16), od_w_out=od_w_out[0].astype(BF16),
        moba_gq=moba_gq[0], moba_gk=moba_gk[0], moba_bias_tiles=tiles, moba_bias_step=bias_step,
        s5=_s5_weights(s5_A_re[0], s5_A_im[0], s5_log_dt[0], s5_B_re[0], s5_B_im[0], s5_C_re[0], s5_C_im[0], s5_D[0]),
        s5_glu_w=s5_glu_w[0].astype(BF16),
        mem_wq=mem_wq.astype(BF16), mem_gq=mem_gq, mem_wo=mem_wo.astype(BF16),
        ffn_w_gu=ffn_w_gu.astype(BF16), ffn_conv_w=ffn_conv_w, ffn_conv_b=ffn_conv_b, ffn_w_down=ffn_w_down.astype(BF16),
    )

    mk_l, mv_l = [], []
    for l in range(2):
        mk, mv = _mem_kv(mem_prompt, norm_memkv[l], mem_wkv[l].astype(BF16), mem_gk[l])
        mk_l.append(mk)
        mv_l.append(mv)
    p_mem_k, p_mem_v = jnp.stack(mk_l), jnp.stack(mv_l)
    zeros = lambda *s: jnp.zeros(s, F32)
    st_p = dict(gdn_conv=zeros(Bp, GDN_CONV - 1, d3), gdn=zeros(Bp, H_A, HD_A, HD_A), hgrn=zeros(Bp, H_B, DK_B, DV_B),
                s5_re=zeros(Bp, D_S5_STATE), s5_im=zeros(Bp, D_S5_STATE), mem_k=p_mem_k, mem_v=p_mem_v,
                ffn_conv=zeros(2, Bp, FFN_CONV - 1, D_FF))
    op = _trunk(x_prompt, st_p, W, decode=False)

    st_s = dict(gdn_conv=state_gdn_conv[0], gdn=state_gdn[0], hgrn=state_hgrn[0],
                s5_re=state_s5_re[0].reshape(Bs, D_S5_STATE), s5_im=state_s5_im[0].reshape(Bs, D_S5_STATE),
                mem_k=cache_mem_k, mem_v=cache_mem_v,
                ffn_conv=state_ffn_conv,
                cache_k=jnp.transpose(cache_moba_k[0], (0, 2, 3, 1)), cache_v=jnp.transpose(cache_moba_v[0], (0, 2, 3, 1)),
                page_table=page_table)
    os_ = _trunk(x_sample, st_s, W, decode=True)

    M = mem_prompt.shape[1]
    return (op['y'], os_['y'],
            op['gdn_conv'][None], op['gdn'][None], op['hgrn'][None], op['moba_k'][None], op['moba_v'][None],
            op['s5_re'][None], op['s5_im'][None],
            p_mem_k.reshape(2, Bp, M, MEM_H, MEM_HD), p_mem_v.reshape(2, Bp, M, MEM_H, MEM_HD),
            jnp.stack([op['ffn0'], op['ffn1']]),
            os_['gdn_conv'][None], os_['gdn'][None], os_['hgrn'][None], os_['moba_k'][None], os_['moba_v'][None],
            os_['s5_re'][None], os_['s5_im'][None], jnp.stack([os_['ffn0'], os_['ffn1']]))
```

```python
import functools
import math

import jax
import jax.numpy as jnp
from jax import lax
from jax.experimental import pallas as pl
from jax.experimental.pallas import tpu as pltpu

F32 = jnp.float32
BF16 = jnp.bfloat16
HI = lax.Precision.HIGHEST
EPS = 1e-6

LANES = 128
SUBLANES = 8
VMEM_LIMIT_BYTES = 56 * 1024 * 1024

D_MODEL = 1024
H_A, HD_A, GDN_CONV = 8, 64, 4
H_B, DK_B, DV_B = 8, 64, 64
GDN_CHUNK, HGRN_CHUNK = 64, 32
GDN_GROUP, HGRN_GROUP = 2, 4
H_C, HD_C = 8, 64
MOBA_BLOCK, MOBA_TOPK = 256, 3
N_BUCKETS, MAX_DISTANCE = 32, 128
S5_GROUPS, S5_GROUP_CH, S5_STATE = 32, 16, 64
MEM_H, MEM_HD = 4, 128
D_FF, FFN_CONV = 2816, 3
PAGE_SIZE = 128
D_A = H_A * HD_A
D_S5 = S5_GROUPS * S5_GROUP_CH
D_S5_STATE = S5_GROUPS * S5_STATE
FF_CHUNK = D_FF // 2
EV_IN_COLS = 8 * D_A + LANES
EV_IN_TILE = EV_IN_COLS // 3


def _cparams(*sem):
    return pltpu.CompilerParams(dimension_semantics=sem, vmem_limit_bytes=VMEM_LIMIT_BYTES)


def _dot(a, b, precision=None):
    return jnp.dot(a, b, preferred_element_type=F32, precision=precision)


def _dot_nt(a, b, precision=None):
    return lax.dot_general(a, b, (((1,), (1,)), ((), ())), preferred_element_type=F32, precision=precision)


def _dot_tn(a, b, precision=None):
    return lax.dot_general(a, b, (((0,), (0,)), ((), ())), preferred_element_type=F32, precision=precision)


def _rms(x, g):
    return x * lax.rsqrt(jnp.mean(x * x, axis=-1, keepdims=True) + EPS) * g


def _silu(x):
    return x * jax.nn.sigmoid(x)


def _tri(n, strict=False):
    r = lax.broadcasted_iota(jnp.int32, (n, n), 0)
    c = lax.broadcasted_iota(jnp.int32, (n, n), 1)
    return (r > c) if strict else (r >= c)


def _norm_proj_kernel(x_ref, g_ref, w_ref, e_ref, gain_ref, o_ref, *, n_norm_tiles, group):
    xn = _rms(x_ref[...], g_ref[...])
    z = _dot(xn.astype(BF16), w_ref[...])
    if n_norm_tiles == 0:
        o_ref[...] = z
        return
    j = pl.program_id(1)

    @pl.when(j < n_norm_tiles)
    def _():
        zz = z * z
        hi = zz.astype(BF16)
        lo = (zz - hi.astype(F32)).astype(BF16)
        s = _dot(hi, e_ref[...]) + _dot(lo, e_ref[...])
        o_ref[...] = z * lax.rsqrt(s * (1.0 / group) + EPS) * gain_ref[...]

    @pl.when(j >= n_norm_tiles)
    def _():
        o_ref[...] = z


def _norm_proj(x, g, w, *, tm, tn, n_norm_tiles=0, group=1, gain=None, time_major=None, name):
    T, D = x.shape
    N = w.shape[1]
    nj = N // tn
    if gain is None:
        gain = jnp.ones((N,), F32)
    gi = lax.broadcasted_iota(jnp.int32, (tn, tn), 0) // group
    gj = lax.broadcasted_iota(jnp.int32, (tn, tn), 1) // group
    e = (gi == gj).astype(BF16)
    if time_major is None:
        out_shape = jax.ShapeDtypeStruct((T, N), F32)
        out_spec = pl.BlockSpec((tm, tn), lambda i, j: (i, j))
    else:
        B, L = time_major
        nt = L // tm
        out_shape = jax.ShapeDtypeStruct((L, B * N), F32)
        out_spec = pl.BlockSpec((tm, tn), lambda i, j: (i % nt, (i // nt) * nj + j))
    return pl.pallas_call(
        functools.partial(_norm_proj_kernel, n_norm_tiles=n_norm_tiles, group=group),
        grid=(T // tm, nj),
        in_specs=[
            pl.BlockSpec((tm, D), lambda i, j: (i, 0)),
            pl.BlockSpec((1, D), lambda i, j: (0, 0)),
            pl.BlockSpec((D, tn), lambda i, j: (0, j)),
            pl.BlockSpec((tn, tn), lambda i, j: (0, 0)),
            pl.BlockSpec((1, tn), lambda i, j: (0, j)),
        ],
        out_specs=out_spec,
        out_shape=out_shape,
        compiler_params=_cparams("parallel", "arbitrary"),
        name=name,
    )(x, g.reshape(1, D), w, e, gain.reshape(1, N))


def _od_qkv_kernel(x_ref, g_ref, w_ref, e_ref, gain_ref, o_ref, *t_refs):
    dc = H_C * HD_C
    xn = _rms(x_ref[...], g_ref[...]).astype(BF16)
    for c in range(3):
        cs = slice(c * dc, (c + 1) * dc)
        z = _dot(xn, w_ref[:, cs])
        if c < 2:
            zz = z * z
            hi = zz.astype(BF16)
            lo = (zz - hi.astype(F32)).astype(BF16)
            s = _dot(hi, e_ref[...]) + _dot(lo, e_ref[...])
            z = z * lax.rsqrt(s * (1.0 / HD_C) + EPS) * gain_ref[:, cs]
        o_ref[:, cs] = z
        if t_refs and c > 0:
            t_refs[c - 1][...] = z.T


def _od_qkv_proj(x, g, w, gain, *, tm, seq=None):
    T, D = x.shape
    dc = H_C * HD_C
    gi = lax.broadcasted_iota(jnp.int32, (dc, dc), 0) // HD_C
    gj = lax.broadcasted_iota(jnp.int32, (dc, dc), 1) // HD_C
    e = (gi == gj).astype(BF16)
    const = lambda shape: pl.BlockSpec(shape, lambda i: (0, 0), pipeline_mode=pl.Buffered(1))
    out_specs = [pl.BlockSpec((tm, 3 * dc), lambda i: (i, 0))]
    out_shape = [jax.ShapeDtypeStruct((T, 3 * dc), F32)]
    if seq is not None:
        B, L = seq
        nt = L // tm
        tspec = pl.BlockSpec((None, dc, tm), lambda i: (i // nt, 0, i % nt))
        out_specs += [tspec, tspec]
        out_shape += [jax.ShapeDtypeStruct((B, dc, L), F32)] * 2
    return pl.pallas_call(
        _od_qkv_kernel,
        grid=(T // tm,),
        in_specs=[pl.BlockSpec((tm, D), lambda i: (i, 0)), const((1, D)), const((D, 3 * dc)), const((dc, dc)),
                  const((1, 2 * dc))],
        out_specs=out_specs,
        out_shape=out_shape,
        compiler_params=_cparams("parallel"),
        name="od_in_proj_qkv",
    )(x, g.reshape(1, D), w, e, gain.reshape(1, 2 * dc))


def _proj_res_kernel(*refs, n_in):
    x_ref, o_ref = refs[0], refs[-1]
    acc = x_ref[...]
    for k in range(n_in):
        a_ref, w_ref = refs[1 + 2 * k], refs[2 + 2 * k]
        acc = acc + _dot(a_ref[...].astype(BF16), w_ref[...])
    o_ref[...] = acc


def _proj_res(x, parts, *, tm, name):
    T, N = x.shape
    in_specs = [pl.BlockSpec((tm, N), lambda i: (i, 0))]
    args = [x]
    for a, w, spec in parts:
        in_specs += [spec, pl.BlockSpec(w.shape, lambda i: (0, 0))]
        args += [a, w]
    return pl.pallas_call(
        functools.partial(_proj_res_kernel, n_in=len(parts)),
        grid=(T // tm,),
        in_specs=in_specs,
        out_specs=pl.BlockSpec((tm, N), lambda i: (i, 0)),
        out_shape=jax.ShapeDtypeStruct((T, N), F32),
        compiler_params=_cparams("parallel"),
        name=name,
    )(*args)


def _neumann_inverse(mats, n):
    m = mats[0].shape[0]
    eye = (lax.broadcasted_iota(jnp.int32, (m, m), 0) == lax.broadcasted_iota(jnp.int32, (m, m), 1)).astype(F32)
    ts = [eye - a for a in mats]
    ps = [a.astype(BF16) for a in mats]
    k = 2
    while k < n:
        ps = [_dot(p, p).astype(BF16) for p in ps]
        ts = [t + _dot(t.astype(BF16), p) for t, p in zip(ts, ps)]
        k *= 2
    return ts


def _even_mixer_kernel(qkv_ref, zg_ref, hq_ref, hf_ref, hi_ref, hg_ref, ba_ref,
                       conv0_ref, sg0_ref, sh0_ref,
                       convw_ref, nega_ref, dtb_ref, gnorm_ref, hnorm_ref, lb_ref,
                       mixed_ref, tail_ref, sg_ref, sh_ref,
                       xp_ref, qkvc_ref, beta_ref, g_ref, sbd_ref,
                       *, tt, cg, ch, l_real):
    ti = pl.program_id(1)
    n_tiles = pl.num_programs(1)

    @pl.when(ti == 0)
    def _():
        xp_ref[0:SUBLANES, :] = conv0_ref[...]
        sh_ref[...] = sh0_ref[...]
        zero = jnp.zeros((HD_A, HD_A), F32)
        for p in range(H_A // 2):
            sbd_ref[p] = jnp.concatenate([jnp.concatenate([sg0_ref[2 * p], zero], axis=-1),
                                          jnp.concatenate([zero, sg0_ref[2 * p + 1]], axis=-1)], axis=0)

    xp_ref[SUBLANES:SUBLANES + tt, :] = qkv_ref[...]
    conv = convw_ref[GDN_CONV - 1:GDN_CONV, :] * qkv_ref[...]
    for j in range(GDN_CONV - 1):
        off = SUBLANES - (GDN_CONV - 1) + j
        conv = conv + convw_ref[j:j + 1, :] * xp_ref[off:off + tt, :]
    qkvc_ref[...] = _silu(conv)
    lr_last = l_real - (l_real - 1) // tt * tt
    tail_ref[...] = xp_ref[lr_last:lr_last + SUBLANES, :]
    carry = xp_ref[tt:tt + SUBLANES, :]
    xp_ref[0:SUBLANES, :] = carry

    padded = (l_real % tt) != 0
    if padded:
        row = lax.broadcasted_iota(jnp.int32, (tt, 1), 0) + ti * tt
        valid = row < l_real
    ba = ba_ref[...]
    beta_all = jax.nn.sigmoid(ba)
    g_all = nega_ref[...] * jax.nn.softplus(ba + dtb_ref[...])
    if padded:
        beta_all = jnp.where(valid, beta_all, 0.0)
        g_all = jnp.where(valid, g_all, 0.0)
    beta_ref[...] = beta_all
    g_ref[...] = g_all

    ltri_g = _tri(cg).astype(F32)
    hnorm = hnorm_ref[...]
    gnorm2 = jnp.concatenate([gnorm_ref[...], gnorm_ref[...]], axis=-1)
    pairs = range(H_A // 2)
    n2 = 2 * cg
    lane = lax.broadcasted_iota(jnp.int32, (1, LANES), 1)
    m0 = (lane < HD_A).astype(F32)
    m1 = 1.0 - m0
    ri = lax.broadcasted_iota(jnp.int32, (n2, n2), 0)
    cj = lax.broadcasted_iota(jnp.int32, (n2, n2), 1)
    same_head = (ri // cg) == (cj // cg)
    bd_tril = same_head & (ri >= cj)
    bd_stril = same_head & (ri > cj)

    def stack2(a):
        return jnp.concatenate([a * m0, a * m1], axis=0)

    def rep2(col0, col1):
        return jnp.concatenate([jnp.broadcast_to(col0, (cg, LANES)), jnp.broadcast_to(col1, (cg, LANES))], axis=0)

    ng = min(GDN_GROUP, tt // cg)

    def gdn_group(gi, _):
        items = [(c, p) for c in range(ng) for p in pairs]
        rows_c, gcum_c, gt_c, bet_c = [], [], [], []
        for c in range(ng):
            rows = pl.ds(pl.multiple_of((gi * ng + c) * cg, cg), cg)
            gcum = _dot(ltri_g, g_ref[rows, :], HI)
            gpad = gcum if cg == LANES else jnp.concatenate([gcum, jnp.zeros((LANES - cg, LANES), F32)], axis=0)
            rows_c.append(rows)
            gcum_c.append(gcum)
            gt_c.append(gpad.T)
            bet_c.append(beta_ref[rows, :])
        qs, ks, vs, betas, gcs, decays, glasts, glrows = {}, {}, {}, {}, {}, {}, {}, {}
        for it in items:
            c, p = it
            rows = rows_c[c]
            h0, h1 = H_A + 2 * p, H_A + 2 * p + 1
            q = stack2(qkvc_ref[rows, p * LANES:(p + 1) * LANES])
            k = stack2(qkvc_ref[rows, D_A + p * LANES:D_A + (p + 1) * LANES])
            qs[it] = q * lax.rsqrt(jnp.sum(q * q, axis=-1, keepdims=True) + EPS) * (HD_A ** -0.5)
            ks[it] = k * lax.rsqrt(jnp.sum(k * k, axis=-1, keepdims=True) + EPS)
            vs[it] = stack2(qkvc_ref[rows, 2 * D_A + p * LANES:2 * D_A + (p + 1) * LANES])
            betas[it] = rep2(bet_c[c][:, 2 * p:2 * p + 1], bet_c[c][:, 2 * p + 1:2 * p + 2])
            gc = rep2(gcum_c[c][:, h0:h0 + 1], gcum_c[c][:, h1:h1 + 1])
            gr = jnp.concatenate([gt_c[c][h0:h0 + 1, 0:cg], gt_c[c][h1:h1 + 1, 0:cg]], axis=-1)
            gcs[it] = gc
            glasts[it] = rep2(gcum_c[c][cg - 1:cg, h0:h0 + 1], gcum_c[c][cg - 1:cg, h1:h1 + 1])
            glrows[it] = jnp.exp(glasts[it][0:1, :]) * m0 + jnp.exp(glasts[it][cg:cg + 1, :]) * m1
            decays[it] = jnp.exp(jnp.where(bd_tril, gc[:, 0:n2] - gr, -jnp.inf))
        kqs = {it: _dot_nt(jnp.concatenate([ks[it], qs[it]], axis=0).astype(BF16), ks[it].astype(BF16))
               for it in items}
        ts = dict(zip(items, _neumann_inverse(
            [jnp.where(bd_stril, betas[it][:, 0:n2] * kqs[it][0:n2] * decays[it], 0.0) for it in items], cg)))
        egs = {it: jnp.exp(gcs[it]) for it in items}
        sols = {it: _dot(ts[it].astype(BF16),
                         jnp.concatenate([betas[it] * vs[it], (betas[it] * egs[it]) * ks[it]], axis=-1).astype(BF16))
                for it in items}
        wqs = {it: jnp.concatenate([sols[it][:, LANES:2 * LANES], qs[it] * egs[it]], axis=0).astype(BF16)
               for it in items}
        qks = {it: (kqs[it][n2:2 * n2] * decays[it]).astype(BF16) for it in items}
        kouts = {it: (ks[it] * jnp.exp(glasts[it] - gcs[it])).astype(BF16) for it in items}
        ss = [sbd_ref[p] for p in pairs]
        for c in range(ng):
            wss = [_dot(wqs[(c, p)], ss[p].astype(BF16)) for p in pairs]
            ubs = [(sols[(c, p)][:, 0:LANES] - wss[p][0:n2]).astype(BF16) for p in pairs]
            os_ = [wss[p][n2:2 * n2] + _dot(qks[(c, p)], ubs[p]) for p in pairs]
            ss = [ss[p] * glrows[(c, p)] + _dot_tn(kouts[(c, p)], ubs[p]) for p in pairs]
            for p in pairs:
                o = os_[p]
                on = o * lax.rsqrt(jnp.sum(o * o, axis=-1, keepdims=True) * (1.0 / HD_A) + EPS)
                cols = slice(p * LANES, (p + 1) * LANES)
                mixed_ref[rows_c[c], cols] = (on[0:cg] + on[cg:n2]) * gnorm2 * _silu(zg_ref[rows_c[c], cols])
        for p in pairs:
            sbd_ref[p] = ss[p]
        return 0

    lax.fori_loop(0, tt // (cg * ng), gdn_group, 0)
    for p in pairs:
        sg_ref[2 * p] = sbd_ref[p, 0:HD_A, 0:HD_A]
        sg_ref[2 * p + 1] = sbd_ref[p, HD_A:2 * HD_A, HD_A:2 * HD_A]

    nh = min(HGRN_GROUP, tt // ch)
    gr = nh * ch
    ri = lax.broadcasted_iota(jnp.int32, (gr, gr), 0)
    cj = lax.broadcasted_iota(jnp.int32, (gr, gr), 1)
    same_chunk_causal = (ri >= cj) & ((ri // ch) == (cj // ch))
    ltri_h = same_chunk_causal.astype(F32)
    lb = lb_ref[...]
    r_mid = (ch - 1) // 2

    def hgrn_group(gi, _):
        r0 = pl.multiple_of(gi * gr, gr)
        rows = pl.ds(r0, gr)
        f = lb + (1.0 - lb) * jax.nn.sigmoid(hf_ref[rows, :])
        logf = jnp.log(f)
        kk = 1.0 - f
        if padded:
            vrow = (lax.broadcasted_iota(jnp.int32, (gr, 1), 0) + r0 + ti * tt) < l_real
            logf = jnp.where(vrow, logf, 0.0)
            kk = jnp.where(vrow, kk, 0.0)
        qh = _silu(hq_ref[rows, :])
        bc = _dot(ltri_h, logf, HI).reshape(nh, ch, D_A)
        bref = bc[:, r_mid:r_mid + 1, :]
        blast = bc[:, ch - 1:ch, :]
        grp = lambda a: a.reshape(gr, D_A)
        qh3, kk3 = qh.reshape(nh, ch, D_A), kk.reshape(nh, ch, D_A)
        qa = grp(qh3 * jnp.exp(bc - bref)).astype(BF16)
        kb = grp(kk3 * jnp.exp(bref - bc)).astype(BF16)
        q_in = grp(qh3 * jnp.exp(bc)).astype(BF16)
        k_out = grp(kk3 * jnp.exp(blast - bc)).astype(BF16)
        fl = jnp.exp(blast)
        vv = hi_ref[rows, :].astype(BF16)
        heads = range(H_B)
        sls = [slice(h * DK_B, (h + 1) * DK_B) for h in heads]
        atts = [jnp.where(same_chunk_causal, _dot_nt(qa[:, sl], kb[:, sl]), 0.0).astype(BF16) for sl in sls]
        intra = [_dot(atts[h], vv[:, sls[h]]) for h in heads]
        sts = [sh_ref[h] for h in heads]
        inter = [[] for _ in heads]
        for c in range(nh):
            rc = slice(c * ch, (c + 1) * ch)
            for h in heads:
                inter[h].append(_dot_nt(q_in[rc, sls[h]], sts[h].astype(BF16)))
            sts = [sts[h] * fl[c][:, sls[h]] + _dot_tn(vv[rc, sls[h]], k_out[rc, sls[h]]) for h in heads]
        for h in heads:
            sh_ref[h] = sts[h]
        os_ = [intra[h] + (inter[h][0] if nh == 1 else jnp.concatenate(inter[h], axis=0)) for h in heads]
        outs = [_rms(os_[h], hnorm) * _silu(hg_ref[rows, h * DV_B:(h + 1) * DV_B]) for h in heads]
        for p in range(H_B // 2):
            c0 = D_A + p * LANES
            mixed_ref[rows, c0:c0 + LANES] = jnp.concatenate([outs[2 * p], outs[2 * p + 1]], axis=-1)
        return 0

    lax.fori_loop(0, tt // gr, hgrn_group, 0)


def _even_mixer(z, conv0, sg0, sh0_t, conv_w, neg_a, dt_bias, gnorm, hnorm, lb, *, l_real, tt):
    B, Lp, _ = z.shape
    cg = min(GDN_CHUNK, tt)
    ch = min(HGRN_CHUNK, tt)
    d3 = 3 * D_A
    nt = Lp // tt
    col = lambda w, idx: pl.BlockSpec((None, tt, w), lambda b, i, idx=idx: (b, i, idx))
    state = lambda: pl.BlockSpec((None, H_A, HD_A, HD_A), lambda b, i: (b, 0, 0, 0))
    vec = lambda n: pl.BlockSpec((1, n), lambda b, i: (0, 0))
    nega_row = jnp.zeros((1, LANES), F32).at[0, H_A:2 * H_A].set(neg_a)
    dtb_row = jnp.zeros((1, LANES), F32).at[0, H_A:2 * H_A].set(dt_bias)
    return pl.pallas_call(
        functools.partial(_even_mixer_kernel, tt=tt, cg=cg, ch=ch, l_real=l_real),
        grid=(B, nt),
        in_specs=[
            col(d3, 0), col(D_A, 3), col(D_A, 4), col(D_A, 5), col(D_A, 6), col(D_A, 7), col(LANES, 32),
            pl.BlockSpec((None, SUBLANES, d3), lambda b, i: (b, 0, 0)),
            state(), state(),
            pl.BlockSpec((GDN_CONV, d3), lambda b, i: (0, 0)),
            vec(LANES), vec(LANES), vec(HD_A), vec(DV_B), vec(D_A),
        ],
        out_specs=[
            pl.BlockSpec((None, tt, 2 * D_A), lambda b, i: (b, i, 0)),
            pl.BlockSpec((None, SUBLANES, d3), lambda b, i: (b, 0, 0)),
            state(), state(),
        ],
        out_shape=[
            jax.ShapeDtypeStruct((B, Lp, 2 * D_A), F32),
            jax.ShapeDtypeStruct((B, SUBLANES, d3), F32),
            jax.ShapeDtypeStruct((B, H_A, HD_A, HD_A), F32),
            jax.ShapeDtypeStruct((B, H_B, DV_B, DK_B), F32),
        ],
        scratch_shapes=[
            pltpu.VMEM((tt + SUBLANES, d3), F32),
            pltpu.VMEM((tt, d3), F32),
            pltpu.VMEM((tt, LANES), F32),
            pltpu.VMEM((tt, LANES), F32),
            pltpu.VMEM((H_A // 2, 2 * HD_A, 2 * HD_A), F32),
        ],
        compiler_params=_cparams("parallel", "arbitrary"),
        name="even_mixer",
    )(z, z, z, z, z, z, z, conv0, sg0, sh0_t, conv_w, nega_row, dtb_row,
      gnorm.reshape(1, HD_A), hnorm.reshape(1, DV_B), lb.reshape(1, D_A))


def _mem_attn_rows(x, mk_ref, mv_ref, g_ref, wq_ref, gq_ref, wo_ref):
    q = _dot(_rms(x, g_ref[...]).astype(BF16), wq_ref[...])
    scale = MEM_HD ** -0.5
    outs = []
    for h in range(MEM_H):
        sl = slice(h * MEM_HD, (h + 1) * MEM_HD)
        qh = _rms(q[:, sl], gq_ref[...])
        s = _dot_nt(qh.astype(BF16), mk_ref[:, sl].astype(BF16)) * scale
        s = s - jnp.max(s, axis=-1, keepdims=True)
        p = jnp.exp(s)
        p = p / jnp.sum(p, axis=-1, keepdims=True)
        outs.append(_dot(p.astype(BF16), mv_ref[:, sl].astype(BF16)))
    o = jnp.concatenate(outs, axis=-1)
    return x + _dot(o.astype(BF16), wo_ref[...])


def _mem_attn_step_kernel(q_ref, mk_ref, mv_ref, o_ref):
    scale = MEM_HD ** -0.5
    outs = []
    for h in range(MEM_H):
        qh = jnp.broadcast_to(q_ref[:, h * MEM_HD:(h + 1) * MEM_HD], (SUBLANES, MEM_HD))
        s = _dot_nt(qh.astype(BF16), mk_ref[:, h, :].astype(BF16)) * scale
        s = s - jnp.max(s, axis=-1, keepdims=True)
        p = jnp.exp(s)
        p = p / jnp.sum(p, axis=-1, keepdims=True)
        outs.append(_dot(p.astype(BF16), mv_ref[:, h, :].astype(BF16))[0:1])
    o_ref[...] = jnp.concatenate(outs, axis=-1)


def _mem_attn_step(q, cache_k, cache_v, l):
    B = q.shape[0]
    M = cache_k.shape[2]
    dq = MEM_H * MEM_HD
    mem = pl.BlockSpec((None, None, M, MEM_H, MEM_HD), lambda b: (l, b, 0, 0, 0))
    row = pl.BlockSpec((None, 1, dq), lambda b: (b, 0, 0))
    return pl.pallas_call(
        _mem_attn_step_kernel,
        grid=(B,),
        in_specs=[row, mem, mem],
        out_specs=row,
        out_shape=jax.ShapeDtypeStruct((B, 1, dq), F32),
        compiler_params=_cparams("parallel"),
        name="mem_attn_step",
    )(q, cache_k, cache_v)


def _post_seq_kernel(*refs, n_parts, tm):
    x_ref = refs[0]
    parts = refs[1:1 + 2 * n_parts]
    (mk_ref, mv_ref, gmem_ref, wq_ref, gq_ref, wo_ref, buf0_ref, g_ref, wgu_ref, cw_ref, cb_ref, wd_ref,
     o_ref, tail_ref, gp_ref, carry_ref) = refs[1 + 2 * n_parts:]
    ti = pl.program_id(1)

    @pl.when(ti == 0)
    def _():
        carry_ref[...] = buf0_ref[...]

    x = x_ref[...]
    for k in range(n_parts):
        x = x + _dot(parts[2 * k][...].astype(BF16), parts[2 * k + 1][...])
    x = _mem_attn_rows(x, mk_ref, mv_ref, gmem_ref, wq_ref, gq_ref, wo_ref)
    xn = _rms(x, g_ref[...]).astype(BF16)
    acc = x
    for c in range(D_FF // FF_CHUNK):
        cs = slice(c * FF_CHUNK, (c + 1) * FF_CHUNK)
        gate = _dot(xn, wgu_ref[:, cs])
        up = _dot(xn, wgu_ref[:, D_FF + c * FF_CHUNK:D_FF + (c + 1) * FF_CHUNK])
        gp_ref[0:SUBLANES, :] = carry_ref[:, cs]
        gp_ref[SUBLANES:SUBLANES + tm, :] = gate
        conv = cw_ref[FFN_CONV - 1:FFN_CONV, cs] * gate + cb_ref[:, cs]
        for j in range(FFN_CONV - 1):
            off = SUBLANES - (FFN_CONV - 1) + j
            conv = conv + cw_ref[j:j + 1, cs] * gp_ref[off:off + tm, :]
        last = gp_ref[tm:tm + SUBLANES, :]
        carry_ref[:, cs] = last
        tail_ref[:, cs] = last
        act = _silu(conv) * up
        acc = acc + _dot(act.astype(BF16), wd_ref[cs, :])
    o_ref[...] = acc


def _post_seq(x, parts, mk, mv, gmem, wq, gq, wo, buf0, g, wgu, cw, cb, wd, *, tm, name):
    B, L, D = x.shape
    M = mk.shape[1]
    dq = MEM_H * MEM_HD
    const = lambda shape: pl.BlockSpec(shape, lambda b, i: (0,) * len(shape), pipeline_mode=pl.Buffered(1))
    row_tile = pl.BlockSpec((None, tm, D), lambda b, i: (b, i, 0))
    mem = pl.BlockSpec((None, M, dq), lambda b, i: (b, 0, 0))
    in_specs, args = [row_tile], [x]
    for a, w, spec in parts:
        in_specs += [spec, const(w.shape)]
        args += [a, w]
    in_specs += [mem, mem, const((1, D)), const((D, dq)), const((1, MEM_HD)), const((dq, D)),
                 pl.BlockSpec((None, SUBLANES, D_FF), lambda b, i: (b, 0, 0)),
                 const((1, D)), const((D, 2 * D_FF)), const((FFN_CONV, D_FF)), const((1, D_FF)), const((D_FF, D))]
    args += [mk, mv, gmem.reshape(1, D), wq, gq.reshape(1, MEM_HD), wo, buf0, g.reshape(1, D), wgu, cw,
             cb.reshape(1, D_FF), wd]
    return pl.pallas_call(
        functools.partial(_post_seq_kernel, n_parts=len(parts), tm=tm),
        grid=(B, L // tm),
        in_specs=in_specs,
        out_specs=[row_tile, pl.BlockSpec((None, SUBLANES, D_FF), lambda b, i: (b, 0, 0))],
        out_shape=[jax.ShapeDtypeStruct((B, L, D), F32), jax.ShapeDtypeStruct((B, SUBLANES, D_FF), F32)],
        scratch_shapes=[pltpu.VMEM((tm + SUBLANES, FF_CHUNK), F32), pltpu.VMEM((SUBLANES, D_FF), F32)],
        compiler_params=_cparams("parallel", "arbitrary"),
        name=name,
    )(*args)


def _ffn_step_kernel(x_ref, p2_ref, p1_ref, g_ref, wgu_ref, cw_ref, cb_ref, wd_ref, o_ref, gate_ref):
    x = x_ref[...]
    xn = _rms(x, g_ref[...]).astype(BF16)
    acc = x
    for c in range(D_FF // FF_CHUNK):
        cs = slice(c * FF_CHUNK, (c + 1) * FF_CHUNK)
        gate = _dot(xn, wgu_ref[:, cs])
        up = _dot(xn, wgu_ref[:, D_FF + c * FF_CHUNK:D_FF + (c + 1) * FF_CHUNK])
        gate_ref[:, cs] = gate
        conv = cw_ref[0:1, cs] * p2_ref[:, cs] + cw_ref[1:2, cs] * p1_ref[:, cs] + cw_ref[2:3, cs] * gate + cb_ref[:, cs]
        act = _silu(conv) * up
        acc = acc + _dot(act.astype(BF16), wd_ref[cs, :])
    o_ref[...] = acc


def _ffn_step(x, p2, p1, g, wgu, cw, cb, wd):
    R, D = x.shape
    full = lambda a: pl.BlockSpec(a.shape, lambda i: (0,) * a.ndim)
    args = (x, p2, p1, g.reshape(1, D), wgu, cw, cb.reshape(1, D_FF), wd)
    return pl.pallas_call(
        _ffn_step_kernel,
        grid=(1,),
        in_specs=[full(a) for a in args],
        out_specs=[pl.BlockSpec((R, D), lambda i: (0, 0)), pl.BlockSpec((R, D_FF), lambda i: (0, 0))],
        out_shape=[jax.ShapeDtypeStruct((R, D), F32), jax.ShapeDtypeStruct((R, D_FF), F32)],
        compiler_params=_cparams("arbitrary"),
        name="conv_ffn_step",
    )(*args)


def _moba_query_tile(j, h, qt_ref, kb_ref, vt_ref, kmean_ref, bias_ref, ot_ref):
    blk = MOBA_BLOCK
    w = (j + 1) * blk
    qt = qt_ref[h]
    st = _dot(kb_ref[h, 0:w, :], (qt * (HD_C ** -0.5)).astype(BF16))
    if j > MOBA_TOPK:
        g = _dot(kmean_ref[h, 0:j, :], qt, HI)
        iota_n = lax.broadcasted_iota(jnp.int32, (j, blk), 0).astype(F32)
        sel = jnp.zeros((j, blk), F32)
        for _ in range(MOBA_TOPK):
            m = jnp.max(g, axis=0, keepdims=True)
            idx = jnp.min(jnp.where(g == m, iota_n, 1e9), axis=0, keepdims=True)
            hit = iota_n == idx
            sel = jnp.where(hit, 1.0, sel)
            g = jnp.where(hit, -jnp.inf, g)
    key_le_query = lax.broadcasted_iota(jnp.int32, (blk, blk), 0) <= lax.broadcasted_iota(jnp.int32, (blk, blk), 1)
    parts = []
    for n in range(j + 1):
        s = st[n * blk:(n + 1) * blk, :]
        if j - n < 2:
            s = s + bias_ref[h, j - n]
        if n == j:
            s = jnp.where(key_le_query, s, -jnp.inf)
        elif j > MOBA_TOPK:
            s = jnp.where(sel[n:n + 1, :] > 0.5, s, -jnp.inf)
        parts.append(s)
    m = jnp.max(parts[0], axis=0, keepdims=True)
    for s in parts[1:]:
        m = jnp.maximum(m, jnp.max(s, axis=0, keepdims=True))
    ps = [jnp.exp(s - m) for s in parts]
    l = jnp.sum(ps[0], axis=0, keepdims=True)
    for p in ps[1:]:
        l = l + jnp.sum(p, axis=0, keepdims=True)
    pcat = ps[0] if j == 0 else jnp.concatenate(ps, axis=0)
    ot_ref[h] = _dot(vt_ref[h, :, 0:w], pcat.astype(BF16)) / l


def _moba_seq_kernel(q_ref, k_ref, vtin_ref, bias_ref, o_ref, kmean_ref, kb_ref, vt_ref, qt_ref, ot_ref, *, nb):
    j = pl.program_id(1)
    blk = MOBA_BLOCK
    dc = H_C * HD_C

    @pl.when(j == 0)
    def _():
        for n in range(nb):
            km = jnp.mean(k_ref[n * blk:(n + 1) * blk, :], axis=0, keepdims=True)
            for h in range(H_C):
                kmean_ref[h, n:n + 1, :] = km[:, h * HD_C:(h + 1) * HD_C]

        def prep(n, _):
            rows = pl.ds(pl.multiple_of(n * blk, blk), blk)
            kblk = k_ref[rows, :]
            vt_ref[:, :, rows] = vtin_ref[:, rows].astype(BF16).reshape(H_C, HD_C, blk)
            for h in range(H_C):
                kb_ref[h, rows, :] = kblk[:, h * HD_C:(h + 1) * HD_C].astype(BF16)
            return 0
        lax.fori_loop(0, nb, prep, 0)

    qt_ref[...] = q_ref[...].T.reshape(H_C, HD_C, blk)
    for jj in range(nb):
        @pl.when(j == jj)
        def _(jj=jj):
            def head(h, _):
                _moba_query_tile(jj, h, qt_ref, kb_ref, vt_ref, kmean_ref, bias_ref, ot_ref)
                return 0
            lax.fori_loop(0, H_C, head, 0)
    o_ref[...] = ot_ref[...].reshape(dc, blk).T


def _moba_seq(qkv, vt, bias_tiles):
    B, L, _ = qkv.shape
    nb = L // MOBA_BLOCK
    dc = H_C * HD_C
    return pl.pallas_call(
        functools.partial(_moba_seq_kernel, nb=nb),
        grid=(B, nb),
        in_specs=[
            pl.BlockSpec((None, MOBA_BLOCK, dc), lambda b, j: (b, j, 0)),
            pl.BlockSpec((None, L, dc), lambda b, j: (b, 0, 1)),
            pl.BlockSpec((None, dc, L), lambda b, j: (b, 0, 0)),
            pl.BlockSpec(bias_tiles.shape, lambda b, j: (0, 0, 0, 0), pipeline_mode=pl.Buffered(1)),
        ],
        out_specs=pl.BlockSpec((None, MOBA_BLOCK, dc), lambda b, j: (b, j, 0)),
        out_shape=jax.ShapeDtypeStruct((B, L, dc), F32),
        scratch_shapes=[pltpu.VMEM((H_C, nb, HD_C), F32), pltpu.VMEM((H_C, L, HD_C), BF16),
                        pltpu.VMEM((H_C, HD_C, L), BF16), pltpu.VMEM((H_C, HD_C, MOBA_BLOCK), F32),
                        pltpu.VMEM((H_C, HD_C, MOBA_BLOCK), F32)],
        compiler_params=_cparams("parallel", "arbitrary"),
        name="moba_seq",
    )(qkv, qkv, vt, bias_tiles)


MOBA_PAGES_PER_BLOCK = MOBA_BLOCK // PAGE_SIZE
MOBA_STEP_PAGES = 8


def _moba_score_kernel(pt_ref, q_ref, *refs, n_pages):
    del pt_ref
    ppg, ppb = MOBA_STEP_PAGES, MOBA_PAGES_PER_BLOCK
    kp_refs = refs[:ppg]
    s_ref, sel_ref, gate_ref = refs[ppg:]
    step = pl.program_id(1)
    nbp = n_pages // ppb
    q = q_ref[...]
    inv = 1.0 / (MOBA_BLOCK * HD_C ** -0.5)
    gsum = None
    for r in range(ppg):
        sc = jnp.sum(kp_refs[r][...] * q, axis=1)
        s_ref[:, r * PAGE_SIZE:(r + 1) * PAGE_SIZE] = sc
        gs = jnp.sum(sc, axis=-1, keepdims=True)
        gsum = gs if r % ppb == 0 else gsum + gs
        if r % ppb == ppb - 1:
            gate_ref[step * (ppg // ppb) + r // ppb] = jnp.broadcast_to(gsum * inv, (H_C, LANES))

    @pl.when(step == n_pages // ppg - 1)
    def _():
        g = gate_ref[...]
        iota_b = lax.broadcasted_iota(jnp.int32, (nbp, H_C, LANES), 0).astype(F32)
        for i in range(MOBA_TOPK):
            m = jnp.max(g, axis=0, keepdims=True)
            idx = jnp.min(jnp.where(g == m, iota_b, 1e9), axis=0, keepdims=True)
            sel_ref[i] = idx[0]
            g = jnp.where(iota_b == idx, -jnp.inf, g)


def _moba_attend_kernel(pt_ref, sel_ref, q_ref, kn_ref, vn_ref, blast_ref, bsc_ref, *refs, nbp):
    del pt_ref
    nsel, ppb = MOBA_TOPK, MOBA_PAGES_PER_BLOCK
    s_refs, v_refs, o_ref = refs[:nsel], refs[nsel:nsel + nsel * ppb], refs[-1]
    b, h = pl.program_id(0), pl.program_id(1)
    far = bsc_ref[h, 0]
    s_new = jnp.sum(q_ref[...] * kn_ref[...], axis=-1, keepdims=True) * (HD_C ** -0.5) + bsc_ref[h, 1]
    ss = []
    for i in range(nsel):
        blk = sel_ref[(b * H_C + h) * nsel + i]
        ss.append(s_refs[i][...] + jnp.where(blk == nbp - 1, blast_ref[...], far))
    m = s_new
    for s in ss:
        m = jnp.maximum(m, jnp.max(s, axis=-1, keepdims=True))
    w_new = jnp.exp(s_new - m)
    den = w_new
    num = w_new * vn_ref[...]
    for i in range(nsel):
        e = jnp.exp(ss[i] - m)
        den = den + jnp.sum(e, axis=-1, keepdims=True)
        eb = jnp.broadcast_to(e, (SUBLANES, MOBA_BLOCK)).astype(BF16)
        for r in range(ppb):
            pv = _dot_nt(eb[:, r * PAGE_SIZE:(r + 1) * PAGE_SIZE], v_refs[i * ppb + r][...].astype(BF16))
            num = num + pv[0:1]
    o_ref[...] = num / den


def _moba_step(q, k_new, v_new, cache_k, cache_v, page_table, bias_last, bias_scalars):
    B = q.shape[0]
    n_pages = page_table.shape[1]
    ppg, ppb, nsel = MOBA_STEP_PAGES, MOBA_PAGES_PER_BLOCK, MOBA_TOPK
    assert n_pages % ppg == 0 and ppg % ppb == 0
    nbp = n_pages // ppb
    assert nbp >= nsel
    pt = page_table.reshape(-1)
    q_rep = jnp.broadcast_to((q * (HD_C ** -0.5))[..., None], (B, H_C, HD_C, LANES))
    kpage = lambda r: pl.BlockSpec((None, H_C, HD_C, PAGE_SIZE),
                                   lambda b, s, pt, r=r: (pt[b * n_pages + s * ppg + r], 0, 0, 0))
    scores, sel = pl.pallas_call(
        functools.partial(_moba_score_kernel, n_pages=n_pages),
        grid_spec=pltpu.PrefetchScalarGridSpec(
            num_scalar_prefetch=1,
            grid=(B, n_pages // ppg),
            in_specs=[pl.BlockSpec((None, H_C, HD_C, LANES), lambda b, s, pt: (b, 0, 0, 0))]
                     + [kpage(r) for r in range(ppg)],
            out_specs=[pl.BlockSpec((None, H_C, ppg * PAGE_SIZE), lambda b, s, pt: (b, 0, s)),
                       pl.BlockSpec((None, nsel, H_C, LANES), lambda b, s, pt: (b, 0, 0, 0))],
            scratch_shapes=[pltpu.VMEM((nbp, H_C, LANES), F32)],
        ),
        out_shape=[jax.ShapeDtypeStruct((B, H_C, n_pages * PAGE_SIZE), F32),
                   jax.ShapeDtypeStruct((B, nsel, H_C, LANES), F32)],
        compiler_params=_cparams("parallel", "arbitrary"),
        name="moba_step_score",
    )(pt, q_rep, *([cache_k] * ppg))
    sel_flat = jnp.swapaxes(sel[:, :, :, 0], 1, 2).astype(jnp.int32).reshape(-1)
    tok = pl.BlockSpec((None, None, 1, HD_C), lambda b, h, pt, sl: (b, h, 0, 0))
    srow = lambda i: pl.BlockSpec((None, None, None, 1, MOBA_BLOCK),
                                  lambda b, h, pt, sl, i=i: (b, h, sl[(b * H_C + h) * nsel + i], 0, 0))
    vpage = lambda i, r: pl.BlockSpec(
        (None, None, HD_C, PAGE_SIZE),
        lambda b, h, pt, sl, i=i, r=r: (pt[b * n_pages + sl[(b * H_C + h) * nsel + i] * ppb + r], h, 0, 0))
    row = lambda a: a.reshape(B, H_C, 1, HD_C)
    out = pl.pallas_call(
        functools.partial(_moba_attend_kernel, nbp=nbp),
        grid_spec=pltpu.PrefetchScalarGridSpec(
            num_scalar_prefetch=2,
            grid=(B, H_C),
            in_specs=[tok, tok, tok,
                      pl.BlockSpec((None, 1, MOBA_BLOCK), lambda b, h, pt, sl: (h, 0, 0)),
                      pl.BlockSpec(memory_space=pltpu.SMEM)]
                     + [srow(i) for i in range(nsel)]
                     + [vpage(i, r) for i in range(nsel) for r in range(ppb)],
            out_specs=tok,
        ),
        out_shape=jax.ShapeDtypeStruct((B, H_C, 1, HD_C), F32),
        compiler_params=_cparams("parallel", "arbitrary"),
        name="moba_step_attend",
    )(pt, sel_flat, row(q), row(k_new), row(v_new), bias_last, bias_scalars,
      *([scores.reshape(B, H_C, nbp, 1, MOBA_BLOCK)] * nsel), *([cache_v] * (nsel * ppb)))
    return out.reshape(B, H_C, HD_C)


def _s5_kernel(u_ref, h0r_ref, h0i_ref, wb_ref, ar_ref, ai_ref, wc_ref, d_ref, wglu_ref,
               o_ref, hr_ref, hi_ref, x_ref, *, tt, nb):
    ti = pl.program_id(0)
    ns = D_S5_STATE

    @pl.when(ti == 0)
    def _():
        hr_ref[...] = h0r_ref[...]
        hi_ref[...] = h0i_ref[...]

    nslab = ns // LANES
    if tt == 1:
        seqs, seq_rows, mat_rows, cols = [0], lambda b: slice(None), lambda b: slice(None), lambda b: slice(None)
        u = u_ref[...]
    else:
        seqs = range(nb)
        seq_rows = lambda b: pl.ds(b, tt, stride=nb)
        mat_rows = lambda b: slice(b * tt, (b + 1) * tt)
        cols = lambda b: slice(b * D_S5, (b + 1) * D_S5)
        u = jnp.concatenate([u_ref[:, cols(b)] for b in seqs], axis=0)
    x = _dot(u.astype(BF16), wb_ref[...])
    for b in seqs:
        for c in range(2 * nslab):
            x_ref[c, seq_rows(b), :] = x[mat_rows(b), c * LANES:(c + 1) * LANES]
    slabs = lambda ref: jnp.stack([ref[:, c * LANES:(c + 1) * LANES] for c in range(nslab)])
    ar = slabs(ar_ref)
    ai = slabs(ai_ref)

    def step(t, carry):
        hr, hi = carry
        rows = pl.ds(pl.multiple_of(t * nb, nb), nb)
        nr = ar * hr - ai * hi + x_ref[0:nslab, rows, :]
        ni = ar * hi + ai * hr + x_ref[nslab:2 * nslab, rows, :]
        x_ref[0:nslab, rows, :] = nr
        x_ref[nslab:2 * nslab, rows, :] = ni
        return nr, ni

    hr, hi = lax.fori_loop(0, tt, step, (slabs(hr_ref), slabs(hi_ref)))
    for c in range(nslab):
        hr_ref[:, c * LANES:(c + 1) * LANES] = hr[c]
        hi_ref[:, c * LANES:(c + 1) * LANES] = hi[c]

    hs = [jnp.concatenate([x_ref[c, seq_rows(b), :] for c in range(2 * nslab)], axis=-1) for b in seqs]
    h = hs[0] if len(hs) == 1 else jnp.concatenate(hs, axis=0)
    y = jax.nn.gelu(_dot(h.astype(BF16), wc_ref[...]) + d_ref[...] * u)
    gl = _dot(y.astype(BF16), wglu_ref[...])
    o = gl[:, 0:D_S5] * jax.nn.sigmoid(gl[:, D_S5:2 * D_S5])
    for b in seqs:
        o_ref[:, cols(b)] = o[mat_rows(b), :]


def _s5(u, h0r, h0i, wb, ar, ai, wc, d, wglu, *, tt, nb):
    L = u.shape[0]
    const = lambda a: pl.BlockSpec(a.shape, lambda i: (0,) * a.ndim)
    rows = pl.BlockSpec((tt, nb * D_S5), lambda i: (i, 0))
    st = pl.BlockSpec((nb, D_S5_STATE), lambda i: (0, 0))
    args = (u, h0r, h0i, wb, ar, ai, wc, d, wglu)
    if tt == 1:
        assert L == 1
        rows = pl.BlockSpec((nb, D_S5), lambda i: (0, 0))
        args = (u.reshape(nb, D_S5),) + args[1:]
    return pl.pallas_call(
        functools.partial(_s5_kernel, tt=tt, nb=nb),
        grid=(L // tt,),
        in_specs=[rows] + [const(a) for a in args[1:]],
        out_specs=[rows, st, st],
        out_shape=[jax.ShapeDtypeStruct(args[0].shape, F32),
                   jax.ShapeDtypeStruct((nb, D_S5_STATE), F32), jax.ShapeDtypeStruct((nb, D_S5_STATE), F32)],
        scratch_shapes=[pltpu.VMEM((2 * D_S5_STATE // LANES, tt * nb, LANES), F32)],
        compiler_params=_cparams("arbitrary"),
        name="s5",
    )(*args)


def _rel_bucket(dist):
    n = jnp.maximum(dist, 0)
    exact = N_BUCKETS // 2
    nf = jnp.maximum(n, exact).astype(F32)
    large = exact + (jnp.log(nf / exact) / math.log(MAX_DISTANCE / exact) * (N_BUCKETS - exact)).astype(jnp.int32)
    return jnp.where(n < exact, n, jnp.minimum(large, N_BUCKETS - 1))


def _block_diag(w):
    G, a, b = w.shape
    eye = jnp.eye(G, dtype=w.dtype)
    return (eye[:, None, :, None] * w[:, :, None, :]).reshape(G * a, G * b)


def _s5_weights(A_re, A_im, log_dt, B_re, B_im, C_re, C_im, D):
    dt = jnp.exp(log_dt.astype(F32))[:, None]
    lr, li = A_re.astype(F32), A_im.astype(F32)
    mag = jnp.exp(lr * dt)
    ar, ai = mag * jnp.cos(li * dt), mag * jnp.sin(li * dt)
    den = lr * lr + li * li
    nr = ar - 1.0
    cr, ci = (nr * lr + ai * li) / den, (ai * lr - nr * li) / den
    bbr = cr[..., None] * B_re - ci[..., None] * B_im
    bbi = cr[..., None] * B_im + ci[..., None] * B_re
    wb = jnp.concatenate([_block_diag(bbr.transpose(0, 2, 1)), _block_diag(bbi.transpose(0, 2, 1))], axis=1)
    wc = jnp.concatenate([_block_diag(C_re.transpose(0, 2, 1)), -_block_diag(C_im.transpose(0, 2, 1))], axis=0)
    return (wb.astype(BF16), ar.reshape(1, -1), ai.reshape(1, -1), wc.astype(BF16), D.reshape(1, -1).astype(F32))


def _pad_rows_front(a, rows):
    return jnp.pad(a, ((0, 0), (rows - a.shape[1], 0), (0, 0)))


def _trunk(x, st, W, *, decode):
    B, L, D = x.shape
    T = B * L
    tm = min(512, T)
    out = {}

    x2 = x.reshape(T, D)
    z = _norm_proj(x2, W['norm_mix'][0], W['ev_w_in'], tm=tm, tn=EV_IN_TILE, name="ev_in_proj")
    zc = z.shape[1]
    if decode:
        lp = SUBLANES
        z3 = jnp.pad(z.reshape(B, L, zc), ((0, 0), (0, lp - L), (0, 0)))
        tt = lp
    else:
        lp = L
        z3 = z.reshape(B, L, zc)
        tt = min(512, L)
    conv0 = _pad_rows_front(st['gdn_conv'], SUBLANES)
    mixed, tail, sg, sh_t = _even_mixer(z3, conv0, st['gdn'], jnp.swapaxes(st['hgrn'], -1, -2), W['gdn_conv_w'],
                                        W['gdn_neg_a'], W['gdn_dt_bias'], W['gdn_norm'], W['hgrn_norm'], W['hgrn_lb'],
                                        l_real=L, tt=tt)
    out['gdn_conv'] = tail[:, SUBLANES - (GDN_CONV - 1):, :]
    out['gdn'] = sg
    out['hgrn'] = jnp.swapaxes(sh_t, -1, -2)
    if decode:
        parts = [(mixed[:, :L].reshape(T, 2 * D_A), W['ev_w_out'], pl.BlockSpec((tm, 2 * D_A), lambda i: (i, 0)))]
    else:
        parts = [(mixed, W['ev_w_out'], pl.BlockSpec((None, tm, 2 * D_A), lambda b, i: (b, i, 0)))]
    x2, fs = _post_mixer(x2, parts, st, W, 0, B, L, decode)
    out['ffn0'] = fs

    dc = H_C * HD_C
    gqk = jnp.concatenate([jnp.tile(W['moba_gq'], H_C), jnp.tile(W['moba_gk'], H_C)])
    u_tm = _norm_proj(x2, W['norm_mix'][1], W['od_w_in'][:, 3 * dc:], tm=tm, tn=D_S5,
                      time_major=None if decode else (B, L), name="od_in_proj_u")
    u_tm = u_tm.reshape(L, B * D_S5)
    if decode:
        (qkv,) = _od_qkv_proj(x2, W['norm_mix'][1], W['od_w_in'][:, :3 * dc], gqk, tm=tm)
        out['moba_k'] = qkv[:, dc:2 * dc].reshape(B, L, H_C, HD_C)
        out['moba_v'] = qkv[:, 2 * dc:].reshape(B, L, H_C, HD_C)
        heads = lambda a: a.reshape(B, H_C, HD_C)
        o_c = _moba_step(heads(qkv[:, :dc]), heads(qkv[:, dc:2 * dc]), heads(qkv[:, 2 * dc:]),
                         st['cache_k'], st['cache_v'], st['page_table'], *W['moba_bias_step'])
        o_c = o_c.reshape(T, dc)
        tts = 1
    else:
        qkv, kt, vt = _od_qkv_proj(x2, W['norm_mix'][1], W['od_w_in'][:, :3 * dc], gqk, tm=tm, seq=(B, L))
        out['moba_k'] = jnp.transpose(kt.reshape(B, H_C, HD_C, L), (0, 3, 1, 2))
        out['moba_v'] = jnp.transpose(vt.reshape(B, H_C, HD_C, L), (0, 3, 1, 2))
        o_c = _moba_seq(qkv.reshape(B, L, 3 * dc), vt, W['moba_bias_tiles']).reshape(T, dc)
        tts = min(64, L)
    o_d, hr, hi = _s5(u_tm, st['s5_re'], st['s5_im'], *W['s5'], W['s5_glu_w'], tt=tts, nb=B)
    out['s5_re'] = hr.reshape(B, S5_GROUPS, S5_STATE)
    out['s5_im'] = hi.reshape(B, S5_GROUPS, S5_STATE)
    if decode:
        parts = [(o_c, W['od_w_out'][:dc], pl.BlockSpec((tm, dc), lambda i: (i, 0))),
                 (o_d, W['od_w_out'][dc:], pl.BlockSpec((tm, D_S5), lambda i: (i, 0)))]
    else:
        parts = [(o_c.reshape(B, L, dc), W['od_w_out'][:dc], pl.BlockSpec((None, tm, dc), lambda b, i: (b, i, 0))),
                 (o_d, W['od_w_out'][dc:], pl.BlockSpec((tm, D_S5), lambda b, i: (i, b)))]
    x2, fs = _post_mixer(x2, parts, st, W, 1, B, L, decode)
    out['ffn1'] = fs
    out['y'] = x2.reshape(B, L, D)
    return out


def _post_mixer(x2, parts, st, W, l, B, L, decode):
    T, D = x2.shape
    dq = MEM_H * MEM_HD
    buf = st['ffn_conv'][l]
    if decode:
        x2 = _proj_res(x2, parts, tm=T, name="mixer_out_proj")
        q = _norm_proj(x2, W['norm_mem'][l], W['mem_wq'][l], tm=T, tn=dq, n_norm_tiles=1, group=MEM_HD,
                       gain=jnp.tile(W['mem_gq'][l], MEM_H), name="mem_q_proj")
        o = _mem_attn_step(q.reshape(B, 1, dq), st['mem_k'], st['mem_v'], l).reshape(T, dq)
        x3 = _proj_res(x2, [(o, W['mem_wo'][l], pl.BlockSpec((T, dq), lambda i: (i, 0)))], tm=T, name="mem_out_proj")
        y, gate = _ffn_step(x3, buf[:, 0], buf[:, 1], W['norm_ffn'][l], W['ffn_w_gu'][l],
                            W['ffn_conv_w'][l], W['ffn_conv_b'][l], W['ffn_w_down'][l])
        fs = jnp.stack([buf[:, 1], gate], axis=1)
        return y, fs
    y, tail = _post_seq(x2.reshape(B, L, D), parts, st['mem_k'][l], st['mem_v'][l], W['norm_mem'][l], W['mem_wq'][l],
                        W['mem_gq'][l], W['mem_wo'][l], _pad_rows_front(buf, SUBLANES), W['norm_ffn'][l],
                        W['ffn_w_gu'][l], W['ffn_conv_w'][l], W['ffn_conv_b'][l], W['ffn_w_down'][l],
                        tm=min(512, L), name="post_mixer_seq")
    return y.reshape(T, D), tail[:, SUBLANES - (FFN_CONV - 1):, :]


def _mem_kv(mem, g_norm, w_kv, g_k):
    B, M, D = mem.shape
    dq = MEM_H * MEM_HD
    gain = jnp.concatenate([jnp.tile(g_k, MEM_H), jnp.ones((dq,), F32)])
    kv = _norm_proj(mem.reshape(B * M, D), g_norm, w_kv, tm=min(512, B * M), tn=dq, n_norm_tiles=1, group=MEM_HD,
                    gain=gain, name="mem_kv_proj")
    return kv[:, :dq].reshape(B, M, dq), kv[:, dq:].reshape(B, M, dq)


def kernel(x_prompt, x_sample, state_gdn_conv, state_gdn, state_hgrn, cache_moba_k, cache_moba_v, state_s5_re, state_s5_im, cache_mem_k, cache_mem_v, state_ffn_conv, page_table, mem_prompt, rel_bias, hgrn_lb_raw, norm_mix, norm_mem, norm_memkv, norm_ffn, mem_wq, mem_wkv, mem_gq, mem_gk, mem_wo, ffn_w_gu, ffn_conv_w, ffn_conv_b, ffn_w_down, ev_w_in, ev_w_out, gdn_conv_w, gdn_A_log, gdn_dt_bias, gdn_norm, hgrn_norm, od_w_in, od_w_out, moba_gq, moba_gk, s5_A_re, s5_A_im, s5_log_dt, s5_B_re, s5_B_im, s5_C_re, s5_C_im, s5_D, s5_glu_w):
    Bp, Lq, D = x_prompt.shape
    Bs = x_sample.shape[0]
    dc = H_C * HD_C
    dq = MEM_H * MEM_HD
    n_pages = page_table.shape[1]
    past = n_pages * PAGE_SIZE

    d3 = 3 * D_A
    w_in = ev_w_in[0]
    ba_cols = w_in[:, 4 * D_A:4 * D_A + 2 * H_A]
    w_in = jnp.concatenate([w_in[:, :4 * D_A], w_in[:, 4 * D_A + 2 * H_A:], ba_cols,
                            jnp.zeros((D, LANES - 2 * H_A), F32)], axis=1)
    lb_all = jnp.cumsum(jax.nn.softmax(hgrn_lb_raw.astype(F32), axis=0), axis=0)
    bias_t = rel_bias.T.astype(F32)
    nblk = MOBA_BLOCK
    by_rel = bias_t[:, _rel_bucket(jnp.arange(2 * nblk, dtype=jnp.int32))]
    mm = jnp.arange(2 * nblk, dtype=jnp.int32)

    def toeplitz(shift):
        y = by_rel[:, (shift + mm) % (2 * nblk)]
        flat = jnp.tile(y, (1, nblk))[:, :nblk * (2 * nblk - 1)]
        return flat.reshape(H_C, nblk, 2 * nblk - 1)[:, :, :nblk]

    far = bias_t[:, _rel_bucket(jnp.int32(2 * nblk))]
    tiles = jnp.stack([toeplitz(0), toeplitz(nblk)], axis=1) - far[:, None, None, None]
    bias_step = (by_rel[:, nblk - jnp.arange(nblk, dtype=jnp.int32)][:, None, :],
                 jnp.stack([far, bias_t[:, 0]], axis=1))
    W = dict(
        norm_mix=norm_mix, norm_mem=norm_mem, norm_ffn=norm_ffn,
        ev_w_in=w_in.astype(BF16), ev_w_out=ev_w_out[0].astype(BF16),
        gdn_conv_w=gdn_conv_w[0], gdn_neg_a=-jnp.exp(gdn_A_log[0].astype(F32)), gdn_dt_bias=gdn_dt_bias[0],
        gdn_norm=gdn_norm[0], hgrn_norm=hgrn_norm[0], hgrn_lb=lb_all[0],
        od_w_in=od_w_in[0].astype(BF16), od_w_out=od_w_out[0].astype(BF16),
        moba_gq=moba_gq[0], moba_gk=moba_gk[0], moba_bias_tiles=tiles, moba_bias_step=bias_step,
        s5=_s5_weights(s5_A_re[0], s5_A_im[0], s5_log_dt[0], s5_B_re[0], s5_B_im[0], s5_C_re[0], s5_C_im[0], s5_D[0]),
        s5_glu_w=s5_glu_w[0].astype(BF16),
        mem_wq=mem_wq.astype(BF16), mem_gq=mem_gq, mem_wo=mem_wo.astype(BF16),
        ffn_w_gu=ffn_w_gu.astype(BF16), ffn_conv_w=ffn_conv_w, ffn_conv_b=ffn_conv_b, ffn_w_down=ffn_w_down.astype(BF16),
    )

    mk_l, mv_l = [], []
    for l in range(2):
        mk, mv = _mem_kv(mem_prompt, norm_memkv[l], mem_wkv[l].astype(BF16), mem_gk[l])
        mk_l.append(mk)
        mv_l.append(mv)
    p_mem_k, p_mem_v = jnp.stack(mk_l), jnp.stack(mv_l)
    zeros = lambda *s: jnp.zeros(s, F32)
    st_p = dict(gdn_conv=zeros(Bp, GDN_CONV - 1, d3), gdn=zeros(Bp, H_A, HD_A, HD_A), hgrn=zeros(Bp, H_B, DK_B, DV_B),
                s5_re=zeros(Bp, D_S5_STATE), s5_im=zeros(Bp, D_S5_STATE), mem_k=p_mem_k, mem_v=p_mem_v,
                ffn_conv=zeros(2, Bp, FFN_CONV - 1, D_FF))
    op = _trunk(x_prompt, st_p, W, decode=False)

    st_s = dict(gdn_conv=state_gdn_conv[0], gdn=state_gdn[0], hgrn=state_hgrn[0],
                s5_re=state_s5_re[0].reshape(Bs, D_S5_STATE), s5_im=state_s5_im[0].reshape(Bs, D_S5_STATE),
                mem_k=cache_mem_k, mem_v=cache_mem_v,
                ffn_conv=state_ffn_conv,
                cache_k=jnp.transpose(cache_moba_k[0], (0, 2, 3, 1)), cache_v=jnp.transpose(cache_moba_v[0], (0, 2, 3, 1)),
                page_table=page_table)
    os_ = _trunk(x_sample, st_s, W, decode=True)

    M = mem_prompt.shape[1]
    return (op['y'], os_['y'],
            op['gdn_conv'][None], op['gdn'][None], op['hgrn'][None], op['moba_k'][None], op['moba_v'][None],
            op['s5_re'][None], op['s5_im'][None],
            p_mem_k.reshape(2, Bp, M, MEM_H, MEM_HD), p_mem_v.reshape(2, Bp, M, MEM_H, MEM_HD),
            jnp.stack([op['ffn0'], op['ffn1']]),
            os_['gdn_conv'][None], os_['gdn'][None], os_['hgrn'][None], os_['moba_k'][None], os_['moba_v'][None],
            os_['s5_re'][None], os_['s5_im'][None], jnp.stack([os_['ffn0'], os_['ffn1']]))
```

```python
import functools
import math

import jax
import jax.numpy as jnp
from jax import lax
from jax.experimental import pallas as pl
from jax.experimental.pallas import tpu as pltpu

F32 = jnp.float32
BF16 = jnp.bfloat16
HI = lax.Precision.HIGHEST
EPS = 1e-6

LANES = 128
SUBLANES = 8
VMEM_LIMIT_BYTES = 56 * 1024 * 1024

D_MODEL = 1024
H_A, HD_A, GDN_CONV = 8, 64, 4
H_B, DK_B, DV_B = 8, 64, 64
GDN_CHUNK, HGRN_CHUNK = 64, 32
GDN_GROUP, HGRN_GROUP = 2, 4
H_C, HD_C = 8, 64
MOBA_BLOCK, MOBA_TOPK = 256, 3
N_BUCKETS, MAX_DISTANCE = 32, 128
S5_GROUPS, S5_GROUP_CH, S5_STATE = 32, 16, 64
MEM_H, MEM_HD = 4, 128
D_FF, FFN_CONV = 2816, 3
PAGE_SIZE = 128
D_A = H_A * HD_A
D_S5 = S5_GROUPS * S5_GROUP_CH
D_S5_STATE = S5_GROUPS * S5_STATE
S5_MM_SPLIT = 4
FF_CHUNK = D_FF // 2
EV_IN_COLS = 8 * D_A + LANES
EV_IN_TILE = EV_IN_COLS // 3


def _cparams(*sem):
    return pltpu.CompilerParams(dimension_semantics=sem, vmem_limit_bytes=VMEM_LIMIT_BYTES)


def _dot(a, b, precision=None):
    return jnp.dot(a, b, preferred_element_type=F32, precision=precision)


def _dot_nt(a, b, precision=None):
    return lax.dot_general(a, b, (((1,), (1,)), ((), ())), preferred_element_type=F32, precision=precision)


def _dot_tn(a, b, precision=None):
    return lax.dot_general(a, b, (((0,), (0,)), ((), ())), preferred_element_type=F32, precision=precision)


def _rms(x, g):
    return x * lax.rsqrt(jnp.mean(x * x, axis=-1, keepdims=True) + EPS) * g


def _silu(x):
    return x * jax.nn.sigmoid(x)


def _tri(n, strict=False):
    r = lax.broadcasted_iota(jnp.int32, (n, n), 0)
    c = lax.broadcasted_iota(jnp.int32, (n, n), 1)
    return (r > c) if strict else (r >= c)


def _norm_proj_kernel(x_ref, g_ref, w_ref, e_ref, gain_ref, o_ref, *, n_norm_tiles, group):
    xn = _rms(x_ref[...], g_ref[...])
    z = _dot(xn.astype(BF16), w_ref[...])
    if n_norm_tiles == 0:
        o_ref[...] = z
        return
    j = pl.program_id(1)

    @pl.when(j < n_norm_tiles)
    def _():
        zz = z * z
        hi = zz.astype(BF16)
        lo = (zz - hi.astype(F32)).astype(BF16)
        s = _dot(hi, e_ref[...]) + _dot(lo, e_ref[...])
        o_ref[...] = z * lax.rsqrt(s * (1.0 / group) + EPS) * gain_ref[...]

    @pl.when(j >= n_norm_tiles)
    def _():
        o_ref[...] = z


def _norm_proj(x, g, w, *, tm, tn, n_norm_tiles=0, group=1, gain=None, time_major=None, name):
    T, D = x.shape
    N = w.shape[1]
    nj = N // tn
    if gain is None:
        gain = jnp.ones((N,), F32)
    gi = lax.broadcasted_iota(jnp.int32, (tn, tn), 0) // group
    gj = lax.broadcasted_iota(jnp.int32, (tn, tn), 1) // group
    e = (gi == gj).astype(BF16)
    if time_major is None:
        out_shape = jax.ShapeDtypeStruct((T, N), F32)
        out_spec = pl.BlockSpec((tm, tn), lambda i, j: (i, j))
    else:
        B, L = time_major
        nt = L // tm
        out_shape = jax.ShapeDtypeStruct((L, B * N), F32)
        out_spec = pl.BlockSpec((tm, tn), lambda i, j: (i % nt, (i // nt) * nj + j))
    return pl.pallas_call(
        functools.partial(_norm_proj_kernel, n_norm_tiles=n_norm_tiles, group=group),
        grid=(T // tm, nj),
        in_specs=[
            pl.BlockSpec((tm, D), lambda i, j: (i, 0)),
            pl.BlockSpec((1, D), lambda i, j: (0, 0)),
            pl.BlockSpec((D, tn), lambda i, j: (0, j)),
            pl.BlockSpec((tn, tn), lambda i, j: (0, 0)),
            pl.BlockSpec((1, tn), lambda i, j: (0, j)),
        ],
        out_specs=out_spec,
        out_shape=out_shape,
        compiler_params=_cparams("parallel", "arbitrary"),
        name=name,
    )(x, g.reshape(1, D), w, e, gain.reshape(1, N))


def _norm_proj_wide_kernel(x_ref, g_ref, w_ref, o_ref, *, chunk):
    xn = _rms(x_ref[...], g_ref[...]).astype(BF16)
    for c in range(w_ref.shape[1] // chunk):
        cs = slice(c * chunk, (c + 1) * chunk)
        o_ref[:, cs] = _dot(xn, w_ref[:, cs])


def _norm_proj_wide(x, g, w, *, tm, chunk, name):
    T, D = x.shape
    N = w.shape[1]
    const = lambda shape: pl.BlockSpec(shape, lambda i: (0, 0), pipeline_mode=pl.Buffered(1))
    return pl.pallas_call(
        functools.partial(_norm_proj_wide_kernel, chunk=chunk),
        grid=(T // tm,),
        in_specs=[pl.BlockSpec((tm, D), lambda i: (i, 0)), const((1, D)), const((D, N))],
        out_specs=pl.BlockSpec((tm, N), lambda i: (i, 0)),
        out_shape=jax.ShapeDtypeStruct((T, N), F32),
        compiler_params=_cparams("parallel"),
        name=name,
    )(x, g.reshape(1, D), w)


def _od_qkv_kernel(x_ref, g_ref, w_ref, e_ref, gain_ref, o_ref, *t_refs):
    dc = H_C * HD_C
    xn = _rms(x_ref[...], g_ref[...]).astype(BF16)
    for c in range(3):
        cs = slice(c * dc, (c + 1) * dc)
        z = _dot(xn, w_ref[:, cs])
        if c < 2:
            zz = z * z
            hi = zz.astype(BF16)
            lo = (zz - hi.astype(F32)).astype(BF16)
            s = _dot(hi, e_ref[...]) + _dot(lo, e_ref[...])
            z = z * lax.rsqrt(s * (1.0 / HD_C) + EPS) * gain_ref[:, cs]
        o_ref[:, cs] = z
        if t_refs and c > 0:
            t_refs[c - 1][...] = z.T


def _od_qkv_proj(x, g, w, gain, *, tm, seq=None):
    T, D = x.shape
    dc = H_C * HD_C
    gi = lax.broadcasted_iota(jnp.int32, (dc, dc), 0) // HD_C
    gj = lax.broadcasted_iota(jnp.int32, (dc, dc), 1) // HD_C
    e = (gi == gj).astype(BF16)
    const = lambda shape: pl.BlockSpec(shape, lambda i: (0, 0), pipeline_mode=pl.Buffered(1))
    out_specs = [pl.BlockSpec((tm, 3 * dc), lambda i: (i, 0))]
    out_shape = [jax.ShapeDtypeStruct((T, 3 * dc), F32)]
    if seq is not None:
        B, L = seq
        nt = L // tm
        tspec = pl.BlockSpec((None, dc, tm), lambda i: (i // nt, 0, i % nt))
        out_specs += [tspec, tspec]
        out_shape += [jax.ShapeDtypeStruct((B, dc, L), F32)] * 2
    return pl.pallas_call(
        _od_qkv_kernel,
        grid=(T // tm,),
        in_specs=[pl.BlockSpec((tm, D), lambda i: (i, 0)), const((1, D)), const((D, 3 * dc)), const((dc, dc)),
                  const((1, 2 * dc))],
        out_specs=out_specs,
        out_shape=out_shape,
        compiler_params=_cparams("parallel"),
        name="od_in_proj_qkv",
    )(x, g.reshape(1, D), w, e, gain.reshape(1, 2 * dc))


def _proj_res_kernel(*refs, n_in):
    x_ref, o_ref = refs[0], refs[-1]
    acc = x_ref[...]
    for k in range(n_in):
        a_ref, w_ref = refs[1 + 2 * k], refs[2 + 2 * k]
        acc = acc + _dot(a_ref[...].astype(BF16), w_ref[...])
    o_ref[...] = acc


def _proj_res(x, parts, *, tm, name):
    T, N = x.shape
    in_specs = [pl.BlockSpec((tm, N), lambda i: (i, 0))]
    args = [x]
    for a, w, spec in parts:
        in_specs += [spec, pl.BlockSpec(w.shape, lambda i: (0, 0))]
        args += [a, w]
    return pl.pallas_call(
        functools.partial(_proj_res_kernel, n_in=len(parts)),
        grid=(T // tm,),
        in_specs=in_specs,
        out_specs=pl.BlockSpec((tm, N), lambda i: (i, 0)),
        out_shape=jax.ShapeDtypeStruct((T, N), F32),
        compiler_params=_cparams("parallel"),
        name=name,
    )(*args)


def _neumann_inverse(mats, n):
    m = mats[0].shape[0]
    eye = (lax.broadcasted_iota(jnp.int32, (m, m), 0) == lax.broadcasted_iota(jnp.int32, (m, m), 1)).astype(F32)
    ts = [eye - a for a in mats]
    ps = [a.astype(BF16) for a in mats]
    k = 2
    while k < n:
        ps = [_dot(p, p).astype(BF16) for p in ps]
        ts = [t + _dot(t.astype(BF16), p) for t, p in zip(ts, ps)]
        k *= 2
    return ts


def _even_mixer_kernel(qkv_ref, zg_ref, hq_ref, hf_ref, hi_ref, hg_ref, ba_ref,
                       conv0_ref, sg0_ref, sh0_ref,
                       convw_ref, nega_ref, dtb_ref, gnorm_ref, hnorm_ref, lb_ref,
                       mixed_ref, tail_ref, sg_ref, sh_ref,
                       xp_ref, qkvc_ref, beta_ref, g_ref, sbd_ref,
                       *, tt, cg, ch, l_real):
    ti = pl.program_id(1)
    n_tiles = pl.num_programs(1)

    @pl.when(ti == 0)
    def _():
        xp_ref[0:SUBLANES, :] = conv0_ref[...]
        sh_ref[...] = sh0_ref[...]
        zero = jnp.zeros((HD_A, HD_A), F32)
        for p in range(H_A // 2):
            sbd_ref[p] = jnp.concatenate([jnp.concatenate([sg0_ref[2 * p], zero], axis=-1),
                                          jnp.concatenate([zero, sg0_ref[2 * p + 1]], axis=-1)], axis=0)

    xp_ref[SUBLANES:SUBLANES + tt, :] = qkv_ref[...]
    conv = convw_ref[GDN_CONV - 1:GDN_CONV, :] * qkv_ref[...]
    for j in range(GDN_CONV - 1):
        off = SUBLANES - (GDN_CONV - 1) + j
        conv = conv + convw_ref[j:j + 1, :] * xp_ref[off:off + tt, :]
    qkvc_ref[...] = _silu(conv)
    lr_last = l_real - (l_real - 1) // tt * tt
    tail_ref[...] = xp_ref[lr_last:lr_last + SUBLANES, :]
    carry = xp_ref[tt:tt + SUBLANES, :]
    xp_ref[0:SUBLANES, :] = carry

    padded = (l_real % tt) != 0
    if padded:
        row = lax.broadcasted_iota(jnp.int32, (tt, 1), 0) + ti * tt
        valid = row < l_real
    ba = ba_ref[...]
    beta_all = jax.nn.sigmoid(ba)
    g_all = nega_ref[...] * jax.nn.softplus(ba + dtb_ref[...])
    if padded:
        beta_all = jnp.where(valid, beta_all, 0.0)
        g_all = jnp.where(valid, g_all, 0.0)
    beta_ref[...] = beta_all
    g_ref[...] = g_all

    ltri_g = _tri(cg).astype(F32)
    hnorm = hnorm_ref[...]
    gnorm2 = jnp.concatenate([gnorm_ref[...], gnorm_ref[...]], axis=-1)
    pairs = range(H_A // 2)
    n2 = 2 * cg
    lane = lax.broadcasted_iota(jnp.int32, (1, LANES), 1)
    m0 = (lane < HD_A).astype(F32)
    m1 = 1.0 - m0
    ri = lax.broadcasted_iota(jnp.int32, (n2, n2), 0)
    cj = lax.broadcasted_iota(jnp.int32, (n2, n2), 1)
    same_head = (ri // cg) == (cj // cg)
    bd_tril = same_head & (ri >= cj)
    bd_stril = same_head & (ri > cj)

    def stack2(a):
        return jnp.concatenate([a * m0, a * m1], axis=0)

    def rep2(col0, col1):
        return jnp.concatenate([jnp.broadcast_to(col0, (cg, LANES)), jnp.broadcast_to(col1, (cg, LANES))], axis=0)

    ng = min(GDN_GROUP, tt // cg)

    def gdn_group(gi, _):
        items = [(c, p) for c in range(ng) for p in pairs]
        rows_c, gcum_c, gt_c, bet_c = [], [], [], []
        for c in range(ng):
            rows = pl.ds(pl.multiple_of((gi * ng + c) * cg, cg), cg)
            gcum = _dot(ltri_g, g_ref[rows, :], HI)
            gpad = gcum if cg == LANES else jnp.concatenate([gcum, jnp.zeros((LANES - cg, LANES), F32)], axis=0)
            rows_c.append(rows)
            gcum_c.append(gcum)
            gt_c.append(gpad.T)
            bet_c.append(beta_ref[rows, :])
        qs, ks, vs, betas, gcs, decays, glasts, glrows = {}, {}, {}, {}, {}, {}, {}, {}
        for it in items:
            c, p = it
            rows = rows_c[c]
            h0, h1 = H_A + 2 * p, H_A + 2 * p + 1
            q = stack2(qkvc_ref[rows, p * LANES:(p + 1) * LANES])
            k = stack2(qkvc_ref[rows, D_A + p * LANES:D_A + (p + 1) * LANES])
            qs[it] = q * lax.rsqrt(jnp.sum(q * q, axis=-1, keepdims=True) + EPS) * (HD_A ** -0.5)
            ks[it] = k * lax.rsqrt(jnp.sum(k * k, axis=-1, keepdims=True) + EPS)
            vs[it] = stack2(qkvc_ref[rows, 2 * D_A + p * LANES:2 * D_A + (p + 1) * LANES])
            betas[it] = rep2(bet_c[c][:, 2 * p:2 * p + 1], bet_c[c][:, 2 * p + 1:2 * p + 2])
            gc = rep2(gcum_c[c][:, h0:h0 + 1], gcum_c[c][:, h1:h1 + 1])
            gr = jnp.concatenate([gt_c[c][h0:h0 + 1, 0:cg], gt_c[c][h1:h1 + 1, 0:cg]], axis=-1)
            gcs[it] = gc
            glasts[it] = rep2(gcum_c[c][cg - 1:cg, h0:h0 + 1], gcum_c[c][cg - 1:cg, h1:h1 + 1])
            glrows[it] = jnp.exp(glasts[it][0:1, :]) * m0 + jnp.exp(glasts[it][cg:cg + 1, :]) * m1
            decays[it] = jnp.exp(jnp.where(bd_tril, gc[:, 0:n2] - gr, -jnp.inf))
        kqs = {it: _dot_nt(jnp.concatenate([ks[it], qs[it]], axis=0).astype(BF16), ks[it].astype(BF16))
               for it in items}
        ts = dict(zip(items, _neumann_inverse(
            [jnp.where(bd_stril, betas[it][:, 0:n2] * kqs[it][0:n2] * decays[it], 0.0) for it in items], cg)))
        egs = {it: jnp.exp(gcs[it]) for it in items}
        sols = {it: _dot(ts[it].astype(BF16),
                         jnp.concatenate([betas[it] * vs[it], (betas[it] * egs[it]) * ks[it]], axis=-1).astype(BF16))
                for it in items}
        wqs = {it: jnp.concatenate([sols[it][:, LANES:2 * LANES], qs[it] * egs[it]], axis=0).astype(BF16)
               for it in items}
        qks = {it: (kqs[it][n2:2 * n2] * decays[it]).astype(BF16) for it in items}
        kouts = {it: (ks[it] * jnp.exp(glasts[it] - gcs[it])).astype(BF16) for it in items}
        ss = [sbd_ref[p] for p in pairs]
        for c in range(ng):
            wss = [_dot(wqs[(c, p)], ss[p].astype(BF16)) for p in pairs]
            ubs = [(sols[(c, p)][:, 0:LANES] - wss[p][0:n2]).astype(BF16) for p in pairs]
            os_ = [wss[p][n2:2 * n2] + _dot(qks[(c, p)], ubs[p]) for p in pairs]
            ss = [ss[p] * glrows[(c, p)] + _dot_tn(kouts[(c, p)], ubs[p]) for p in pairs]
            for p in pairs:
                o = os_[p]
                on = o * lax.rsqrt(jnp.sum(o * o, axis=-1, keepdims=True) * (1.0 / HD_A) + EPS)
                cols = slice(p * LANES, (p + 1) * LANES)
                mixed_ref[rows_c[c], cols] = (on[0:cg] + on[cg:n2]) * gnorm2 * _silu(zg_ref[rows_c[c], cols])
        for p in pairs:
            sbd_ref[p] = ss[p]
        return 0

    lax.fori_loop(0, tt // (cg * ng), gdn_group, 0)
    for p in pairs:
        sg_ref[2 * p] = sbd_ref[p, 0:HD_A, 0:HD_A]
        sg_ref[2 * p + 1] = sbd_ref[p, HD_A:2 * HD_A, HD_A:2 * HD_A]

    nh = min(HGRN_GROUP, tt // ch)
    gr = nh * ch
    ri = lax.broadcasted_iota(jnp.int32, (gr, gr), 0)
    cj = lax.broadcasted_iota(jnp.int32, (gr, gr), 1)
    same_chunk_causal = (ri >= cj) & ((ri // ch) == (cj // ch))
    ltri_h = same_chunk_causal.astype(F32)
    lb = lb_ref[...]
    r_mid = (ch - 1) // 2

    def hgrn_group(gi, _):
        r0 = pl.multiple_of(gi * gr, gr)
        rows = pl.ds(r0, gr)
        f = lb + (1.0 - lb) * jax.nn.sigmoid(hf_ref[rows, :])
        logf = jnp.log(f)
        kk = 1.0 - f
        if padded:
            vrow = (lax.broadcasted_iota(jnp.int32, (gr, 1), 0) + r0 + ti * tt) < l_real
            logf = jnp.where(vrow, logf, 0.0)
            kk = jnp.where(vrow, kk, 0.0)
        qh = _silu(hq_ref[rows, :])
        bc = _dot(ltri_h, logf, HI).reshape(nh, ch, D_A)
        bref = bc[:, r_mid:r_mid + 1, :]
        blast = bc[:, ch - 1:ch, :]
        grp = lambda a: a.reshape(gr, D_A)
        qh3, kk3 = qh.reshape(nh, ch, D_A), kk.reshape(nh, ch, D_A)
        qa = grp(qh3 * jnp.exp(bc - bref)).astype(BF16)
        kb = grp(kk3 * jnp.exp(bref - bc)).astype(BF16)
        q_in = grp(qh3 * jnp.exp(bc)).astype(BF16)
        k_out = grp(kk3 * jnp.exp(blast - bc)).astype(BF16)
        fl = jnp.exp(blast)
        vv = hi_ref[rows, :].astype(BF16)
        heads = range(H_B)
        sls = [slice(h * DK_B, (h + 1) * DK_B) for h in heads]
        atts = [jnp.where(same_chunk_causal, _dot_nt(qa[:, sl], kb[:, sl]), 0.0).astype(BF16) for sl in sls]
        intra = [_dot(atts[h], vv[:, sls[h]]) for h in heads]
        sts = [sh_ref[h] for h in heads]
        inter = [[] for _ in heads]
        for c in range(nh):
            rc = slice(c * ch, (c + 1) * ch)
            for h in heads:
                inter[h].append(_dot_nt(q_in[rc, sls[h]], sts[h].astype(BF16)))
            sts = [sts[h] * fl[c][:, sls[h]] + _dot_tn(vv[rc, sls[h]], k_out[rc, sls[h]]) for h in heads]
        for h in heads:
            sh_ref[h] = sts[h]
        os_ = [intra[h] + (inter[h][0] if nh == 1 else jnp.concatenate(inter[h], axis=0)) for h in heads]
        outs = [_rms(os_[h], hnorm) * _silu(hg_ref[rows, h * DV_B:(h + 1) * DV_B]) for h in heads]
        for p in range(H_B // 2):
            c0 = D_A + p * LANES
            mixed_ref[rows, c0:c0 + LANES] = jnp.concatenate([outs[2 * p], outs[2 * p + 1]], axis=-1)
        return 0

    lax.fori_loop(0, tt // gr, hgrn_group, 0)


def _even_mixer(z, conv0, sg0, sh0_t, conv_w, neg_a, dt_bias, gnorm, hnorm, lb, *, l_real, tt):
    B, Lp, _ = z.shape
    cg = min(GDN_CHUNK, tt)
    ch = min(HGRN_CHUNK, tt)
    d3 = 3 * D_A
    nt = Lp // tt
    col = lambda w, idx: pl.BlockSpec((None, tt, w), lambda b, i, idx=idx: (b, i, idx))
    state = lambda: pl.BlockSpec((None, H_A, HD_A, HD_A), lambda b, i: (b, 0, 0, 0))
    vec = lambda n: pl.BlockSpec((1, n), lambda b, i: (0, 0))
    nega_row = jnp.zeros((1, LANES), F32).at[0, H_A:2 * H_A].set(neg_a)
    dtb_row = jnp.zeros((1, LANES), F32).at[0, H_A:2 * H_A].set(dt_bias)
    return pl.pallas_call(
        functools.partial(_even_mixer_kernel, tt=tt, cg=cg, ch=ch, l_real=l_real),
        grid=(B, nt),
        in_specs=[
            col(d3, 0), col(D_A, 3), col(D_A, 4), col(D_A, 5), col(D_A, 6), col(D_A, 7), col(LANES, 32),
            pl.BlockSpec((None, SUBLANES, d3), lambda b, i: (b, 0, 0)),
            state(), state(),
            pl.BlockSpec((GDN_CONV, d3), lambda b, i: (0, 0)),
            vec(LANES), vec(LANES), vec(HD_A), vec(DV_B), vec(D_A),
        ],
        out_specs=[
            pl.BlockSpec((None, tt, 2 * D_A), lambda b, i: (b, i, 0)),
            pl.BlockSpec((None, SUBLANES, d3), lambda b, i: (b, 0, 0)),
            state(), state(),
        ],
        out_shape=[
            jax.ShapeDtypeStruct((B, Lp, 2 * D_A), F32),
            jax.ShapeDtypeStruct((B, SUBLANES, d3), F32),
            jax.ShapeDtypeStruct((B, H_A, HD_A, HD_A), F32),
            jax.ShapeDtypeStruct((B, H_B, DV_B, DK_B), F32),
        ],
        scratch_shapes=[
            pltpu.VMEM((tt + SUBLANES, d3), F32),
            pltpu.VMEM((tt, d3), F32),
            pltpu.VMEM((tt, LANES), F32),
            pltpu.VMEM((tt, LANES), F32),
            pltpu.VMEM((H_A // 2, 2 * HD_A, 2 * HD_A), F32),
        ],
        compiler_params=_cparams("parallel", "arbitrary"),
        name="even_mixer",
    )(z, z, z, z, z, z, z, conv0, sg0, sh0_t, conv_w, nega_row, dtb_row,
      gnorm.reshape(1, HD_A), hnorm.reshape(1, DV_B), lb.reshape(1, D_A))


def _mem_attn_rows(x, mk_ref, mv_ref, g_ref, wq_ref, gq_ref, wo_ref):
    q = _dot(_rms(x, g_ref[...]).astype(BF16), wq_ref[...])
    scale = MEM_HD ** -0.5
    outs = []
    for h in range(MEM_H):
        sl = slice(h * MEM_HD, (h + 1) * MEM_HD)
        qh = _rms(q[:, sl], gq_ref[...])
        s = _dot_nt(qh.astype(BF16), mk_ref[:, sl].astype(BF16)) * scale
        s = s - jnp.max(s, axis=-1, keepdims=True)
        p = jnp.exp(s)
        p = p / jnp.sum(p, axis=-1, keepdims=True)
        outs.append(_dot(p.astype(BF16), mv_ref[:, sl].astype(BF16)))
    o = jnp.concatenate(outs, axis=-1)
    return x + _dot(o.astype(BF16), wo_ref[...])


def _mem_attn_step_kernel(q_ref, mk_ref, mv_ref, o_ref):
    scale = MEM_HD ** -0.5
    outs = []
    for h in range(MEM_H):
        qh = jnp.broadcast_to(q_ref[:, h * MEM_HD:(h + 1) * MEM_HD], (SUBLANES, MEM_HD))
        s = _dot_nt(qh.astype(BF16), mk_ref[:, h, :].astype(BF16)) * scale
        s = s - jnp.max(s, axis=-1, keepdims=True)
        p = jnp.exp(s)
        p = p / jnp.sum(p, axis=-1, keepdims=True)
        outs.append(_dot(p.astype(BF16), mv_ref[:, h, :].astype(BF16))[0:1])
    o_ref[...] = jnp.concatenate(outs, axis=-1)


def _mem_attn_step(q, cache_k, cache_v, l):
    B = q.shape[0]
    M = cache_k.shape[2]
    dq = MEM_H * MEM_HD
    mem = pl.BlockSpec((None, None, M, MEM_H, MEM_HD), lambda b: (l, b, 0, 0, 0))
    row = pl.BlockSpec((None, 1, dq), lambda b: (b, 0, 0))
    return pl.pallas_call(
        _mem_attn_step_kernel,
        grid=(B,),
        in_specs=[row, mem, mem],
        out_specs=row,
        out_shape=jax.ShapeDtypeStruct((B, 1, dq), F32),
        compiler_params=_cparams("parallel"),
        name="mem_attn_step",
    )(q, cache_k, cache_v)


def _post_seq_kernel(*refs, n_parts, tm):
    x_ref = refs[0]
    parts = refs[1:1 + 2 * n_parts]
    (mk_ref, mv_ref, gmem_ref, wq_ref, gq_ref, wo_ref, buf0_ref, g_ref, wgu_ref, cw_ref, cb_ref, wd_ref,
     o_ref, tail_ref, gp_ref, carry_ref) = refs[1 + 2 * n_parts:]
    ti = pl.program_id(1)

    @pl.when(ti == 0)
    def _():
        carry_ref[...] = buf0_ref[...]

    x = x_ref[...]
    for k in range(n_parts):
        x = x + _dot(parts[2 * k][...].astype(BF16), parts[2 * k + 1][...])
    x = _mem_attn_rows(x, mk_ref, mv_ref, gmem_ref, wq_ref, gq_ref, wo_ref)
    xn = _rms(x, g_ref[...]).astype(BF16)
    acc = x
    for c in range(D_FF // FF_CHUNK):
        cs = slice(c * FF_CHUNK, (c + 1) * FF_CHUNK)
        gate = _dot(xn, wgu_ref[:, cs])
        up = _dot(xn, wgu_ref[:, D_FF + c * FF_CHUNK:D_FF + (c + 1) * FF_CHUNK])
        gp_ref[0:SUBLANES, :] = carry_ref[:, cs]
        gp_ref[SUBLANES:SUBLANES + tm, :] = gate
        conv = cw_ref[FFN_CONV - 1:FFN_CONV, cs] * gate + cb_ref[:, cs]
        for j in range(FFN_CONV - 1):
            off = SUBLANES - (FFN_CONV - 1) + j
            conv = conv + cw_ref[j:j + 1, cs] * gp_ref[off:off + tm, :]
        last = gp_ref[tm:tm + SUBLANES, :]
        carry_ref[:, cs] = last
        tail_ref[:, cs] = last
        act = _silu(conv) * up
        acc = acc + _dot(act.astype(BF16), wd_ref[cs, :])
    o_ref[...] = acc


def _post_seq(x, parts, mk, mv, gmem, wq, gq, wo, buf0, g, wgu, cw, cb, wd, *, tm, name):
    B, L, D = x.shape
    M = mk.shape[1]
    dq = MEM_H * MEM_HD
    const = lambda shape: pl.BlockSpec(shape, lambda b, i: (0,) * len(shape), pipeline_mode=pl.Buffered(1))
    row_tile = pl.BlockSpec((None, tm, D), lambda b, i: (b, i, 0))
    mem = pl.BlockSpec((None, M, dq), lambda b, i: (b, 0, 0))
    in_specs, args = [row_tile], [x]
    for a, w, spec in parts:
        in_specs += [spec, const(w.shape)]
        args += [a, w]
    in_specs += [mem, mem, const((1, D)), const((D, dq)), const((1, MEM_HD)), const((dq, D)),
                 pl.BlockSpec((None, SUBLANES, D_FF), lambda b, i: (b, 0, 0)),
                 const((1, D)), const((D, 2 * D_FF)), const((FFN_CONV, D_FF)), const((1, D_FF)), const((D_FF, D))]
    args += [mk, mv, gmem.reshape(1, D), wq, gq.reshape(1, MEM_HD), wo, buf0, g.reshape(1, D), wgu, cw,
             cb.reshape(1, D_FF), wd]
    return pl.pallas_call(
        functools.partial(_post_seq_kernel, n_parts=len(parts), tm=tm),
        grid=(B, L // tm),
        in_specs=in_specs,
        out_specs=[row_tile, pl.BlockSpec((None, SUBLANES, D_FF), lambda b, i: (b, 0, 0))],
        out_shape=[jax.ShapeDtypeStruct((B, L, D), F32), jax.ShapeDtypeStruct((B, SUBLANES, D_FF), F32)],
        scratch_shapes=[pltpu.VMEM((tm + SUBLANES, FF_CHUNK), F32), pltpu.VMEM((SUBLANES, D_FF), F32)],
        compiler_params=_cparams("parallel", "arbitrary"),
        name=name,
    )(*args)


def _ffn_step_kernel(x_ref, p2_ref, p1_ref, g_ref, wgu_ref, cw_ref, cb_ref, wd_ref, o_ref, gate_ref):
    x = x_ref[...]
    xn = _rms(x, g_ref[...]).astype(BF16)
    acc = x
    for c in range(D_FF // FF_CHUNK):
        cs = slice(c * FF_CHUNK, (c + 1) * FF_CHUNK)
        gate = _dot(xn, wgu_ref[:, cs])
        up = _dot(xn, wgu_ref[:, D_FF + c * FF_CHUNK:D_FF + (c + 1) * FF_CHUNK])
        gate_ref[:, cs] = gate
        conv = cw_ref[0:1, cs] * p2_ref[:, cs] + cw_ref[1:2, cs] * p1_ref[:, cs] + cw_ref[2:3, cs] * gate + cb_ref[:, cs]
        act = _silu(conv) * up
        acc = acc + _dot(act.astype(BF16), wd_ref[cs, :])
    o_ref[...] = acc


def _ffn_step(x, p2, p1, g, wgu, cw, cb, wd):
    R, D = x.shape
    full = lambda a: pl.BlockSpec(a.shape, lambda i: (0,) * a.ndim)
    args = (x, p2, p1, g.reshape(1, D), wgu, cw, cb.reshape(1, D_FF), wd)
    return pl.pallas_call(
        _ffn_step_kernel,
        grid=(1,),
        in_specs=[full(a) for a in args],
        out_specs=[pl.BlockSpec((R, D), lambda i: (0, 0)), pl.BlockSpec((R, D_FF), lambda i: (0, 0))],
        out_shape=[jax.ShapeDtypeStruct((R, D), F32), jax.ShapeDtypeStruct((R, D_FF), F32)],
        compiler_params=_cparams("arbitrary"),
        name="conv_ffn_step",
    )(*args)


def _moba_query_tile(j, hs, qt_ref, kb_ref, vt_ref, kmean_ref, bias_ref, ot_ref):
    blk = MOBA_BLOCK
    w = (j + 1) * blk
    qts = [qt_ref[h] for h in hs]
    sts = [_dot(kb_ref[h, 0:w, :], (qt * (HD_C ** -0.5)).astype(BF16)) for h, qt in zip(hs, qts)]
    sels = []
    if j > MOBA_TOPK:
        iota_n = lax.broadcasted_iota(jnp.int32, (j, blk), 0).astype(F32)
        gs = [_dot(kmean_ref[h, 0:j, :], qt, HI) for h, qt in zip(hs, qts)]
        for g in gs:
            sel = jnp.zeros((j, blk), F32)
            for _ in range(MOBA_TOPK):
                m = jnp.max(g, axis=0, keepdims=True)
                idx = jnp.min(jnp.where(g == m, iota_n, 1e9), axis=0, keepdims=True)
                hit = iota_n == idx
                sel = jnp.where(hit, 1.0, sel)
                g = jnp.where(hit, -jnp.inf, g)
            sels.append(sel)
    key_le_query = lax.broadcasted_iota(jnp.int32, (blk, blk), 0) <= lax.broadcasted_iota(jnp.int32, (blk, blk), 1)
    pcats, ls = [], []
    for i, h in enumerate(hs):
        parts = []
        for n in range(j + 1):
            s = sts[i][n * blk:(n + 1) * blk, :]
            if j - n < 2:
                s = s + bias_ref[h, j - n]
            if n == j:
                s = jnp.where(key_le_query, s, -jnp.inf)
            elif j > MOBA_TOPK:
                s = jnp.where(sels[i][n:n + 1, :] > 0.5, s, -jnp.inf)
            parts.append(s)
        m = jnp.max(parts[0], axis=0, keepdims=True)
        for s in parts[1:]:
            m = jnp.maximum(m, jnp.max(s, axis=0, keepdims=True))
        ps = [jnp.exp(s - m) for s in parts]
        l = jnp.sum(ps[0], axis=0, keepdims=True)
        for p in ps[1:]:
            l = l + jnp.sum(p, axis=0, keepdims=True)
        pcats.append((ps[0] if j == 0 else jnp.concatenate(ps, axis=0)).astype(BF16))
        ls.append(l)
    outs = [_dot(vt_ref[h, :, 0:w], pc) for h, pc in zip(hs, pcats)]
    for h, o, l in zip(hs, outs, ls):
        ot_ref[h] = o / l


def _moba_seq_kernel(q_ref, k_ref, vtin_ref, bias_ref, o_ref, kmean_ref, kb_ref, vt_ref, qt_ref, ot_ref, *, nb):
    j = pl.program_id(1)
    blk = MOBA_BLOCK
    dc = H_C * HD_C

    @pl.when(j == 0)
    def _():
        for n in range(nb):
            km = jnp.mean(k_ref[n * blk:(n + 1) * blk, :], axis=0, keepdims=True)
            for h in range(H_C):
                kmean_ref[h, n:n + 1, :] = km[:, h * HD_C:(h + 1) * HD_C]

        def prep(n, _):
            rows = pl.ds(pl.multiple_of(n * blk, blk), blk)
            kblk = k_ref[rows, :]
            vt_ref[:, :, rows] = vtin_ref[:, rows].astype(BF16).reshape(H_C, HD_C, blk)
            for h in range(H_C):
                kb_ref[h, rows, :] = kblk[:, h * HD_C:(h + 1) * HD_C].astype(BF16)
            return 0
        lax.fori_loop(0, nb, prep, 0)

    qt_ref[...] = q_ref[...].T.reshape(H_C, HD_C, blk)
    for jj in range(nb):
        @pl.when(j == jj)
        def _(jj=jj):
            def head_pair(hp, _):
                _moba_query_tile(jj, [2 * hp, 2 * hp + 1], qt_ref, kb_ref, vt_ref, kmean_ref, bias_ref, ot_ref)
                return 0
            lax.fori_loop(0, H_C // 2, head_pair, 0)
    o_ref[...] = ot_ref[...].reshape(dc, blk).T


def _moba_seq(qkv, vt, bias_tiles):
    B, L, _ = qkv.shape
    nb = L // MOBA_BLOCK
    dc = H_C * HD_C
    return pl.pallas_call(
        functools.partial(_moba_seq_kernel, nb=nb),
        grid=(B, nb),
        in_specs=[
            pl.BlockSpec((None, MOBA_BLOCK, dc), lambda b, j: (b, j, 0)),
            pl.BlockSpec((None, L, dc), lambda b, j: (b, 0, 1)),
            pl.BlockSpec((None, dc, L), lambda b, j: (b, 0, 0)),
            pl.BlockSpec(bias_tiles.shape, lambda b, j: (0, 0, 0, 0), pipeline_mode=pl.Buffered(1)),
        ],
        out_specs=pl.BlockSpec((None, MOBA_BLOCK, dc), lambda b, j: (b, j, 0)),
        out_shape=jax.ShapeDtypeStruct((B, L, dc), F32),
        scratch_shapes=[pltpu.VMEM((H_C, nb, HD_C), F32), pltpu.VMEM((H_C, L, HD_C), BF16),
                        pltpu.VMEM((H_C, HD_C, L), BF16), pltpu.VMEM((H_C, HD_C, MOBA_BLOCK), F32),
                        pltpu.VMEM((H_C, HD_C, MOBA_BLOCK), F32)],
        compiler_params=_cparams("parallel", "arbitrary"),
        name="moba_seq",
    )(qkv, qkv, vt, bias_tiles)


MOBA_PAGES_PER_BLOCK = MOBA_BLOCK // PAGE_SIZE
MOBA_STEP_PAGES = 8


def _moba_score_kernel(pt_ref, q_ref, *refs, n_pages):
    del pt_ref
    ppg, ppb = MOBA_STEP_PAGES, MOBA_PAGES_PER_BLOCK
    kp_refs = refs[:ppg]
    s_ref, sel_ref, gate_ref = refs[ppg:]
    step = pl.program_id(1)
    nbp = n_pages // ppb
    q = q_ref[...]
    inv = 1.0 / (MOBA_BLOCK * HD_C ** -0.5)
    gsum = None
    for r in range(ppg):
        sc = jnp.sum(kp_refs[r][...] * q, axis=1)
        s_ref[:, r * PAGE_SIZE:(r + 1) * PAGE_SIZE] = sc
        gs = jnp.sum(sc, axis=-1, keepdims=True)
        gsum = gs if r % ppb == 0 else gsum + gs
        if r % ppb == ppb - 1:
            gate_ref[step * (ppg // ppb) + r // ppb] = jnp.broadcast_to(gsum * inv, (H_C, LANES))

    @pl.when(step == n_pages // ppg - 1)
    def _():
        g = gate_ref[...]
        iota_b = lax.broadcasted_iota(jnp.int32, (nbp, H_C, LANES), 0).astype(F32)
        for i in range(MOBA_TOPK):
            m = jnp.max(g, axis=0, keepdims=True)
            idx = jnp.min(jnp.where(g == m, iota_b, 1e9), axis=0, keepdims=True)
            sel_ref[i] = idx[0]
            g = jnp.where(iota_b == idx, -jnp.inf, g)


def _moba_attend_kernel(pt_ref, sel_ref, q_ref, kn_ref, vn_ref, blast_ref, bsc_ref, *refs, nbp):
    del pt_ref
    nsel, ppb = MOBA_TOPK, MOBA_PAGES_PER_BLOCK
    s_refs, v_refs, o_ref = refs[:nsel], refs[nsel:nsel + nsel * ppb], refs[-1]
    b, h = pl.program_id(0), pl.program_id(1)
    far = bsc_ref[h, 0]
    s_new = jnp.sum(q_ref[...] * kn_ref[...], axis=-1, keepdims=True) * (HD_C ** -0.5) + bsc_ref[h, 1]
    ss = []
    for i in range(nsel):
        blk = sel_ref[(b * H_C + h) * nsel + i]
        ss.append(s_refs[i][...] + jnp.where(blk == nbp - 1, blast_ref[...], far))
    m = s_new
    for s in ss:
        m = jnp.maximum(m, jnp.max(s, axis=-1, keepdims=True))
    w_new = jnp.exp(s_new - m)
    den = w_new
    num = w_new * vn_ref[...]
    for i in range(nsel):
        e = jnp.exp(ss[i] - m)
        den = den + jnp.sum(e, axis=-1, keepdims=True)
        eb = jnp.broadcast_to(e, (SUBLANES, MOBA_BLOCK)).astype(BF16)
        for r in range(ppb):
            pv = _dot_nt(eb[:, r * PAGE_SIZE:(r + 1) * PAGE_SIZE], v_refs[i * ppb + r][...].astype(BF16))
            num = num + pv[0:1]
    o_ref[...] = num / den


def _moba_step(q, k_new, v_new, cache_k, cache_v, page_table, bias_last, bias_scalars):
    B = q.shape[0]
    n_pages = page_table.shape[1]
    ppg, ppb, nsel = MOBA_STEP_PAGES, MOBA_PAGES_PER_BLOCK, MOBA_TOPK
    assert n_pages % ppg == 0 and ppg % ppb == 0
    nbp = n_pages // ppb
    assert nbp >= nsel
    pt = page_table.reshape(-1)
    q_rep = jnp.broadcast_to((q * (HD_C ** -0.5))[..., None], (B, H_C, HD_C, LANES))
    kpage = lambda r: pl.BlockSpec((None, H_C, HD_C, PAGE_SIZE),
                                   lambda b, s, pt, r=r: (pt[b * n_pages + s * ppg + r], 0, 0, 0))
    scores, sel = pl.pallas_call(
        functools.partial(_moba_score_kernel, n_pages=n_pages),
        grid_spec=pltpu.PrefetchScalarGridSpec(
            num_scalar_prefetch=1,
            grid=(B, n_pages // ppg),
            in_specs=[pl.BlockSpec((None, H_C, HD_C, LANES), lambda b, s, pt: (b, 0, 0, 0))]
                     + [kpage(r) for r in range(ppg)],
            out_specs=[pl.BlockSpec((None, H_C, ppg * PAGE_SIZE), lambda b, s, pt: (b, 0, s)),
                       pl.BlockSpec((None, nsel, H_C, LANES), lambda b, s, pt: (b, 0, 0, 0))],
            scratch_shapes=[pltpu.VMEM((nbp, H_C, LANES), F32)],
        ),
        out_shape=[jax.ShapeDtypeStruct((B, H_C, n_pages * PAGE_SIZE), F32),
                   jax.ShapeDtypeStruct((B, nsel, H_C, LANES), F32)],
        compiler_params=_cparams("parallel", "arbitrary"),
        name="moba_step_score",
    )(pt, q_rep, *([cache_k] * ppg))
    sel_flat = jnp.swapaxes(sel[:, :, :, 0], 1, 2).astype(jnp.int32).reshape(-1)
    tok = pl.BlockSpec((None, None, 1, HD_C), lambda b, h, pt, sl: (b, h, 0, 0))
    srow = lambda i: pl.BlockSpec((None, None, None, 1, MOBA_BLOCK),
                                  lambda b, h, pt, sl, i=i: (b, h, sl[(b * H_C + h) * nsel + i], 0, 0))
    vpage = lambda i, r: pl.BlockSpec(
        (None, None, HD_C, PAGE_SIZE),
        lambda b, h, pt, sl, i=i, r=r: (pt[b * n_pages + sl[(b * H_C + h) * nsel + i] * ppb + r], h, 0, 0))
    row = lambda a: a.reshape(B, H_C, 1, HD_C)
    out = pl.pallas_call(
        functools.partial(_moba_attend_kernel, nbp=nbp),
        grid_spec=pltpu.PrefetchScalarGridSpec(
            num_scalar_prefetch=2,
            grid=(B, H_C),
            in_specs=[tok, tok, tok,
                      pl.BlockSpec((None, 1, MOBA_BLOCK), lambda b, h, pt, sl: (h, 0, 0)),
                      pl.BlockSpec(memory_space=pltpu.SMEM)]
                     + [srow(i) for i in range(nsel)]
                     + [vpage(i, r) for i in range(nsel) for r in range(ppb)],
            out_specs=tok,
        ),
        out_shape=jax.ShapeDtypeStruct((B, H_C, 1, HD_C), F32),
        compiler_params=_cparams("parallel", "arbitrary"),
        name="moba_step_attend",
    )(pt, sel_flat, row(q), row(k_new), row(v_new), bias_last, bias_scalars,
      *([scores.reshape(B, H_C, nbp, 1, MOBA_BLOCK)] * nsel), *([cache_v] * (nsel * ppb)))
    return out.reshape(B, H_C, HD_C)


def _s5_kernel(u_ref, h0r_ref, h0i_ref, wb_ref, ar_ref, ai_ref, wc_ref, d_ref, wglu_ref,
               o_ref, hr_ref, hi_ref, x_ref, *, tt, nb):
    ti = pl.program_id(0)
    ns = D_S5_STATE

    @pl.when(ti == 0)
    def _():
        hr_ref[...] = h0r_ref[...]
        hi_ref[...] = h0i_ref[...]

    nslab = ns // LANES
    if tt == 1:
        seqs, seq_rows, mat_rows, cols = [0], lambda b: slice(None), lambda b: slice(None), lambda b: slice(None)
        u = u_ref[...]
    else:
        seqs = range(nb)
        seq_rows = lambda b: pl.ds(b, tt, stride=nb)
        mat_rows = lambda b: slice(b * tt, (b + 1) * tt)
        cols = lambda b: slice(b * D_S5, (b + 1) * D_S5)
        u = jnp.concatenate([u_ref[:, cols(b)] for b in seqs], axis=0)
    nsp = S5_MM_SPLIT
    du, dsx = D_S5 // nsp, ns // nsp
    ub = u.astype(BF16)
    for part in range(2):
        for g in range(nsp):
            xg = _dot(ub[:, g * du:(g + 1) * du], wb_ref[g * du:(g + 1) * du, part * ns + g * dsx:part * ns + (g + 1) * dsx])
            for b in seqs:
                for c in range(dsx // LANES):
                    slab = (part * ns + g * dsx) // LANES + c
                    x_ref[slab, seq_rows(b), :] = xg[mat_rows(b), c * LANES:(c + 1) * LANES]
    slabs = lambda ref: jnp.stack([ref[:, c * LANES:(c + 1) * LANES] for c in range(nslab)])
    ar = slabs(ar_ref)
    ai = slabs(ai_ref)

    def step(t, carry):
        hr, hi = carry
        rows = pl.ds(pl.multiple_of(t * nb, nb), nb)
        nr = ar * hr - ai * hi + x_ref[0:nslab, rows, :]
        ni = ar * hi + ai * hr + x_ref[nslab:2 * nslab, rows, :]
        x_ref[0:nslab, rows, :] = nr
        x_ref[nslab:2 * nslab, rows, :] = ni
        return nr, ni

    hr, hi = lax.fori_loop(0, tt, step, (slabs(hr_ref), slabs(hi_ref)))
    for c in range(nslab):
        hr_ref[:, c * LANES:(c + 1) * LANES] = hr[c]
        hi_ref[:, c * LANES:(c + 1) * LANES] = hi[c]

    def state_cols(first_slab, n):
        hs = [jnp.concatenate([x_ref[first_slab + c, seq_rows(b), :] for c in range(n)], axis=-1) for b in seqs]
        return (hs[0] if len(hs) == 1 else jnp.concatenate(hs, axis=0)).astype(BF16)

    ys = []
    for g in range(nsp):
        yg = None
        for part in range(2):
            hg = state_cols((part * ns + g * dsx) // LANES, dsx // LANES)
            t = _dot(hg, wc_ref[part * ns + g * dsx:part * ns + (g + 1) * dsx, g * du:(g + 1) * du])
            yg = t if yg is None else yg + t
        ys.append(yg)
    y = jax.nn.gelu(jnp.concatenate(ys, axis=-1) + d_ref[...] * u)
    gl = _dot(y.astype(BF16), wglu_ref[...])
    o = gl[:, 0:D_S5] * jax.nn.sigmoid(gl[:, D_S5:2 * D_S5])
    for b in seqs:
        o_ref[:, cols(b)] = o[mat_rows(b), :]


def _s5(u, h0r, h0i, wb, ar, ai, wc, d, wglu, *, tt, nb):
    L = u.shape[0]
    const = lambda a: pl.BlockSpec(a.shape, lambda i: (0,) * a.ndim)
    rows = pl.BlockSpec((tt, nb * D_S5), lambda i: (i, 0))
    st = pl.BlockSpec((nb, D_S5_STATE), lambda i: (0, 0))
    args = (u, h0r, h0i, wb, ar, ai, wc, d, wglu)
    if tt == 1:
        assert L == 1
        rows = pl.BlockSpec((nb, D_S5), lambda i: (0, 0))
        args = (u.reshape(nb, D_S5),) + args[1:]
    return pl.pallas_call(
        functools.partial(_s5_kernel, tt=tt, nb=nb),
        grid=(L // tt,),
        in_specs=[rows] + [const(a) for a in args[1:]],
        out_specs=[rows, st, st],
        out_shape=[jax.ShapeDtypeStruct(args[0].shape, F32),
                   jax.ShapeDtypeStruct((nb, D_S5_STATE), F32), jax.ShapeDtypeStruct((nb, D_S5_STATE), F32)],
        scratch_shapes=[pltpu.VMEM((2 * D_S5_STATE // LANES, tt * nb, LANES), F32)],
        compiler_params=_cparams("arbitrary"),
        name="s5",
    )(*args)


def _rel_bucket(dist):
    n = jnp.maximum(dist, 0)
    exact = N_BUCKETS // 2
    nf = jnp.maximum(n, exact).astype(F32)
    large = exact + (jnp.log(nf / exact) / math.log(MAX_DISTANCE / exact) * (N_BUCKETS - exact)).astype(jnp.int32)
    return jnp.where(n < exact, n, jnp.minimum(large, N_BUCKETS - 1))


def _block_diag(w):
    G, a, b = w.shape
    eye = jnp.eye(G, dtype=w.dtype)
    return (eye[:, None, :, None] * w[:, :, None, :]).reshape(G * a, G * b)


def _s5_weights(A_re, A_im, log_dt, B_re, B_im, C_re, C_im, D):
    dt = jnp.exp(log_dt.astype(F32))[:, None]
    lr, li = A_re.astype(F32), A_im.astype(F32)
    mag = jnp.exp(lr * dt)
    ar, ai = mag * jnp.cos(li * dt), mag * jnp.sin(li * dt)
    den = lr * lr + li * li
    nr = ar - 1.0
    cr, ci = (nr * lr + ai * li) / den, (ai * lr - nr * li) / den
    bbr = cr[..., None] * B_re - ci[..., None] * B_im
    bbi = cr[..., None] * B_im + ci[..., None] * B_re
    wb = jnp.concatenate([_block_diag(bbr.transpose(0, 2, 1)), _block_diag(bbi.transpose(0, 2, 1))], axis=1)
    wc = jnp.concatenate([_block_diag(C_re.transpose(0, 2, 1)), -_block_diag(C_im.transpose(0, 2, 1))], axis=0)
    return (wb.astype(BF16), ar.reshape(1, -1), ai.reshape(1, -1), wc.astype(BF16), D.reshape(1, -1).astype(F32))


def _pad_rows_front(a, rows):
    return jnp.pad(a, ((0, 0), (rows - a.shape[1], 0), (0, 0)))


def _trunk(x, st, W, *, decode):
    B, L, D = x.shape
    T = B * L
    tm = min(512, T)
    out = {}

    x2 = x.reshape(T, D)
    z = _norm_proj_wide(x2, W['norm_mix'][0], W['ev_w_in'], tm=tm, chunk=EV_IN_TILE, name="ev_in_proj")
    zc = z.shape[1]
    if decode:
        lp = SUBLANES
        z3 = jnp.pad(z.reshape(B, L, zc), ((0, 0), (0, lp - L), (0, 0)))
        tt = lp
    else:
        lp = L
        z3 = z.reshape(B, L, zc)
        tt = min(512, L)
    conv0 = _pad_rows_front(st['gdn_conv'], SUBLANES)
    mixed, tail, sg, sh_t = _even_mixer(z3, conv0, st['gdn'], jnp.swapaxes(st['hgrn'], -1, -2), W['gdn_conv_w'],
                                        W['gdn_neg_a'], W['gdn_dt_bias'], W['gdn_norm'], W['hgrn_norm'], W['hgrn_lb'],
                                        l_real=L, tt=tt)
    out['gdn_conv'] = tail[:, SUBLANES - (GDN_CONV - 1):, :]
    out['gdn'] = sg
    out['hgrn'] = jnp.swapaxes(sh_t, -1, -2)
    if decode:
        parts = [(mixed[:, :L].reshape(T, 2 * D_A), W['ev_w_out'], pl.BlockSpec((tm, 2 * D_A), lambda i: (i, 0)))]
    else:
        parts = [(mixed, W['ev_w_out'], pl.BlockSpec((None, tm, 2 * D_A), lambda b, i: (b, i, 0)))]
    x2, fs = _post_mixer(x2, parts, st, W, 0, B, L, decode)
    out['ffn0'] = fs

    dc = H_C * HD_C
    gqk = jnp.concatenate([jnp.tile(W['moba_gq'], H_C), jnp.tile(W['moba_gk'], H_C)])
    u_tm = _norm_proj(x2, W['norm_mix'][1], W['od_w_in'][:, 3 * dc:], tm=tm, tn=D_S5,
                      time_major=None if decode else (B, L), name="od_in_proj_u")
    u_tm = u_tm.reshape(L, B * D_S5)
    if decode:
        (qkv,) = _od_qkv_proj(x2, W['norm_mix'][1], W['od_w_in'][:, :3 * dc], gqk, tm=tm)
        out['moba_k'] = qkv[:, dc:2 * dc].reshape(B, L, H_C, HD_C)
        out['moba_v'] = qkv[:, 2 * dc:].reshape(B, L, H_C, HD_C)
        heads = lambda a: a.reshape(B, H_C, HD_C)
        o_c = _moba_step(heads(qkv[:, :dc]), heads(qkv[:, dc:2 * dc]), heads(qkv[:, 2 * dc:]),
                         st['cache_k'], st['cache_v'], st['page_table'], *W['moba_bias_step'])
        o_c = o_c.reshape(T, dc)
        tts = 1
    else:
        qkv, kt, vt = _od_qkv_proj(x2, W['norm_mix'][1], W['od_w_in'][:, :3 * dc], gqk, tm=tm, seq=(B, L))
        out['moba_k'] = jnp.transpose(kt.reshape(B, H_C, HD_C, L), (0, 3, 1, 2))
        out['moba_v'] = jnp.transpose(vt.reshape(B, H_C, HD_C, L), (0, 3, 1, 2))
        o_c = _moba_seq(qkv.reshape(B, L, 3 * dc), vt, W['moba_bias_tiles']).reshape(T, dc)
        tts = min(64, L)
    o_d, hr, hi = _s5(u_tm, st['s5_re'], st['s5_im'], *W['s5'], W['s5_glu_w'], tt=tts, nb=B)
    out['s5_re'] = hr.reshape(B, S5_GROUPS, S5_STATE)
    out['s5_im'] = hi.reshape(B, S5_GROUPS, S5_STATE)
    if decode:
        parts = [(o_c, W['od_w_out'][:dc], pl.BlockSpec((tm, dc), lambda i: (i, 0))),
                 (o_d, W['od_w_out'][dc:], pl.BlockSpec((tm, D_S5), lambda i: (i, 0)))]
    else:
        parts = [(o_c.reshape(B, L, dc), W['od_w_out'][:dc], pl.BlockSpec((None, tm, dc), lambda b, i: (b, i, 0))),
                 (o_d, W['od_w_out'][dc:], pl.BlockSpec((tm, D_S5), lambda b, i: (i, b)))]
    x2, fs = _post_mixer(x2, parts, st, W, 1, B, L, decode)
    out['ffn1'] = fs
    out['y'] = x2.reshape(B, L, D)
    return out


def _post_mixer(x2, parts, st, W, l, B, L, decode):
    T, D = x2.shape
    dq = MEM_H * MEM_HD
    buf = st['ffn_conv'][l]
    if decode:
        x2 = _proj_res(x2, parts, tm=T, name="mixer_out_proj")
        q = _norm_proj(x2, W['norm_mem'][l], W['mem_wq'][l], tm=T, tn=dq, n_norm_tiles=1, group=MEM_HD,
                       gain=jnp.tile(W['mem_gq'][l], MEM_H), name="mem_q_proj")
        o = _mem_attn_step(q.reshape(B, 1, dq), st['mem_k'], st['mem_v'], l).reshape(T, dq)
        x3 = _proj_res(x2, [(o, W['mem_wo'][l], pl.BlockSpec((T, dq), lambda i: (i, 0)))], tm=T, name="mem_out_proj")
        y, gate = _ffn_step(x3, buf[:, 0], buf[:, 1], W['norm_ffn'][l], W['ffn_w_gu'][l],
                            W['ffn_conv_w'][l], W['ffn_conv_b'][l], W['ffn_w_down'][l])
        fs = jnp.stack([buf[:, 1], gate], axis=1)
        return y, fs
    y, tail = _post_seq(x2.reshape(B, L, D), parts, st['mem_k'][l], st['mem_v'][l], W['norm_mem'][l], W['mem_wq'][l],
                        W['mem_gq'][l], W['mem_wo'][l], _pad_rows_front(buf, SUBLANES), W['norm_ffn'][l],
                        W['ffn_w_gu'][l], W['ffn_conv_w'][l], W['ffn_conv_b'][l], W['ffn_w_down'][l],
                        tm=min(512, L), name="post_mixer_seq")
    return y.reshape(T, D), tail[:, SUBLANES - (FFN_CONV - 1):, :]


def _mem_kv(mem, g_norm, w_kv, g_k):
    B, M, D = mem.shape
    dq = MEM_H * MEM_HD
    gain = jnp.concatenate([jnp.tile(g_k, MEM_H), jnp.ones((dq,), F32)])
    kv = _norm_proj(mem.reshape(B * M, D), g_norm, w_kv, tm=min(512, B * M), tn=dq, n_norm_tiles=1, group=MEM_HD,
                    gain=gain, name="mem_kv_proj")
    return kv[:, :dq].reshape(B, M, dq), kv[:, dq:].reshape(B, M, dq)


def kernel(x_prompt, x_sample, state_gdn_conv, state_gdn, state_hgrn, cache_moba_k, cache_moba_v, state_s5_re, state_s5_im, cache_mem_k, cache_mem_v, state_ffn_conv, page_table, mem_prompt, rel_bias, hgrn_lb_raw, norm_mix, norm_mem, norm_memkv, norm_ffn, mem_wq, mem_wkv, mem_gq, mem_gk, mem_wo, ffn_w_gu, ffn_conv_w, ffn_conv_b, ffn_w_down, ev_w_in, ev_w_out, gdn_conv_w, gdn_A_log, gdn_dt_bias, gdn_norm, hgrn_norm, od_w_in, od_w_out, moba_gq, moba_gk, s5_A_re, s5_A_im, s5_log_dt, s5_B_re, s5_B_im, s5_C_re, s5_C_im, s5_D, s5_glu_w):
    Bp, Lq, D = x_prompt.shape
    Bs = x_sample.shape[0]
    dc = H_C * HD_C
    dq = MEM_H * MEM_HD
    n_pages = page_table.shape[1]
    past = n_pages * PAGE_SIZE

    d3 = 3 * D_A
    w_in = ev_w_in[0]
    ba_cols = w_in[:, 4 * D_A:4 * D_A + 2 * H_A]
    w_in = jnp.concatenate([w_in[:, :4 * D_A], w_in[:, 4 * D_A + 2 * H_A:], ba_cols,
                            jnp.zeros((D, LANES - 2 * H_A), F32)], axis=1)
    lb_all = jnp.cumsum(jax.nn.softmax(hgrn_lb_raw.astype(F32), axis=0), axis=0)
    bias_t = rel_bias.T.astype(F32)
    nblk = MOBA_BLOCK
    by_rel = bias_t[:, _rel_bucket(jnp.arange(2 * nblk, dtype=jnp.int32))]
    mm = jnp.arange(2 * nblk, dtype=jnp.int32)

    def toeplitz(shift):
        y = by_rel[:, (shift + mm) % (2 * nblk)]
        flat = jnp.tile(y, (1, nblk))[:, :nblk * (2 * nblk - 1)]
        return flat.reshape(H_C, nblk, 2 * nblk - 1)[:, :, :nblk]

    far = bias_t[:, _rel_bucket(jnp.int32(2 * nblk))]
    tiles = jnp.stack([toeplitz(0), toeplitz(nblk)], axis=1) - far[:, None, None, None]
    bias_step = (by_rel[:, nblk - jnp.arange(nblk, dtype=jnp.int32)][:, None, :],
                 jnp.stack([far, bias_t[:, 0]], axis=1))
    W = dict(
        norm_mix=norm_mix, norm_mem=norm_mem, norm_ffn=norm_ffn,
        ev_w_in=w_in.astype(BF16), ev_w_out=ev_w_out[0].astype(BF16),
        gdn_conv_w=gdn_conv_w[0], gdn_neg_a=-jnp.exp(gdn_A_log[0].astype(F32)), gdn_dt_bias=gdn_dt_bias[0],
        gdn_norm=gdn_norm[0], hgrn_norm=hgrn_norm[0], hgrn_lb=lb_all[0],
        od_w_in=od_w_in[0].astype(BF16), od_w_out=od_w_out[0].astype(BF16),
        moba_gq=moba_gq[0], moba_gk=moba_gk[0], moba_bias_tiles=tiles, moba_bias_step=bias_step,
        s5=_s5_weights(s5_A_re[0], s5_A_im[0], s5_log_dt[0], s5_B_re[0], s5_B_im[0], s5_C_re[0], s5_C_im[0], s5_D[0]),
        s5_glu_w=s5_glu_w[0].astype(BF16),
        mem_wq=mem_wq.astype(BF16), mem_gq=mem_gq, mem_wo=mem_wo.astype(BF16),
        ffn_w_gu=ffn_w_gu.astype(BF16), ffn_conv_w=ffn_conv_w, ffn_conv_b=ffn_conv_b, ffn_w_down=ffn_w_down.astype(BF16),
    )

    mk_l, mv_l = [], []
    for l in range(2):
        mk, mv = _mem_kv(mem_prompt, norm_memkv[l], mem_wkv[l].astype(BF16), mem_gk[l])
        mk_l.append(mk)
        mv_l.append(mv)
    p_mem_k, p_mem_v = jnp.stack(mk_l), jnp.stack(mv_l)
    zeros = lambda *s: jnp.zeros(s, F32)
    st_p = dict(gdn_conv=zeros(Bp, GDN_CONV - 1, d3), gdn=zeros(Bp, H_A, HD_A, HD_A), hgrn=zeros(Bp, H_B, DK_B, DV_B),
                s5_re=zeros(Bp, D_S5_STATE), s5_im=zeros(Bp, D_S5_STATE), mem_k=p_mem_k, mem_v=p_mem_v,
                ffn_conv=zeros(2, Bp, FFN_CONV - 1, D_FF))
    op = _trunk(x_prompt, st_p, W, decode=False)

    st_s = dict(gdn_conv=state_gdn_conv[0], gdn=state_gdn[0], hgrn=state_hgrn[0],
                s5_re=state_s5_re[0].reshape(Bs, D_S5_STATE), s5_im=state_s5_im[0].reshape(Bs, D_S5_STATE),
                mem_k=cache_mem_k, mem_v=cache_mem_v,
                ffn_conv=state_ffn_conv,
                cache_k=jnp.transpose(cache_moba_k[0], (0, 2, 3, 1)), cache_v=jnp.transpose(cache_moba_v[0], (0, 2, 3, 1)),
                page_table=page_table)
    os_ = _trunk(x_sample, st_s, W, decode=True)

    M = mem_prompt.shape[1]
    return (op['y'], os_['y'],
            op['gdn_conv'][None], op['gdn'][None], op['hgrn'][None], op['moba_k'][None], op['moba_v'][None],
            op['s5_re'][None], op['s5_im'][None],
            p_mem_k.reshape(2, Bp, M, MEM_H, MEM_HD), p_mem_v.reshape(2, Bp, M, MEM_H, MEM_HD),
            jnp.stack([op['ffn0'], op['ffn1']]),
            os_['gdn_conv'][None], os_['gdn'][None], os_['hgrn'][None], os_['moba_k'][None], os_['moba_v'][None],
            os_['s5_re'][None], os_['s5_im'][None], jnp.stack([os_['ffn0'], os_['ffn1']]))
```

```python
import functools
import math

import jax
import jax.numpy as jnp
from jax import lax
from jax.experimental import pallas as pl
from jax.experimental.pallas import tpu as pltpu

F32 = jnp.float32
BF16 = jnp.bfloat16
HI = lax.Precision.HIGHEST
EPS = 1e-6

LANES = 128
SUBLANES = 8
VMEM_LIMIT_BYTES = 56 * 1024 * 1024

D_MODEL = 1024
H_A, HD_A, GDN_CONV = 8, 64, 4
H_B, DK_B, DV_B = 8, 64, 64
GDN_CHUNK, HGRN_CHUNK = 64, 32
GDN_GROUP, HGRN_GROUP = 2, 4
H_C, HD_C = 8, 64
MOBA_BLOCK, MOBA_TOPK = 256, 3
N_BUCKETS, MAX_DISTANCE = 32, 128
S5_GROUPS, S5_GROUP_CH, S5_STATE = 32, 16, 64
MEM_H, MEM_HD = 4, 128
D_FF, FFN_CONV = 2816, 3
PAGE_SIZE = 128
D_A = H_A * HD_A
D_S5 = S5_GROUPS * S5_GROUP_CH
D_S5_STATE = S5_GROUPS * S5_STATE
S5_MM_SPLIT = 4
FF_CHUNK = D_FF // 2
EV_IN_COLS = 8 * D_A + LANES
EV_IN_TILE = EV_IN_COLS // 3


def _cparams(*sem):
    return pltpu.CompilerParams(dimension_semantics=sem, vmem_limit_bytes=VMEM_LIMIT_BYTES)


def _dot(a, b, precision=None):
    return jnp.dot(a, b, preferred_element_type=F32, precision=precision)


def _dot_nt(a, b, precision=None):
    return lax.dot_general(a, b, (((1,), (1,)), ((), ())), preferred_element_type=F32, precision=precision)


def _dot_tn(a, b, precision=None):
    return lax.dot_general(a, b, (((0,), (0,)), ((), ())), preferred_element_type=F32, precision=precision)


def _rms(x, g):
    return x * lax.rsqrt(jnp.mean(x * x, axis=-1, keepdims=True) + EPS) * g


def _silu(x):
    return x * jax.nn.sigmoid(x)


def _tri(n, strict=False):
    r = lax.broadcasted_iota(jnp.int32, (n, n), 0)
    c = lax.broadcasted_iota(jnp.int32, (n, n), 1)
    return (r > c) if strict else (r >= c)


def _norm_proj_kernel(x_ref, g_ref, w_ref, e_ref, gain_ref, o_ref, *, n_norm_tiles, group):
    xn = _rms(x_ref[...], g_ref[...])
    z = _dot(xn.astype(BF16), w_ref[...])
    if n_norm_tiles == 0:
        o_ref[...] = z
        return
    j = pl.program_id(1)

    @pl.when(j < n_norm_tiles)
    def _():
        zz = z * z
        hi = zz.astype(BF16)
        lo = (zz - hi.astype(F32)).astype(BF16)
        s = _dot(hi, e_ref[...]) + _dot(lo, e_ref[...])
        o_ref[...] = z * lax.rsqrt(s * (1.0 / group) + EPS) * gain_ref[...]

    @pl.when(j >= n_norm_tiles)
    def _():
        o_ref[...] = z


def _norm_proj(x, g, w, *, tm, tn, n_norm_tiles=0, group=1, gain=None, time_major=None, name):
    T, D = x.shape
    N = w.shape[1]
    nj = N // tn
    if gain is None:
        gain = jnp.ones((N,), F32)
    gi = lax.broadcasted_iota(jnp.int32, (tn, tn), 0) // group
    gj = lax.broadcasted_iota(jnp.int32, (tn, tn), 1) // group
    e = (gi == gj).astype(BF16)
    if time_major is None:
        out_shape = jax.ShapeDtypeStruct((T, N), F32)
        out_spec = pl.BlockSpec((tm, tn), lambda i, j: (i, j))
    else:
        B, L = time_major
        nt = L // tm
        out_shape = jax.ShapeDtypeStruct((L, B * N), F32)
        out_spec = pl.BlockSpec((tm, tn), lambda i, j: (i % nt, (i // nt) * nj + j))
    return pl.pallas_call(
        functools.partial(_norm_proj_kernel, n_norm_tiles=n_norm_tiles, group=group),
        grid=(T // tm, nj),
        in_specs=[
            pl.BlockSpec((tm, D), lambda i, j: (i, 0)),
            pl.BlockSpec((1, D), lambda i, j: (0, 0)),
            pl.BlockSpec((D, tn), lambda i, j: (0, j)),
            pl.BlockSpec((tn, tn), lambda i, j: (0, 0)),
            pl.BlockSpec((1, tn), lambda i, j: (0, j)),
        ],
        out_specs=out_spec,
        out_shape=out_shape,
        compiler_params=_cparams("parallel", "arbitrary"),
        name=name,
    )(x, g.reshape(1, D), w, e, gain.reshape(1, N))


def _norm_proj_wide_kernel(x_ref, g_ref, w_ref, o_ref, *, chunk):
    xn = _rms(x_ref[...], g_ref[...]).astype(BF16)
    for c in range(w_ref.shape[1] // chunk):
        cs = slice(c * chunk, (c + 1) * chunk)
        o_ref[:, cs] = _dot(xn, w_ref[:, cs])


def _norm_proj_wide(x, g, w, *, tm, chunk, name):
    T, D = x.shape
    N = w.shape[1]
    const = lambda shape: pl.BlockSpec(shape, lambda i: (0, 0), pipeline_mode=pl.Buffered(1))
    return pl.pallas_call(
        functools.partial(_norm_proj_wide_kernel, chunk=chunk),
        grid=(T // tm,),
        in_specs=[pl.BlockSpec((tm, D), lambda i: (i, 0)), const((1, D)), const((D, N))],
        out_specs=pl.BlockSpec((tm, N), lambda i: (i, 0)),
        out_shape=jax.ShapeDtypeStruct((T, N), F32),
        compiler_params=_cparams("parallel"),
        name=name,
    )(x, g.reshape(1, D), w)


def _od_qkv_kernel(x_ref, g_ref, w_ref, e_ref, gain_ref, o_ref, *t_refs):
    dc = H_C * HD_C
    xn = _rms(x_ref[...], g_ref[...]).astype(BF16)
    for c in range(3):
        cs = slice(c * dc, (c + 1) * dc)
        z = _dot(xn, w_ref[:, cs])
        if c < 2:
            zz = z * z
            hi = zz.astype(BF16)
            lo = (zz - hi.astype(F32)).astype(BF16)
            s = _dot(hi, e_ref[...]) + _dot(lo, e_ref[...])
            z = z * lax.rsqrt(s * (1.0 / HD_C) + EPS) * gain_ref[:, cs]
        o_ref[:, cs] = z
        if t_refs and c > 0:
            t_refs[c - 1][...] = z.T


def _od_qkv_proj(x, g, w, gain, *, tm, seq=None):
    T, D = x.shape
    dc = H_C * HD_C
    gi = lax.broadcasted_iota(jnp.int32, (dc, dc), 0) // HD_C
    gj = lax.broadcasted_iota(jnp.int32, (dc, dc), 1) // HD_C
    e = (gi == gj).astype(BF16)
    const = lambda shape: pl.BlockSpec(shape, lambda i: (0, 0), pipeline_mode=pl.Buffered(1))
    out_specs = [pl.BlockSpec((tm, 3 * dc), lambda i: (i, 0))]
    out_shape = [jax.ShapeDtypeStruct((T, 3 * dc), F32)]
    if seq is not None:
        B, L = seq
        nt = L // tm
        tspec = pl.BlockSpec((None, dc, tm), lambda i: (i // nt, 0, i % nt))
        out_specs += [tspec, tspec]
        out_shape += [jax.ShapeDtypeStruct((B, dc, L), F32)] * 2
    return pl.pallas_call(
        _od_qkv_kernel,
        grid=(T // tm,),
        in_specs=[pl.BlockSpec((tm, D), lambda i: (i, 0)), const((1, D)), const((D, 3 * dc)), const((dc, dc)),
                  const((1, 2 * dc))],
        out_specs=out_specs,
        out_shape=out_shape,
        compiler_params=_cparams("parallel"),
        name="od_in_proj_qkv",
    )(x, g.reshape(1, D), w, e, gain.reshape(1, 2 * dc))


def _proj_res_kernel(*refs, n_in):
    x_ref, o_ref = refs[0], refs[-1]
    acc = x_ref[...]
    for k in range(n_in):
        a_ref, w_ref = refs[1 + 2 * k], refs[2 + 2 * k]
        acc = acc + _dot(a_ref[...].astype(BF16), w_ref[...])
    o_ref[...] = acc


def _proj_res(x, parts, *, tm, name):
    T, N = x.shape
    in_specs = [pl.BlockSpec((tm, N), lambda i: (i, 0))]
    args = [x]
    for a, w, spec in parts:
        in_specs += [spec, pl.BlockSpec(w.shape, lambda i: (0, 0))]
        args += [a, w]
    return pl.pallas_call(
        functools.partial(_proj_res_kernel, n_in=len(parts)),
        grid=(T // tm,),
        in_specs=in_specs,
        out_specs=pl.BlockSpec((tm, N), lambda i: (i, 0)),
        out_shape=jax.ShapeDtypeStruct((T, N), F32),
        compiler_params=_cparams("parallel"),
        name=name,
    )(*args)


def _neumann_inverse(mats, n):
    m = mats[0].shape[0]
    eye = (lax.broadcasted_iota(jnp.int32, (m, m), 0) == lax.broadcasted_iota(jnp.int32, (m, m), 1)).astype(F32)
    ts = [eye - a for a in mats]
    ps = [a.astype(BF16) for a in mats]
    k = 2
    while k < n:
        ps = [_dot(p, p).astype(BF16) for p in ps]
        ts = [t + _dot(t.astype(BF16), p) for t, p in zip(ts, ps)]
        k *= 2
    return ts


def _even_mixer_kernel(qkv_ref, zg_ref, hq_ref, hf_ref, hi_ref, hg_ref, ba_ref,
                       conv0_ref, sg0_ref, sh0_ref,
                       convw_ref, nega_ref, dtb_ref, gnorm_ref, hnorm_ref, lb_ref,
                       mixed_ref, tail_ref, sg_ref, sh_ref,
                       xp_ref, beta_ref, g_ref, sbd_ref,
                       *, tt, cg, ch, l_real):
    ti = pl.program_id(1)
    n_tiles = pl.num_programs(1)

    @pl.when(ti == 0)
    def _():
        xp_ref[0:SUBLANES, :] = conv0_ref[...]
        sh_ref[...] = sh0_ref[...]
        zero = jnp.zeros((HD_A, HD_A), F32)
        for p in range(H_A // 2):
            sbd_ref[p] = jnp.concatenate([jnp.concatenate([sg0_ref[2 * p], zero], axis=-1),
                                          jnp.concatenate([zero, sg0_ref[2 * p + 1]], axis=-1)], axis=0)

    xp_ref[SUBLANES:SUBLANES + tt, :] = qkv_ref[...]
    lr_last = l_real - (l_real - 1) // tt * tt
    tail_ref[...] = xp_ref[lr_last:lr_last + SUBLANES, :]

    padded = (l_real % tt) != 0
    if padded:
        row = lax.broadcasted_iota(jnp.int32, (tt, 1), 0) + ti * tt
        valid = row < l_real
    ba = ba_ref[...]
    beta_all = jax.nn.sigmoid(ba)
    g_all = nega_ref[...] * jax.nn.softplus(ba + dtb_ref[...])
    if padded:
        beta_all = jnp.where(valid, beta_all, 0.0)
        g_all = jnp.where(valid, g_all, 0.0)
    beta_ref[...] = beta_all
    g_ref[...] = g_all

    ltri_g = _tri(cg).astype(F32)
    hnorm = hnorm_ref[...]
    gnorm2 = jnp.concatenate([gnorm_ref[...], gnorm_ref[...]], axis=-1)
    pairs = range(H_A // 2)
    n2 = 2 * cg
    lane = lax.broadcasted_iota(jnp.int32, (1, LANES), 1)
    m0 = (lane < HD_A).astype(F32)
    m1 = 1.0 - m0
    ri = lax.broadcasted_iota(jnp.int32, (n2, n2), 0)
    cj = lax.broadcasted_iota(jnp.int32, (n2, n2), 1)
    same_head = (ri // cg) == (cj // cg)
    bd_tril = same_head & (ri >= cj)
    bd_stril = same_head & (ri > cj)

    def stack2(a):
        return jnp.concatenate([a * m0, a * m1], axis=0)

    def rep2(col0, col1):
        return jnp.concatenate([jnp.broadcast_to(col0, (cg, LANES)), jnp.broadcast_to(col1, (cg, LANES))], axis=0)

    ng = min(GDN_GROUP, tt // cg)

    def gdn_group(gi, _):
        items = [(c, p) for c in range(ng) for p in pairs]
        rows_c, gcum_c, gt_c, bet_c, qkv_c = [], [], [], [], []
        for c in range(ng):
            r0 = pl.multiple_of((gi * ng + c) * cg, cg)
            rows = pl.ds(r0, cg)
            win = xp_ref[pl.ds(r0, cg + SUBLANES), :]
            conv = convw_ref[GDN_CONV - 1:GDN_CONV, :] * win[SUBLANES:SUBLANES + cg]
            for j in range(GDN_CONV - 1):
                off = SUBLANES - (GDN_CONV - 1) + j
                conv = conv + convw_ref[j:j + 1, :] * win[off:off + cg]
            qkv_c.append(_silu(conv))
            gcum = _dot(ltri_g, g_ref[rows, :], HI)
            gpad = gcum if cg == LANES else jnp.concatenate([gcum, jnp.zeros((LANES - cg, LANES), F32)], axis=0)
            rows_c.append(rows)
            gcum_c.append(gcum)
            gt_c.append(gpad.T)
            bet_c.append(beta_ref[rows, :])
        qs, ks, vs, betas, gcs, decays, glasts, glrows = {}, {}, {}, {}, {}, {}, {}, {}
        for it in items:
            c, p = it
            rows = rows_c[c]
            h0, h1 = H_A + 2 * p, H_A + 2 * p + 1
            q = stack2(qkv_c[c][:, p * LANES:(p + 1) * LANES])
            k = stack2(qkv_c[c][:, D_A + p * LANES:D_A + (p + 1) * LANES])
            qs[it] = q * lax.rsqrt(jnp.sum(q * q, axis=-1, keepdims=True) + EPS) * (HD_A ** -0.5)
            ks[it] = k * lax.rsqrt(jnp.sum(k * k, axis=-1, keepdims=True) + EPS)
            vs[it] = stack2(qkv_c[c][:, 2 * D_A + p * LANES:2 * D_A + (p + 1) * LANES])
            betas[it] = rep2(bet_c[c][:, 2 * p:2 * p + 1], bet_c[c][:, 2 * p + 1:2 * p + 2])
            gc = rep2(gcum_c[c][:, h0:h0 + 1], gcum_c[c][:, h1:h1 + 1])
            gr = jnp.concatenate([gt_c[c][h0:h0 + 1, 0:cg], gt_c[c][h1:h1 + 1, 0:cg]], axis=-1)
            gcs[it] = gc
            glasts[it] = rep2(gcum_c[c][cg - 1:cg, h0:h0 + 1], gcum_c[c][cg - 1:cg, h1:h1 + 1])
            glrows[it] = jnp.exp(glasts[it][0:1, :]) * m0 + jnp.exp(glasts[it][cg:cg + 1, :]) * m1
            decays[it] = jnp.exp(jnp.where(bd_tril, gc[:, 0:n2] - gr, -jnp.inf))
        kqs = {it: _dot_nt(jnp.concatenate([ks[it], qs[it]], axis=0).astype(BF16), ks[it].astype(BF16))
               for it in items}
        ts = dict(zip(items, _neumann_inverse(
            [jnp.where(bd_stril, betas[it][:, 0:n2] * kqs[it][0:n2] * decays[it], 0.0) for it in items], cg)))
        egs = {it: jnp.exp(gcs[it]) for it in items}
        sols = {it: _dot(ts[it].astype(BF16),
                         jnp.concatenate([betas[it] * vs[it], (betas[it] * egs[it]) * ks[it]], axis=-1).astype(BF16))
                for it in items}
        wqs = {it: jnp.concatenate([sols[it][:, LANES:2 * LANES], qs[it] * egs[it]], axis=0).astype(BF16)
               for it in items}
        qks = {it: (kqs[it][n2:2 * n2] * decays[it]).astype(BF16) for it in items}
        kouts = {it: (ks[it] * jnp.exp(glasts[it] - gcs[it])).astype(BF16) for it in items}
        ss = [sbd_ref[p] for p in pairs]
        for c in range(ng):
            wss = [_dot(wqs[(c, p)], ss[p].astype(BF16)) for p in pairs]
            ubs = [(sols[(c, p)][:, 0:LANES] - wss[p][0:n2]).astype(BF16) for p in pairs]
            os_ = [wss[p][n2:2 * n2] + _dot(qks[(c, p)], ubs[p]) for p in pairs]
            ss = [ss[p] * glrows[(c, p)] + _dot_tn(kouts[(c, p)], ubs[p]) for p in pairs]
            for p in pairs:
                o = os_[p]
                on = o * lax.rsqrt(jnp.sum(o * o, axis=-1, keepdims=True) * (1.0 / HD_A) + EPS)
                cols = slice(p * LANES, (p + 1) * LANES)
                mixed_ref[rows_c[c], cols] = (on[0:cg] + on[cg:n2]) * gnorm2 * _silu(zg_ref[rows_c[c], cols])
        for p in pairs:
            sbd_ref[p] = ss[p]
        return 0

    lax.fori_loop(0, tt // (cg * ng), gdn_group, 0)
    xp_ref[0:SUBLANES, :] = xp_ref[tt:tt + SUBLANES, :]
    for p in pairs:
        sg_ref[2 * p] = sbd_ref[p, 0:HD_A, 0:HD_A]
        sg_ref[2 * p + 1] = sbd_ref[p, HD_A:2 * HD_A, HD_A:2 * HD_A]

    nh = min(HGRN_GROUP, tt // ch)
    gr = nh * ch
    ri = lax.broadcasted_iota(jnp.int32, (gr, gr), 0)
    cj = lax.broadcasted_iota(jnp.int32, (gr, gr), 1)
    same_chunk_causal = (ri >= cj) & ((ri // ch) == (cj // ch))
    ltri_h = same_chunk_causal.astype(F32)
    lb = lb_ref[...]
    r_mid = (ch - 1) // 2

    def hgrn_group(gi, _):
        r0 = pl.multiple_of(gi * gr, gr)
        rows = pl.ds(r0, gr)
        f = lb + (1.0 - lb) * jax.nn.sigmoid(hf_ref[rows, :])
        logf = jnp.log(f)
        kk = 1.0 - f
        if padded:
            vrow = (lax.broadcasted_iota(jnp.int32, (gr, 1), 0) + r0 + ti * tt) < l_real
            logf = jnp.where(vrow, logf, 0.0)
            kk = jnp.where(vrow, kk, 0.0)
        qh = _silu(hq_ref[rows, :])
        bc = _dot(ltri_h, logf, HI).reshape(nh, ch, D_A)
        bref = bc[:, r_mid:r_mid + 1, :]
        blast = bc[:, ch - 1:ch, :]
        grp = lambda a: a.reshape(gr, D_A)
        qh3, kk3 = qh.reshape(nh, ch, D_A), kk.reshape(nh, ch, D_A)
        qa = grp(qh3 * jnp.exp(bc - bref)).astype(BF16)
        kb = grp(kk3 * jnp.exp(bref - bc)).astype(BF16)
        q_in = grp(qh3 * jnp.exp(bc)).astype(BF16)
        k_out = grp(kk3 * jnp.exp(blast - bc)).astype(BF16)
        fl = jnp.exp(blast)
        vv = hi_ref[rows, :].astype(BF16)
        heads = range(H_B)
        sls = [slice(h * DK_B, (h + 1) * DK_B) for h in heads]
        atts = [jnp.where(same_chunk_causal, _dot_nt(qa[:, sl], kb[:, sl]), 0.0).astype(BF16) for sl in sls]
        intra = [_dot(atts[h], vv[:, sls[h]]) for h in heads]
        sts = [sh_ref[h] for h in heads]
        inter = [[] for _ in heads]
        for c in range(nh):
            rc = slice(c * ch, (c + 1) * ch)
            for h in heads:
                inter[h].append(_dot_nt(q_in[rc, sls[h]], sts[h].astype(BF16)))
            sts = [sts[h] * fl[c][:, sls[h]] + _dot_tn(vv[rc, sls[h]], k_out[rc, sls[h]]) for h in heads]
        for h in heads:
            sh_ref[h] = sts[h]
        os_ = [intra[h] + (inter[h][0] if nh == 1 else jnp.concatenate(inter[h], axis=0)) for h in heads]
        outs = [_rms(os_[h], hnorm) * _silu(hg_ref[rows, h * DV_B:(h + 1) * DV_B]) for h in heads]
        for p in range(H_B // 2):
            c0 = D_A + p * LANES
            mixed_ref[rows, c0:c0 + LANES] = jnp.concatenate([outs[2 * p], outs[2 * p + 1]], axis=-1)
        return 0

    lax.fori_loop(0, tt // gr, hgrn_group, 0)


def _even_mixer(z, conv0, sg0, sh0_t, conv_w, neg_a, dt_bias, gnorm, hnorm, lb, *, l_real, tt):
    B, Lp, _ = z.shape
    cg = min(GDN_CHUNK, tt)
    ch = min(HGRN_CHUNK, tt)
    d3 = 3 * D_A
    nt = Lp // tt
    col = lambda w, idx: pl.BlockSpec((None, tt, w), lambda b, i, idx=idx: (b, i, idx))
    state = lambda: pl.BlockSpec((None, H_A, HD_A, HD_A), lambda b, i: (b, 0, 0, 0))
    vec = lambda n: pl.BlockSpec((1, n), lambda b, i: (0, 0))
    nega_row = jnp.zeros((1, LANES), F32).at[0, H_A:2 * H_A].set(neg_a)
    dtb_row = jnp.zeros((1, LANES), F32).at[0, H_A:2 * H_A].set(dt_bias)
    return pl.pallas_call(
        functools.partial(_even_mixer_kernel, tt=tt, cg=cg, ch=ch, l_real=l_real),
        grid=(B, nt),
        in_specs=[
            col(d3, 0), col(D_A, 3), col(D_A, 4), col(D_A, 5), col(D_A, 6), col(D_A, 7), col(LANES, 32),
            pl.BlockSpec((None, SUBLANES, d3), lambda b, i: (b, 0, 0)),
            state(), state(),
            pl.BlockSpec((GDN_CONV, d3), lambda b, i: (0, 0)),
            vec(LANES), vec(LANES), vec(HD_A), vec(DV_B), vec(D_A),
        ],
        out_specs=[
            pl.BlockSpec((None, tt, 2 * D_A), lambda b, i: (b, i, 0)),
            pl.BlockSpec((None, SUBLANES, d3), lambda b, i: (b, 0, 0)),
            state(), state(),
        ],
        out_shape=[
            jax.ShapeDtypeStruct((B, Lp, 2 * D_A), F32),
            jax.ShapeDtypeStruct((B, SUBLANES, d3), F32),
            jax.ShapeDtypeStruct((B, H_A, HD_A, HD_A), F32),
            jax.ShapeDtypeStruct((B, H_B, DV_B, DK_B), F32),
        ],
        scratch_shapes=[
            pltpu.VMEM((tt + SUBLANES, d3), F32),
            pltpu.VMEM((tt, LANES), F32),
            pltpu.VMEM((tt, LANES), F32),
            pltpu.VMEM((H_A // 2, 2 * HD_A, 2 * HD_A), F32),
        ],
        compiler_params=_cparams("parallel", "arbitrary"),
        name="even_mixer",
    )(z, z, z, z, z, z, z, conv0, sg0, sh0_t, conv_w, nega_row, dtb_row,
      gnorm.reshape(1, HD_A), hnorm.reshape(1, DV_B), lb.reshape(1, D_A))


def _mem_attn_rows(x, mk_ref, mv_ref, g_ref, wq_ref, gq_ref, wo_ref):
    q = _dot(_rms(x, g_ref[...]).astype(BF16), wq_ref[...])
    scale = MEM_HD ** -0.5
    outs = []
    for h in range(MEM_H):
        sl = slice(h * MEM_HD, (h + 1) * MEM_HD)
        qh = _rms(q[:, sl], gq_ref[...])
        s = _dot_nt(qh.astype(BF16), mk_ref[:, sl].astype(BF16)) * scale
        s = s - jnp.max(s, axis=-1, keepdims=True)
        p = jnp.exp(s)
        p = p / jnp.sum(p, axis=-1, keepdims=True)
        outs.append(_dot(p.astype(BF16), mv_ref[:, sl].astype(BF16)))
    o = jnp.concatenate(outs, axis=-1)
    return x + _dot(o.astype(BF16), wo_ref[...])


def _mem_attn_step_kernel(q_ref, mk_ref, mv_ref, o_ref):
    scale = MEM_HD ** -0.5
    outs = []
    for h in range(MEM_H):
        qh = jnp.broadcast_to(q_ref[:, h * MEM_HD:(h + 1) * MEM_HD], (SUBLANES, MEM_HD))
        s = _dot_nt(qh.astype(BF16), mk_ref[:, h, :].astype(BF16)) * scale
        s = s - jnp.max(s, axis=-1, keepdims=True)
        p = jnp.exp(s)
        p = p / jnp.sum(p, axis=-1, keepdims=True)
        outs.append(_dot(p.astype(BF16), mv_ref[:, h, :].astype(BF16))[0:1])
    o_ref[...] = jnp.concatenate(outs, axis=-1)


def _mem_attn_step(q, cache_k, cache_v, l):
    B = q.shape[0]
    M = cache_k.shape[2]
    dq = MEM_H * MEM_HD
    mem = pl.BlockSpec((None, None, M, MEM_H, MEM_HD), lambda b: (l, b, 0, 0, 0))
    row = pl.BlockSpec((None, 1, dq), lambda b: (b, 0, 0))
    return pl.pallas_call(
        _mem_attn_step_kernel,
        grid=(B,),
        in_specs=[row, mem, mem],
        out_specs=row,
        out_shape=jax.ShapeDtypeStruct((B, 1, dq), F32),
        compiler_params=_cparams("parallel"),
        name="mem_attn_step",
    )(q, cache_k, cache_v)


def _post_seq_kernel(*refs, n_parts, tm):
    x_ref = refs[0]
    parts = refs[1:1 + 2 * n_parts]
    (mk_ref, mv_ref, gmem_ref, wq_ref, gq_ref, wo_ref, buf0_ref, g_ref, wgu_ref, cw_ref, cb_ref, wd_ref,
     o_ref, tail_ref, gp_ref, carry_ref) = refs[1 + 2 * n_parts:]
    ti = pl.program_id(1)

    @pl.when(ti == 0)
    def _():
        carry_ref[...] = buf0_ref[...]

    x = x_ref[...]
    for k in range(n_parts):
        x = x + _dot(parts[2 * k][...].astype(BF16), parts[2 * k + 1][...])
    x = _mem_attn_rows(x, mk_ref, mv_ref, gmem_ref, wq_ref, gq_ref, wo_ref)
    xn = _rms(x, g_ref[...]).astype(BF16)
    acc = x
    for c in range(D_FF // FF_CHUNK):
        cs = slice(c * FF_CHUNK, (c + 1) * FF_CHUNK)
        gate = _dot(xn, wgu_ref[:, cs])
        up = _dot(xn, wgu_ref[:, D_FF + c * FF_CHUNK:D_FF + (c + 1) * FF_CHUNK])
        gp_ref[0:SUBLANES, :] = carry_ref[:, cs]
        gp_ref[SUBLANES:SUBLANES + tm, :] = gate
        conv = cw_ref[FFN_CONV - 1:FFN_CONV, cs] * gate + cb_ref[:, cs]
        for j in range(FFN_CONV - 1):
            off = SUBLANES - (FFN_CONV - 1) + j
            conv = conv + cw_ref[j:j + 1, cs] * gp_ref[off:off + tm, :]
        last = gp_ref[tm:tm + SUBLANES, :]
        carry_ref[:, cs] = last
        tail_ref[:, cs] = last
        act = _silu(conv) * up
        acc = acc + _dot(act.astype(BF16), wd_ref[cs, :])
    o_ref[...] = acc


def _post_seq(x, parts, mk, mv, gmem, wq, gq, wo, buf0, g, wgu, cw, cb, wd, *, tm, name):
    B, L, D = x.shape
    M = mk.shape[1]
    dq = MEM_H * MEM_HD
    const = lambda shape: pl.BlockSpec(shape, lambda b, i: (0,) * len(shape), pipeline_mode=pl.Buffered(1))
    row_tile = pl.BlockSpec((None, tm, D), lambda b, i: (b, i, 0))
    mem = pl.BlockSpec((None, M, dq), lambda b, i: (b, 0, 0))
    in_specs, args = [row_tile], [x]
    for a, w, spec in parts:
        in_specs += [spec, const(w.shape)]
        args += [a, w]
    in_specs += [mem, mem, const((1, D)), const((D, dq)), const((1, MEM_HD)), const((dq, D)),
                 pl.BlockSpec((None, SUBLANES, D_FF), lambda b, i: (b, 0, 0)),
                 const((1, D)), const((D, 2 * D_FF)), const((FFN_CONV, D_FF)), const((1, D_FF)), const((D_FF, D))]
    args += [mk, mv, gmem.reshape(1, D), wq, gq.reshape(1, MEM_HD), wo, buf0, g.reshape(1, D), wgu, cw,
             cb.reshape(1, D_FF), wd]
    return pl.pallas_call(
        functools.partial(_post_seq_kernel, n_parts=len(parts), tm=tm),
        grid=(B, L // tm),
        in_specs=in_specs,
        out_specs=[row_tile, pl.BlockSpec((None, SUBLANES, D_FF), lambda b, i: (b, 0, 0))],
        out_shape=[jax.ShapeDtypeStruct((B, L, D), F32), jax.ShapeDtypeStruct((B, SUBLANES, D_FF), F32)],
        scratch_shapes=[pltpu.VMEM((tm + SUBLANES, FF_CHUNK), F32), pltpu.VMEM((SUBLANES, D_FF), F32)],
        compiler_params=_cparams("parallel", "arbitrary"),
        name=name,
    )(*args)


def _ffn_step_kernel(x_ref, p2_ref, p1_ref, g_ref, wgu_ref, cw_ref, cb_ref, wd_ref, o_ref, gate_ref):
    x = x_ref[...]
    xn = _rms(x, g_ref[...]).astype(BF16)
    acc = x
    for c in range(D_FF // FF_CHUNK):
        cs = slice(c * FF_CHUNK, (c + 1) * FF_CHUNK)
        gate = _dot(xn, wgu_ref[:, cs])
        up = _dot(xn, wgu_ref[:, D_FF + c * FF_CHUNK:D_FF + (c + 1) * FF_CHUNK])
        gate_ref[:, cs] = gate
        conv = cw_ref[0:1, cs] * p2_ref[:, cs] + cw_ref[1:2, cs] * p1_ref[:, cs] + cw_ref[2:3, cs] * gate + cb_ref[:, cs]
        act = _silu(conv) * up
        acc = acc + _dot(act.astype(BF16), wd_ref[cs, :])
    o_ref[...] = acc


def _ffn_step(x, p2, p1, g, wgu, cw, cb, wd):
    R, D = x.shape
    full = lambda a: pl.BlockSpec(a.shape, lambda i: (0,) * a.ndim)
    args = (x, p2, p1, g.reshape(1, D), wgu, cw, cb.reshape(1, D_FF), wd)
    return pl.pallas_call(
        _ffn_step_kernel,
        grid=(1,),
        in_specs=[full(a) for a in args],
        out_specs=[pl.BlockSpec((R, D), lambda i: (0, 0)), pl.BlockSpec((R, D_FF), lambda i: (0, 0))],
        out_shape=[jax.ShapeDtypeStruct((R, D), F32), jax.ShapeDtypeStruct((R, D_FF), F32)],
        compiler_params=_cparams("arbitrary"),
        name="conv_ffn_step",
    )(*args)


def _moba_query_tile(j, hs, qt_ref, kb_ref, vt_ref, kmean_ref, bias_ref, ot_ref):
    blk = MOBA_BLOCK
    w = (j + 1) * blk
    qts = [qt_ref[h] for h in hs]
    sts = [_dot(kb_ref[h, 0:w, :], (qt * (HD_C ** -0.5)).astype(BF16)) for h, qt in zip(hs, qts)]
    sels = []
    if j > MOBA_TOPK:
        iota_n = lax.broadcasted_iota(jnp.int32, (j, blk), 0).astype(F32)
        gs = [_dot(kmean_ref[h, 0:j, :], qt, HI) for h, qt in zip(hs, qts)]
        for g in gs:
            sel = jnp.zeros((j, blk), F32)
            for _ in range(MOBA_TOPK):
                m = jnp.max(g, axis=0, keepdims=True)
                idx = jnp.min(jnp.where(g == m, iota_n, 1e9), axis=0, keepdims=True)
                hit = iota_n == idx
                sel = jnp.where(hit, 1.0, sel)
                g = jnp.where(hit, -jnp.inf, g)
            sels.append(sel)
    key_le_query = lax.broadcasted_iota(jnp.int32, (blk, blk), 0) <= lax.broadcasted_iota(jnp.int32, (blk, blk), 1)
    pcats, ls = [], []
    for i, h in enumerate(hs):
        parts = []
        for n in range(j + 1):
            s = sts[i][n * blk:(n + 1) * blk, :]
            if j - n < 2:
                s = s + bias_ref[h, j - n]
            if n == j:
                s = jnp.where(key_le_query, s, -jnp.inf)
            elif j > MOBA_TOPK:
                s = jnp.where(sels[i][n:n + 1, :] > 0.5, s, -jnp.inf)
            parts.append(s)
        m = jnp.max(parts[0], axis=0, keepdims=True)
        for s in parts[1:]:
            m = jnp.maximum(m, jnp.max(s, axis=0, keepdims=True))
        ps = [jnp.exp(s - m) for s in parts]
        l = jnp.sum(ps[0], axis=0, keepdims=True)
        for p in ps[1:]:
            l = l + jnp.sum(p, axis=0, keepdims=True)
        pcats.append((ps[0] if j == 0 else jnp.concatenate(ps, axis=0)).astype(BF16))
        ls.append(l)
    outs = [_dot(vt_ref[h, :, 0:w], pc) for h, pc in zip(hs, pcats)]
    for h, o, l in zip(hs, outs, ls):
        ot_ref[h] = o / l


def _moba_seq_kernel(q_ref, k_ref, vtin_ref, bias_ref, o_ref, kmean_ref, kb_ref, vt_ref, qt_ref, ot_ref, *, nb):
    j = pl.program_id(1)
    blk = MOBA_BLOCK
    dc = H_C * HD_C

    @pl.when(j == 0)
    def _():
        for n in range(nb):
            km = jnp.mean(k_ref[n * blk:(n + 1) * blk, :], axis=0, keepdims=True)
            for h in range(H_C):
                kmean_ref[h, n:n + 1, :] = km[:, h * HD_C:(h + 1) * HD_C]

        def prep(n, _):
            rows = pl.ds(pl.multiple_of(n * blk, blk), blk)
            kblk = k_ref[rows, :]
            vt_ref[:, :, rows] = vtin_ref[:, rows].astype(BF16).reshape(H_C, HD_C, blk)
            for h in range(H_C):
                kb_ref[h, rows, :] = kblk[:, h * HD_C:(h + 1) * HD_C].astype(BF16)
            return 0
        lax.fori_loop(0, nb, prep, 0)

    qt_ref[...] = q_ref[...].T.reshape(H_C, HD_C, blk)
    for jj in range(nb):
        @pl.when(j == jj)
        def _(jj=jj):
            def head_pair(hp, _):
                _moba_query_tile(jj, [2 * hp, 2 * hp + 1], qt_ref, kb_ref, vt_ref, kmean_ref, bias_ref, ot_ref)
                return 0
            lax.fori_loop(0, H_C // 2, head_pair, 0)
    o_ref[...] = ot_ref[...].reshape(dc, blk).T


def _moba_seq(qkv, vt, bias_tiles):
    B, L, _ = qkv.shape
    nb = L // MOBA_BLOCK
    dc = H_C * HD_C
    return pl.pallas_call(
        functools.partial(_moba_seq_kernel, nb=nb),
        grid=(B, nb),
        in_specs=[
            pl.BlockSpec((None, MOBA_BLOCK, dc), lambda b, j: (b, j, 0)),
            pl.BlockSpec((None, L, dc), lambda b, j: (b, 0, 1)),
            pl.BlockSpec((None, dc, L), lambda b, j: (b, 0, 0)),
            pl.BlockSpec(bias_tiles.shape, lambda b, j: (0, 0, 0, 0), pipeline_mode=pl.Buffered(1)),
        ],
        out_specs=pl.BlockSpec((None, MOBA_BLOCK, dc), lambda b, j: (b, j, 0)),
        out_shape=jax.ShapeDtypeStruct((B, L, dc), F32),
        scratch_shapes=[pltpu.VMEM((H_C, nb, HD_C), F32), pltpu.VMEM((H_C, L, HD_C), BF16),
                        pltpu.VMEM((H_C, HD_C, L), BF16), pltpu.VMEM((H_C, HD_C, MOBA_BLOCK), F32),
                        pltpu.VMEM((H_C, HD_C, MOBA_BLOCK), F32)],
        compiler_params=_cparams("parallel", "arbitrary"),
        name="moba_seq",
    )(qkv, qkv, vt, bias_tiles)


MOBA_PAGES_PER_BLOCK = MOBA_BLOCK // PAGE_SIZE
MOBA_STEP_PAGES = 8


def _moba_score_kernel(pt_ref, q_ref, *refs, n_pages):
    del pt_ref
    ppg, ppb = MOBA_STEP_PAGES, MOBA_PAGES_PER_BLOCK
    kp_refs = refs[:ppg]
    s_ref, sel_ref, gate_ref = refs[ppg:]
    step = pl.program_id(1)
    nbp = n_pages // ppb
    q = q_ref[...]
    inv = 1.0 / (MOBA_BLOCK * HD_C ** -0.5)
    gsum = None
    for r in range(ppg):
        sc = jnp.sum(kp_refs[r][...] * q, axis=1)
        s_ref[:, r * PAGE_SIZE:(r + 1) * PAGE_SIZE] = sc
        gs = jnp.sum(sc, axis=-1, keepdims=True)
        gsum = gs if r % ppb == 0 else gsum + gs
        if r % ppb == ppb - 1:
            gate_ref[step * (ppg // ppb) + r // ppb] = jnp.broadcast_to(gsum * inv, (H_C, LANES))

    @pl.when(step == n_pages // ppg - 1)
    def _():
        g = gate_ref[...]
        iota_b = lax.broadcasted_iota(jnp.int32, (nbp, H_C, LANES), 0).astype(F32)
        for i in range(MOBA_TOPK):
            m = jnp.max(g, axis=0, keepdims=True)
            idx = jnp.min(jnp.where(g == m, iota_b, 1e9), axis=0, keepdims=True)
            sel_ref[i] = idx[0]
            g = jnp.where(iota_b == idx, -jnp.inf, g)


def _moba_attend_kernel(pt_ref, sel_ref, q_ref, kn_ref, vn_ref, blast_ref, bsc_ref, *refs, nbp):
    del pt_ref
    nsel, ppb = MOBA_TOPK, MOBA_PAGES_PER_BLOCK
    s_refs, v_refs, o_ref = refs[:nsel], refs[nsel:nsel + nsel * ppb], refs[-1]
    b, h = pl.program_id(0), pl.program_id(1)
    far = bsc_ref[h, 0]
    s_new = jnp.sum(q_ref[...] * kn_ref[...], axis=-1, keepdims=True) * (HD_C ** -0.5) + bsc_ref[h, 1]
    ss = []
    for i in range(nsel):
        blk = sel_ref[(b * H_C + h) * nsel + i]
        ss.append(s_refs[i][...] + jnp.where(blk == nbp - 1, blast_ref[...], far))
    s_all = jnp.concatenate(ss, axis=-1)
    m = jnp.maximum(s_new, jnp.max(s_all, axis=-1, keepdims=True))
    w_new = jnp.exp(s_new - m)
    e = jnp.exp(s_all - m)
    den = w_new + jnp.sum(e, axis=-1, keepdims=True)
    v_all = jnp.concatenate([v_refs[k][...] for k in range(nsel * ppb)], axis=-1)
    eb = jnp.broadcast_to(e, (SUBLANES, nsel * MOBA_BLOCK)).astype(BF16)
    num = w_new * vn_ref[...] + _dot_nt(eb, v_all.astype(BF16))[0:1]
    o_ref[...] = num / den


def _moba_step(q, k_new, v_new, cache_k, cache_v, page_table, bias_last, bias_scalars):
    B = q.shape[0]
    n_pages = page_table.shape[1]
    ppg, ppb, nsel = MOBA_STEP_PAGES, MOBA_PAGES_PER_BLOCK, MOBA_TOPK
    assert n_pages % ppg == 0 and ppg % ppb == 0
    nbp = n_pages // ppb
    assert nbp >= nsel
    pt = page_table.reshape(-1)
    q_rep = jnp.broadcast_to((q * (HD_C ** -0.5))[..., None], (B, H_C, HD_C, LANES))
    kpage = lambda r: pl.BlockSpec((None, H_C, HD_C, PAGE_SIZE),
                                   lambda b, s, pt, r=r: (pt[b * n_pages + s * ppg + r], 0, 0, 0))
    scores, sel = pl.pallas_call(
        functools.partial(_moba_score_kernel, n_pages=n_pages),
        grid_spec=pltpu.PrefetchScalarGridSpec(
            num_scalar_prefetch=1,
            grid=(B, n_pages // ppg),
            in_specs=[pl.BlockSpec((None, H_C, HD_C, LANES), lambda b, s, pt: (b, 0, 0, 0))]
                     + [kpage(r) for r in range(ppg)],
            out_specs=[pl.BlockSpec((None, H_C, ppg * PAGE_SIZE), lambda b, s, pt: (b, 0, s)),
                       pl.BlockSpec((None, nsel, H_C, LANES), lambda b, s, pt: (b, 0, 0, 0))],
            scratch_shapes=[pltpu.VMEM((nbp, H_C, LANES), F32)],
        ),
        out_shape=[jax.ShapeDtypeStruct((B, H_C, n_pages * PAGE_SIZE), F32),
                   jax.ShapeDtypeStruct((B, nsel, H_C, LANES), F32)],
        compiler_params=_cparams("parallel", "arbitrary"),
        name="moba_step_score",
    )(pt, q_rep, *([cache_k] * ppg))
    sel_flat = jnp.swapaxes(sel[:, :, :, 0], 1, 2).astype(jnp.int32).reshape(-1)
    tok = pl.BlockSpec((None, None, 1, HD_C), lambda b, h, pt, sl: (b, h, 0, 0))
    srow = lambda i: pl.BlockSpec((None, None, None, 1, MOBA_BLOCK),
                                  lambda b, h, pt, sl, i=i: (b, h, sl[(b * H_C + h) * nsel + i], 0, 0))
    vpage = lambda i, r: pl.BlockSpec(
        (None, None, HD_C, PAGE_SIZE),
        lambda b, h, pt, sl, i=i, r=r: (pt[b * n_pages + sl[(b * H_C + h) * nsel + i] * ppb + r], h, 0, 0))
    row = lambda a: a.reshape(B, H_C, 1, HD_C)
    out = pl.pallas_call(
        functools.partial(_moba_attend_kernel, nbp=nbp),
        grid_spec=pltpu.PrefetchScalarGridSpec(
            num_scalar_prefetch=2,
            grid=(B, H_C),
            in_specs=[tok, tok, tok,
                      pl.BlockSpec((None, 1, MOBA_BLOCK), lambda b, h, pt, sl: (h, 0, 0)),
                      pl.BlockSpec(memory_space=pltpu.SMEM)]
                     + [srow(i) for i in range(nsel)]
                     + [vpage(i, r) for i in range(nsel) for r in range(ppb)],
            out_specs=tok,
        ),
        out_shape=jax.ShapeDtypeStruct((B, H_C, 1, HD_C), F32),
        compiler_params=_cparams("parallel", "arbitrary"),
        name="moba_step_attend",
    )(pt, sel_flat, row(q), row(k_new), row(v_new), bias_last, bias_scalars,
      *([scores.reshape(B, H_C, nbp, 1, MOBA_BLOCK)] * nsel), *([cache_v] * (nsel * ppb)))
    return out.reshape(B, H_C, HD_C)


def _s5_kernel(u_ref, h0r_ref, h0i_ref, wb_ref, ar_ref, ai_ref, wc_ref, d_ref, wglu_ref,
               o_ref, hr_ref, hi_ref, x_ref, *, tt, nb):
    ti = pl.program_id(0)
    ns = D_S5_STATE

    @pl.when(ti == 0)
    def _():
        hr_ref[...] = h0r_ref[...]
        hi_ref[...] = h0i_ref[...]

    nslab = ns // LANES
    if tt == 1:
        seqs, seq_rows, mat_rows, cols = [0], lambda b: slice(None), lambda b: slice(None), lambda b: slice(None)
        u = u_ref[...]
    else:
        seqs = range(nb)
        seq_rows = lambda b: pl.ds(b, tt, stride=nb)
        mat_rows = lambda b: slice(b * tt, (b + 1) * tt)
        cols = lambda b: slice(b * D_S5, (b + 1) * D_S5)
        u = jnp.concatenate([u_ref[:, cols(b)] for b in seqs], axis=0)
    nsp = S5_MM_SPLIT
    du, dsx = D_S5 // nsp, ns // nsp
    ub = u.astype(BF16)
    for part in range(2):
        for g in range(nsp):
            xg = _dot(ub[:, g * du:(g + 1) * du], wb_ref[g * du:(g + 1) * du, part * ns + g * dsx:part * ns + (g + 1) * dsx])
            for b in seqs:
                for c in range(dsx // LANES):
                    slab = (part * ns + g * dsx) // LANES + c
                    x_ref[slab, seq_rows(b), :] = xg[mat_rows(b), c * LANES:(c + 1) * LANES]
    slabs = lambda ref: jnp.stack([ref[:, c * LANES:(c + 1) * LANES] for c in range(nslab)])
    ar = slabs(ar_ref)
    ai = slabs(ai_ref)

    def step(t, carry):
        hr, hi = carry
        rows = pl.ds(pl.multiple_of(t * nb, nb), nb)
        nr = ar * hr - ai * hi + x_ref[0:nslab, rows, :]
        ni = ar * hi + ai * hr + x_ref[nslab:2 * nslab, rows, :]
        x_ref[0:nslab, rows, :] = nr
        x_ref[nslab:2 * nslab, rows, :] = ni
        return nr, ni

    hr, hi = lax.fori_loop(0, tt, step, (slabs(hr_ref), slabs(hi_ref)))
    for c in range(nslab):
        hr_ref[:, c * LANES:(c + 1) * LANES] = hr[c]
        hi_ref[:, c * LANES:(c + 1) * LANES] = hi[c]

    def state_cols(first_slab, n):
        hs = [jnp.concatenate([x_ref[first_slab + c, seq_rows(b), :] for c in range(n)], axis=-1) for b in seqs]
        return (hs[0] if len(hs) == 1 else jnp.concatenate(hs, axis=0)).astype(BF16)

    ys = []
    for g in range(nsp):
        yg = None
        for part in range(2):
            hg = state_cols((part * ns + g * dsx) // LANES, dsx // LANES)
            t = _dot(hg, wc_ref[part * ns + g * dsx:part * ns + (g + 1) * dsx, g * du:(g + 1) * du])
            yg = t if yg is None else yg + t
        ys.append(yg)
    y = jax.nn.gelu(jnp.concatenate(ys, axis=-1) + d_ref[...] * u)
    gl = _dot(y.astype(BF16), wglu_ref[...])
    o = gl[:, 0:D_S5] * jax.nn.sigmoid(gl[:, D_S5:2 * D_S5])
    for b in seqs:
        o_ref[:, cols(b)] = o[mat_rows(b), :]


def _s5(u, h0r, h0i, wb, ar, ai, wc, d, wglu, *, tt, nb):
    L = u.shape[0]
    const = lambda a: pl.BlockSpec(a.shape, lambda i: (0,) * a.ndim)
    rows = pl.BlockSpec((tt, nb * D_S5), lambda i: (i, 0))
    st = pl.BlockSpec((nb, D_S5_STATE), lambda i: (0, 0))
    args = (u, h0r, h0i, wb, ar, ai, wc, d, wglu)
    if tt == 1:
        assert L == 1
        rows = pl.BlockSpec((nb, D_S5), lambda i: (0, 0))
        args = (u.reshape(nb, D_S5),) + args[1:]
    return pl.pallas_call(
        functools.partial(_s5_kernel, tt=tt, nb=nb),
        grid=(L // tt,),
        in_specs=[rows] + [const(a) for a in args[1:]],
        out_specs=[rows, st, st],
        out_shape=[jax.ShapeDtypeStruct(args[0].shape, F32),
                   jax.ShapeDtypeStruct((nb, D_S5_STATE), F32), jax.ShapeDtypeStruct((nb, D_S5_STATE), F32)],
        scratch_shapes=[pltpu.VMEM((2 * D_S5_STATE // LANES, tt * nb, LANES), F32)],
        compiler_params=_cparams("arbitrary"),
        name="s5",
    )(*args)


def _rel_bucket(dist):
    n = jnp.maximum(dist, 0)
    exact = N_BUCKETS // 2
    nf = jnp.maximum(n, exact).astype(F32)
    large = exact + (jnp.log(nf / exact) / math.log(MAX_DISTANCE / exact) * (N_BUCKETS - exact)).astype(jnp.int32)
    return jnp.where(n < exact, n, jnp.minimum(large, N_BUCKETS - 1))


def _block_diag(w):
    G, a, b = w.shape
    eye = jnp.eye(G, dtype=w.dtype)
    return (eye[:, None, :, None] * w[:, :, None, :]).reshape(G * a, G * b)


def _s5_weights(A_re, A_im, log_dt, B_re, B_im, C_re, C_im, D):
    dt = jnp.exp(log_dt.astype(F32))[:, None]
    lr, li = A_re.astype(F32), A_im.astype(F32)
    mag = jnp.exp(lr * dt)
    ar, ai = mag * jnp.cos(li * dt), mag * jnp.sin(li * dt)
    den = lr * lr + li * li
    nr = ar - 1.0
    cr, ci = (nr * lr + ai * li) / den, (ai * lr - nr * li) / den
    bbr = cr[..., None] * B_re - ci[..., None] * B_im
    bbi = cr[..., None] * B_im + ci[..., None] * B_re
    wb = jnp.concatenate([_block_diag(bbr.transpose(0, 2, 1)), _block_diag(bbi.transpose(0, 2, 1))], axis=1)
    wc = jnp.concatenate([_block_diag(C_re.transpose(0, 2, 1)), -_block_diag(C_im.transpose(0, 2, 1))], axis=0)
    return (wb.astype(BF16), ar.reshape(1, -1), ai.reshape(1, -1), wc.astype(BF16), D.reshape(1, -1).astype(F32))


def _pad_rows_front(a, rows):
    return jnp.pad(a, ((0, 0), (rows - a.shape[1], 0), (0, 0)))


def _trunk(x, st, W, *, decode):
    B, L, D = x.shape
    T = B * L
    tm = min(512, T)
    out = {}

    x2 = x.reshape(T, D)
    z = _norm_proj_wide(x2, W['norm_mix'][0], W['ev_w_in'], tm=tm, chunk=EV_IN_TILE, name="ev_in_proj")
    zc = z.shape[1]
    if decode:
        lp = SUBLANES
        z3 = jnp.pad(z.reshape(B, L, zc), ((0, 0), (0, lp - L), (0, 0)))
        tt = lp
    else:
        lp = L
        z3 = z.reshape(B, L, zc)
        tt = min(512, L)
    conv0 = _pad_rows_front(st['gdn_conv'], SUBLANES)
    mixed, tail, sg, sh_t = _even_mixer(z3, conv0, st['gdn'], jnp.swapaxes(st['hgrn'], -1, -2), W['gdn_conv_w'],
                                        W['gdn_neg_a'], W['gdn_dt_bias'], W['gdn_norm'], W['hgrn_norm'], W['hgrn_lb'],
                                        l_real=L, tt=tt)
    out['gdn_conv'] = tail[:, SUBLANES - (GDN_CONV - 1):, :]
    out['gdn'] = sg
    out['hgrn'] = jnp.swapaxes(sh_t, -1, -2)
    if decode:
        parts = [(mixed[:, :L].reshape(T, 2 * D_A), W['ev_w_out'], pl.BlockSpec((tm, 2 * D_A), lambda i: (i, 0)))]
    else:
        parts = [(mixed, W['ev_w_out'], pl.BlockSpec((None, tm, 2 * D_A), lambda b, i: (b, i, 0)))]
    x2, fs = _post_mixer(x2, parts, st, W, 0, B, L, decode)
    out['ffn0'] = fs

    dc = H_C * HD_C
    gqk = jnp.concatenate([jnp.tile(W['moba_gq'], H_C), jnp.tile(W['moba_gk'], H_C)])
    u_tm = _norm_proj(x2, W['norm_mix'][1], W['od_w_in'][:, 3 * dc:], tm=tm, tn=D_S5,
                      time_major=None if decode else (B, L), name="od_in_proj_u")
    u_tm = u_tm.reshape(L, B * D_S5)
    if decode:
        (qkv,) = _od_qkv_proj(x2, W['norm_mix'][1], W['od_w_in'][:, :3 * dc], gqk, tm=tm)
        out['moba_k'] = qkv[:, dc:2 * dc].reshape(B, L, H_C, HD_C)
        out['moba_v'] = qkv[:, 2 * dc:].reshape(B, L, H_C, HD_C)
        heads = lambda a: a.reshape(B, H_C, HD_C)
        o_c = _moba_step(heads(qkv[:, :dc]), heads(qkv[:, dc:2 * dc]), heads(qkv[:, 2 * dc:]),
                         st['cache_k'], st['cache_v'], st['page_table'], *W['moba_bias_step'])
        o_c = o_c.reshape(T, dc)
        tts = 1
    else:
        qkv, kt, vt = _od_qkv_proj(x2, W['norm_mix'][1], W['od_w_in'][:, :3 * dc], gqk, tm=tm, seq=(B, L))
        out['moba_k'] = jnp.transpose(kt.reshape(B, H_C, HD_C, L), (0, 3, 1, 2))
        out['moba_v'] = jnp.transpose(vt.reshape(B, H_C, HD_C, L), (0, 3, 1, 2))
        o_c = _moba_seq(qkv.reshape(B, L, 3 * dc), vt, W['moba_bias_tiles']).reshape(T, dc)
        tts = min(64, L)
    o_d, hr, hi = _s5(u_tm, st['s5_re'], st['s5_im'], *W['s5'], W['s5_glu_w'], tt=tts, nb=B)
    out['s5_re'] = hr.reshape(B, S5_GROUPS, S5_STATE)
    out['s5_im'] = hi.reshape(B, S5_GROUPS, S5_STATE)
    if decode:
        parts = [(o_c, W['od_w_out'][:dc], pl.BlockSpec((tm, dc), lambda i: (i, 0))),
                 (o_d, W['od_w_out'][dc:], pl.BlockSpec((tm, D_S5), lambda i: (i, 0)))]
    else:
        parts = [(o_c.reshape(B, L, dc), W['od_w_out'][:dc], pl.BlockSpec((None, tm, dc), lambda b, i: (b, i, 0))),
                 (o_d, W['od_w_out'][dc:], pl.BlockSpec((tm, D_S5), lambda b, i: (i, b)))]
    x2, fs = _post_mixer(x2, parts, st, W, 1, B, L, decode)
    out['ffn1'] = fs
    out['y'] = x2.reshape(B, L, D)
    return out


def _post_mixer(x2, parts, st, W, l, B, L, decode):
    T, D = x2.shape
    dq = MEM_H * MEM_HD
    buf = st['ffn_conv'][l]
    if decode:
        x2 = _proj_res(x2, parts, tm=T, name="mixer_out_proj")
        q = _norm_proj(x2, W['norm_mem'][l], W['mem_wq'][l], tm=T, tn=dq, n_norm_tiles=1, group=MEM_HD,
                       gain=jnp.tile(W['mem_gq'][l], MEM_H), name="mem_q_proj")
        o = _mem_attn_step(q.reshape(B, 1, dq), st['mem_k'], st['mem_v'], l).reshape(T, dq)
        x3 = _proj_res(x2, [(o, W['mem_wo'][l], pl.BlockSpec((T, dq), lambda i: (i, 0)))], tm=T, name="mem_out_proj")
        y, gate = _ffn_step(x3, buf[:, 0], buf[:, 1], W['norm_ffn'][l], W['ffn_w_gu'][l],
                            W['ffn_conv_w'][l], W['ffn_conv_b'][l], W['ffn_w_down'][l])
        fs = jnp.stack([buf[:, 1], gate], axis=1)
        return y, fs
    y, tail = _post_seq(x2.reshape(B, L, D), parts, st['mem_k'][l], st['mem_v'][l], W['norm_mem'][l], W['mem_wq'][l],
                        W['mem_gq'][l], W['mem_wo'][l], _pad_rows_front(buf, SUBLANES), W['norm_ffn'][l],
                        W['ffn_w_gu'][l], W['ffn_conv_w'][l], W['ffn_conv_b'][l], W['ffn_w_down'][l],
                        tm=min(512, L), name="post_mixer_seq")
    return y.reshape(T, D), tail[:, SUBLANES - (FFN_CONV - 1):, :]


def _mem_kv(mem, g_norm, w_kv, g_k):
    B, M, D = mem.shape
    dq = MEM_H * MEM_HD
    gain = jnp.concatenate([jnp.tile(g_k, MEM_H), jnp.ones((dq,), F32)])
    kv = _norm_proj(mem.reshape(B * M, D), g_norm, w_kv, tm=min(512, B * M), tn=dq, n_norm_tiles=1, group=MEM_HD,
                    gain=gain, name="mem_kv_proj")
    return kv[:, :dq].reshape(B, M, dq), kv[:, dq:].reshape(B, M, dq)


def kernel(x_prompt, x_sample, state_gdn_conv, state_gdn, state_hgrn, cache_moba_k, cache_moba_v, state_s5_re, state_s5_im, cache_mem_k, cache_mem_v, state_ffn_conv, page_table, mem_prompt, rel_bias, hgrn_lb_raw, norm_mix, norm_mem, norm_memkv, norm_ffn, mem_wq, mem_wkv, mem_gq, mem_gk, mem_wo, ffn_w_gu, ffn_conv_w, ffn_conv_b, ffn_w_down, ev_w_in, ev_w_out, gdn_conv_w, gdn_A_log, gdn_dt_bias, gdn_norm, hgrn_norm, od_w_in, od_w_out, moba_gq, moba_gk, s5_A_re, s5_A_im, s5_log_dt, s5_B_re, s5_B_im, s5_C_re, s5_C_im, s5_D, s5_glu_w):
    Bp, Lq, D = x_prompt.shape
    Bs = x_sample.shape[0]
    dc = H_C * HD_C
    dq = MEM_H * MEM_HD
    n_pages = page_table.shape[1]
    past = n_pages * PAGE_SIZE

    d3 = 3 * D_A
    w_in = ev_w_in[0]
    ba_cols = w_in[:, 4 * D_A:4 * D_A + 2 * H_A]
    w_in = jnp.concatenate([w_in[:, :4 * D_A], w_in[:, 4 * D_A + 2 * H_A:], ba_cols,
                            jnp.zeros((D, LANES - 2 * H_A), F32)], axis=1)
    lb_all = jnp.cumsum(jax.nn.softmax(hgrn_lb_raw.astype(F32), axis=0), axis=0)
    bias_t = rel_bias.T.astype(F32)
    nblk = MOBA_BLOCK
    by_rel = bias_t[:, _rel_bucket(jnp.arange(2 * nblk, dtype=jnp.int32))]
    mm = jnp.arange(2 * nblk, dtype=jnp.int32)

    def toeplitz(shift):
        y = by_rel[:, (shift + mm) % (2 * nblk)]
        flat = jnp.tile(y, (1, nblk))[:, :nblk * (2 * nblk - 1)]
        return flat.reshape(H_C, nblk, 2 * nblk - 1)[:, :, :nblk]

    far = bias_t[:, _rel_bucket(jnp.int32(2 * nblk))]
    tiles = jnp.stack([toeplitz(0), toeplitz(nblk)], axis=1) - far[:, None, None, None]
    bias_step = (by_rel[:, nblk - jnp.arange(nblk, dtype=jnp.int32)][:, None, :],
                 jnp.stack([far, bias_t[:, 0]], axis=1))
    W = dict(
        norm_mix=norm_mix, norm_mem=norm_mem, norm_ffn=norm_ffn,
        ev_w_in=w_in.astype(BF16), ev_w_out=ev_w_out[0].astype(BF16),
        gdn_conv_w=gdn_conv_w[0], gdn_neg_a=-jnp.exp(gdn_A_log[0].astype(F32)), gdn_dt_bias=gdn_dt_bias[0],
        gdn_norm=gdn_norm[0], hgrn_norm=hgrn_norm[0], hgrn_lb=lb_all[0],
        od_w_in=od_w_in[0].astype(BF16), od_w_out=od_w_out[0].astype(BF16),
        moba_gq=moba_gq[0], moba_gk=moba_gk[0], moba_bias_tiles=tiles, moba_bias_step=bias_step,
        s5=_s5_weights(s5_A_re[0], s5_A_im[0], s5_log_dt[0], s5_B_re[0], s5_B_im[0], s5_C_re[0], s5_C_im[0], s5_D[0]),
        s5_glu_w=s5_glu_w[0].astype(BF16),
        mem_wq=mem_wq.astype(BF16), mem_gq=mem_gq, mem_wo=mem_wo.astype(BF16),
        ffn_w_gu=ffn_w_gu.astype(BF16), ffn_conv_w=ffn_conv_w, ffn_conv_b=ffn_conv_b, ffn_w_down=ffn_w_down.astype(BF16),
    )

    mk_l, mv_l = [], []
    for l in range(2):
        mk, mv = _mem_kv(mem_prompt, norm_memkv[l], mem_wkv[l].astype(BF16), mem_gk[l])
        mk_l.append(mk)
        mv_l.append(mv)
    p_mem_k, p_mem_v = jnp.stack(mk_l), jnp.stack(mv_l)
    zeros = lambda *s: jnp.zeros(s, F32)
    st_p = dict(gdn_conv=zeros(Bp, GDN_CONV - 1, d3), gdn=zeros(Bp, H_A, HD_A, HD_A), hgrn=zeros(Bp, H_B, DK_B, DV_B),
                s5_re=zeros(Bp, D_S5_STATE), s5_im=zeros(Bp, D_S5_STATE), mem_k=p_mem_k, mem_v=p_mem_v,
                ffn_conv=zeros(2, Bp, FFN_CONV - 1, D_FF))
    op = _trunk(x_prompt, st_p, W, decode=False)

    st_s = dict(gdn_conv=state_gdn_conv[0], gdn=state_gdn[0], hgrn=state_hgrn[0],
                s5_re=state_s5_re[0].reshape(Bs, D_S5_STATE), s5_im=state_s5_im[0].reshape(Bs, D_S5_STATE),
                mem_k=cache_mem_k, mem_v=cache_mem_v,
                ffn_conv=state_ffn_conv,
                cache_k=jnp.transpose(cache_moba_k[0], (0, 2, 3, 1)), cache_v=jnp.transpose(cache_moba_v[0], (0, 2, 3, 1)),
                page_table=page_table)
    os_ = _trunk(x_sample, st_s, W, decode=True)

    M = mem_prompt.shape[1]
    return (op['y'], os_['y'],
            op['gdn_conv'][None], op['gdn'][None], op['hgrn'][None], op['moba_k'][None], op['moba_v'][None],
            op['s5_re'][None], op['s5_im'][None],
            p_mem_k.reshape(2, Bp, M, MEM_H, MEM_HD), p_mem_v.reshape(2, Bp, M, MEM_H, MEM_HD),
            jnp.stack([op['ffn0'], op['ffn1']]),
            os_['gdn_conv'][None], os_['gdn'][None], os_['hgrn'][None], os_['moba_k'][None], os_['moba_v'][None],
            os_['s5_re'][None], os_['s5_im'][None], jnp.stack([os_['ffn0'], os_['ffn1']]))
```

```python
import functools
import math

import jax
import jax.numpy as jnp
from jax import lax
from jax.experimental import pallas as pl
from jax.experimental.pallas import tpu as pltpu

F32 = jnp.float32
BF16 = jnp.bfloat16
HI = lax.Precision.HIGHEST
EPS = 1e-6

LANES = 128
SUBLANES = 8
VMEM_LIMIT_BYTES = 56 * 1024 * 1024

D_MODEL = 1024
H_A, HD_A, GDN_CONV = 8, 64, 4
H_B, DK_B, DV_B = 8, 64, 64
GDN_CHUNK, HGRN_CHUNK = 64, 32
GDN_GROUP, HGRN_GROUP = 2, 4
H_C, HD_C = 8, 64
MOBA_BLOCK, MOBA_TOPK = 256, 3
N_BUCKETS, MAX_DISTANCE = 32, 128
S5_GROUPS, S5_GROUP_CH, S5_STATE = 32, 16, 64
MEM_H, MEM_HD = 4, 128
D_FF, FFN_CONV = 2816, 3
PAGE_SIZE = 128
D_A = H_A * HD_A
D_S5 = S5_GROUPS * S5_GROUP_CH
D_S5_STATE = S5_GROUPS * S5_STATE
S5_MM_SPLIT = 4
FF_CHUNK = D_FF // 2
EV_IN_COLS = 8 * D_A + LANES
EV_IN_TILE = EV_IN_COLS // 3


def _cparams(*sem):
    return pltpu.CompilerParams(dimension_semantics=sem, vmem_limit_bytes=VMEM_LIMIT_BYTES)


def _dot(a, b, precision=None):
    return jnp.dot(a, b, preferred_element_type=F32, precision=precision)


def _dot_nt(a, b, precision=None):
    return lax.dot_general(a, b, (((1,), (1,)), ((), ())), preferred_element_type=F32, precision=precision)


def _dot_tn(a, b, precision=None):
    return lax.dot_general(a, b, (((0,), (0,)), ((), ())), preferred_element_type=F32, precision=precision)


def _rms(x, g):
    return x * lax.rsqrt(jnp.mean(x * x, axis=-1, keepdims=True) + EPS) * g


def _silu(x):
    return x * jax.nn.sigmoid(x)


def _tri(n, strict=False):
    r = lax.broadcasted_iota(jnp.int32, (n, n), 0)
    c = lax.broadcasted_iota(jnp.int32, (n, n), 1)
    return (r > c) if strict else (r >= c)


def _norm_proj_kernel(x_ref, g_ref, w_ref, e_ref, gain_ref, o_ref, *, n_norm_tiles, group):
    xn = _rms(x_ref[...], g_ref[...])
    z = _dot(xn.astype(BF16), w_ref[...])
    if n_norm_tiles == 0:
        o_ref[...] = z
        return
    j = pl.program_id(1)

    @pl.when(j < n_norm_tiles)
    def _():
        zz = z * z
        hi = zz.astype(BF16)
        lo = (zz - hi.astype(F32)).astype(BF16)
        s = _dot(hi, e_ref[...]) + _dot(lo, e_ref[...])
        o_ref[...] = z * lax.rsqrt(s * (1.0 / group) + EPS) * gain_ref[...]

    @pl.when(j >= n_norm_tiles)
    def _():
        o_ref[...] = z


def _norm_proj(x, g, w, *, tm, tn, n_norm_tiles=0, group=1, gain=None, time_major=None, name):
    T, D = x.shape
    N = w.shape[1]
    nj = N // tn
    if gain is None:
        gain = jnp.ones((N,), F32)
    gi = lax.broadcasted_iota(jnp.int32, (tn, tn), 0) // group
    gj = lax.broadcasted_iota(jnp.int32, (tn, tn), 1) // group
    e = (gi == gj).astype(BF16)
    if time_major is None:
        out_shape = jax.ShapeDtypeStruct((T, N), F32)
        out_spec = pl.BlockSpec((tm, tn), lambda i, j: (i, j))
    else:
        B, L = time_major
        nt = L // tm
        out_shape = jax.ShapeDtypeStruct((L, B * N), F32)
        out_spec = pl.BlockSpec((tm, tn), lambda i, j: (i % nt, (i // nt) * nj + j))
    return pl.pallas_call(
        functools.partial(_norm_proj_kernel, n_norm_tiles=n_norm_tiles, group=group),
        grid=(T // tm, nj),
        in_specs=[
            pl.BlockSpec((tm, D), lambda i, j: (i, 0)),
            pl.BlockSpec((1, D), lambda i, j: (0, 0)),
            pl.BlockSpec((D, tn), lambda i, j: (0, j)),
            pl.BlockSpec((tn, tn), lambda i, j: (0, 0)),
            pl.BlockSpec((1, tn), lambda i, j: (0, j)),
        ],
        out_specs=out_spec,
        out_shape=out_shape,
        compiler_params=_cparams("parallel", "arbitrary"),
        name=name,
    )(x, g.reshape(1, D), w, e, gain.reshape(1, N))


def _norm_proj_wide_kernel(x_ref, g_ref, w_ref, o_ref, *, chunk):
    xn = _rms(x_ref[...], g_ref[...]).astype(BF16)
    for c in range(w_ref.shape[1] // chunk):
        cs = slice(c * chunk, (c + 1) * chunk)
        o_ref[:, cs] = _dot(xn, w_ref[:, cs])


def _norm_proj_wide(x, g, w, *, tm, chunk, name):
    T, D = x.shape
    N = w.shape[1]
    const = lambda shape: pl.BlockSpec(shape, lambda i: (0, 0), pipeline_mode=pl.Buffered(1))
    return pl.pallas_call(
        functools.partial(_norm_proj_wide_kernel, chunk=chunk),
        grid=(T // tm,),
        in_specs=[pl.BlockSpec((tm, D), lambda i: (i, 0)), const((1, D)), const((D, N))],
        out_specs=pl.BlockSpec((tm, N), lambda i: (i, 0)),
        out_shape=jax.ShapeDtypeStruct((T, N), F32),
        compiler_params=_cparams("parallel"),
        name=name,
    )(x, g.reshape(1, D), w)


def _od_qkv_kernel(x_ref, g_ref, w_ref, e_ref, gain_ref, o_ref, u_ref, *t_refs):
    dc = H_C * HD_C
    xn = _rms(x_ref[...], g_ref[...]).astype(BF16)
    u_ref[...] = _dot(xn, w_ref[:, 3 * dc:3 * dc + D_S5])
    for c in range(3):
        cs = slice(c * dc, (c + 1) * dc)
        z = _dot(xn, w_ref[:, cs])
        if c < 2:
            zz = z * z
            hi = zz.astype(BF16)
            lo = (zz - hi.astype(F32)).astype(BF16)
            s = _dot(hi, e_ref[...]) + _dot(lo, e_ref[...])
            z = z * lax.rsqrt(s * (1.0 / HD_C) + EPS) * gain_ref[:, cs]
        o_ref[:, cs] = z
        if t_refs and c > 0:
            t_refs[c - 1][...] = z.T


def _od_qkv_proj(x, g, w, gain, *, tm, seq=None):
    T, D = x.shape
    dc = H_C * HD_C
    gi = lax.broadcasted_iota(jnp.int32, (dc, dc), 0) // HD_C
    gj = lax.broadcasted_iota(jnp.int32, (dc, dc), 1) // HD_C
    e = (gi == gj).astype(BF16)
    const = lambda shape: pl.BlockSpec(shape, lambda i: (0, 0), pipeline_mode=pl.Buffered(1))
    out_specs = [pl.BlockSpec((tm, 3 * dc), lambda i: (i, 0))]
    out_shape = [jax.ShapeDtypeStruct((T, 3 * dc), F32)]
    if seq is None:
        out_specs += [pl.BlockSpec((tm, D_S5), lambda i: (i, 0))]
        out_shape += [jax.ShapeDtypeStruct((T, D_S5), F32)]
    else:
        B, L = seq
        nt = L // tm
        tspec = pl.BlockSpec((None, dc, tm), lambda i: (i // nt, 0, i % nt))
        out_specs += [pl.BlockSpec((tm, D_S5), lambda i: (i % nt, i // nt)), tspec, tspec]
        out_shape += [jax.ShapeDtypeStruct((L, B * D_S5), F32)] + [jax.ShapeDtypeStruct((B, dc, L), F32)] * 2
    return pl.pallas_call(
        _od_qkv_kernel,
        grid=(T // tm,),
        in_specs=[pl.BlockSpec((tm, D), lambda i: (i, 0)), const((1, D)), const((D, 3 * dc + D_S5)), const((dc, dc)),
                  const((1, 2 * dc))],
        out_specs=out_specs,
        out_shape=out_shape,
        compiler_params=_cparams("parallel"),
        name="od_in_proj_qkv",
    )(x, g.reshape(1, D), w, e, gain.reshape(1, 2 * dc))


def _proj_res_kernel(*refs, n_in):
    x_ref, o_ref = refs[0], refs[-1]
    acc = x_ref[...]
    for k in range(n_in):
        a_ref, w_ref = refs[1 + 2 * k], refs[2 + 2 * k]
        acc = acc + _dot(a_ref[...].astype(BF16), w_ref[...])
    o_ref[...] = acc


def _proj_res(x, parts, *, tm, name):
    T, N = x.shape
    in_specs = [pl.BlockSpec((tm, N), lambda i: (i, 0))]
    args = [x]
    for a, w, spec in parts:
        in_specs += [spec, pl.BlockSpec(w.shape, lambda i: (0, 0))]
        args += [a, w]
    return pl.pallas_call(
        functools.partial(_proj_res_kernel, n_in=len(parts)),
        grid=(T // tm,),
        in_specs=in_specs,
        out_specs=pl.BlockSpec((tm, N), lambda i: (i, 0)),
        out_shape=jax.ShapeDtypeStruct((T, N), F32),
        compiler_params=_cparams("parallel"),
        name=name,
    )(*args)


def _neumann_inverse(mats, n):
    m = mats[0].shape[0]
    eye = (lax.broadcasted_iota(jnp.int32, (m, m), 0) == lax.broadcasted_iota(jnp.int32, (m, m), 1)).astype(F32)
    ts = [eye - a for a in mats]
    ps = [a.astype(BF16) for a in mats]
    k = 2
    while k < n:
        ps = [_dot(p, p).astype(BF16) for p in ps]
        ts = [t + _dot(t.astype(BF16), p) for t, p in zip(ts, ps)]
        k *= 2
    return ts


def _even_mixer_kernel(qkv_ref, zg_ref, hq_ref, hf_ref, hi_ref, hg_ref, ba_ref,
                       conv0_ref, sg0_ref, sh0_ref,
                       convw_ref, nega_ref, dtb_ref, gnorm_ref, hnorm_ref, lb_ref,
                       mixed_ref, tail_ref, sg_ref, sh_ref,
                       xp_ref, beta_ref, g_ref, sbd_ref,
                       *, tt, cg, ch, l_real):
    ti = pl.program_id(1)
    n_tiles = pl.num_programs(1)

    @pl.when(ti == 0)
    def _():
        xp_ref[0:SUBLANES, :] = conv0_ref[...]
        sh_ref[...] = sh0_ref[...]
        zero = jnp.zeros((HD_A, HD_A), F32)
        for p in range(H_A // 2):
            sbd_ref[p] = jnp.concatenate([jnp.concatenate([sg0_ref[2 * p], zero], axis=-1),
                                          jnp.concatenate([zero, sg0_ref[2 * p + 1]], axis=-1)], axis=0)

    xp_ref[SUBLANES:SUBLANES + tt, :] = qkv_ref[...]
    lr_last = l_real - (l_real - 1) // tt * tt
    tail_ref[...] = xp_ref[lr_last:lr_last + SUBLANES, :]

    padded = (l_real % tt) != 0
    if padded:
        row = lax.broadcasted_iota(jnp.int32, (tt, 1), 0) + ti * tt
        valid = row < l_real
    ba = ba_ref[...]
    beta_all = jax.nn.sigmoid(ba)
    g_all = nega_ref[...] * jax.nn.softplus(ba + dtb_ref[...])
    if padded:
        beta_all = jnp.where(valid, beta_all, 0.0)
        g_all = jnp.where(valid, g_all, 0.0)
    beta_ref[...] = beta_all
    g_ref[...] = g_all

    ltri_g = _tri(cg).astype(F32)
    hnorm = hnorm_ref[...]
    gnorm2 = jnp.concatenate([gnorm_ref[...], gnorm_ref[...]], axis=-1)
    pairs = range(H_A // 2)
    n2 = 2 * cg
    lane = lax.broadcasted_iota(jnp.int32, (1, LANES), 1)
    m0 = (lane < HD_A).astype(F32)
    m1 = 1.0 - m0
    ri = lax.broadcasted_iota(jnp.int32, (n2, n2), 0)
    cj = lax.broadcasted_iota(jnp.int32, (n2, n2), 1)
    same_head = (ri // cg) == (cj // cg)
    bd_tril = same_head & (ri >= cj)
    bd_stril = same_head & (ri > cj)

    def stack2(a):
        return jnp.concatenate([a * m0, a * m1], axis=0)

    def rep2(col0, col1):
        return jnp.concatenate([jnp.broadcast_to(col0, (cg, LANES)), jnp.broadcast_to(col1, (cg, LANES))], axis=0)

    ng = min(GDN_GROUP, tt // cg)

    def gdn_group(gi, _):
        items = [(c, p) for c in range(ng) for p in pairs]
        rows_c, gcum_c, gt_c, bet_c, qkv_c = [], [], [], [], []
        for c in range(ng):
            r0 = pl.multiple_of((gi * ng + c) * cg, cg)
            rows = pl.ds(r0, cg)
            win = xp_ref[pl.ds(r0, cg + SUBLANES), :]
            conv = convw_ref[GDN_CONV - 1:GDN_CONV, :] * win[SUBLANES:SUBLANES + cg]
            for j in range(GDN_CONV - 1):
                off = SUBLANES - (GDN_CONV - 1) + j
                conv = conv + convw_ref[j:j + 1, :] * win[off:off + cg]
            qkv_c.append(_silu(conv))
            gcum = _dot(ltri_g, g_ref[rows, :], HI)
            gpad = gcum if cg == LANES else jnp.concatenate([gcum, jnp.zeros((LANES - cg, LANES), F32)], axis=0)
            rows_c.append(rows)
            gcum_c.append(gcum)
            gt_c.append(gpad.T)
            bet_c.append(beta_ref[rows, :])
        qs, ks, vs, betas, gcs, decays, glasts, glrows = {}, {}, {}, {}, {}, {}, {}, {}
        for it in items:
            c, p = it
            rows = rows_c[c]
            h0, h1 = H_A + 2 * p, H_A + 2 * p + 1
            q = stack2(qkv_c[c][:, p * LANES:(p + 1) * LANES])
            k = stack2(qkv_c[c][:, D_A + p * LANES:D_A + (p + 1) * LANES])
            qs[it] = q * lax.rsqrt(jnp.sum(q * q, axis=-1, keepdims=True) + EPS) * (HD_A ** -0.5)
            ks[it] = k * lax.rsqrt(jnp.sum(k * k, axis=-1, keepdims=True) + EPS)
            vs[it] = stack2(qkv_c[c][:, 2 * D_A + p * LANES:2 * D_A + (p + 1) * LANES])
            betas[it] = rep2(bet_c[c][:, 2 * p:2 * p + 1], bet_c[c][:, 2 * p + 1:2 * p + 2])
            gc = rep2(gcum_c[c][:, h0:h0 + 1], gcum_c[c][:, h1:h1 + 1])
            gr = jnp.concatenate([gt_c[c][h0:h0 + 1, 0:cg], gt_c[c][h1:h1 + 1, 0:cg]], axis=-1)
            gcs[it] = gc
            glasts[it] = rep2(gcum_c[c][cg - 1:cg, h0:h0 + 1], gcum_c[c][cg - 1:cg, h1:h1 + 1])
            glrows[it] = jnp.exp(glasts[it][0:1, :]) * m0 + jnp.exp(glasts[it][cg:cg + 1, :]) * m1
            decays[it] = jnp.exp(jnp.where(bd_tril, gc[:, 0:n2] - gr, -jnp.inf))
        kqs = {it: _dot_nt(jnp.concatenate([ks[it], qs[it]], axis=0).astype(BF16), ks[it].astype(BF16))
               for it in items}
        ts = dict(zip(items, _neumann_inverse(
            [jnp.where(bd_stril, betas[it][:, 0:n2] * kqs[it][0:n2] * decays[it], 0.0) for it in items], cg)))
        egs = {it: jnp.exp(gcs[it]) for it in items}
        sols = {it: _dot(ts[it].astype(BF16),
                         jnp.concatenate([betas[it] * vs[it], (betas[it] * egs[it]) * ks[it]], axis=-1).astype(BF16))
                for it in items}
        wqs = {it: jnp.concatenate([sols[it][:, LANES:2 * LANES], qs[it] * egs[it]], axis=0).astype(BF16)
               for it in items}
        qks = {it: (kqs[it][n2:2 * n2] * decays[it]).astype(BF16) for it in items}
        kouts = {it: (ks[it] * jnp.exp(glasts[it] - gcs[it])).astype(BF16) for it in items}
        ss = [sbd_ref[p] for p in pairs]
        for c in range(ng):
            wss = [_dot(wqs[(c, p)], ss[p].astype(BF16)) for p in pairs]
            ubs = [(sols[(c, p)][:, 0:LANES] - wss[p][0:n2]).astype(BF16) for p in pairs]
            os_ = [wss[p][n2:2 * n2] + _dot(qks[(c, p)], ubs[p]) for p in pairs]
            ss = [ss[p] * glrows[(c, p)] + _dot_tn(kouts[(c, p)], ubs[p]) for p in pairs]
            for p in pairs:
                o = os_[p]
                on = o * lax.rsqrt(jnp.sum(o * o, axis=-1, keepdims=True) * (1.0 / HD_A) + EPS)
                cols = slice(p * LANES, (p + 1) * LANES)
                mixed_ref[rows_c[c], cols] = (on[0:cg] + on[cg:n2]) * gnorm2 * _silu(zg_ref[rows_c[c], cols])
        for p in pairs:
            sbd_ref[p] = ss[p]
        return 0

    lax.fori_loop(0, tt // (cg * ng), gdn_group, 0)
    xp_ref[0:SUBLANES, :] = xp_ref[tt:tt + SUBLANES, :]
    for p in pairs:
        sg_ref[2 * p] = sbd_ref[p, 0:HD_A, 0:HD_A]
        sg_ref[2 * p + 1] = sbd_ref[p, HD_A:2 * HD_A, HD_A:2 * HD_A]

    nh = min(HGRN_GROUP, tt // ch)
    gr = nh * ch
    ri = lax.broadcasted_iota(jnp.int32, (gr, gr), 0)
    cj = lax.broadcasted_iota(jnp.int32, (gr, gr), 1)
    same_chunk_causal = (ri >= cj) & ((ri // ch) == (cj // ch))
    ltri_h = same_chunk_causal.astype(F32)
    lb = lb_ref[...]
    r_mid = (ch - 1) // 2

    def hgrn_group(gi, _):
        r0 = pl.multiple_of(gi * gr, gr)
        rows = pl.ds(r0, gr)
        f = lb + (1.0 - lb) * jax.nn.sigmoid(hf_ref[rows, :])
        logf = jnp.log(f)
        kk = 1.0 - f
        if padded:
            vrow = (lax.broadcasted_iota(jnp.int32, (gr, 1), 0) + r0 + ti * tt) < l_real
            logf = jnp.where(vrow, logf, 0.0)
            kk = jnp.where(vrow, kk, 0.0)
        qh = _silu(hq_ref[rows, :])
        bc = _dot(ltri_h, logf, HI).reshape(nh, ch, D_A)
        bref = bc[:, r_mid:r_mid + 1, :]
        blast = bc[:, ch - 1:ch, :]
        grp = lambda a: a.reshape(gr, D_A)
        qh3, kk3 = qh.reshape(nh, ch, D_A), kk.reshape(nh, ch, D_A)
        qa = grp(qh3 * jnp.exp(bc - bref)).astype(BF16)
        kb = grp(kk3 * jnp.exp(bref - bc)).astype(BF16)
        q_in = grp(qh3 * jnp.exp(bc)).astype(BF16)
        k_out = grp(kk3 * jnp.exp(blast - bc)).astype(BF16)
        fl = jnp.exp(blast)
        vv = hi_ref[rows, :].astype(BF16)
        heads = range(H_B)
        sls = [slice(h * DK_B, (h + 1) * DK_B) for h in heads]
        atts = [jnp.where(same_chunk_causal, _dot_nt(qa[:, sl], kb[:, sl]), 0.0).astype(BF16) for sl in sls]
        intra = [_dot(atts[h], vv[:, sls[h]]) for h in heads]
        sts = [sh_ref[h] for h in heads]
        inter = [[] for _ in heads]
        for c in range(nh):
            rc = slice(c * ch, (c + 1) * ch)
            for h in heads:
                inter[h].append(_dot_nt(q_in[rc, sls[h]], sts[h].astype(BF16)))
            sts = [sts[h] * fl[c][:, sls[h]] + _dot_tn(vv[rc, sls[h]], k_out[rc, sls[h]]) for h in heads]
        for h in heads:
            sh_ref[h] = sts[h]
        os_ = [intra[h] + (inter[h][0] if nh == 1 else jnp.concatenate(inter[h], axis=0)) for h in heads]
        outs = [_rms(os_[h], hnorm) * _silu(hg_ref[rows, h * DV_B:(h + 1) * DV_B]) for h in heads]
        for p in range(H_B // 2):
            c0 = D_A + p * LANES
            mixed_ref[rows, c0:c0 + LANES] = jnp.concatenate([outs[2 * p], outs[2 * p + 1]], axis=-1)
        return 0

    lax.fori_loop(0, tt // gr, hgrn_group, 0)


def _even_mixer(z, conv0, sg0, sh0_t, conv_w, neg_a, dt_bias, gnorm, hnorm, lb, *, l_real, tt):
    B, Lp, _ = z.shape
    cg = min(GDN_CHUNK, tt)
    ch = min(HGRN_CHUNK, tt)
    d3 = 3 * D_A
    nt = Lp // tt
    col = lambda w, idx: pl.BlockSpec((None, tt, w), lambda b, i, idx=idx: (b, i, idx))
    state = lambda: pl.BlockSpec((None, H_A, HD_A, HD_A), lambda b, i: (b, 0, 0, 0))
    vec = lambda n: pl.BlockSpec((1, n), lambda b, i: (0, 0))
    nega_row = jnp.zeros((1, LANES), F32).at[0, H_A:2 * H_A].set(neg_a)
    dtb_row = jnp.zeros((1, LANES), F32).at[0, H_A:2 * H_A].set(dt_bias)
    return pl.pallas_call(
        functools.partial(_even_mixer_kernel, tt=tt, cg=cg, ch=ch, l_real=l_real),
        grid=(B, nt),
        in_specs=[
            col(d3, 0), col(D_A, 3), col(D_A, 4), col(D_A, 5), col(D_A, 6), col(D_A, 7), col(LANES, 32),
            pl.BlockSpec((None, SUBLANES, d3), lambda b, i: (b, 0, 0)),
            state(), state(),
            pl.BlockSpec((GDN_CONV, d3), lambda b, i: (0, 0)),
            vec(LANES), vec(LANES), vec(HD_A), vec(DV_B), vec(D_A),
        ],
        out_specs=[
            pl.BlockSpec((None, tt, 2 * D_A), lambda b, i: (b, i, 0)),
            pl.BlockSpec((None, SUBLANES, d3), lambda b, i: (b, 0, 0)),
            state(), state(),
        ],
        out_shape=[
            jax.ShapeDtypeStruct((B, Lp, 2 * D_A), F32),
            jax.ShapeDtypeStruct((B, SUBLANES, d3), F32),
            jax.ShapeDtypeStruct((B, H_A, HD_A, HD_A), F32),
            jax.ShapeDtypeStruct((B, H_B, DV_B, DK_B), F32),
        ],
        scratch_shapes=[
            pltpu.VMEM((tt + SUBLANES, d3), F32),
            pltpu.VMEM((tt, LANES), F32),
            pltpu.VMEM((tt, LANES), F32),
            pltpu.VMEM((H_A // 2, 2 * HD_A, 2 * HD_A), F32),
        ],
        compiler_params=_cparams("parallel", "arbitrary"),
        name="even_mixer",
    )(z, z, z, z, z, z, z, conv0, sg0, sh0_t, conv_w, nega_row, dtb_row,
      gnorm.reshape(1, HD_A), hnorm.reshape(1, DV_B), lb.reshape(1, D_A))


def _mem_attn_rows(x, mk_ref, mv_ref, g_ref, wq_ref, gq_ref, wo_ref):
    q = _dot(_rms(x, g_ref[...]).astype(BF16), wq_ref[...])
    scale = MEM_HD ** -0.5
    outs = []
    for h in range(MEM_H):
        sl = slice(h * MEM_HD, (h + 1) * MEM_HD)
        qh = _rms(q[:, sl], gq_ref[...])
        s = _dot_nt(qh.astype(BF16), mk_ref[:, sl].astype(BF16)) * scale
        s = s - jnp.max(s, axis=-1, keepdims=True)
        p = jnp.exp(s)
        p = p / jnp.sum(p, axis=-1, keepdims=True)
        outs.append(_dot(p.astype(BF16), mv_ref[:, sl].astype(BF16)))
    o = jnp.concatenate(outs, axis=-1)
    return x + _dot(o.astype(BF16), wo_ref[...])


def _mem_attn_step_kernel(q_ref, mk_ref, mv_ref, o_ref):
    scale = MEM_HD ** -0.5
    outs = []
    for h in range(MEM_H):
        qh = jnp.broadcast_to(q_ref[:, h * MEM_HD:(h + 1) * MEM_HD], (SUBLANES, MEM_HD))
        s = _dot_nt(qh.astype(BF16), mk_ref[:, h, :].astype(BF16)) * scale
        s = s - jnp.max(s, axis=-1, keepdims=True)
        p = jnp.exp(s)
        p = p / jnp.sum(p, axis=-1, keepdims=True)
        outs.append(_dot(p.astype(BF16), mv_ref[:, h, :].astype(BF16))[0:1])
    o_ref[...] = jnp.concatenate(outs, axis=-1)


def _mem_attn_step(q, cache_k, cache_v, l):
    B = q.shape[0]
    M = cache_k.shape[2]
    dq = MEM_H * MEM_HD
    mem = pl.BlockSpec((None, None, M, MEM_H, MEM_HD), lambda b: (l, b, 0, 0, 0))
    row = pl.BlockSpec((None, 1, dq), lambda b: (b, 0, 0))
    return pl.pallas_call(
        _mem_attn_step_kernel,
        grid=(B,),
        in_specs=[row, mem, mem],
        out_specs=row,
        out_shape=jax.ShapeDtypeStruct((B, 1, dq), F32),
        compiler_params=_cparams("parallel"),
        name="mem_attn_step",
    )(q, cache_k, cache_v)


def _post_seq_kernel(*refs, n_parts, tm):
    x_ref = refs[0]
    parts = refs[1:1 + 2 * n_parts]
    (mk_ref, mv_ref, gmem_ref, wq_ref, gq_ref, wo_ref, buf0_ref, g_ref, wgu_ref, cw_ref, cb_ref, wd_ref,
     o_ref, tail_ref, gp_ref, carry_ref) = refs[1 + 2 * n_parts:]
    ti = pl.program_id(1)

    @pl.when(ti == 0)
    def _():
        carry_ref[...] = buf0_ref[...]

    x = x_ref[...]
    for k in range(n_parts):
        x = x + _dot(parts[2 * k][...].astype(BF16), parts[2 * k + 1][...])
    x = _mem_attn_rows(x, mk_ref, mv_ref, gmem_ref, wq_ref, gq_ref, wo_ref)
    xn = _rms(x, g_ref[...]).astype(BF16)
    acc = x
    for c in range(D_FF // FF_CHUNK):
        cs = slice(c * FF_CHUNK, (c + 1) * FF_CHUNK)
        gate = _dot(xn, wgu_ref[:, cs])
        up = _dot(xn, wgu_ref[:, D_FF + c * FF_CHUNK:D_FF + (c + 1) * FF_CHUNK])
        gp_ref[0:SUBLANES, :] = carry_ref[:, cs]
        gp_ref[SUBLANES:SUBLANES + tm, :] = gate
        conv = cw_ref[FFN_CONV - 1:FFN_CONV, cs] * gate + cb_ref[:, cs]
        for j in range(FFN_CONV - 1):
            off = SUBLANES - (FFN_CONV - 1) + j
            conv = conv + cw_ref[j:j + 1, cs] * gp_ref[off:off + tm, :]
        last = gp_ref[tm:tm + SUBLANES, :]
        carry_ref[:, cs] = last
        tail_ref[:, cs] = last
        act = _silu(conv) * up
        acc = acc + _dot(act.astype(BF16), wd_ref[cs, :])
    o_ref[...] = acc


def _post_seq(x, parts, mk, mv, gmem, wq, gq, wo, buf0, g, wgu, cw, cb, wd, *, tm, name):
    B, L, D = x.shape
    M = mk.shape[1]
    dq = MEM_H * MEM_HD
    const = lambda shape: pl.BlockSpec(shape, lambda b, i: (0,) * len(shape), pipeline_mode=pl.Buffered(1))
    row_tile = pl.BlockSpec((None, tm, D), lambda b, i: (b, i, 0))
    mem = pl.BlockSpec((None, M, dq), lambda b, i: (b, 0, 0))
    in_specs, args = [row_tile], [x]
    for a, w, spec in parts:
        in_specs += [spec, const(w.shape)]
        args += [a, w]
    in_specs += [mem, mem, const((1, D)), const((D, dq)), const((1, MEM_HD)), const((dq, D)),
                 pl.BlockSpec((None, SUBLANES, D_FF), lambda b, i: (b, 0, 0)),
                 const((1, D)), const((D, 2 * D_FF)), const((FFN_CONV, D_FF)), const((1, D_FF)), const((D_FF, D))]
    args += [mk, mv, gmem.reshape(1, D), wq, gq.reshape(1, MEM_HD), wo, buf0, g.reshape(1, D), wgu, cw,
             cb.reshape(1, D_FF), wd]
    return pl.pallas_call(
        functools.partial(_post_seq_kernel, n_parts=len(parts), tm=tm),
        grid=(B, L // tm),
        in_specs=in_specs,
        out_specs=[row_tile, pl.BlockSpec((None, SUBLANES, D_FF), lambda b, i: (b, 0, 0))],
        out_shape=[jax.ShapeDtypeStruct((B, L, D), F32), jax.ShapeDtypeStruct((B, SUBLANES, D_FF), F32)],
        scratch_shapes=[pltpu.VMEM((tm + SUBLANES, FF_CHUNK), F32), pltpu.VMEM((SUBLANES, D_FF), F32)],
        compiler_params=_cparams("parallel", "arbitrary"),
        name=name,
    )(*args)


def _ffn_step_kernel(x_ref, p2_ref, p1_ref, g_ref, wgu_ref, cw_ref, cb_ref, wd_ref, o_ref, gate_ref):
    x = x_ref[...]
    xn = _rms(x, g_ref[...]).astype(BF16)
    acc = x
    for c in range(D_FF // FF_CHUNK):
        cs = slice(c * FF_CHUNK, (c + 1) * FF_CHUNK)
        gate = _dot(xn, wgu_ref[:, cs])
        up = _dot(xn, wgu_ref[:, D_FF + c * FF_CHUNK:D_FF + (c + 1) * FF_CHUNK])
        gate_ref[:, cs] = gate
        conv = cw_ref[0:1, cs] * p2_ref[:, cs] + cw_ref[1:2, cs] * p1_ref[:, cs] + cw_ref[2:3, cs] * gate + cb_ref[:, cs]
        act = _silu(conv) * up
        acc = acc + _dot(act.astype(BF16), wd_ref[cs, :])
    o_ref[...] = acc


def _ffn_step(x, p2, p1, g, wgu, cw, cb, wd):
    R, D = x.shape
    full = lambda a: pl.BlockSpec(a.shape, lambda i: (0,) * a.ndim)
    args = (x, p2, p1, g.reshape(1, D), wgu, cw, cb.reshape(1, D_FF), wd)
    return pl.pallas_call(
        _ffn_step_kernel,
        grid=(1,),
        in_specs=[full(a) for a in args],
        out_specs=[pl.BlockSpec((R, D), lambda i: (0, 0)), pl.BlockSpec((R, D_FF), lambda i: (0, 0))],
        out_shape=[jax.ShapeDtypeStruct((R, D), F32), jax.ShapeDtypeStruct((R, D_FF), F32)],
        compiler_params=_cparams("arbitrary"),
        name="conv_ffn_step",
    )(*args)


def _moba_query_tile(j, hs, qt_ref, kb_ref, vt_ref, kmean_ref, bias_ref, ot_ref):
    blk = MOBA_BLOCK
    w = (j + 1) * blk
    qts = [qt_ref[h] for h in hs]
    sts = [_dot(kb_ref[h, 0:w, :], (qt * (HD_C ** -0.5)).astype(BF16)) for h, qt in zip(hs, qts)]
    sels = []
    if j > MOBA_TOPK:
        iota_n = lax.broadcasted_iota(jnp.int32, (j, blk), 0).astype(F32)
        gs = [_dot(kmean_ref[h, 0:j, :], qt, HI) for h, qt in zip(hs, qts)]
        for g in gs:
            sel = jnp.zeros((j, blk), F32)
            for _ in range(MOBA_TOPK):
                m = jnp.max(g, axis=0, keepdims=True)
                idx = jnp.min(jnp.where(g == m, iota_n, 1e9), axis=0, keepdims=True)
                hit = iota_n == idx
                sel = jnp.where(hit, 1.0, sel)
                g = jnp.where(hit, -jnp.inf, g)
            sels.append(sel)
    key_le_query = lax.broadcasted_iota(jnp.int32, (blk, blk), 0) <= lax.broadcasted_iota(jnp.int32, (blk, blk), 1)
    pcats, ls = [], []
    for i, h in enumerate(hs):
        parts = []
        for n in range(j + 1):
            s = sts[i][n * blk:(n + 1) * blk, :]
            if j - n < 2:
                s = s + bias_ref[h, j - n]
            if n == j:
                s = jnp.where(key_le_query, s, -jnp.inf)
            elif j > MOBA_TOPK:
                s = jnp.where(sels[i][n:n + 1, :] > 0.5, s, -jnp.inf)
            parts.append(s)
        m = jnp.max(parts[0], axis=0, keepdims=True)
        for s in parts[1:]:
            m = jnp.maximum(m, jnp.max(s, axis=0, keepdims=True))
        ps = [jnp.exp(s - m) for s in parts]
        l = jnp.sum(ps[0], axis=0, keepdims=True)
        for p in ps[1:]:
            l = l + jnp.sum(p, axis=0, keepdims=True)
        pcats.append((ps[0] if j == 0 else jnp.concatenate(ps, axis=0)).astype(BF16))
        ls.append(l)
    outs = [_dot(vt_ref[h, :, 0:w], pc) for h, pc in zip(hs, pcats)]
    for h, o, l in zip(hs, outs, ls):
        ot_ref[h] = o / l


def _moba_seq_kernel(q_ref, k_ref, vtin_ref, bias_ref, o_ref, kmean_ref, kb_ref, vt_ref, qt_ref, ot_ref, *, nb):
    j = pl.program_id(1)
    blk = MOBA_BLOCK
    dc = H_C * HD_C

    @pl.when(j == 0)
    def _():
        for n in range(nb):
            km = jnp.mean(k_ref[n * blk:(n + 1) * blk, :], axis=0, keepdims=True)
            for h in range(H_C):
                kmean_ref[h, n:n + 1, :] = km[:, h * HD_C:(h + 1) * HD_C]

        def prep(n, _):
            rows = pl.ds(pl.multiple_of(n * blk, blk), blk)
            kblk = k_ref[rows, :]
            vt_ref[:, :, rows] = vtin_ref[:, rows].astype(BF16).reshape(H_C, HD_C, blk)
            for h in range(H_C):
                kb_ref[h, rows, :] = kblk[:, h * HD_C:(h + 1) * HD_C].astype(BF16)
            return 0
        lax.fori_loop(0, nb, prep, 0)

    qt_ref[...] = q_ref[...].T.reshape(H_C, HD_C, blk)
    for jj in range(nb):
        @pl.when(j == jj)
        def _(jj=jj):
            def head_pair(hp, _):
                _moba_query_tile(jj, [2 * hp, 2 * hp + 1], qt_ref, kb_ref, vt_ref, kmean_ref, bias_ref, ot_ref)
                return 0
            lax.fori_loop(0, H_C // 2, head_pair, 0)
    o_ref[...] = ot_ref[...].reshape(dc, blk).T


def _moba_seq(qkv, vt, bias_tiles):
    B, L, _ = qkv.shape
    nb = L // MOBA_BLOCK
    dc = H_C * HD_C
    return pl.pallas_call(
        functools.partial(_moba_seq_kernel, nb=nb),
        grid=(B, nb),
        in_specs=[
            pl.BlockSpec((None, MOBA_BLOCK, dc), lambda b, j: (b, j, 0)),
            pl.BlockSpec((None, L, dc), lambda b, j: (b, 0, 1)),
            pl.BlockSpec((None, dc, L), lambda b, j: (b, 0, 0)),
            pl.BlockSpec(bias_tiles.shape, lambda b, j: (0, 0, 0, 0), pipeline_mode=pl.Buffered(1)),
        ],
        out_specs=pl.BlockSpec((None, MOBA_BLOCK, dc), lambda b, j: (b, j, 0)),
        out_shape=jax.ShapeDtypeStruct((B, L, dc), F32),
        scratch_shapes=[pltpu.VMEM((H_C, nb, HD_C), F32), pltpu.VMEM((H_C, L, HD_C), BF16),
                        pltpu.VMEM((H_C, HD_C, L), BF16), pltpu.VMEM((H_C, HD_C, MOBA_BLOCK), F32),
                        pltpu.VMEM((H_C, HD_C, MOBA_BLOCK), F32)],
        compiler_params=_cparams("parallel", "arbitrary"),
        name="moba_seq",
    )(qkv, qkv, vt, bias_tiles)


MOBA_PAGES_PER_BLOCK = MOBA_BLOCK // PAGE_SIZE
MOBA_STEP_PAGES = 16


def _moba_score_kernel(pt_ref, q_ref, *refs, n_pages):
    del pt_ref
    ppg, ppb = MOBA_STEP_PAGES, MOBA_PAGES_PER_BLOCK
    kp_refs = refs[:ppg]
    s_ref, sel_ref, gate_ref = refs[ppg:]
    step = pl.program_id(1)
    nbp = n_pages // ppb
    q = q_ref[...]
    inv = 1.0 / (MOBA_BLOCK * HD_C ** -0.5)
    gsum = None
    for r in range(ppg):
        sc = jnp.sum(kp_refs[r][...] * q, axis=1)
        s_ref[:, r * PAGE_SIZE:(r + 1) * PAGE_SIZE] = sc
        gs = jnp.sum(sc, axis=-1, keepdims=True)
        gsum = gs if r % ppb == 0 else gsum + gs
        if r % ppb == ppb - 1:
            gate_ref[step * (ppg // ppb) + r // ppb] = jnp.broadcast_to(gsum * inv, (H_C, LANES))

    @pl.when(step == n_pages // ppg - 1)
    def _():
        g = gate_ref[...]
        iota_b = lax.broadcasted_iota(jnp.int32, (nbp, H_C, LANES), 0).astype(F32)
        for i in range(MOBA_TOPK):
            m = jnp.max(g, axis=0, keepdims=True)
            idx = jnp.min(jnp.where(g == m, iota_b, 1e9), axis=0, keepdims=True)
            sel_ref[i] = idx[0]
            g = jnp.where(iota_b == idx, -jnp.inf, g)


def _moba_attend_kernel(pt_ref, sel_ref, q_ref, kn_ref, vn_ref, blast_ref, bsc_ref, *refs, nbp):
    del pt_ref
    nsel, ppb = MOBA_TOPK, MOBA_PAGES_PER_BLOCK
    s_refs, v_refs, o_ref = refs[:nsel], refs[nsel:nsel + nsel * ppb], refs[-1]
    b, h = pl.program_id(0), pl.program_id(1)
    far = bsc_ref[h, 0]
    s_new = jnp.sum(q_ref[...] * kn_ref[...], axis=-1, keepdims=True) * (HD_C ** -0.5) + bsc_ref[h, 1]
    ss = []
    own_row = lax.broadcasted_iota(jnp.int32, (H_C, MOBA_BLOCK), 0) == h
    for i in range(nsel):
        blk = sel_ref[(b * H_C + h) * nsel + i]
        s = jnp.sum(jnp.where(own_row, s_refs[i][...], 0.0), axis=0, keepdims=True)
        ss.append(s + jnp.where(blk == nbp - 1, blast_ref[...], far))
    s_all = jnp.concatenate(ss, axis=-1)
    m = jnp.maximum(s_new, jnp.max(s_all, axis=-1, keepdims=True))
    w_new = jnp.exp(s_new - m)
    e = jnp.exp(s_all - m)
    den = w_new + jnp.sum(e, axis=-1, keepdims=True)
    v_all = jnp.concatenate([v_refs[k][...] for k in range(nsel * ppb)], axis=-1)
    eb = jnp.broadcast_to(e, (SUBLANES, nsel * MOBA_BLOCK)).astype(BF16)
    num = w_new * vn_ref[...] + _dot_nt(eb, v_all.astype(BF16))[0:1]
    o_ref[...] = num / den


def _moba_step(q, k_new, v_new, cache_k, cache_v, page_table, bias_last, bias_scalars):
    B = q.shape[0]
    n_pages = page_table.shape[1]
    ppg, ppb, nsel = MOBA_STEP_PAGES, MOBA_PAGES_PER_BLOCK, MOBA_TOPK
    assert n_pages % ppg == 0 and ppg % ppb == 0
    nbp = n_pages // ppb
    assert nbp >= nsel
    pt = page_table.reshape(-1)
    q_rep = jnp.broadcast_to((q * (HD_C ** -0.5))[..., None], (B, H_C, HD_C, LANES))
    kpage = lambda r: pl.BlockSpec((None, H_C, HD_C, PAGE_SIZE),
                                   lambda b, s, pt, r=r: (pt[b * n_pages + s * ppg + r], 0, 0, 0))
    scores, sel = pl.pallas_call(
        functools.partial(_moba_score_kernel, n_pages=n_pages),
        grid_spec=pltpu.PrefetchScalarGridSpec(
            num_scalar_prefetch=1,
            grid=(B, n_pages // ppg),
            in_specs=[pl.BlockSpec((None, H_C, HD_C, LANES), lambda b, s, pt: (b, 0, 0, 0))]
                     + [kpage(r) for r in range(ppg)],
            out_specs=[pl.BlockSpec((None, H_C, ppg * PAGE_SIZE), lambda b, s, pt: (b, 0, s)),
                       pl.BlockSpec((None, nsel, H_C, LANES), lambda b, s, pt: (b, 0, 0, 0))],
            scratch_shapes=[pltpu.VMEM((nbp, H_C, LANES), F32)],
        ),
        out_shape=[jax.ShapeDtypeStruct((B, H_C, n_pages * PAGE_SIZE), F32),
                   jax.ShapeDtypeStruct((B, nsel, H_C, LANES), F32)],
        compiler_params=_cparams("parallel", "arbitrary"),
        name="moba_step_score",
    )(pt, q_rep, *([cache_k] * ppg))
    sel_flat = jnp.swapaxes(sel[:, :, :, 0], 1, 2).astype(jnp.int32).reshape(-1)
    tok = pl.BlockSpec((None, None, 1, HD_C), lambda b, h, pt, sl: (b, h, 0, 0))
    srow = lambda i: pl.BlockSpec((None, H_C, MOBA_BLOCK),
                                  lambda b, h, pt, sl, i=i: (b, 0, sl[(b * H_C + h) * nsel + i]))
    vpage = lambda i, r: pl.BlockSpec(
        (None, None, HD_C, PAGE_SIZE),
        lambda b, h, pt, sl, i=i, r=r: (pt[b * n_pages + sl[(b * H_C + h) * nsel + i] * ppb + r], h, 0, 0))
    row = lambda a: a.reshape(B, H_C, 1, HD_C)
    out = pl.pallas_call(
        functools.partial(_moba_attend_kernel, nbp=nbp),
        grid_spec=pltpu.PrefetchScalarGridSpec(
            num_scalar_prefetch=2,
            grid=(B, H_C),
            in_specs=[tok, tok, tok,
                      pl.BlockSpec((None, 1, MOBA_BLOCK), lambda b, h, pt, sl: (h, 0, 0)),
                      pl.BlockSpec(memory_space=pltpu.SMEM)]
                     + [srow(i) for i in range(nsel)]
                     + [vpage(i, r) for i in range(nsel) for r in range(ppb)],
            out_specs=tok,
        ),
        out_shape=jax.ShapeDtypeStruct((B, H_C, 1, HD_C), F32),
        compiler_params=_cparams("parallel", "arbitrary"),
        name="moba_step_attend",
    )(pt, sel_flat, row(q), row(k_new), row(v_new), bias_last, bias_scalars,
      *([scores] * nsel), *([cache_v] * (nsel * ppb)))
    return out.reshape(B, H_C, HD_C)


def _s5_kernel(u_ref, h0r_ref, h0i_ref, wb_ref, ar_ref, ai_ref, wc_ref, d_ref, wglu_ref,
               o_ref, hr_ref, hi_ref, x_ref, *, tt, nb):
    ti = pl.program_id(0)
    ns = D_S5_STATE

    @pl.when(ti == 0)
    def _():
        hr_ref[...] = h0r_ref[...]
        hi_ref[...] = h0i_ref[...]

    nslab = ns // LANES
    if tt == 1:
        seqs, seq_rows, mat_rows, cols = [0], lambda b: slice(None), lambda b: slice(None), lambda b: slice(None)
        u = u_ref[...]
    else:
        seqs = range(nb)
        seq_rows = lambda b: pl.ds(b, tt, stride=nb)
        mat_rows = lambda b: slice(b * tt, (b + 1) * tt)
        cols = lambda b: slice(b * D_S5, (b + 1) * D_S5)
        u = jnp.concatenate([u_ref[:, cols(b)] for b in seqs], axis=0)
    nsp = S5_MM_SPLIT
    du, dsx = D_S5 // nsp, ns // nsp
    ub = u.astype(BF16)
    for part in range(2):
        for g in range(nsp):
            xg = _dot(ub[:, g * du:(g + 1) * du], wb_ref[g * du:(g + 1) * du, part * ns + g * dsx:part * ns + (g + 1) * dsx])
            for b in seqs:
                for c in range(dsx // LANES):
                    slab = (part * ns + g * dsx) // LANES + c
                    x_ref[slab, seq_rows(b), :] = xg[mat_rows(b), c * LANES:(c + 1) * LANES]
    slabs = lambda ref: jnp.stack([ref[:, c * LANES:(c + 1) * LANES] for c in range(nslab)])
    ar = slabs(ar_ref)
    ai = slabs(ai_ref)

    def step(t, carry):
        hr, hi = carry
        rows = pl.ds(pl.multiple_of(t * nb, nb), nb)
        nr = ar * hr - ai * hi + x_ref[0:nslab, rows, :]
        ni = ar * hi + ai * hr + x_ref[nslab:2 * nslab, rows, :]
        x_ref[0:nslab, rows, :] = nr
        x_ref[nslab:2 * nslab, rows, :] = ni
        return nr, ni

    hr, hi = lax.fori_loop(0, tt, step, (slabs(hr_ref), slabs(hi_ref)))
    for c in range(nslab):
        hr_ref[:, c * LANES:(c + 1) * LANES] = hr[c]
        hi_ref[:, c * LANES:(c + 1) * LANES] = hi[c]

    def state_cols(first_slab, n):
        hs = [jnp.concatenate([x_ref[first_slab + c, seq_rows(b), :] for c in range(n)], axis=-1) for b in seqs]
        return (hs[0] if len(hs) == 1 else jnp.concatenate(hs, axis=0)).astype(BF16)

    ys = []
    for g in range(nsp):
        yg = None
        for part in range(2):
            hg = state_cols((part * ns + g * dsx) // LANES, dsx // LANES)
            t = _dot(hg, wc_ref[part * ns + g * dsx:part * ns + (g + 1) * dsx, g * du:(g + 1) * du])
            yg = t if yg is None else yg + t
        ys.append(yg)
    y = jax.nn.gelu(jnp.concatenate(ys, axis=-1) + d_ref[...] * u)
    gl = _dot(y.astype(BF16), wglu_ref[...])
    o = gl[:, 0:D_S5] * jax.nn.sigmoid(gl[:, D_S5:2 * D_S5])
    for b in seqs:
        o_ref[:, cols(b)] = o[mat_rows(b), :]


def _s5(u, h0r, h0i, wb, ar, ai, wc, d, wglu, *, tt, nb):
    L = u.shape[0]
    const = lambda a: pl.BlockSpec(a.shape, lambda i: (0,) * a.ndim)
    rows = pl.BlockSpec((tt, nb * D_S5), lambda i: (i, 0))
    st = pl.BlockSpec((nb, D_S5_STATE), lambda i: (0, 0))
    args = (u, h0r, h0i, wb, ar, ai, wc, d, wglu)
    if tt == 1:
        assert L == 1
        rows = pl.BlockSpec((nb, D_S5), lambda i: (0, 0))
        args = (u.reshape(nb, D_S5),) + args[1:]
    return pl.pallas_call(
        functools.partial(_s5_kernel, tt=tt, nb=nb),
        grid=(L // tt,),
        in_specs=[rows] + [const(a) for a in args[1:]],
        out_specs=[rows, st, st],
        out_shape=[jax.ShapeDtypeStruct(args[0].shape, F32),
                   jax.ShapeDtypeStruct((nb, D_S5_STATE), F32), jax.ShapeDtypeStruct((nb, D_S5_STATE), F32)],
        scratch_shapes=[pltpu.VMEM((2 * D_S5_STATE // LANES, tt * nb, LANES), F32)],
        compiler_params=_cparams("arbitrary"),
        name="s5",
    )(*args)


def _rel_bucket(dist):
    n = jnp.maximum(dist, 0)
    exact = N_BUCKETS // 2
    nf = jnp.maximum(n, exact).astype(F32)
    large = exact + (jnp.log(nf / exact) / math.log(MAX_DISTANCE / exact) * (N_BUCKETS - exact)).astype(jnp.int32)
    return jnp.where(n < exact, n, jnp.minimum(large, N_BUCKETS - 1))


def _block_diag(w):
    G, a, b = w.shape
    eye = jnp.eye(G, dtype=w.dtype)
    return (eye[:, None, :, None] * w[:, :, None, :]).reshape(G * a, G * b)


def _s5_weights(A_re, A_im, log_dt, B_re, B_im, C_re, C_im, D):
    dt = jnp.exp(log_dt.astype(F32))[:, None]
    lr, li = A_re.astype(F32), A_im.astype(F32)
    mag = jnp.exp(lr * dt)
    ar, ai = mag * jnp.cos(li * dt), mag * jnp.sin(li * dt)
    den = lr * lr + li * li
    nr = ar - 1.0
    cr, ci = (nr * lr + ai * li) / den, (ai * lr - nr * li) / den
    bbr = cr[..., None] * B_re - ci[..., None] * B_im
    bbi = cr[..., None] * B_im + ci[..., None] * B_re
    wb = jnp.concatenate([_block_diag(bbr.transpose(0, 2, 1)), _block_diag(bbi.transpose(0, 2, 1))], axis=1)
    wc = jnp.concatenate([_block_diag(C_re.transpose(0, 2, 1)), -_block_diag(C_im.transpose(0, 2, 1))], axis=0)
    return (wb.astype(BF16), ar.reshape(1, -1), ai.reshape(1, -1), wc.astype(BF16), D.reshape(1, -1).astype(F32))


def _pad_rows_front(a, rows):
    return jnp.pad(a, ((0, 0), (rows - a.shape[1], 0), (0, 0)))


def _trunk(x, st, W, *, decode):
    B, L, D = x.shape
    T = B * L
    tm = min(512, T)
    out = {}

    x2 = x.reshape(T, D)
    z = _norm_proj_wide(x2, W['norm_mix'][0], W['ev_w_in'], tm=tm, chunk=EV_IN_TILE, name="ev_in_proj")
    zc = z.shape[1]
    if decode:
        lp = SUBLANES
        z3 = jnp.pad(z.reshape(B, L, zc), ((0, 0), (0, lp - L), (0, 0)))
        tt = lp
    else:
        lp = L
        z3 = z.reshape(B, L, zc)
        tt = min(512, L)
    conv0 = _pad_rows_front(st['gdn_conv'], SUBLANES)
    mixed, tail, sg, sh_t = _even_mixer(z3, conv0, st['gdn'], jnp.swapaxes(st['hgrn'], -1, -2), W['gdn_conv_w'],
                                        W['gdn_neg_a'], W['gdn_dt_bias'], W['gdn_norm'], W['hgrn_norm'], W['hgrn_lb'],
                                        l_real=L, tt=tt)
    out['gdn_conv'] = tail[:, SUBLANES - (GDN_CONV - 1):, :]
    out['gdn'] = sg
    out['hgrn'] = jnp.swapaxes(sh_t, -1, -2)
    if decode:
        parts = [(mixed[:, :L].reshape(T, 2 * D_A), W['ev_w_out'], pl.BlockSpec((tm, 2 * D_A), lambda i: (i, 0)))]
    else:
        parts = [(mixed, W['ev_w_out'], pl.BlockSpec((None, tm, 2 * D_A), lambda b, i: (b, i, 0)))]
    x2, fs = _post_mixer(x2, parts, st, W, 0, B, L, decode)
    out['ffn0'] = fs

    dc = H_C * HD_C
    gqk = jnp.concatenate([jnp.tile(W['moba_gq'], H_C), jnp.tile(W['moba_gk'], H_C)])
    if decode:
        qkv, u_tm = _od_qkv_proj(x2, W['norm_mix'][1], W['od_w_in'], gqk, tm=tm)
        u_tm = u_tm.reshape(L, B * D_S5)
        out['moba_k'] = qkv[:, dc:2 * dc].reshape(B, L, H_C, HD_C)
        out['moba_v'] = qkv[:, 2 * dc:].reshape(B, L, H_C, HD_C)
        heads = lambda a: a.reshape(B, H_C, HD_C)
        o_c = _moba_step(heads(qkv[:, :dc]), heads(qkv[:, dc:2 * dc]), heads(qkv[:, 2 * dc:]),
                         st['cache_k'], st['cache_v'], st['page_table'], *W['moba_bias_step'])
        o_c = o_c.reshape(T, dc)
        tts = 1
    else:
        qkv, u_tm, kt, vt = _od_qkv_proj(x2, W['norm_mix'][1], W['od_w_in'], gqk, tm=tm, seq=(B, L))
        out['moba_k'] = jnp.transpose(kt.reshape(B, H_C, HD_C, L), (0, 3, 1, 2))
        out['moba_v'] = jnp.transpose(vt.reshape(B, H_C, HD_C, L), (0, 3, 1, 2))
        o_c = _moba_seq(qkv.reshape(B, L, 3 * dc), vt, W['moba_bias_tiles']).reshape(T, dc)
        tts = min(64, L)
    o_d, hr, hi = _s5(u_tm, st['s5_re'], st['s5_im'], *W['s5'], W['s5_glu_w'], tt=tts, nb=B)
    out['s5_re'] = hr.reshape(B, S5_GROUPS, S5_STATE)
    out['s5_im'] = hi.reshape(B, S5_GROUPS, S5_STATE)
    if decode:
        parts = [(o_c, W['od_w_out'][:dc], pl.BlockSpec((tm, dc), lambda i: (i, 0))),
                 (o_d, W['od_w_out'][dc:], pl.BlockSpec((tm, D_S5), lambda i: (i, 0)))]
    else:
        parts = [(o_c.reshape(B, L, dc), W['od_w_out'][:dc], pl.BlockSpec((None, tm, dc), lambda b, i: (b, i, 0))),
                 (o_d, W['od_w_out'][dc:], pl.BlockSpec((tm, D_S5), lambda b, i: (i, b)))]
    x2, fs = _post_mixer(x2, parts, st, W, 1, B, L, decode)
    out['ffn1'] = fs
    out['y'] = x2.reshape(B, L, D)
    return out


def _post_mixer(x2, parts, st, W, l, B, L, decode):
    T, D = x2.shape
    dq = MEM_H * MEM_HD
    buf = st['ffn_conv'][l]
    if decode:
        x2 = _proj_res(x2, parts, tm=T, name="mixer_out_proj")
        q = _norm_proj(x2, W['norm_mem'][l], W['mem_wq'][l], tm=T, tn=dq, n_norm_tiles=1, group=MEM_HD,
                       gain=jnp.tile(W['mem_gq'][l], MEM_H), name="mem_q_proj")
        o = _mem_attn_step(q.reshape(B, 1, dq), st['mem_k'], st['mem_v'], l).reshape(T, dq)
        x3 = _proj_res(x2, [(o, W['mem_wo'][l], pl.BlockSpec((T, dq), lambda i: (i, 0)))], tm=T, name="mem_out_proj")
        y, gate = _ffn_step(x3, buf[:, 0], buf[:, 1], W['norm_ffn'][l], W['ffn_w_gu'][l],
                            W['ffn_conv_w'][l], W['ffn_conv_b'][l], W['ffn_w_down'][l])
        fs = jnp.stack([buf[:, 1], gate], axis=1)
        return y, fs
    y, tail = _post_seq(x2.reshape(B, L, D), parts, st['mem_k'][l], st['mem_v'][l], W['norm_mem'][l], W['mem_wq'][l],
                        W['mem_gq'][l], W['mem_wo'][l], _pad_rows_front(buf, SUBLANES), W['norm_ffn'][l],
                        W['ffn_w_gu'][l], W['ffn_conv_w'][l], W['ffn_conv_b'][l], W['ffn_w_down'][l],
                        tm=min(512, L), name="post_mixer_seq")
    return y.reshape(T, D), tail[:, SUBLANES - (FFN_CONV - 1):, :]


def _mem_kv(mem, g_norm, w_kv, g_k):
    B, M, D = mem.shape
    dq = MEM_H * MEM_HD
    gain = jnp.concatenate([jnp.tile(g_k, MEM_H), jnp.ones((dq,), F32)])
    kv = _norm_proj(mem.reshape(B * M, D), g_norm, w_kv, tm=min(512, B * M), tn=dq, n_norm_tiles=1, group=MEM_HD,
                    gain=gain, name="mem_kv_proj")
    return kv[:, :dq].reshape(B, M, dq), kv[:, dq:].reshape(B, M, dq)


def kernel(x_prompt, x_sample, state_gdn_conv, state_gdn, state_hgrn, cache_moba_k, cache_moba_v, state_s5_re, state_s5_im, cache_mem_k, cache_mem_v, state_ffn_conv, page_table, mem_prompt, rel_bias, hgrn_lb_raw, norm_mix, norm_mem, norm_memkv, norm_ffn, mem_wq, mem_wkv, mem_gq, mem_gk, mem_wo, ffn_w_gu, ffn_conv_w, ffn_conv_b, ffn_w_down, ev_w_in, ev_w_out, gdn_conv_w, gdn_A_log, gdn_dt_bias, gdn_norm, hgrn_norm, od_w_in, od_w_out, moba_gq, moba_gk, s5_A_re, s5_A_im, s5_log_dt, s5_B_re, s5_B_im, s5_C_re, s5_C_im, s5_D, s5_glu_w):
    Bp, Lq, D = x_prompt.shape
    Bs = x_sample.shape[0]
    dc = H_C * HD_C
    dq = MEM_H * MEM_HD
    n_pages = page_table.shape[1]
    past = n_pages * PAGE_SIZE

    d3 = 3 * D_A
    w_in = ev_w_in[0]
    ba_cols = w_in[:, 4 * D_A:4 * D_A + 2 * H_A]
    w_in = jnp.concatenate([w_in[:, :4 * D_A], w_in[:, 4 * D_A + 2 * H_A:], ba_cols,
                            jnp.zeros((D, LANES - 2 * H_A), F32)], axis=1)
    lb_all = jnp.cumsum(jax.nn.softmax(hgrn_lb_raw.astype(F32), axis=0), axis=0)
    bias_t = rel_bias.T.astype(F32)
    nblk = MOBA_BLOCK
    by_rel = bias_t[:, _rel_bucket(jnp.arange(2 * nblk, dtype=jnp.int32))]
    mm = jnp.arange(2 * nblk, dtype=jnp.int32)

    def toeplitz(shift):
        y = by_rel[:, (shift + mm) % (2 * nblk)]
        flat = jnp.tile(y, (1, nblk))[:, :nblk * (2 * nblk - 1)]
        return flat.reshape(H_C, nblk, 2 * nblk - 1)[:, :, :nblk]

    far = bias_t[:, _rel_bucket(jnp.int32(2 * nblk))]
    tiles = jnp.stack([toeplitz(0), toeplitz(nblk)], axis=1) - far[:, None, None, None]
    bias_step = (by_rel[:, nblk - jnp.arange(nblk, dtype=jnp.int32)][:, None, :],
                 jnp.stack([far, bias_t[:, 0]], axis=1))
    W = dict(
        norm_mix=norm_mix, norm_mem=norm_mem, norm_ffn=norm_ffn,
        ev_w_in=w_in.astype(BF16), ev_w_out=ev_w_out[0].astype(BF16),
        gdn_conv_w=gdn_conv_w[0], gdn_neg_a=-jnp.exp(gdn_A_log[0].astype(F32)), gdn_dt_bias=gdn_dt_bias[0],
        gdn_norm=gdn_norm[0], hgrn_norm=hgrn_norm[0], hgrn_lb=lb_all[0],
        od_w_in=od_w_in[0].astype(BF16), od_w_out=od_w_out[0].astype(BF16),
        moba_gq=moba_gq[0], moba_gk=moba_gk[0], moba_bias_tiles=tiles, moba_bias_step=bias_step,
        s5=_s5_weights(s5_A_re[0], s5_A_im[0], s5_log_dt[0], s5_B_re[0], s5_B_im[0], s5_C_re[0], s5_C_im[0], s5_D[0]),
        s5_glu_w=s5_glu_w[0].astype(BF16),
        mem_wq=mem_wq.astype(BF16), mem_gq=mem_gq, mem_wo=mem_wo.astype(BF16),
        ffn_w_gu=ffn_w_gu.astype(BF16), ffn_conv_w=ffn_conv_w, ffn_conv_b=ffn_conv_b, ffn_w_down=ffn_w_down.astype(BF16),
    )

    mk_l, mv_l = [], []
    for l in range(2):
        mk, mv = _mem_kv(mem_prompt, norm_memkv[l], mem_wkv[l].astype(BF16), mem_gk[l])
        mk_l.append(mk)
        mv_l.append(mv)
    p_mem_k, p_mem_v = jnp.stack(mk_l), jnp.stack(mv_l)
    zeros = lambda *s: jnp.zeros(s, F32)
    st_p = dict(gdn_conv=zeros(Bp, GDN_CONV - 1, d3), gdn=zeros(Bp, H_A, HD_A, HD_A), hgrn=zeros(Bp, H_B, DK_B, DV_B),
                s5_re=zeros(Bp, D_S5_STATE), s5_im=zeros(Bp, D_S5_STATE), mem_k=p_mem_k, mem_v=p_mem_v,
                ffn_conv=zeros(2, Bp, FFN_CONV - 1, D_FF))
    op = _trunk(x_prompt, st_p, W, decode=False)

    st_s = dict(gdn_conv=state_gdn_conv[0], gdn=state_gdn[0], hgrn=state_hgrn[0],
                s5_re=state_s5_re[0].reshape(Bs, D_S5_STATE), s5_im=state_s5_im[0].reshape(Bs, D_S5_STATE),
                mem_k=cache_mem_k, mem_v=cache_mem_v,
                ffn_conv=state_ffn_conv,
                cache_k=jnp.transpose(cache_moba_k[0], (0, 2, 3, 1)), cache_v=jnp.transpose(cache_moba_v[0], (0, 2, 3, 1)),
                page_table=page_table)
    os_ = _trunk(x_sample, st_s, W, decode=True)

    M = mem_prompt.shape[1]
    return (op['y'], os_['y'],
            op['gdn_conv'][None], op['gdn'][None], op['hgrn'][None], op['moba_k'][None], op['moba_v'][None],
            op['s5_re'][None], op['s5_im'][None],
            p_mem_k.reshape(2, Bp, M, MEM_H, MEM_HD), p_mem_v.reshape(2, Bp, M, MEM_H, MEM_HD),
            jnp.stack([op['ffn0'], op['ffn1']]),
            os_['gdn_conv'][None], os_['gdn'][None], os_['hgrn'][None], os_['moba_k'][None], os_['moba_v'][None],
            os_['s5_re'][None], os_['s5_im'][None], jnp.stack([os_['ffn0'], os_['ffn1']]))
```

```python
import functools
import math

import jax
import jax.numpy as jnp
from jax import lax
from jax.experimental import pallas as pl
from jax.experimental.pallas import tpu as pltpu

F32 = jnp.float32
BF16 = jnp.bfloat16
HI = lax.Precision.HIGHEST
EPS = 1e-6

LANES = 128
SUBLANES = 8
VMEM_LIMIT_BYTES = 56 * 1024 * 1024

D_MODEL = 1024
H_A, HD_A, GDN_CONV = 8, 64, 4
H_B, DK_B, DV_B = 8, 64, 64
GDN_CHUNK, HGRN_CHUNK = 64, 32
GDN_GROUP, HGRN_GROUP = 4, 4
H_C, HD_C = 8, 64
MOBA_BLOCK, MOBA_TOPK = 256, 3
N_BUCKETS, MAX_DISTANCE = 32, 128
S5_GROUPS, S5_GROUP_CH, S5_STATE = 32, 16, 64
MEM_H, MEM_HD = 4, 128
D_FF, FFN_CONV = 2816, 3
PAGE_SIZE = 128
D_A = H_A * HD_A
D_S5 = S5_GROUPS * S5_GROUP_CH
D_S5_STATE = S5_GROUPS * S5_STATE
S5_MM_SPLIT = 4
FF_CHUNK = D_FF // 2
EV_IN_COLS = 8 * D_A + LANES
EV_IN_TILE = EV_IN_COLS // 3


def _cparams(*sem):
    return pltpu.CompilerParams(dimension_semantics=sem, vmem_limit_bytes=VMEM_LIMIT_BYTES)


def _dot(a, b, precision=None):
    return jnp.dot(a, b, preferred_element_type=F32, precision=precision)


def _dot_nt(a, b, precision=None):
    return lax.dot_general(a, b, (((1,), (1,)), ((), ())), preferred_element_type=F32, precision=precision)


def _dot_tn(a, b, precision=None):
    return lax.dot_general(a, b, (((0,), (0,)), ((), ())), preferred_element_type=F32, precision=precision)


def _rms(x, g):
    return x * lax.rsqrt(jnp.mean(x * x, axis=-1, keepdims=True) + EPS) * g


def _silu(x):
    return x * jax.nn.sigmoid(x)


def _tri(n, strict=False):
    r = lax.broadcasted_iota(jnp.int32, (n, n), 0)
    c = lax.broadcasted_iota(jnp.int32, (n, n), 1)
    return (r > c) if strict else (r >= c)


def _norm_proj_kernel(x_ref, g_ref, w_ref, e_ref, gain_ref, o_ref, *, n_norm_tiles, group):
    xn = _rms(x_ref[...], g_ref[...])
    z = _dot(xn.astype(BF16), w_ref[...])
    if n_norm_tiles == 0:
        o_ref[...] = z
        return
    j = pl.program_id(1)

    @pl.when(j < n_norm_tiles)
    def _():
        zz = z * z
        hi = zz.astype(BF16)
        lo = (zz - hi.astype(F32)).astype(BF16)
        s = _dot(hi, e_ref[...]) + _dot(lo, e_ref[...])
        o_ref[...] = z * lax.rsqrt(s * (1.0 / group) + EPS) * gain_ref[...]

    @pl.when(j >= n_norm_tiles)
    def _():
        o_ref[...] = z


def _norm_proj(x, g, w, *, tm, tn, n_norm_tiles=0, group=1, gain=None, time_major=None, name):
    T, D = x.shape
    N = w.shape[1]
    nj = N // tn
    if gain is None:
        gain = jnp.ones((N,), F32)
    gi = lax.broadcasted_iota(jnp.int32, (tn, tn), 0) // group
    gj = lax.broadcasted_iota(jnp.int32, (tn, tn), 1) // group
    e = (gi == gj).astype(BF16)
    if time_major is None:
        out_shape = jax.ShapeDtypeStruct((T, N), F32)
        out_spec = pl.BlockSpec((tm, tn), lambda i, j: (i, j))
    else:
        B, L = time_major
        nt = L // tm
        out_shape = jax.ShapeDtypeStruct((L, B * N), F32)
        out_spec = pl.BlockSpec((tm, tn), lambda i, j: (i % nt, (i // nt) * nj + j))
    return pl.pallas_call(
        functools.partial(_norm_proj_kernel, n_norm_tiles=n_norm_tiles, group=group),
        grid=(T // tm, nj),
        in_specs=[
            pl.BlockSpec((tm, D), lambda i, j: (i, 0)),
            pl.BlockSpec((1, D), lambda i, j: (0, 0)),
            pl.BlockSpec((D, tn), lambda i, j: (0, j)),
            pl.BlockSpec((tn, tn), lambda i, j: (0, 0)),
            pl.BlockSpec((1, tn), lambda i, j: (0, j)),
        ],
        out_specs=out_spec,
        out_shape=out_shape,
        compiler_params=_cparams("parallel", "arbitrary"),
        name=name,
    )(x, g.reshape(1, D), w, e, gain.reshape(1, N))


def _norm_proj_wide_kernel(x_ref, g_ref, w_ref, o_ref, *, chunk):
    xn = _rms(x_ref[...], g_ref[...]).astype(BF16)
    for c in range(w_ref.shape[1] // chunk):
        cs = slice(c * chunk, (c + 1) * chunk)
        o_ref[:, cs] = _dot(xn, w_ref[:, cs])


def _norm_proj_wide(x, g, w, *, tm, chunk, name):
    T, D = x.shape
    N = w.shape[1]
    const = lambda shape: pl.BlockSpec(shape, lambda i: (0, 0), pipeline_mode=pl.Buffered(1))
    return pl.pallas_call(
        functools.partial(_norm_proj_wide_kernel, chunk=chunk),
        grid=(T // tm,),
        in_specs=[pl.BlockSpec((tm, D), lambda i: (i, 0)), const((1, D)), const((D, N))],
        out_specs=pl.BlockSpec((tm, N), lambda i: (i, 0)),
        out_shape=jax.ShapeDtypeStruct((T, N), F32),
        compiler_params=_cparams("parallel"),
        name=name,
    )(x, g.reshape(1, D), w)


def _od_qkv_kernel(x_ref, g_ref, w_ref, e_ref, gain_ref, o_ref, u_ref, *t_refs):
    dc = H_C * HD_C
    xn = _rms(x_ref[...], g_ref[...]).astype(BF16)
    u_ref[...] = _dot(xn, w_ref[:, 3 * dc:3 * dc + D_S5])
    for c in range(3):
        cs = slice(c * dc, (c + 1) * dc)
        z = _dot(xn, w_ref[:, cs])
        if c < 2:
            zz = z * z
            hi = zz.astype(BF16)
            lo = (zz - hi.astype(F32)).astype(BF16)
            s = _dot(hi, e_ref[...]) + _dot(lo, e_ref[...])
            z = z * lax.rsqrt(s * (1.0 / HD_C) + EPS) * gain_ref[:, cs]
        o_ref[:, cs] = z
        if t_refs and c > 0:
            t_refs[c - 1][...] = z.T


def _od_qkv_proj(x, g, w, gain, *, tm, seq=None):
    T, D = x.shape
    dc = H_C * HD_C
    gi = lax.broadcasted_iota(jnp.int32, (dc, dc), 0) // HD_C
    gj = lax.broadcasted_iota(jnp.int32, (dc, dc), 1) // HD_C
    e = (gi == gj).astype(BF16)
    const = lambda shape: pl.BlockSpec(shape, lambda i: (0, 0), pipeline_mode=pl.Buffered(1))
    out_specs = [pl.BlockSpec((tm, 3 * dc), lambda i: (i, 0))]
    out_shape = [jax.ShapeDtypeStruct((T, 3 * dc), F32)]
    if seq is None:
        out_specs += [pl.BlockSpec((tm, D_S5), lambda i: (i, 0))]
        out_shape += [jax.ShapeDtypeStruct((T, D_S5), F32)]
    else:
        B, L = seq
        nt = L // tm
        tspec = pl.BlockSpec((None, dc, tm), lambda i: (i // nt, 0, i % nt))
        out_specs += [pl.BlockSpec((tm, D_S5), lambda i: (i % nt, i // nt)), tspec, tspec]
        out_shape += [jax.ShapeDtypeStruct((L, B * D_S5), F32)] + [jax.ShapeDtypeStruct((B, dc, L), F32)] * 2
    return pl.pallas_call(
        _od_qkv_kernel,
        grid=(T // tm,),
        in_specs=[pl.BlockSpec((tm, D), lambda i: (i, 0)), const((1, D)), const((D, 3 * dc + D_S5)), const((dc, dc)),
                  const((1, 2 * dc))],
        out_specs=out_specs,
        out_shape=out_shape,
        compiler_params=_cparams("parallel"),
        name="od_in_proj_qkv",
    )(x, g.reshape(1, D), w, e, gain.reshape(1, 2 * dc))


def _proj_res_kernel(*refs, n_in):
    x_ref, o_ref = refs[0], refs[-1]
    acc = x_ref[...]
    for k in range(n_in):
        a_ref, w_ref = refs[1 + 2 * k], refs[2 + 2 * k]
        acc = acc + _dot(a_ref[...].astype(BF16), w_ref[...])
    o_ref[...] = acc


def _proj_res(x, parts, *, tm, name):
    T, N = x.shape
    in_specs = [pl.BlockSpec((tm, N), lambda i: (i, 0))]
    args = [x]
    for a, w, spec in parts:
        in_specs += [spec, pl.BlockSpec(w.shape, lambda i: (0, 0))]
        args += [a, w]
    return pl.pallas_call(
        functools.partial(_proj_res_kernel, n_in=len(parts)),
        grid=(T // tm,),
        in_specs=in_specs,
        out_specs=pl.BlockSpec((tm, N), lambda i: (i, 0)),
        out_shape=jax.ShapeDtypeStruct((T, N), F32),
        compiler_params=_cparams("parallel"),
        name=name,
    )(*args)


def _neumann_inverse(mats, n):
    m = mats[0].shape[0]
    eye = (lax.broadcasted_iota(jnp.int32, (m, m), 0) == lax.broadcasted_iota(jnp.int32, (m, m), 1)).astype(F32)
    ts = [eye - a for a in mats]
    ps = [a.astype(BF16) for a in mats]
    k = 2
    while k < n:
        ps = [_dot(p, p).astype(BF16) for p in ps]
        ts = [t + _dot(t.astype(BF16), p) for t, p in zip(ts, ps)]
        k *= 2
    return ts


def _even_mixer_kernel(qkv_ref, zg_ref, hq_ref, hf_ref, hi_ref, hg_ref, ba_ref,
                       conv0_ref, sg0_ref, sh0_ref,
                       convw_ref, nega_ref, dtb_ref, gnorm_ref, hnorm_ref, lb_ref,
                       mixed_ref, tail_ref, sg_ref, sh_ref,
                       xp_ref, beta_ref, g_ref, sbd_ref,
                       *, tt, cg, ch, l_real):
    ti = pl.program_id(1)
    n_tiles = pl.num_programs(1)

    @pl.when(ti == 0)
    def _():
        xp_ref[0:SUBLANES, :] = conv0_ref[...]
        sh_ref[...] = sh0_ref[...]
        zero = jnp.zeros((HD_A, HD_A), F32)
        for p in range(H_A // 2):
            sbd_ref[p] = jnp.concatenate([jnp.concatenate([sg0_ref[2 * p], zero], axis=-1),
                                          jnp.concatenate([zero, sg0_ref[2 * p + 1]], axis=-1)], axis=0)

    xp_ref[SUBLANES:SUBLANES + tt, :] = qkv_ref[...]
    lr_last = l_real - (l_real - 1) // tt * tt
    tail_ref[...] = xp_ref[lr_last:lr_last + SUBLANES, :]

    padded = (l_real % tt) != 0
    if padded:
        row = lax.broadcasted_iota(jnp.int32, (tt, 1), 0) + ti * tt
        valid = row < l_real
    ba = ba_ref[...]
    beta_all = jax.nn.sigmoid(ba)
    g_all = nega_ref[...] * jax.nn.softplus(ba + dtb_ref[...])
    if padded:
        beta_all = jnp.where(valid, beta_all, 0.0)
        g_all = jnp.where(valid, g_all, 0.0)
    beta_ref[...] = beta_all
    g_ref[...] = g_all

    ltri_g = _tri(cg).astype(F32)
    hnorm = hnorm_ref[...]
    gnorm2 = jnp.concatenate([gnorm_ref[...], gnorm_ref[...]], axis=-1)
    pairs = range(H_A // 2)
    n2 = 2 * cg
    lane = lax.broadcasted_iota(jnp.int32, (1, LANES), 1)
    m0 = (lane < HD_A).astype(F32)
    m1 = 1.0 - m0
    ri = lax.broadcasted_iota(jnp.int32, (n2, n2), 0)
    cj = lax.broadcasted_iota(jnp.int32, (n2, n2), 1)
    same_head = (ri // cg) == (cj // cg)
    bd_tril = same_head & (ri >= cj)
    bd_stril = same_head & (ri > cj)

    def stack2(a):
        return jnp.concatenate([a * m0, a * m1], axis=0)

    def rep2(col0, col1):
        return jnp.concatenate([jnp.broadcast_to(col0, (cg, LANES)), jnp.broadcast_to(col1, (cg, LANES))], axis=0)

    ng = min(GDN_GROUP, tt // cg)

    def gdn_group(gi, _):
        items = [(c, p) for c in range(ng) for p in pairs]
        rows_c, gcum_c, gt_c, bet_c, qkv_c = [], [], [], [], []
        for c in range(ng):
            r0 = pl.multiple_of((gi * ng + c) * cg, cg)
            rows = pl.ds(r0, cg)
            win = xp_ref[pl.ds(r0, cg + SUBLANES), :]
            conv = convw_ref[GDN_CONV - 1:GDN_CONV, :] * win[SUBLANES:SUBLANES + cg]
            for j in range(GDN_CONV - 1):
                off = SUBLANES - (GDN_CONV - 1) + j
                conv = conv + convw_ref[j:j + 1, :] * win[off:off + cg]
            qkv_c.append(_silu(conv))
            gcum = _dot(ltri_g, g_ref[rows, :], HI)
            gpad = gcum if cg == LANES else jnp.concatenate([gcum, jnp.zeros((LANES - cg, LANES), F32)], axis=0)
            rows_c.append(rows)
            gcum_c.append(gcum)
            gt_c.append(gpad.T)
            bet_c.append(beta_ref[rows, :])
        qs, ks, vs, betas, gcs, decays, glasts, glrows = {}, {}, {}, {}, {}, {}, {}, {}
        for it in items:
            c, p = it
            rows = rows_c[c]
            h0, h1 = H_A + 2 * p, H_A + 2 * p + 1
            q = stack2(qkv_c[c][:, p * LANES:(p + 1) * LANES])
            k = stack2(qkv_c[c][:, D_A + p * LANES:D_A + (p + 1) * LANES])
            qs[it] = q * lax.rsqrt(jnp.sum(q * q, axis=-1, keepdims=True) + EPS) * (HD_A ** -0.5)
            ks[it] = k * lax.rsqrt(jnp.sum(k * k, axis=-1, keepdims=True) + EPS)
            vs[it] = stack2(qkv_c[c][:, 2 * D_A + p * LANES:2 * D_A + (p + 1) * LANES])
            betas[it] = rep2(bet_c[c][:, 2 * p:2 * p + 1], bet_c[c][:, 2 * p + 1:2 * p + 2])
            gc = rep2(gcum_c[c][:, h0:h0 + 1], gcum_c[c][:, h1:h1 + 1])
            gr = jnp.concatenate([gt_c[c][h0:h0 + 1, 0:cg], gt_c[c][h1:h1 + 1, 0:cg]], axis=-1)
            gcs[it] = gc
            glasts[it] = rep2(gcum_c[c][cg - 1:cg, h0:h0 + 1], gcum_c[c][cg - 1:cg, h1:h1 + 1])
            glrows[it] = jnp.exp(glasts[it][0:1, :]) * m0 + jnp.exp(glasts[it][cg:cg + 1, :]) * m1
            decays[it] = jnp.exp(jnp.where(bd_tril, gc[:, 0:n2] - gr, -jnp.inf))
        kqs = {it: _dot_nt(jnp.concatenate([ks[it], qs[it]], axis=0).astype(BF16), ks[it].astype(BF16))
               for it in items}
        ts = dict(zip(items, _neumann_inverse(
            [jnp.where(bd_stril, betas[it][:, 0:n2] * kqs[it][0:n2] * decays[it], 0.0) for it in items], cg)))
        egs = {it: jnp.exp(gcs[it]) for it in items}
        sols = {it: _dot(ts[it].astype(BF16),
                         jnp.concatenate([betas[it] * vs[it], (betas[it] * egs[it]) * ks[it]], axis=-1).astype(BF16))
                for it in items}
        wqs = {it: jnp.concatenate([sols[it][:, LANES:2 * LANES], qs[it] * egs[it]], axis=0).astype(BF16)
               for it in items}
        qks = {it: (kqs[it][n2:2 * n2] * decays[it]).astype(BF16) for it in items}
        kouts = {it: (ks[it] * jnp.exp(glasts[it] - gcs[it])).astype(BF16) for it in items}
        ss = [sbd_ref[p] for p in pairs]
        for c in range(ng):
            wss = [_dot(wqs[(c, p)], ss[p].astype(BF16)) for p in pairs]
            ubs = [(sols[(c, p)][:, 0:LANES] - wss[p][0:n2]).astype(BF16) for p in pairs]
            os_ = [wss[p][n2:2 * n2] + _dot(qks[(c, p)], ubs[p]) for p in pairs]
            ss = [ss[p] * glrows[(c, p)] + _dot_tn(kouts[(c, p)], ubs[p]) for p in pairs]
            for p in pairs:
                o = os_[p]
                on = o * lax.rsqrt(jnp.sum(o * o, axis=-1, keepdims=True) * (1.0 / HD_A) + EPS)
                cols = slice(p * LANES, (p + 1) * LANES)
                mixed_ref[rows_c[c], cols] = (on[0:cg] + on[cg:n2]) * gnorm2 * _silu(zg_ref[rows_c[c], cols])
        for p in pairs:
            sbd_ref[p] = ss[p]
        return 0

    lax.fori_loop(0, tt // (cg * ng), gdn_group, 0)
    xp_ref[0:SUBLANES, :] = xp_ref[tt:tt + SUBLANES, :]
    for p in pairs:
        sg_ref[2 * p] = sbd_ref[p, 0:HD_A, 0:HD_A]
        sg_ref[2 * p + 1] = sbd_ref[p, HD_A:2 * HD_A, HD_A:2 * HD_A]

    nh = min(HGRN_GROUP, tt // ch)
    gr = nh * ch
    ri = lax.broadcasted_iota(jnp.int32, (gr, gr), 0)
    cj = lax.broadcasted_iota(jnp.int32, (gr, gr), 1)
    same_chunk_causal = (ri >= cj) & ((ri // ch) == (cj // ch))
    ltri_h = same_chunk_causal.astype(F32)
    lb = lb_ref[...]
    r_mid = (ch - 1) // 2

    def hgrn_group(gi, _):
        r0 = pl.multiple_of(gi * gr, gr)
        rows = pl.ds(r0, gr)
        f = lb + (1.0 - lb) * jax.nn.sigmoid(hf_ref[rows, :])
        logf = jnp.log(f)
        kk = 1.0 - f
        if padded:
            vrow = (lax.broadcasted_iota(jnp.int32, (gr, 1), 0) + r0 + ti * tt) < l_real
            logf = jnp.where(vrow, logf, 0.0)
            kk = jnp.where(vrow, kk, 0.0)
        qh = _silu(hq_ref[rows, :])
        bc = _dot(ltri_h, logf, HI).reshape(nh, ch, D_A)
        bref = bc[:, r_mid:r_mid + 1, :]
        blast = bc[:, ch - 1:ch, :]
        grp = lambda a: a.reshape(gr, D_A)
        qh3, kk3 = qh.reshape(nh, ch, D_A), kk.reshape(nh, ch, D_A)
        qa = grp(qh3 * jnp.exp(bc - bref)).astype(BF16)
        kb = grp(kk3 * jnp.exp(bref - bc)).astype(BF16)
        q_in = grp(qh3 * jnp.exp(bc)).astype(BF16)
        k_out = grp(kk3 * jnp.exp(blast - bc)).astype(BF16)
        fl = jnp.exp(blast)
        vv = hi_ref[rows, :].astype(BF16)
        heads = range(H_B)
        sls = [slice(h * DK_B, (h + 1) * DK_B) for h in heads]
        atts = [jnp.where(same_chunk_causal, _dot_nt(qa[:, sl], kb[:, sl]), 0.0).astype(BF16) for sl in sls]
        intra = [_dot(atts[h], vv[:, sls[h]]) for h in heads]
        sts = [sh_ref[h] for h in heads]
        inter = [[] for _ in heads]
        for c in range(nh):
            rc = slice(c * ch, (c + 1) * ch)
            for h in heads:
                inter[h].append(_dot_nt(q_in[rc, sls[h]], sts[h].astype(BF16)))
            sts = [sts[h] * fl[c][:, sls[h]] + _dot_tn(vv[rc, sls[h]], k_out[rc, sls[h]]) for h in heads]
        for h in heads:
            sh_ref[h] = sts[h]
        os_ = [intra[h] + (inter[h][0] if nh == 1 else jnp.concatenate(inter[h], axis=0)) for h in heads]
        outs = [_rms(os_[h], hnorm) * _silu(hg_ref[rows, h * DV_B:(h + 1) * DV_B]) for h in heads]
        for p in range(H_B // 2):
            c0 = D_A + p * LANES
            mixed_ref[rows, c0:c0 + LANES] = jnp.concatenate([outs[2 * p], outs[2 * p + 1]], axis=-1)
        return 0

    lax.fori_loop(0, tt // gr, hgrn_group, 0)


def _even_mixer(z, conv0, sg0, sh0_t, conv_w, neg_a, dt_bias, gnorm, hnorm, lb, *, l_real, tt):
    B, Lp, _ = z.shape
    cg = min(GDN_CHUNK, tt)
    ch = min(HGRN_CHUNK, tt)
    d3 = 3 * D_A
    nt = Lp // tt
    col = lambda w, idx: pl.BlockSpec((None, tt, w), lambda b, i, idx=idx: (b, i, idx))
    state = lambda: pl.BlockSpec((None, H_A, HD_A, HD_A), lambda b, i: (b, 0, 0, 0))
    vec = lambda n: pl.BlockSpec((1, n), lambda b, i: (0, 0))
    nega_row = jnp.zeros((1, LANES), F32).at[0, H_A:2 * H_A].set(neg_a)
    dtb_row = jnp.zeros((1, LANES), F32).at[0, H_A:2 * H_A].set(dt_bias)
    return pl.pallas_call(
        functools.partial(_even_mixer_kernel, tt=tt, cg=cg, ch=ch, l_real=l_real),
        grid=(B, nt),
        in_specs=[
            col(d3, 0), col(D_A, 3), col(D_A, 4), col(D_A, 5), col(D_A, 6), col(D_A, 7), col(LANES, 32),
            pl.BlockSpec((None, SUBLANES, d3), lambda b, i: (b, 0, 0)),
            state(), state(),
            pl.BlockSpec((GDN_CONV, d3), lambda b, i: (0, 0)),
            vec(LANES), vec(LANES), vec(HD_A), vec(DV_B), vec(D_A),
        ],
        out_specs=[
            pl.BlockSpec((None, tt, 2 * D_A), lambda b, i: (b, i, 0)),
            pl.BlockSpec((None, SUBLANES, d3), lambda b, i: (b, 0, 0)),
            state(), state(),
        ],
        out_shape=[
            jax.ShapeDtypeStruct((B, Lp, 2 * D_A), F32),
            jax.ShapeDtypeStruct((B, SUBLANES, d3), F32),
            jax.ShapeDtypeStruct((B, H_A, HD_A, HD_A), F32),
            jax.ShapeDtypeStruct((B, H_B, DV_B, DK_B), F32),
        ],
        scratch_shapes=[
            pltpu.VMEM((tt + SUBLANES, d3), F32),
            pltpu.VMEM((tt, LANES), F32),
            pltpu.VMEM((tt, LANES), F32),
            pltpu.VMEM((H_A // 2, 2 * HD_A, 2 * HD_A), F32),
        ],
        compiler_params=_cparams("parallel", "arbitrary"),
        name="even_mixer",
    )(z, z, z, z, z, z, z, conv0, sg0, sh0_t, conv_w, nega_row, dtb_row,
      gnorm.reshape(1, HD_A), hnorm.reshape(1, DV_B), lb.reshape(1, D_A))


def _mem_attn_rows(x, mk_ref, mv_ref, g_ref, wq_ref, gq_ref, wo_ref):
    q = _dot(_rms(x, g_ref[...]).astype(BF16), wq_ref[...])
    scale = MEM_HD ** -0.5
    outs = []
    for h in range(MEM_H):
        sl = slice(h * MEM_HD, (h + 1) * MEM_HD)
        qh = _rms(q[:, sl], gq_ref[...])
        s = _dot_nt(qh.astype(BF16), mk_ref[:, sl].astype(BF16)) * scale
        s = s - jnp.max(s, axis=-1, keepdims=True)
        p = jnp.exp(s)
        p = p / jnp.sum(p, axis=-1, keepdims=True)
        outs.append(_dot(p.astype(BF16), mv_ref[:, sl].astype(BF16)))
    o = jnp.concatenate(outs, axis=-1)
    return x + _dot(o.astype(BF16), wo_ref[...])


def _mem_attn_step_kernel(q_ref, mk_ref, mv_ref, o_ref):
    scale = MEM_HD ** -0.5
    outs = []
    for h in range(MEM_H):
        qh = jnp.broadcast_to(q_ref[:, h * MEM_HD:(h + 1) * MEM_HD], (SUBLANES, MEM_HD))
        s = _dot_nt(qh.astype(BF16), mk_ref[:, h, :].astype(BF16)) * scale
        s = s - jnp.max(s, axis=-1, keepdims=True)
        p = jnp.exp(s)
        p = p / jnp.sum(p, axis=-1, keepdims=True)
        outs.append(_dot(p.astype(BF16), mv_ref[:, h, :].astype(BF16))[0:1])
    o_ref[...] = jnp.concatenate(outs, axis=-1)


def _mem_attn_step(q, cache_k, cache_v, l):
    B = q.shape[0]
    M = cache_k.shape[2]
    dq = MEM_H * MEM_HD
    mem = pl.BlockSpec((None, None, M, MEM_H, MEM_HD), lambda b: (l, b, 0, 0, 0))
    row = pl.BlockSpec((None, 1, dq), lambda b: (b, 0, 0))
    return pl.pallas_call(
        _mem_attn_step_kernel,
        grid=(B,),
        in_specs=[row, mem, mem],
        out_specs=row,
        out_shape=jax.ShapeDtypeStruct((B, 1, dq), F32),
        compiler_params=_cparams("parallel"),
        name="mem_attn_step",
    )(q, cache_k, cache_v)


def _post_seq_kernel(*refs, n_parts, tm):
    x_ref = refs[0]
    parts = refs[1:1 + 2 * n_parts]
    (mk_ref, mv_ref, gmem_ref, wq_ref, gq_ref, wo_ref, buf0_ref, g_ref, wgu_ref, cw_ref, cb_ref, wd_ref,
     o_ref, tail_ref, gp_ref, carry_ref) = refs[1 + 2 * n_parts:]
    ti = pl.program_id(1)

    @pl.when(ti == 0)
    def _():
        carry_ref[...] = buf0_ref[...]

    x = x_ref[...]
    for k in range(n_parts):
        x = x + _dot(parts[2 * k][...].astype(BF16), parts[2 * k + 1][...])
    x = _mem_attn_rows(x, mk_ref, mv_ref, gmem_ref, wq_ref, gq_ref, wo_ref)
    xn = _rms(x, g_ref[...]).astype(BF16)
    acc = x
    for c in range(D_FF // FF_CHUNK):
        cs = slice(c * FF_CHUNK, (c + 1) * FF_CHUNK)
        gate = _dot(xn, wgu_ref[:, cs])
        up = _dot(xn, wgu_ref[:, D_FF + c * FF_CHUNK:D_FF + (c + 1) * FF_CHUNK])
        gp_ref[0:SUBLANES, :] = carry_ref[:, cs]
        gp_ref[SUBLANES:SUBLANES + tm, :] = gate
        conv = cw_ref[FFN_CONV - 1:FFN_CONV, cs] * gate + cb_ref[:, cs]
        for j in range(FFN_CONV - 1):
            off = SUBLANES - (FFN_CONV - 1) + j
            conv = conv + cw_ref[j:j + 1, cs] * gp_ref[off:off + tm, :]
        last = gp_ref[tm:tm + SUBLANES, :]
        carry_ref[:, cs] = last
        tail_ref[:, cs] = last
        act = _silu(conv) * up
        acc = acc + _dot(act.astype(BF16), wd_ref[cs, :])
    o_ref[...] = acc


def _post_seq(x, parts, mk, mv, gmem, wq, gq, wo, buf0, g, wgu, cw, cb, wd, *, tm, name):
    B, L, D = x.shape
    M = mk.shape[1]
    dq = MEM_H * MEM_HD
    const = lambda shape: pl.BlockSpec(shape, lambda b, i: (0,) * len(shape), pipeline_mode=pl.Buffered(1))
    row_tile = pl.BlockSpec((None, tm, D), lambda b, i: (b, i, 0))
    mem = pl.BlockSpec((None, M, dq), lambda b, i: (b, 0, 0))
    in_specs, args = [row_tile], [x]
    for a, w, spec in parts:
        in_specs += [spec, const(w.shape)]
        args += [a, w]
    in_specs += [mem, mem, const((1, D)), const((D, dq)), const((1, MEM_HD)), const((dq, D)),
                 pl.BlockSpec((None, SUBLANES, D_FF), lambda b, i: (b, 0, 0)),
                 const((1, D)), const((D, 2 * D_FF)), const((FFN_CONV, D_FF)), const((1, D_FF)), const((D_FF, D))]
    args += [mk, mv, gmem.reshape(1, D), wq, gq.reshape(1, MEM_HD), wo, buf0, g.reshape(1, D), wgu, cw,
             cb.reshape(1, D_FF), wd]
    return pl.pallas_call(
        functools.partial(_post_seq_kernel, n_parts=len(parts), tm=tm),
        grid=(B, L // tm),
        in_specs=in_specs,
        out_specs=[row_tile, pl.BlockSpec((None, SUBLANES, D_FF), lambda b, i: (b, 0, 0))],
        out_shape=[jax.ShapeDtypeStruct((B, L, D), F32), jax.ShapeDtypeStruct((B, SUBLANES, D_FF), F32)],
        scratch_shapes=[pltpu.VMEM((tm + SUBLANES, FF_CHUNK), F32), pltpu.VMEM((SUBLANES, D_FF), F32)],
        compiler_params=_cparams("parallel", "arbitrary"),
        name=name,
    )(*args)


def _ffn_step_kernel(x_ref, p2_ref, p1_ref, g_ref, wgu_ref, cw_ref, cb_ref, wd_ref, o_ref, gate_ref):
    x = x_ref[...]
    xn = _rms(x, g_ref[...]).astype(BF16)
    acc = x
    for c in range(D_FF // FF_CHUNK):
        cs = slice(c * FF_CHUNK, (c + 1) * FF_CHUNK)
        gate = _dot(xn, wgu_ref[:, cs])
        up = _dot(xn, wgu_ref[:, D_FF + c * FF_CHUNK:D_FF + (c + 1) * FF_CHUNK])
        gate_ref[:, cs] = gate
        conv = cw_ref[0:1, cs] * p2_ref[:, cs] + cw_ref[1:2, cs] * p1_ref[:, cs] + cw_ref[2:3, cs] * gate + cb_ref[:, cs]
        act = _silu(conv) * up
        acc = acc + _dot(act.astype(BF16), wd_ref[cs, :])
    o_ref[...] = acc


def _ffn_step(x, p2, p1, g, wgu, cw, cb, wd):
    R, D = x.shape
    full = lambda a: pl.BlockSpec(a.shape, lambda i: (0,) * a.ndim)
    args = (x, p2, p1, g.reshape(1, D), wgu, cw, cb.reshape(1, D_FF), wd)
    return pl.pallas_call(
        _ffn_step_kernel,
        grid=(1,),
        in_specs=[full(a) for a in args],
        out_specs=[pl.BlockSpec((R, D), lambda i: (0, 0)), pl.BlockSpec((R, D_FF), lambda i: (0, 0))],
        out_shape=[jax.ShapeDtypeStruct((R, D), F32), jax.ShapeDtypeStruct((R, D_FF), F32)],
        compiler_params=_cparams("arbitrary"),
        name="conv_ffn_step",
    )(*args)


def _moba_query_tile(j, hs, qt_ref, kb_ref, vt_ref, kmean_ref, bias_ref, ot_ref):
    blk = MOBA_BLOCK
    w = (j + 1) * blk
    qts = [qt_ref[h] for h in hs]
    sts = [_dot(kb_ref[h, 0:w, :], (qt * (HD_C ** -0.5)).astype(BF16)) for h, qt in zip(hs, qts)]
    sels = []
    if j > MOBA_TOPK:
        iota_n = lax.broadcasted_iota(jnp.int32, (j, blk), 0).astype(F32)
        gs = [_dot(kmean_ref[h, 0:j, :], qt, HI) for h, qt in zip(hs, qts)]
        for g in gs:
            sel = jnp.zeros((j, blk), F32)
            for _ in range(MOBA_TOPK):
                m = jnp.max(g, axis=0, keepdims=True)
                idx = jnp.min(jnp.where(g == m, iota_n, 1e9), axis=0, keepdims=True)
                hit = iota_n == idx
                sel = jnp.where(hit, 1.0, sel)
                g = jnp.where(hit, -jnp.inf, g)
            sels.append(sel)
    key_le_query = lax.broadcasted_iota(jnp.int32, (blk, blk), 0) <= lax.broadcasted_iota(jnp.int32, (blk, blk), 1)
    pcats, ls = [], []
    for i, h in enumerate(hs):
        parts = []
        for n in range(j + 1):
            s = sts[i][n * blk:(n + 1) * blk, :]
            if j - n < 2:
                s = s + bias_ref[h, j - n]
            if n == j:
                s = jnp.where(key_le_query, s, -jnp.inf)
            elif j > MOBA_TOPK:
                s = jnp.where(sels[i][n:n + 1, :] > 0.5, s, -jnp.inf)
            parts.append(s)
        m = jnp.max(parts[0], axis=0, keepdims=True)
        for s in parts[1:]:
            m = jnp.maximum(m, jnp.max(s, axis=0, keepdims=True))
        ps = [jnp.exp(s - m) for s in parts]
        l = jnp.sum(ps[0], axis=0, keepdims=True)
        for p in ps[1:]:
            l = l + jnp.sum(p, axis=0, keepdims=True)
        pcats.append((ps[0] if j == 0 else jnp.concatenate(ps, axis=0)).astype(BF16))
        ls.append(l)
    outs = [_dot(vt_ref[h, :, 0:w], pc) for h, pc in zip(hs, pcats)]
    for h, o, l in zip(hs, outs, ls):
        ot_ref[h] = o / l


def _moba_seq_kernel(q_ref, k_ref, vtin_ref, bias_ref, o_ref, kmean_ref, kb_ref, vt_ref, qt_ref, ot_ref, *, nb):
    j = pl.program_id(1)
    blk = MOBA_BLOCK
    dc = H_C * HD_C

    @pl.when(j == 0)
    def _():
        for n in range(nb):
            km = jnp.mean(k_ref[n * blk:(n + 1) * blk, :], axis=0, keepdims=True)
            for h in range(H_C):
                kmean_ref[h, n:n + 1, :] = km[:, h * HD_C:(h + 1) * HD_C]

        def prep(n, _):
            rows = pl.ds(pl.multiple_of(n * blk, blk), blk)
            kblk = k_ref[rows, :]
            vt_ref[:, :, rows] = vtin_ref[:, rows].astype(BF16).reshape(H_C, HD_C, blk)
            for h in range(H_C):
                kb_ref[h, rows, :] = kblk[:, h * HD_C:(h + 1) * HD_C].astype(BF16)
            return 0
        lax.fori_loop(0, nb, prep, 0)

    qt_ref[...] = q_ref[...].T.reshape(H_C, HD_C, blk)
    for jj in range(nb):
        @pl.when(j == jj)
        def _(jj=jj):
            def head_pair(hp, _):
                _moba_query_tile(jj, [2 * hp, 2 * hp + 1], qt_ref, kb_ref, vt_ref, kmean_ref, bias_ref, ot_ref)
                return 0
            lax.fori_loop(0, H_C // 2, head_pair, 0)
    o_ref[...] = ot_ref[...].reshape(dc, blk).T


def _moba_seq(qkv, vt, bias_tiles):
    B, L, _ = qkv.shape
    nb = L // MOBA_BLOCK
    dc = H_C * HD_C
    return pl.pallas_call(
        functools.partial(_moba_seq_kernel, nb=nb),
        grid=(B, nb),
        in_specs=[
            pl.BlockSpec((None, MOBA_BLOCK, dc), lambda b, j: (b, j, 0)),
            pl.BlockSpec((None, L, dc), lambda b, j: (b, 0, 1)),
            pl.BlockSpec((None, dc, L), lambda b, j: (b, 0, 0)),
            pl.BlockSpec(bias_tiles.shape, lambda b, j: (0, 0, 0, 0), pipeline_mode=pl.Buffered(1)),
        ],
        out_specs=pl.BlockSpec((None, MOBA_BLOCK, dc), lambda b, j: (b, j, 0)),
        out_shape=jax.ShapeDtypeStruct((B, L, dc), F32),
        scratch_shapes=[pltpu.VMEM((H_C, nb, HD_C), F32), pltpu.VMEM((H_C, L, HD_C), BF16),
                        pltpu.VMEM((H_C, HD_C, L), BF16), pltpu.VMEM((H_C, HD_C, MOBA_BLOCK), F32),
                        pltpu.VMEM((H_C, HD_C, MOBA_BLOCK), F32)],
        compiler_params=_cparams("parallel", "arbitrary"),
        name="moba_seq",
    )(qkv, qkv, vt, bias_tiles)


MOBA_PAGES_PER_BLOCK = MOBA_BLOCK // PAGE_SIZE
MOBA_STEP_PAGES = 16


def _moba_score_kernel(pt_ref, q_ref, *refs, n_pages):
    del pt_ref
    ppg, ppb = MOBA_STEP_PAGES, MOBA_PAGES_PER_BLOCK
    kp_refs = refs[:ppg]
    s_ref, sel_ref, gate_ref = refs[ppg:]
    step = pl.program_id(1)
    nbp = n_pages // ppb
    q = q_ref[...]
    inv = 1.0 / (MOBA_BLOCK * HD_C ** -0.5)
    gsum = None
    for r in range(ppg):
        sc = jnp.sum(kp_refs[r][...] * q, axis=1)
        s_ref[:, r * PAGE_SIZE:(r + 1) * PAGE_SIZE] = sc
        gs = jnp.sum(sc, axis=-1, keepdims=True)
        gsum = gs if r % ppb == 0 else gsum + gs
        if r % ppb == ppb - 1:
            gate_ref[step * (ppg // ppb) + r // ppb] = jnp.broadcast_to(gsum * inv, (H_C, LANES))

    @pl.when(step == n_pages // ppg - 1)
    def _():
        g = gate_ref[...]
        iota_b = lax.broadcasted_iota(jnp.int32, (nbp, H_C, LANES), 0).astype(F32)
        for i in range(MOBA_TOPK):
            m = jnp.max(g, axis=0, keepdims=True)
            idx = jnp.min(jnp.where(g == m, iota_b, 1e9), axis=0, keepdims=True)
            sel_ref[i] = idx[0]
            g = jnp.where(iota_b == idx, -jnp.inf, g)


def _moba_attend_kernel(pt_ref, sel_ref, q_ref, kn_ref, vn_ref, blast_ref, bsc_ref, *refs, nbp):
    del pt_ref
    nsel, ppb = MOBA_TOPK, MOBA_PAGES_PER_BLOCK
    s_refs, v_refs, o_ref = refs[:nsel], refs[nsel:nsel + nsel * ppb], refs[-1]
    b, h = pl.program_id(0), pl.program_id(1)
    far = bsc_ref[h, 0]
    s_new = jnp.sum(q_ref[...] * kn_ref[...], axis=-1, keepdims=True) * (HD_C ** -0.5) + bsc_ref[h, 1]
    ss = []
    own_row = lax.broadcasted_iota(jnp.int32, (H_C, MOBA_BLOCK), 0) == h
    for i in range(nsel):
        blk = sel_ref[(b * H_C + h) * nsel + i]
        s = jnp.sum(jnp.where(own_row, s_refs[i][...], 0.0), axis=0, keepdims=True)
        ss.append(s + jnp.where(blk == nbp - 1, blast_ref[...], far))
    s_all = jnp.concatenate(ss, axis=-1)
    m = jnp.maximum(s_new, jnp.max(s_all, axis=-1, keepdims=True))
    w_new = jnp.exp(s_new - m)
    e = jnp.exp(s_all - m)
    den = w_new + jnp.sum(e, axis=-1, keepdims=True)
    v_all = jnp.concatenate([v_refs[k][...] for k in range(nsel * ppb)], axis=-1)
    eb = jnp.broadcast_to(e, (SUBLANES, nsel * MOBA_BLOCK)).astype(BF16)
    num = w_new * vn_ref[...] + _dot_nt(eb, v_all.astype(BF16))[0:1]
    o_ref[...] = num / den


def _moba_step(q, k_new, v_new, cache_k, cache_v, page_table, bias_last, bias_scalars):
    B = q.shape[0]
    n_pages = page_table.shape[1]
    ppg, ppb, nsel = MOBA_STEP_PAGES, MOBA_PAGES_PER_BLOCK, MOBA_TOPK
    assert n_pages % ppg == 0 and ppg % ppb == 0
    nbp = n_pages // ppb
    assert nbp >= nsel
    pt = page_table.reshape(-1)
    q_rep = jnp.broadcast_to((q * (HD_C ** -0.5))[..., None], (B, H_C, HD_C, LANES))
    kpage = lambda r: pl.BlockSpec((None, H_C, HD_C, PAGE_SIZE),
                                   lambda b, s, pt, r=r: (pt[b * n_pages + s * ppg + r], 0, 0, 0))
    scores, sel = pl.pallas_call(
        functools.partial(_moba_score_kernel, n_pages=n_pages),
        grid_spec=pltpu.PrefetchScalarGridSpec(
            num_scalar_prefetch=1,
            grid=(B, n_pages // ppg),
            in_specs=[pl.BlockSpec((None, H_C, HD_C, LANES), lambda b, s, pt: (b, 0, 0, 0))]
                     + [kpage(r) for r in range(ppg)],
            out_specs=[pl.BlockSpec((None, H_C, ppg * PAGE_SIZE), lambda b, s, pt: (b, 0, s)),
                       pl.BlockSpec((None, nsel, H_C, LANES), lambda b, s, pt: (b, 0, 0, 0))],
            scratch_shapes=[pltpu.VMEM((nbp, H_C, LANES), F32)],
        ),
        out_shape=[jax.ShapeDtypeStruct((B, H_C, n_pages * PAGE_SIZE), F32),
                   jax.ShapeDtypeStruct((B, nsel, H_C, LANES), F32)],
        compiler_params=_cparams("parallel", "arbitrary"),
        name="moba_step_score",
    )(pt, q_rep, *([cache_k] * ppg))
    sel_flat = jnp.swapaxes(sel[:, :, :, 0], 1, 2).astype(jnp.int32).reshape(-1)
    tok = pl.BlockSpec((None, None, 1, HD_C), lambda b, h, pt, sl: (b, h, 0, 0))
    srow = lambda i: pl.BlockSpec((None, H_C, MOBA_BLOCK),
                                  lambda b, h, pt, sl, i=i: (b, 0, sl[(b * H_C + h) * nsel + i]))
    vpage = lambda i, r: pl.BlockSpec(
        (None, None, HD_C, PAGE_SIZE),
        lambda b, h, pt, sl, i=i, r=r: (pt[b * n_pages + sl[(b * H_C + h) * nsel + i] * ppb + r], h, 0, 0))
    row = lambda a: a.reshape(B, H_C, 1, HD_C)
    out = pl.pallas_call(
        functools.partial(_moba_attend_kernel, nbp=nbp),
        grid_spec=pltpu.PrefetchScalarGridSpec(
            num_scalar_prefetch=2,
            grid=(B, H_C),
            in_specs=[tok, tok, tok,
                      pl.BlockSpec((None, 1, MOBA_BLOCK), lambda b, h, pt, sl: (h, 0, 0)),
                      pl.BlockSpec(memory_space=pltpu.SMEM)]
                     + [srow(i) for i in range(nsel)]
                     + [vpage(i, r) for i in range(nsel) for r in range(ppb)],
            out_specs=tok,
        ),
        out_shape=jax.ShapeDtypeStruct((B, H_C, 1, HD_C), F32),
        compiler_params=_cparams("parallel", "arbitrary"),
        name="moba_step_attend",
    )(pt, sel_flat, row(q), row(k_new), row(v_new), bias_last, bias_scalars,
      *([scores] * nsel), *([cache_v] * (nsel * ppb)))
    return out.reshape(B, H_C, HD_C)


def _s5_kernel(u_ref, h0r_ref, h0i_ref, wb_ref, ar_ref, ai_ref, wc_ref, d_ref, wglu_ref,
               o_ref, hr_ref, hi_ref, x_ref, *, tt, nb):
    ti = pl.program_id(0)
    ns = D_S5_STATE

    @pl.when(ti == 0)
    def _():
        hr_ref[...] = h0r_ref[...]
        hi_ref[...] = h0i_ref[...]

    nslab = ns // LANES
    if tt == 1:
        seqs, seq_rows, mat_rows, cols = [0], lambda b: slice(None), lambda b: slice(None), lambda b: slice(None)
        u = u_ref[...]
    else:
        seqs = range(nb)
        seq_rows = lambda b: pl.ds(b, tt, stride=nb)
        mat_rows = lambda b: slice(b * tt, (b + 1) * tt)
        cols = lambda b: slice(b * D_S5, (b + 1) * D_S5)
        u = jnp.concatenate([u_ref[:, cols(b)] for b in seqs], axis=0)
    nsp = S5_MM_SPLIT
    du, dsx = D_S5 // nsp, ns // nsp
    ub = u.astype(BF16)
    for part in range(2):
        for g in range(nsp):
            xg = _dot(ub[:, g * du:(g + 1) * du], wb_ref[g * du:(g + 1) * du, part * ns + g * dsx:part * ns + (g + 1) * dsx])
            for b in seqs:
                for c in range(dsx // LANES):
                    slab = (part * ns + g * dsx) // LANES + c
                    x_ref[slab, seq_rows(b), :] = xg[mat_rows(b), c * LANES:(c + 1) * LANES]
    slabs = lambda ref: jnp.stack([ref[:, c * LANES:(c + 1) * LANES] for c in range(nslab)])
    ar = slabs(ar_ref)
    ai = slabs(ai_ref)

    def step(t, carry):
        hr, hi = carry
        rows = pl.ds(pl.multiple_of(t * nb, nb), nb)
        nr = ar * hr - ai * hi + x_ref[0:nslab, rows, :]
        ni = ar * hi + ai * hr + x_ref[nslab:2 * nslab, rows, :]
        x_ref[0:nslab, rows, :] = nr
        x_ref[nslab:2 * nslab, rows, :] = ni
        return nr, ni

    hr, hi = lax.fori_loop(0, tt, step, (slabs(hr_ref), slabs(hi_ref)))
    for c in range(nslab):
        hr_ref[:, c * LANES:(c + 1) * LANES] = hr[c]
        hi_ref[:, c * LANES:(c + 1) * LANES] = hi[c]

    def state_cols(first_slab, n):
        hs = [jnp.concatenate([x_ref[first_slab + c, seq_rows(b), :] for c in range(n)], axis=-1) for b in seqs]
        return (hs[0] if len(hs) == 1 else jnp.concatenate(hs, axis=0)).astype(BF16)

    ys = []
    for g in range(nsp):
        yg = None
        for part in range(2):
            hg = state_cols((part * ns + g * dsx) // LANES, dsx // LANES)
            t = _dot(hg, wc_ref[part * ns + g * dsx:part * ns + (g + 1) * dsx, g * du:(g + 1) * du])
            yg = t if yg is None else yg + t
        ys.append(yg)
    y = jax.nn.gelu(jnp.concatenate(ys, axis=-1) + d_ref[...] * u)
    gl = _dot(y.astype(BF16), wglu_ref[...])
    o = gl[:, 0:D_S5] * jax.nn.sigmoid(gl[:, D_S5:2 * D_S5])
    for b in seqs:
        o_ref[:, cols(b)] = o[mat_rows(b), :]


def _s5(u, h0r, h0i, wb, ar, ai, wc, d, wglu, *, tt, nb):
    L = u.shape[0]
    const = lambda a: pl.BlockSpec(a.shape, lambda i: (0,) * a.ndim)
    rows = pl.BlockSpec((tt, nb * D_S5), lambda i: (i, 0))
    st = pl.BlockSpec((nb, D_S5_STATE), lambda i: (0, 0))
    args = (u, h0r, h0i, wb, ar, ai, wc, d, wglu)
    if tt == 1:
        assert L == 1
        rows = pl.BlockSpec((nb, D_S5), lambda i: (0, 0))
        args = (u.reshape(nb, D_S5),) + args[1:]
    return pl.pallas_call(
        functools.partial(_s5_kernel, tt=tt, nb=nb),
        grid=(L // tt,),
        in_specs=[rows] + [const(a) for a in args[1:]],
        out_specs=[rows, st, st],
        out_shape=[jax.ShapeDtypeStruct(args[0].shape, F32),
                   jax.ShapeDtypeStruct((nb, D_S5_STATE), F32), jax.ShapeDtypeStruct((nb, D_S5_STATE), F32)],
        scratch_shapes=[pltpu.VMEM((2 * D_S5_STATE // LANES, tt * nb, LANES), F32)],
        compiler_params=_cparams("arbitrary"),
        name="s5",
    )(*args)


def _rel_bucket(dist):
    n = jnp.maximum(dist, 0)
    exact = N_BUCKETS // 2
    nf = jnp.maximum(n, exact).astype(F32)
    large = exact + (jnp.log(nf / exact) / math.log(MAX_DISTANCE / exact) * (N_BUCKETS - exact)).astype(jnp.int32)
    return jnp.where(n < exact, n, jnp.minimum(large, N_BUCKETS - 1))


def _block_diag(w):
    G, a, b = w.shape
    eye = jnp.eye(G, dtype=w.dtype)
    return (eye[:, None, :, None] * w[:, :, None, :]).reshape(G * a, G * b)


def _s5_weights(A_re, A_im, log_dt, B_re, B_im, C_re, C_im, D):
    dt = jnp.exp(log_dt.astype(F32))[:, None]
    lr, li = A_re.astype(F32), A_im.astype(F32)
    mag = jnp.exp(lr * dt)
    ar, ai = mag * jnp.cos(li * dt), mag * jnp.sin(li * dt)
    den = lr * lr + li * li
    nr = ar - 1.0
    cr, ci = (nr * lr + ai * li) / den, (ai * lr - nr * li) / den
    bbr = cr[..., None] * B_re - ci[..., None] * B_im
    bbi = cr[..., None] * B_im + ci[..., None] * B_re
    wb = jnp.concatenate([_block_diag(bbr.transpose(0, 2, 1)), _block_diag(bbi.transpose(0, 2, 1))], axis=1)
    wc = jnp.concatenate([_block_diag(C_re.transpose(0, 2, 1)), -_block_diag(C_im.transpose(0, 2, 1))], axis=0)
    return (wb.astype(BF16), ar.reshape(1, -1), ai.reshape(1, -1), wc.astype(BF16), D.reshape(1, -1).astype(F32))


def _pad_rows_front(a, rows):
    return jnp.pad(a, ((0, 0), (rows - a.shape[1], 0), (0, 0)))


def _trunk(x, st, W, *, decode):
    B, L, D = x.shape
    T = B * L
    tm = min(512, T)
    out = {}

    x2 = x.reshape(T, D)
    z = _norm_proj_wide(x2, W['norm_mix'][0], W['ev_w_in'], tm=tm, chunk=EV_IN_TILE, name="ev_in_proj")
    zc = z.shape[1]
    if decode:
        lp = SUBLANES
        z3 = jnp.pad(z.reshape(B, L, zc), ((0, 0), (0, lp - L), (0, 0)))
        tt = lp
    else:
        lp = L
        z3 = z.reshape(B, L, zc)
        tt = min(512, L)
    conv0 = _pad_rows_front(st['gdn_conv'], SUBLANES)
    mixed, tail, sg, sh_t = _even_mixer(z3, conv0, st['gdn'], jnp.swapaxes(st['hgrn'], -1, -2), W['gdn_conv_w'],
                                        W['gdn_neg_a'], W['gdn_dt_bias'], W['gdn_norm'], W['hgrn_norm'], W['hgrn_lb'],
                                        l_real=L, tt=tt)
    out['gdn_conv'] = tail[:, SUBLANES - (GDN_CONV - 1):, :]
    out['gdn'] = sg
    out['hgrn'] = jnp.swapaxes(sh_t, -1, -2)
    if decode:
        parts = [(mixed[:, :L].reshape(T, 2 * D_A), W['ev_w_out'], pl.BlockSpec((tm, 2 * D_A), lambda i: (i, 0)))]
    else:
        parts = [(mixed, W['ev_w_out'], pl.BlockSpec((None, tm, 2 * D_A), lambda b, i: (b, i, 0)))]
    x2, fs = _post_mixer(x2, parts, st, W, 0, B, L, decode)
    out['ffn0'] = fs

    dc = H_C * HD_C
    gqk = jnp.concatenate([jnp.tile(W['moba_gq'], H_C), jnp.tile(W['moba_gk'], H_C)])
    if decode:
        qkv, u_tm = _od_qkv_proj(x2, W['norm_mix'][1], W['od_w_in'], gqk, tm=tm)
        u_tm = u_tm.reshape(L, B * D_S5)
        out['moba_k'] = qkv[:, dc:2 * dc].reshape(B, L, H_C, HD_C)
        out['moba_v'] = qkv[:, 2 * dc:].reshape(B, L, H_C, HD_C)
        heads = lambda a: a.reshape(B, H_C, HD_C)
        o_c = _moba_step(heads(qkv[:, :dc]), heads(qkv[:, dc:2 * dc]), heads(qkv[:, 2 * dc:]),
                         st['cache_k'], st['cache_v'], st['page_table'], *W['moba_bias_step'])
        o_c = o_c.reshape(T, dc)
        tts = 1
    else:
        qkv, u_tm, kt, vt = _od_qkv_proj(x2, W['norm_mix'][1], W['od_w_in'], gqk, tm=tm, seq=(B, L))
        out['moba_k'] = jnp.transpose(kt.reshape(B, H_C, HD_C, L), (0, 3, 1, 2))
        out['moba_v'] = jnp.transpose(vt.reshape(B, H_C, HD_C, L), (0, 3, 1, 2))
        o_c = _moba_seq(qkv.reshape(B, L, 3 * dc), vt, W['moba_bias_tiles']).reshape(T, dc)
        tts = min(64, L)
    o_d, hr, hi = _s5(u_tm, st['s5_re'], st['s5_im'], *W['s5'], W['s5_glu_w'], tt=tts, nb=B)
    out['s5_re'] = hr.reshape(B, S5_GROUPS, S5_STATE)
    out['s5_im'] = hi.reshape(B, S5_GROUPS, S5_STATE)
    if decode:
        parts = [(o_c, W['od_w_out'][:dc], pl.BlockSpec((tm, dc), lambda i: (i, 0))),
                 (o_d, W['od_w_out'][dc:], pl.BlockSpec((tm, D_S5), lambda i: (i, 0)))]
    else:
        parts = [(o_c.reshape(B, L, dc), W['od_w_out'][:dc], pl.BlockSpec((None, tm, dc), lambda b, i: (b, i, 0))),
                 (o_d, W['od_w_out'][dc:], pl.BlockSpec((tm, D_S5), lambda b, i: (i, b)))]
    x2, fs = _post_mixer(x2, parts, st, W, 1, B, L, decode)
    out['ffn1'] = fs
    out['y'] = x2.reshape(B, L, D)
    return out


def _post_mixer(x2, parts, st, W, l, B, L, decode):
    T, D = x2.shape
    dq = MEM_H * MEM_HD
    buf = st['ffn_conv'][l]
    if decode:
        x2 = _proj_res(x2, parts, tm=T, name="mixer_out_proj")
        q = _norm_proj(x2, W['norm_mem'][l], W['mem_wq'][l], tm=T, tn=dq, n_norm_tiles=1, group=MEM_HD,
                       gain=jnp.tile(W['mem_gq'][l], MEM_H), name="mem_q_proj")
        o = _mem_attn_step(q.reshape(B, 1, dq), st['mem_k'], st['mem_v'], l).reshape(T, dq)
        x3 = _proj_res(x2, [(o, W['mem_wo'][l], pl.BlockSpec((T, dq), lambda i: (i, 0)))], tm=T, name="mem_out_proj")
        y, gate = _ffn_step(x3, buf[:, 0], buf[:, 1], W['norm_ffn'][l], W['ffn_w_gu'][l],
                            W['ffn_conv_w'][l], W['ffn_conv_b'][l], W['ffn_w_down'][l])
        fs = jnp.stack([buf[:, 1], gate], axis=1)
        return y, fs
    y, tail = _post_seq(x2.reshape(B, L, D), parts, st['mem_k'][l], st['mem_v'][l], W['norm_mem'][l], W['mem_wq'][l],
                        W['mem_gq'][l], W['mem_wo'][l], _pad_rows_front(buf, SUBLANES), W['norm_ffn'][l],
                        W['ffn_w_gu'][l], W['ffn_conv_w'][l], W['ffn_conv_b'][l], W['ffn_w_down'][l],
                        tm=min(512, L), name="post_mixer_seq")
    return y.reshape(T, D), tail[:, SUBLANES - (FFN_CONV - 1):, :]


def _mem_kv(mem, g_norm, w_kv, g_k):
    B, M, D = mem.shape
    dq = MEM_H * MEM_HD
    gain = jnp.concatenate([jnp.tile(g_k, MEM_H), jnp.ones((dq,), F32)])
    kv = _norm_proj(mem.reshape(B * M, D), g_norm, w_kv, tm=min(512, B * M), tn=dq, n_norm_tiles=1, group=MEM_HD,
                    gain=gain, name="mem_kv_proj")
    return kv[:, :dq].reshape(B, M, dq), kv[:, dq:].reshape(B, M, dq)


def kernel(x_prompt, x_sample, state_gdn_conv, state_gdn, state_hgrn, cache_moba_k, cache_moba_v, state_s5_re, state_s5_im, cache_mem_k, cache_mem_v, state_ffn_conv, page_table, mem_prompt, rel_bias, hgrn_lb_raw, norm_mix, norm_mem, norm_memkv, norm_ffn, mem_wq, mem_wkv, mem_gq, mem_gk, mem_wo, ffn_w_gu, ffn_conv_w, ffn_conv_b, ffn_w_down, ev_w_in, ev_w_out, gdn_conv_w, gdn_A_log, gdn_dt_bias, gdn_norm, hgrn_norm, od_w_in, od_w_out, moba_gq, moba_gk, s5_A_re, s5_A_im, s5_log_dt, s5_B_re, s5_B_im, s5_C_re, s5_C_im, s5_D, s5_glu_w):
    Bp, Lq, D = x_prompt.shape
    Bs = x_sample.shape[0]
    dc = H_C * HD_C
    dq = MEM_H * MEM_HD
    n_pages = page_table.shape[1]
    past = n_pages * PAGE_SIZE

    d3 = 3 * D_A
    w_in = ev_w_in[0]
    ba_cols = w_in[:, 4 * D_A:4 * D_A + 2 * H_A]
    w_in = jnp.concatenate([w_in[:, :4 * D_A], w_in[:, 4 * D_A + 2 * H_A:], ba_cols,
                            jnp.zeros((D, LANES - 2 * H_A), F32)], axis=1)
    lb_all = jnp.cumsum(jax.nn.softmax(hgrn_lb_raw.astype(F32), axis=0), axis=0)
    bias_t = rel_bias.T.astype(F32)
    nblk = MOBA_BLOCK
    by_rel = bias_t[:, _rel_bucket(jnp.arange(2 * nblk, dtype=jnp.int32))]
    mm = jnp.arange(2 * nblk, dtype=jnp.int32)

    def toeplitz(shift):
        y = by_rel[:, (shift + mm) % (2 * nblk)]
        flat = jnp.tile(y, (1, nblk))[:, :nblk * (2 * nblk - 1)]
        return flat.reshape(H_C, nblk, 2 * nblk - 1)[:, :, :nblk]

    far = bias_t[:, _rel_bucket(jnp.int32(2 * nblk))]
    tiles = jnp.stack([toeplitz(0), toeplitz(nblk)], axis=1) - far[:, None, None, None]
    bias_step = (by_rel[:, nblk - jnp.arange(nblk, dtype=jnp.int32)][:, None, :],
                 jnp.stack([far, bias_t[:, 0]], axis=1))
    W = dict(
        norm_mix=norm_mix, norm_mem=norm_mem, norm_ffn=norm_ffn,
        ev_w_in=w_in.astype(BF16), ev_w_out=ev_w_out[0].astype(BF16),
        gdn_conv_w=gdn_conv_w[0], gdn_neg_a=-jnp.exp(gdn_A_log[0].astype(F32)), gdn_dt_bias=gdn_dt_bias[0],
        gdn_norm=gdn_norm[0], hgrn_norm=hgrn_norm[0], hgrn_lb=lb_all[0],
        od_w_in=od_w_in[0].astype(BF16), od_w_out=od_w_out[0].astype(BF16),
        moba_gq=moba_gq[0], moba_gk=moba_gk[0], moba_bias_tiles=tiles, moba_bias_step=bias_step,
        s5=_s5_weights(s5_A_re[0], s5_A_im[0], s5_log_dt[0], s5_B_re[0], s5_B_im[0], s5_C_re[0], s5_C_im[0], s5_D[0]),
        s5_glu_w=s5_glu_w[0].astype(BF16),
        mem_wq=mem_wq.astype(BF16), mem_gq=mem_gq, mem_wo=mem_wo.astype(BF16),
        ffn_w_gu=ffn_w_gu.astype(BF16), ffn_conv_w=ffn_conv_w, ffn_conv_b=ffn_conv_b, ffn_w_down=ffn_w_down.astype(BF16),
    )

    mk_l, mv_l = [], []
    for l in range(2):
        mk, mv = _mem_kv(mem_prompt, norm_memkv[l], mem_wkv[l].astype(BF16), mem_gk[l])
        mk_l.append(mk)
        mv_l.append(mv)
    p_mem_k, p_mem_v = jnp.stack(mk_l), jnp.stack(mv_l)
    zeros = lambda *s: jnp.zeros(s, F32)
    st_p = dict(gdn_conv=zeros(Bp, GDN_CONV - 1, d3), gdn=zeros(Bp, H_A, HD_A, HD_A), hgrn=zeros(Bp, H_B, DK_B, DV_B),
                s5_re=zeros(Bp, D_S5_STATE), s5_im=zeros(Bp, D_S5_STATE), mem_k=p_mem_k, mem_v=p_mem_v,
                ffn_conv=zeros(2, Bp, FFN_CONV - 1, D_FF))
    op = _trunk(x_prompt, st_p, W, decode=False)

    st_s = dict(gdn_conv=state_gdn_conv[0], gdn=state_gdn[0], hgrn=state_hgrn[0],
                s5_re=state_s5_re[0].reshape(Bs, D_S5_STATE), s5_im=state_s5_im[0].reshape(Bs, D_S5_STATE),
                mem_k=cache_mem_k, mem_v=cache_mem_v,
                ffn_conv=state_ffn_conv,
                cache_k=jnp.transpose(cache_moba_k[0], (0, 2, 3, 1)), cache_v=jnp.transpose(cache_moba_v[0], (0, 2, 3, 1)),
                page_table=page_table)
    os_ = _trunk(x_sample, st_s, W, decode=True)

    M = mem_prompt.shape[1]
    return (op['y'], os_['y'],
            op['gdn_conv'][None], op['gdn'][None], op['hgrn'][None], op['moba_k'][None], op['moba_v'][None],
            op['s5_re'][None], op['s5_im'][None],
            p_mem_k.reshape(2, Bp, M, MEM_H, MEM_HD), p_mem_v.reshape(2, Bp, M, MEM_H, MEM_HD),
            jnp.stack([op['ffn0'], op['ffn1']]),
            os_['gdn_conv'][None], os_['gdn'][None], os_['hgrn'][None], os_['moba_k'][None], os_['moba_v'][None],
            os_['s5_re'][None], os_['s5_im'][None], jnp.stack([os_['ffn0'], os_['ffn1']]))
```

```python
import functools
import math

import jax
import jax.numpy as jnp
from jax import lax
from jax.experimental import pallas as pl
from jax.experimental.pallas import tpu as pltpu

F32 = jnp.float32
BF16 = jnp.bfloat16
HI = lax.Precision.HIGHEST
EPS = 1e-6

LANES = 128
SUBLANES = 8
VMEM_LIMIT_BYTES = 56 * 1024 * 1024

ROW_TILE = 512
S5_TIME_TILE = 64
H_A, HD_A, GDN_CONV = 8, 64, 4
H_B, DK_B, DV_B = 8, 64, 64
GDN_CHUNK, HGRN_CHUNK = 64, 32
GDN_GROUP, HGRN_GROUP = 4, 4
H_C, HD_C = 8, 64
MOBA_BLOCK, MOBA_TOPK = 256, 3
N_BUCKETS, MAX_DISTANCE = 32, 128
S5_GROUPS, S5_GROUP_CH, S5_STATE = 32, 16, 64
MEM_H, MEM_HD = 4, 128
D_FF, FFN_CONV = 2816, 3
PAGE_SIZE = 128
D_A = H_A * HD_A
D_S5 = S5_GROUPS * S5_GROUP_CH
D_S5_STATE = S5_GROUPS * S5_STATE
S5_MM_SPLIT = 4
FF_CHUNK = D_FF // 2
EV_IN_COLS = 8 * D_A + LANES
EV_IN_TILE = EV_IN_COLS // 3


def _cparams(*sem):
    return pltpu.CompilerParams(dimension_semantics=sem, vmem_limit_bytes=VMEM_LIMIT_BYTES)


def _dot(a, b, precision=None):
    return jnp.dot(a, b, preferred_element_type=F32, precision=precision)


def _dot_nt(a, b, precision=None):
    return lax.dot_general(a, b, (((1,), (1,)), ((), ())), preferred_element_type=F32, precision=precision)


def _dot_tn(a, b, precision=None):
    return lax.dot_general(a, b, (((0,), (0,)), ((), ())), preferred_element_type=F32, precision=precision)


def _rms(x, g):
    return x * lax.rsqrt(jnp.mean(x * x, axis=-1, keepdims=True) + EPS) * g


def _silu(x):
    return x * jax.nn.sigmoid(x)


def _tri(n, strict=False):
    r = lax.broadcasted_iota(jnp.int32, (n, n), 0)
    c = lax.broadcasted_iota(jnp.int32, (n, n), 1)
    return (r > c) if strict else (r >= c)


def _norm_proj_kernel(x_ref, g_ref, w_ref, e_ref, gain_ref, o_ref, *, n_norm_tiles, group):
    xn = _rms(x_ref[...], g_ref[...])
    z = _dot(xn.astype(BF16), w_ref[...])
    j = pl.program_id(1)

    @pl.when(j < n_norm_tiles)
    def _():
        zz = z * z
        hi = zz.astype(BF16)
        lo = (zz - hi.astype(F32)).astype(BF16)
        s = _dot(hi, e_ref[...]) + _dot(lo, e_ref[...])
        o_ref[...] = z * lax.rsqrt(s * (1.0 / group) + EPS) * gain_ref[...]

    @pl.when(j >= n_norm_tiles)
    def _():
        o_ref[...] = z


def _norm_proj(x, g, w, *, tm, tn, n_norm_tiles, group, gain, name):
    T, D = x.shape
    N = w.shape[1]
    nj = N // tn
    gi = lax.broadcasted_iota(jnp.int32, (tn, tn), 0) // group
    gj = lax.broadcasted_iota(jnp.int32, (tn, tn), 1) // group
    e = (gi == gj).astype(BF16)
    out_shape = jax.ShapeDtypeStruct((T, N), F32)
    out_spec = pl.BlockSpec((tm, tn), lambda i, j: (i, j))
    return pl.pallas_call(
        functools.partial(_norm_proj_kernel, n_norm_tiles=n_norm_tiles, group=group),
        grid=(T // tm, nj),
        in_specs=[
            pl.BlockSpec((tm, D), lambda i, j: (i, 0)),
            pl.BlockSpec((1, D), lambda i, j: (0, 0)),
            pl.BlockSpec((D, tn), lambda i, j: (0, j)),
            pl.BlockSpec((tn, tn), lambda i, j: (0, 0)),
            pl.BlockSpec((1, tn), lambda i, j: (0, j)),
        ],
        out_specs=out_spec,
        out_shape=out_shape,
        compiler_params=_cparams("parallel", "arbitrary"),
        name=name,
    )(x, g.reshape(1, D), w, e, gain.reshape(1, N))


def _norm_proj_wide_kernel(x_ref, g_ref, w_ref, o_ref, *, chunk):
    xn = _rms(x_ref[...], g_ref[...]).astype(BF16)
    for c in range(w_ref.shape[1] // chunk):
        cs = slice(c * chunk, (c + 1) * chunk)
        o_ref[:, cs] = _dot(xn, w_ref[:, cs])


def _norm_proj_wide(x, g, w, *, tm, chunk, name):
    T, D = x.shape
    N = w.shape[1]
    const = lambda shape: pl.BlockSpec(shape, lambda i: (0, 0), pipeline_mode=pl.Buffered(1))
    return pl.pallas_call(
        functools.partial(_norm_proj_wide_kernel, chunk=chunk),
        grid=(T // tm,),
        in_specs=[pl.BlockSpec((tm, D), lambda i: (i, 0)), const((1, D)), const((D, N))],
        out_specs=pl.BlockSpec((tm, N), lambda i: (i, 0)),
        out_shape=jax.ShapeDtypeStruct((T, N), F32),
        compiler_params=_cparams("parallel"),
        name=name,
    )(x, g.reshape(1, D), w)


def _od_qkv_kernel(x_ref, g_ref, w_ref, e_ref, gain_ref, o_ref, u_ref, *t_refs):
    dc = H_C * HD_C
    xn = _rms(x_ref[...], g_ref[...]).astype(BF16)
    u_ref[...] = _dot(xn, w_ref[:, 3 * dc:3 * dc + D_S5])
    for c in range(3):
        cs = slice(c * dc, (c + 1) * dc)
        z = _dot(xn, w_ref[:, cs])
        if c < 2:
            zz = z * z
            hi = zz.astype(BF16)
            lo = (zz - hi.astype(F32)).astype(BF16)
            s = _dot(hi, e_ref[...]) + _dot(lo, e_ref[...])
            z = z * lax.rsqrt(s * (1.0 / HD_C) + EPS) * gain_ref[:, cs]
        o_ref[:, cs] = z
        if t_refs and c > 0:
            t_refs[c - 1][...] = z.T


def _od_qkv_proj(x, g, w, gain, *, tm, seq=None):
    T, D = x.shape
    dc = H_C * HD_C
    gi = lax.broadcasted_iota(jnp.int32, (dc, dc), 0) // HD_C
    gj = lax.broadcasted_iota(jnp.int32, (dc, dc), 1) // HD_C
    e = (gi == gj).astype(BF16)
    const = lambda shape: pl.BlockSpec(shape, lambda i: (0, 0), pipeline_mode=pl.Buffered(1))
    out_specs = [pl.BlockSpec((tm, 3 * dc), lambda i: (i, 0))]
    out_shape = [jax.ShapeDtypeStruct((T, 3 * dc), F32)]
    if seq is None:
        out_specs += [pl.BlockSpec((tm, D_S5), lambda i: (i, 0))]
        out_shape += [jax.ShapeDtypeStruct((T, D_S5), F32)]
    else:
        B, L = seq
        nt = L // tm
        tspec = pl.BlockSpec((None, dc, tm), lambda i: (i // nt, 0, i % nt))
        out_specs += [pl.BlockSpec((tm, D_S5), lambda i: (i % nt, i // nt)), tspec, tspec]
        out_shape += [jax.ShapeDtypeStruct((L, B * D_S5), F32)] + [jax.ShapeDtypeStruct((B, dc, L), F32)] * 2
    return pl.pallas_call(
        _od_qkv_kernel,
        grid=(T // tm,),
        in_specs=[pl.BlockSpec((tm, D), lambda i: (i, 0)), const((1, D)), const((D, 3 * dc + D_S5)), const((dc, dc)),
                  const((1, 2 * dc))],
        out_specs=out_specs,
        out_shape=out_shape,
        compiler_params=_cparams("parallel"),
        name="od_in_proj_qkv",
    )(x, g.reshape(1, D), w, e, gain.reshape(1, 2 * dc))


def _proj_res_kernel(*refs, n_in):
    x_ref, o_ref = refs[0], refs[-1]
    acc = x_ref[...]
    for k in range(n_in):
        a_ref, w_ref = refs[1 + 2 * k], refs[2 + 2 * k]
        acc = acc + _dot(a_ref[...].astype(BF16), w_ref[...])
    o_ref[...] = acc


def _proj_res(x, parts, *, tm, name):
    T, N = x.shape
    in_specs = [pl.BlockSpec((tm, N), lambda i: (i, 0))]
    args = [x]
    for a, w, spec in parts:
        in_specs += [spec, pl.BlockSpec(w.shape, lambda i: (0, 0))]
        args += [a, w]
    return pl.pallas_call(
        functools.partial(_proj_res_kernel, n_in=len(parts)),
        grid=(T // tm,),
        in_specs=in_specs,
        out_specs=pl.BlockSpec((tm, N), lambda i: (i, 0)),
        out_shape=jax.ShapeDtypeStruct((T, N), F32),
        compiler_params=_cparams("parallel"),
        name=name,
    )(*args)


def _neumann_inverse(mats, n):
    m = mats[0].shape[0]
    eye = (lax.broadcasted_iota(jnp.int32, (m, m), 0) == lax.broadcasted_iota(jnp.int32, (m, m), 1)).astype(F32)
    ts = [eye - a for a in mats]
    ps = [a.astype(BF16) for a in mats]
    k = 2
    while k < n:
        ps = [_dot(p, p).astype(BF16) for p in ps]
        ts = [t + _dot(t.astype(BF16), p) for t, p in zip(ts, ps)]
        k *= 2
    return ts


def _even_mixer_kernel(qkv_ref, zg_ref, hq_ref, hf_ref, hi_ref, hg_ref, ba_ref,
                       conv0_ref, sg0_ref, sh0_ref,
                       convw_ref, nega_ref, dtb_ref, gnorm_ref, hnorm_ref, lb_ref,
                       mixed_ref, tail_ref, sg_ref, sh_ref,
                       xp_ref, beta_ref, g_ref, sbd_ref,
                       *, tt, cg, ch, l_real):
    ti = pl.program_id(1)

    @pl.when(ti == 0)
    def _():
        xp_ref[0:SUBLANES, :] = conv0_ref[...]
        sh_ref[...] = sh0_ref[...]
        zero = jnp.zeros((HD_A, HD_A), F32)
        for p in range(H_A // 2):
            sbd_ref[p] = jnp.concatenate([jnp.concatenate([sg0_ref[2 * p], zero], axis=-1),
                                          jnp.concatenate([zero, sg0_ref[2 * p + 1]], axis=-1)], axis=0)

    xp_ref[SUBLANES:SUBLANES + tt, :] = qkv_ref[...]
    lr_last = l_real - (l_real - 1) // tt * tt
    tail_ref[...] = xp_ref[lr_last:lr_last + SUBLANES, :]

    padded = (l_real % tt) != 0
    if padded:
        row = lax.broadcasted_iota(jnp.int32, (tt, 1), 0) + ti * tt
        valid = row < l_real
    ba = ba_ref[...]
    beta_all = jax.nn.sigmoid(ba)
    g_all = nega_ref[...] * jax.nn.softplus(ba + dtb_ref[...])
    if padded:
        beta_all = jnp.where(valid, beta_all, 0.0)
        g_all = jnp.where(valid, g_all, 0.0)
    beta_ref[...] = beta_all
    g_ref[...] = g_all

    ltri_g = _tri(cg).astype(F32)
    hnorm = hnorm_ref[...]
    gnorm2 = jnp.concatenate([gnorm_ref[...], gnorm_ref[...]], axis=-1)
    pairs = range(H_A // 2)
    n2 = 2 * cg
    lane = lax.broadcasted_iota(jnp.int32, (1, LANES), 1)
    m0 = (lane < HD_A).astype(F32)
    m1 = 1.0 - m0
    ri = lax.broadcasted_iota(jnp.int32, (n2, n2), 0)
    cj = lax.broadcasted_iota(jnp.int32, (n2, n2), 1)
    same_head = (ri // cg) == (cj // cg)
    bd_tril = same_head & (ri >= cj)
    bd_stril = same_head & (ri > cj)

    def stack2(a):
        return jnp.concatenate([a * m0, a * m1], axis=0)

    def rep2(col0, col1):
        return jnp.concatenate([jnp.broadcast_to(col0, (cg, LANES)), jnp.broadcast_to(col1, (cg, LANES))], axis=0)

    ng = min(GDN_GROUP, tt // cg)

    def gdn_group(gi, _):
        items = [(c, p) for c in range(ng) for p in pairs]
        rows_c, gcum_c, gt_c, bet_c, qkv_c = [], [], [], [], []
        for c in range(ng):
            r0 = pl.multiple_of((gi * ng + c) * cg, cg)
            rows = pl.ds(r0, cg)
            win = xp_ref[pl.ds(r0, cg + SUBLANES), :]
            conv = convw_ref[GDN_CONV - 1:GDN_CONV, :] * win[SUBLANES:SUBLANES + cg]
            for j in range(GDN_CONV - 1):
                off = SUBLANES - (GDN_CONV - 1) + j
                conv = conv + convw_ref[j:j + 1, :] * win[off:off + cg]
            qkv_c.append(_silu(conv))
            gcum = _dot(ltri_g, g_ref[rows, :], HI)
            gpad = gcum if cg == LANES else jnp.concatenate([gcum, jnp.zeros((LANES - cg, LANES), F32)], axis=0)
            rows_c.append(rows)
            gcum_c.append(gcum)
            gt_c.append(gpad.T)
            bet_c.append(beta_ref[rows, :])
        qs, ks, vs, betas, gcs, decays, glasts, glrows = {}, {}, {}, {}, {}, {}, {}, {}
        for it in items:
            c, p = it
            rows = rows_c[c]
            h0, h1 = H_A + 2 * p, H_A + 2 * p + 1
            q = stack2(qkv_c[c][:, p * LANES:(p + 1) * LANES])
            k = stack2(qkv_c[c][:, D_A + p * LANES:D_A + (p + 1) * LANES])
            qs[it] = q * lax.rsqrt(jnp.sum(q * q, axis=-1, keepdims=True) + EPS) * (HD_A ** -0.5)
            ks[it] = k * lax.rsqrt(jnp.sum(k * k, axis=-1, keepdims=True) + EPS)
            vs[it] = stack2(qkv_c[c][:, 2 * D_A + p * LANES:2 * D_A + (p + 1) * LANES])
            betas[it] = rep2(bet_c[c][:, 2 * p:2 * p + 1], bet_c[c][:, 2 * p + 1:2 * p + 2])
            gc = rep2(gcum_c[c][:, h0:h0 + 1], gcum_c[c][:, h1:h1 + 1])
            gr = jnp.concatenate([gt_c[c][h0:h0 + 1, 0:cg], gt_c[c][h1:h1 + 1, 0:cg]], axis=-1)
            gcs[it] = gc
            glasts[it] = rep2(gcum_c[c][cg - 1:cg, h0:h0 + 1], gcum_c[c][cg - 1:cg, h1:h1 + 1])
            glrows[it] = jnp.exp(glasts[it][0:1, :]) * m0 + jnp.exp(glasts[it][cg:cg + 1, :]) * m1
            decays[it] = jnp.exp(jnp.where(bd_tril, gc[:, 0:n2] - gr, -jnp.inf))
        kqs = {it: _dot_nt(jnp.concatenate([ks[it], qs[it]], axis=0).astype(BF16), ks[it].astype(BF16))
               for it in items}
        ts = dict(zip(items, _neumann_inverse(
            [jnp.where(bd_stril, betas[it][:, 0:n2] * kqs[it][0:n2] * decays[it], 0.0) for it in items], cg)))
        egs = {it: jnp.exp(gcs[it]) for it in items}
        sols = {it: _dot(ts[it].astype(BF16),
                         jnp.concatenate([betas[it] * vs[it], (betas[it] * egs[it]) * ks[it]], axis=-1).astype(BF16))
                for it in items}
        wqs = {it: jnp.concatenate([sols[it][:, LANES:2 * LANES], qs[it] * egs[it]], axis=0).astype(BF16)
               for it in items}
        qks = {it: (kqs[it][n2:2 * n2] * decays[it]).astype(BF16) for it in items}
        kouts = {it: (ks[it] * jnp.exp(glasts[it] - gcs[it])).astype(BF16) for it in items}
        ss = [sbd_ref[p] for p in pairs]
        for c in range(ng):
            wss = [_dot(wqs[(c, p)], ss[p].astype(BF16)) for p in pairs]
            ubs = [(sols[(c, p)][:, 0:LANES] - wss[p][0:n2]).astype(BF16) for p in pairs]
            os_ = [wss[p][n2:2 * n2] + _dot(qks[(c, p)], ubs[p]) for p in pairs]
            ss = [ss[p] * glrows[(c, p)] + _dot_tn(kouts[(c, p)], ubs[p]) for p in pairs]
            for p in pairs:
                o = os_[p]
                on = o * lax.rsqrt(jnp.sum(o * o, axis=-1, keepdims=True) * (1.0 / HD_A) + EPS)
                cols = slice(p * LANES, (p + 1) * LANES)
                mixed_ref[rows_c[c], cols] = (on[0:cg] + on[cg:n2]) * gnorm2 * _silu(zg_ref[rows_c[c], cols])
        for p in pairs:
            sbd_ref[p] = ss[p]
        return 0

    lax.fori_loop(0, tt // (cg * ng), gdn_group, 0)
    xp_ref[0:SUBLANES, :] = xp_ref[tt:tt + SUBLANES, :]
    for p in pairs:
        sg_ref[2 * p] = sbd_ref[p, 0:HD_A, 0:HD_A]
        sg_ref[2 * p + 1] = sbd_ref[p, HD_A:2 * HD_A, HD_A:2 * HD_A]

    nh = min(HGRN_GROUP, tt // ch)
    gr = nh * ch
    ri = lax.broadcasted_iota(jnp.int32, (gr, gr), 0)
    cj = lax.broadcasted_iota(jnp.int32, (gr, gr), 1)
    same_chunk_causal = (ri >= cj) & ((ri // ch) == (cj // ch))
    ltri_h = same_chunk_causal.astype(F32)
    lb = lb_ref[...]
    r_mid = (ch - 1) // 2

    def hgrn_group(gi, _):
        r0 = pl.multiple_of(gi * gr, gr)
        rows = pl.ds(r0, gr)
        f = lb + (1.0 - lb) * jax.nn.sigmoid(hf_ref[rows, :])
        logf = jnp.log(f)
        kk = 1.0 - f
        if padded:
            vrow = (lax.broadcasted_iota(jnp.int32, (gr, 1), 0) + r0 + ti * tt) < l_real
            logf = jnp.where(vrow, logf, 0.0)
            kk = jnp.where(vrow, kk, 0.0)
        qh = _silu(hq_ref[rows, :])
        bc = _dot(ltri_h, logf, HI).reshape(nh, ch, D_A)
        bref = bc[:, r_mid:r_mid + 1, :]
        blast = bc[:, ch - 1:ch, :]
        grp = lambda a: a.reshape(gr, D_A)
        qh3, kk3 = qh.reshape(nh, ch, D_A), kk.reshape(nh, ch, D_A)
        qa = grp(qh3 * jnp.exp(bc - bref)).astype(BF16)
        kb = grp(kk3 * jnp.exp(bref - bc)).astype(BF16)
        q_in = grp(qh3 * jnp.exp(bc)).astype(BF16)
        k_out = grp(kk3 * jnp.exp(blast - bc)).astype(BF16)
        fl = jnp.exp(blast)
        vv = hi_ref[rows, :].astype(BF16)
        heads = range(H_B)
        sls = [slice(h * DK_B, (h + 1) * DK_B) for h in heads]
        atts = [jnp.where(same_chunk_causal, _dot_nt(qa[:, sl], kb[:, sl]), 0.0).astype(BF16) for sl in sls]
        intra = [_dot(atts[h], vv[:, sls[h]]) for h in heads]
        sts = [sh_ref[h] for h in heads]
        inter = [[] for _ in heads]
        for c in range(nh):
            rc = slice(c * ch, (c + 1) * ch)
            for h in heads:
                inter[h].append(_dot_nt(q_in[rc, sls[h]], sts[h].astype(BF16)))
            sts = [sts[h] * fl[c][:, sls[h]] + _dot_tn(vv[rc, sls[h]], k_out[rc, sls[h]]) for h in heads]
        for h in heads:
            sh_ref[h] = sts[h]
        os_ = [intra[h] + (inter[h][0] if nh == 1 else jnp.concatenate(inter[h], axis=0)) for h in heads]
        outs = [_rms(os_[h], hnorm) * _silu(hg_ref[rows, h * DV_B:(h + 1) * DV_B]) for h in heads]
        for p in range(H_B // 2):
            c0 = D_A + p * LANES
            mixed_ref[rows, c0:c0 + LANES] = jnp.concatenate([outs[2 * p], outs[2 * p + 1]], axis=-1)
        return 0

    lax.fori_loop(0, tt // gr, hgrn_group, 0)


def _even_mixer(z, conv0, sg0, sh0_t, conv_w, neg_a, dt_bias, gnorm, hnorm, lb, *, l_real, tt):
    B, Lp, _ = z.shape
    cg = min(GDN_CHUNK, tt)
    ch = min(HGRN_CHUNK, tt)
    d3 = 3 * D_A
    nt = Lp // tt
    col = lambda w, idx: pl.BlockSpec((None, tt, w), lambda b, i, idx=idx: (b, i, idx))
    state = lambda: pl.BlockSpec((None, H_A, HD_A, HD_A), lambda b, i: (b, 0, 0, 0))
    vec = lambda n: pl.BlockSpec((1, n), lambda b, i: (0, 0))
    nega_row = jnp.zeros((1, LANES), F32).at[0, H_A:2 * H_A].set(neg_a)
    dtb_row = jnp.zeros((1, LANES), F32).at[0, H_A:2 * H_A].set(dt_bias)
    return pl.pallas_call(
        functools.partial(_even_mixer_kernel, tt=tt, cg=cg, ch=ch, l_real=l_real),
        grid=(B, nt),
        in_specs=[
            col(d3, 0), col(D_A, 3), col(D_A, 4), col(D_A, 5), col(D_A, 6), col(D_A, 7), col(LANES, 32),
            pl.BlockSpec((None, SUBLANES, d3), lambda b, i: (b, 0, 0)),
            state(), state(),
            pl.BlockSpec((GDN_CONV, d3), lambda b, i: (0, 0)),
            vec(LANES), vec(LANES), vec(HD_A), vec(DV_B), vec(D_A),
        ],
        out_specs=[
            pl.BlockSpec((None, tt, 2 * D_A), lambda b, i: (b, i, 0)),
            pl.BlockSpec((None, SUBLANES, d3), lambda b, i: (b, 0, 0)),
            state(), state(),
        ],
        out_shape=[
            jax.ShapeDtypeStruct((B, Lp, 2 * D_A), F32),
            jax.ShapeDtypeStruct((B, SUBLANES, d3), F32),
            jax.ShapeDtypeStruct((B, H_A, HD_A, HD_A), F32),
            jax.ShapeDtypeStruct((B, H_B, DV_B, DK_B), F32),
        ],
        scratch_shapes=[
            pltpu.VMEM((tt + SUBLANES, d3), F32),
            pltpu.VMEM((tt, LANES), F32),
            pltpu.VMEM((tt, LANES), F32),
            pltpu.VMEM((H_A // 2, 2 * HD_A, 2 * HD_A), F32),
        ],
        compiler_params=_cparams("parallel", "arbitrary"),
        name="even_mixer",
    )(z, z, z, z, z, z, z, conv0, sg0, sh0_t, conv_w, nega_row, dtb_row,
      gnorm.reshape(1, HD_A), hnorm.reshape(1, DV_B), lb.reshape(1, D_A))


def _mem_attn_rows(x, mk_ref, mv_ref, g_ref, wq_ref, gq_ref, wo_ref):
    q = _dot(_rms(x, g_ref[...]).astype(BF16), wq_ref[...])
    scale = MEM_HD ** -0.5
    outs = []
    for h in range(MEM_H):
        sl = slice(h * MEM_HD, (h + 1) * MEM_HD)
        qh = _rms(q[:, sl], gq_ref[...])
        s = _dot_nt(qh.astype(BF16), mk_ref[:, sl].astype(BF16)) * scale
        s = s - jnp.max(s, axis=-1, keepdims=True)
        p = jnp.exp(s)
        p = p / jnp.sum(p, axis=-1, keepdims=True)
        outs.append(_dot(p.astype(BF16), mv_ref[:, sl].astype(BF16)))
    o = jnp.concatenate(outs, axis=-1)
    return x + _dot(o.astype(BF16), wo_ref[...])


def _mem_attn_step_kernel(q_ref, mk_ref, mv_ref, o_ref):
    scale = MEM_HD ** -0.5
    outs = []
    for h in range(MEM_H):
        qh = jnp.broadcast_to(q_ref[:, h * MEM_HD:(h + 1) * MEM_HD], (SUBLANES, MEM_HD))
        s = _dot_nt(qh.astype(BF16), mk_ref[:, h, :].astype(BF16)) * scale
        s = s - jnp.max(s, axis=-1, keepdims=True)
        p = jnp.exp(s)
        p = p / jnp.sum(p, axis=-1, keepdims=True)
        outs.append(_dot(p.astype(BF16), mv_ref[:, h, :].astype(BF16))[0:1])
    o_ref[...] = jnp.concatenate(outs, axis=-1)


def _mem_attn_step(q, cache_k, cache_v, l):
    B = q.shape[0]
    M = cache_k.shape[2]
    dq = MEM_H * MEM_HD
    mem = pl.BlockSpec((None, None, M, MEM_H, MEM_HD), lambda b: (l, b, 0, 0, 0))
    row = pl.BlockSpec((None, 1, dq), lambda b: (b, 0, 0))
    return pl.pallas_call(
        _mem_attn_step_kernel,
        grid=(B,),
        in_specs=[row, mem, mem],
        out_specs=row,
        out_shape=jax.ShapeDtypeStruct((B, 1, dq), F32),
        compiler_params=_cparams("parallel"),
        name="mem_attn_step",
    )(q, cache_k, cache_v)


def _post_seq_kernel(*refs, n_parts, tm):
    x_ref = refs[0]
    parts = refs[1:1 + 2 * n_parts]
    (mk_ref, mv_ref, gmem_ref, wq_ref, gq_ref, wo_ref, buf0_ref, g_ref, wgu_ref, cw_ref, cb_ref, wd_ref,
     o_ref, tail_ref, gp_ref, carry_ref) = refs[1 + 2 * n_parts:]
    ti = pl.program_id(1)

    @pl.when(ti == 0)
    def _():
        carry_ref[...] = buf0_ref[...]

    x = x_ref[...]
    for k in range(n_parts):
        x = x + _dot(parts[2 * k][...].astype(BF16), parts[2 * k + 1][...])
    x = _mem_attn_rows(x, mk_ref, mv_ref, gmem_ref, wq_ref, gq_ref, wo_ref)
    xn = _rms(x, g_ref[...]).astype(BF16)
    acc = x
    for c in range(D_FF // FF_CHUNK):
        cs = slice(c * FF_CHUNK, (c + 1) * FF_CHUNK)
        gate = _dot(xn, wgu_ref[:, cs])
        up = _dot(xn, wgu_ref[:, D_FF + c * FF_CHUNK:D_FF + (c + 1) * FF_CHUNK])
        gp_ref[0:SUBLANES, :] = carry_ref[:, cs]
        gp_ref[SUBLANES:SUBLANES + tm, :] = gate
        conv = cw_ref[FFN_CONV - 1:FFN_CONV, cs] * gate + cb_ref[:, cs]
        for j in range(FFN_CONV - 1):
            off = SUBLANES - (FFN_CONV - 1) + j
            conv = conv + cw_ref[j:j + 1, cs] * gp_ref[off:off + tm, :]
        last = gp_ref[tm:tm + SUBLANES, :]
        carry_ref[:, cs] = last
        tail_ref[:, cs] = last
        act = _silu(conv) * up
        acc = acc + _dot(act.astype(BF16), wd_ref[cs, :])
    o_ref[...] = acc


def _post_seq(x, parts, mk, mv, gmem, wq, gq, wo, buf0, g, wgu, cw, cb, wd, *, tm, name):
    B, L, D = x.shape
    M = mk.shape[1]
    dq = MEM_H * MEM_HD
    const = lambda shape: pl.BlockSpec(shape, lambda b, i: (0,) * len(shape), pipeline_mode=pl.Buffered(1))
    row_tile = pl.BlockSpec((None, tm, D), lambda b, i: (b, i, 0))
    mem = pl.BlockSpec((None, M, dq), lambda b, i: (b, 0, 0))
    in_specs, args = [row_tile], [x]
    for a, w, spec in parts:
        in_specs += [spec, const(w.shape)]
        args += [a, w]
    in_specs += [mem, mem, const((1, D)), const((D, dq)), const((1, MEM_HD)), const((dq, D)),
                 pl.BlockSpec((None, SUBLANES, D_FF), lambda b, i: (b, 0, 0)),
                 const((1, D)), const((D, 2 * D_FF)), const((FFN_CONV, D_FF)), const((1, D_FF)), const((D_FF, D))]
    args += [mk, mv, gmem.reshape(1, D), wq, gq.reshape(1, MEM_HD), wo, buf0, g.reshape(1, D), wgu, cw,
             cb.reshape(1, D_FF), wd]
    return pl.pallas_call(
        functools.partial(_post_seq_kernel, n_parts=len(parts), tm=tm),
        grid=(B, L // tm),
        in_specs=in_specs,
        out_specs=[row_tile, pl.BlockSpec((None, SUBLANES, D_FF), lambda b, i: (b, 0, 0))],
        out_shape=[jax.ShapeDtypeStruct((B, L, D), F32), jax.ShapeDtypeStruct((B, SUBLANES, D_FF), F32)],
        scratch_shapes=[pltpu.VMEM((tm + SUBLANES, FF_CHUNK), F32), pltpu.VMEM((SUBLANES, D_FF), F32)],
        compiler_params=_cparams("parallel", "arbitrary"),
        name=name,
    )(*args)


def _ffn_step_kernel(x_ref, p2_ref, p1_ref, g_ref, wgu_ref, cw_ref, cb_ref, wd_ref, o_ref, gate_ref):
    x = x_ref[...]
    xn = _rms(x, g_ref[...]).astype(BF16)
    acc = x
    for c in range(D_FF // FF_CHUNK):
        cs = slice(c * FF_CHUNK, (c + 1) * FF_CHUNK)
        gate = _dot(xn, wgu_ref[:, cs])
        up = _dot(xn, wgu_ref[:, D_FF + c * FF_CHUNK:D_FF + (c + 1) * FF_CHUNK])
        gate_ref[:, cs] = gate
        conv = cw_ref[0:1, cs] * p2_ref[:, cs] + cw_ref[1:2, cs] * p1_ref[:, cs] + cw_ref[2:3, cs] * gate + cb_ref[:, cs]
        act = _silu(conv) * up
        acc = acc + _dot(act.astype(BF16), wd_ref[cs, :])
    o_ref[...] = acc


def _ffn_step(x, p2, p1, g, wgu, cw, cb, wd):
    R, D = x.shape
    full = lambda a: pl.BlockSpec(a.shape, lambda i: (0,) * a.ndim)
    args = (x, p2, p1, g.reshape(1, D), wgu, cw, cb.reshape(1, D_FF), wd)
    return pl.pallas_call(
        _ffn_step_kernel,
        grid=(1,),
        in_specs=[full(a) for a in args],
        out_specs=[pl.BlockSpec((R, D), lambda i: (0, 0)), pl.BlockSpec((R, D_FF), lambda i: (0, 0))],
        out_shape=[jax.ShapeDtypeStruct((R, D), F32), jax.ShapeDtypeStruct((R, D_FF), F32)],
        compiler_params=_cparams("arbitrary"),
        name="conv_ffn_step",
    )(*args)


def _moba_query_tile(j, hs, qt_ref, kb_ref, vt_ref, kmean_ref, bias_ref, ot_ref):
    blk = MOBA_BLOCK
    w = (j + 1) * blk
    qts = [qt_ref[h] for h in hs]
    sts = [_dot(kb_ref[h, 0:w, :], (qt * (HD_C ** -0.5)).astype(BF16)) for h, qt in zip(hs, qts)]
    sels = []
    if j > MOBA_TOPK:
        iota_n = lax.broadcasted_iota(jnp.int32, (j, blk), 0).astype(F32)
        gs = [_dot(kmean_ref[h, 0:j, :], qt, HI) for h, qt in zip(hs, qts)]
        for g in gs:
            sel = jnp.zeros((j, blk), F32)
            for _ in range(MOBA_TOPK):
                m = jnp.max(g, axis=0, keepdims=True)
                idx = jnp.min(jnp.where(g == m, iota_n, 1e9), axis=0, keepdims=True)
                hit = iota_n == idx
                sel = jnp.where(hit, 1.0, sel)
                g = jnp.where(hit, -jnp.inf, g)
            sels.append(sel)
    key_le_query = lax.broadcasted_iota(jnp.int32, (blk, blk), 0) <= lax.broadcasted_iota(jnp.int32, (blk, blk), 1)
    pcats, ls = [], []
    for i, h in enumerate(hs):
        parts = []
        for n in range(j + 1):
            s = sts[i][n * blk:(n + 1) * blk, :]
            if j - n < 2:
                s = s + bias_ref[h, j - n]
            if n == j:
                s = jnp.where(key_le_query, s, -jnp.inf)
            elif j > MOBA_TOPK:
                s = jnp.where(sels[i][n:n + 1, :] > 0.5, s, -jnp.inf)
            parts.append(s)
        m = jnp.max(parts[0], axis=0, keepdims=True)
        for s in parts[1:]:
            m = jnp.maximum(m, jnp.max(s, axis=0, keepdims=True))
        ps = [jnp.exp(s - m) for s in parts]
        l = jnp.sum(ps[0], axis=0, keepdims=True)
        for p in ps[1:]:
            l = l + jnp.sum(p, axis=0, keepdims=True)
        pcats.append((ps[0] if j == 0 else jnp.concatenate(ps, axis=0)).astype(BF16))
        ls.append(l)
    outs = [_dot(vt_ref[h, :, 0:w], pc) for h, pc in zip(hs, pcats)]
    for h, o, l in zip(hs, outs, ls):
        ot_ref[h] = o / l


def _moba_seq_kernel(q_ref, k_ref, vtin_ref, bias_ref, o_ref, kmean_ref, kb_ref, vt_ref, qt_ref, ot_ref, *, nb):
    j = pl.program_id(1)
    blk = MOBA_BLOCK
    dc = H_C * HD_C

    @pl.when(j == 0)
    def _():
        for n in range(nb):
            km = jnp.mean(k_ref[n * blk:(n + 1) * blk, :], axis=0, keepdims=True)
            for h in range(H_C):
                kmean_ref[h, n:n + 1, :] = km[:, h * HD_C:(h + 1) * HD_C]

        def prep(n, _):
            rows = pl.ds(pl.multiple_of(n * blk, blk), blk)
            kblk = k_ref[rows, :]
            vt_ref[:, :, rows] = vtin_ref[:, rows].astype(BF16).reshape(H_C, HD_C, blk)
            for h in range(H_C):
                kb_ref[h, rows, :] = kblk[:, h * HD_C:(h + 1) * HD_C].astype(BF16)
            return 0
        lax.fori_loop(0, nb, prep, 0)

    qt_ref[...] = q_ref[...].T.reshape(H_C, HD_C, blk)
    for jj in range(nb):
        @pl.when(j == jj)
        def _(jj=jj):
            def head_pair(hp, _):
                _moba_query_tile(jj, [2 * hp, 2 * hp + 1], qt_ref, kb_ref, vt_ref, kmean_ref, bias_ref, ot_ref)
                return 0
            lax.fori_loop(0, H_C // 2, head_pair, 0)
    o_ref[...] = ot_ref[...].reshape(dc, blk).T


def _moba_seq(qkv, vt, bias_tiles):
    B, L, _ = qkv.shape
    nb = L // MOBA_BLOCK
    dc = H_C * HD_C
    return pl.pallas_call(
        functools.partial(_moba_seq_kernel, nb=nb),
        grid=(B, nb),
        in_specs=[
            pl.BlockSpec((None, MOBA_BLOCK, dc), lambda b, j: (b, j, 0)),
            pl.BlockSpec((None, L, dc), lambda b, j: (b, 0, 1)),
            pl.BlockSpec((None, dc, L), lambda b, j: (b, 0, 0)),
            pl.BlockSpec(bias_tiles.shape, lambda b, j: (0, 0, 0, 0), pipeline_mode=pl.Buffered(1)),
        ],
        out_specs=pl.BlockSpec((None, MOBA_BLOCK, dc), lambda b, j: (b, j, 0)),
        out_shape=jax.ShapeDtypeStruct((B, L, dc), F32),
        scratch_shapes=[pltpu.VMEM((H_C, nb, HD_C), F32), pltpu.VMEM((H_C, L, HD_C), BF16),
                        pltpu.VMEM((H_C, HD_C, L), BF16), pltpu.VMEM((H_C, HD_C, MOBA_BLOCK), F32),
                        pltpu.VMEM((H_C, HD_C, MOBA_BLOCK), F32)],
        compiler_params=_cparams("parallel", "arbitrary"),
        name="moba_seq",
    )(qkv, qkv, vt, bias_tiles)


MOBA_PAGES_PER_BLOCK = MOBA_BLOCK // PAGE_SIZE
MOBA_STEP_PAGES = 16


def _moba_score_kernel(pt_ref, q_ref, *refs, n_pages):
    del pt_ref
    ppg, ppb = MOBA_STEP_PAGES, MOBA_PAGES_PER_BLOCK
    kp_refs = refs[:ppg]
    s_ref, sel_ref, gate_ref = refs[ppg:]
    step = pl.program_id(1)
    nbp = n_pages // ppb
    q = q_ref[...]
    inv = 1.0 / (MOBA_BLOCK * HD_C ** -0.5)
    gsum = None
    for r in range(ppg):
        sc = jnp.sum(kp_refs[r][...] * q, axis=1)
        s_ref[:, r * PAGE_SIZE:(r + 1) * PAGE_SIZE] = sc
        gs = jnp.sum(sc, axis=-1, keepdims=True)
        gsum = gs if r % ppb == 0 else gsum + gs
        if r % ppb == ppb - 1:
            gate_ref[step * (ppg // ppb) + r // ppb] = jnp.broadcast_to(gsum * inv, (H_C, LANES))

    @pl.when(step == n_pages // ppg - 1)
    def _():
        g = gate_ref[...]
        iota_b = lax.broadcasted_iota(jnp.int32, (nbp, H_C, LANES), 0).astype(F32)
        for i in range(MOBA_TOPK):
            m = jnp.max(g, axis=0, keepdims=True)
            idx = jnp.min(jnp.where(g == m, iota_b, 1e9), axis=0, keepdims=True)
            sel_ref[i] = idx[0]
            g = jnp.where(iota_b == idx, -jnp.inf, g)


def _moba_attend_kernel(pt_ref, sel_ref, q_ref, kn_ref, vn_ref, blast_ref, bsc_ref, *refs, nbp):
    del pt_ref
    nsel, ppb = MOBA_TOPK, MOBA_PAGES_PER_BLOCK
    s_refs, v_refs, o_ref = refs[:nsel], refs[nsel:nsel + nsel * ppb], refs[-1]
    b, h = pl.program_id(0), pl.program_id(1)
    far = bsc_ref[h, 0]
    s_new = jnp.sum(q_ref[...] * kn_ref[...], axis=-1, keepdims=True) * (HD_C ** -0.5) + bsc_ref[h, 1]
    ss = []
    own_row = lax.broadcasted_iota(jnp.int32, (H_C, MOBA_BLOCK), 0) == h
    for i in range(nsel):
        blk = sel_ref[(b * H_C + h) * nsel + i]
        s = jnp.sum(jnp.where(own_row, s_refs[i][...], 0.0), axis=0, keepdims=True)
        ss.append(s + jnp.where(blk == nbp - 1, blast_ref[...], far))
    s_all = jnp.concatenate(ss, axis=-1)
    m = jnp.maximum(s_new, jnp.max(s_all, axis=-1, keepdims=True))
    w_new = jnp.exp(s_new - m)
    e = jnp.exp(s_all - m)
    den = w_new + jnp.sum(e, axis=-1, keepdims=True)
    v_all = jnp.concatenate([v_refs[k][...] for k in range(nsel * ppb)], axis=-1)
    eb = jnp.broadcast_to(e, (SUBLANES, nsel * MOBA_BLOCK)).astype(BF16)
    num = w_new * vn_ref[...] + _dot_nt(eb, v_all.astype(BF16))[0:1]
    o_ref[...] = num / den


def _moba_step(q, k_new, v_new, cache_k, cache_v, page_table, bias_last, bias_scalars):
    B = q.shape[0]
    n_pages = page_table.shape[1]
    ppg, ppb, nsel = MOBA_STEP_PAGES, MOBA_PAGES_PER_BLOCK, MOBA_TOPK
    assert n_pages % ppg == 0 and ppg % ppb == 0
    nbp = n_pages // ppb
    assert nbp >= nsel
    pt = page_table.reshape(-1)
    q_rep = jnp.broadcast_to((q * (HD_C ** -0.5))[..., None], (B, H_C, HD_C, LANES))
    kpage = lambda r: pl.BlockSpec((None, H_C, HD_C, PAGE_SIZE),
                                   lambda b, s, pt, r=r: (pt[b * n_pages + s * ppg + r], 0, 0, 0))
    scores, sel = pl.pallas_call(
        functools.partial(_moba_score_kernel, n_pages=n_pages),
        grid_spec=pltpu.PrefetchScalarGridSpec(
            num_scalar_prefetch=1,
            grid=(B, n_pages // ppg),
            in_specs=[pl.BlockSpec((None, H_C, HD_C, LANES), lambda b, s, pt: (b, 0, 0, 0))]
                     + [kpage(r) for r in range(ppg)],
            out_specs=[pl.BlockSpec((None, H_C, ppg * PAGE_SIZE), lambda b, s, pt: (b, 0, s)),
                       pl.BlockSpec((None, nsel, H_C, LANES), lambda b, s, pt: (b, 0, 0, 0))],
            scratch_shapes=[pltpu.VMEM((nbp, H_C, LANES), F32)],
        ),
        out_shape=[jax.ShapeDtypeStruct((B, H_C, n_pages * PAGE_SIZE), F32),
                   jax.ShapeDtypeStruct((B, nsel, H_C, LANES), F32)],
        compiler_params=_cparams("parallel", "arbitrary"),
        name="moba_step_score",
    )(pt, q_rep, *([cache_k] * ppg))
    sel_flat = jnp.swapaxes(sel[:, :, :, 0], 1, 2).astype(jnp.int32).reshape(-1)
    tok = pl.BlockSpec((None, None, 1, HD_C), lambda b, h, pt, sl: (b, h, 0, 0))
    srow = lambda i: pl.BlockSpec((None, H_C, MOBA_BLOCK),
                                  lambda b, h, pt, sl, i=i: (b, 0, sl[(b * H_C + h) * nsel + i]))
    vpage = lambda i, r: pl.BlockSpec(
        (None, None, HD_C, PAGE_SIZE),
        lambda b, h, pt, sl, i=i, r=r: (pt[b * n_pages + sl[(b * H_C + h) * nsel + i] * ppb + r], h, 0, 0))
    row = lambda a: a.reshape(B, H_C, 1, HD_C)
    out = pl.pallas_call(
        functools.partial(_moba_attend_kernel, nbp=nbp),
        grid_spec=pltpu.PrefetchScalarGridSpec(
            num_scalar_prefetch=2,
            grid=(B, H_C),
            in_specs=[tok, tok, tok,
                      pl.BlockSpec((None, 1, MOBA_BLOCK), lambda b, h, pt, sl: (h, 0, 0)),
                      pl.BlockSpec(memory_space=pltpu.SMEM)]
                     + [srow(i) for i in range(nsel)]
                     + [vpage(i, r) for i in range(nsel) for r in range(ppb)],
            out_specs=tok,
        ),
        out_shape=jax.ShapeDtypeStruct((B, H_C, 1, HD_C), F32),
        compiler_params=_cparams("parallel", "arbitrary"),
        name="moba_step_attend",
    )(pt, sel_flat, row(q), row(k_new), row(v_new), bias_last, bias_scalars,
      *([scores] * nsel), *([cache_v] * (nsel * ppb)))
    return out.reshape(B, H_C, HD_C)


def _s5_kernel(u_ref, h0r_ref, h0i_ref, wb_ref, ar_ref, ai_ref, wc_ref, d_ref, wglu_ref,
               o_ref, hr_ref, hi_ref, x_ref, *, tt, nb):
    ti = pl.program_id(0)
    ns = D_S5_STATE

    @pl.when(ti == 0)
    def _():
        hr_ref[...] = h0r_ref[...]
        hi_ref[...] = h0i_ref[...]

    nslab = ns // LANES
    if tt == 1:
        seqs, seq_rows, mat_rows, cols = [0], lambda b: slice(None), lambda b: slice(None), lambda b: slice(None)
        u = u_ref[...]
    else:
        seqs = range(nb)
        seq_rows = lambda b: pl.ds(b, tt, stride=nb)
        mat_rows = lambda b: slice(b * tt, (b + 1) * tt)
        cols = lambda b: slice(b * D_S5, (b + 1) * D_S5)
        u = jnp.concatenate([u_ref[:, cols(b)] for b in seqs], axis=0)
    nsp = S5_MM_SPLIT
    du, dsx = D_S5 // nsp, ns // nsp
    ub = u.astype(BF16)
    for part in range(2):
        for g in range(nsp):
            xg = _dot(ub[:, g * du:(g + 1) * du], wb_ref[g * du:(g + 1) * du, part * ns + g * dsx:part * ns + (g + 1) * dsx])
            for b in seqs:
                for c in range(dsx // LANES):
                    slab = (part * ns + g * dsx) // LANES + c
                    x_ref[slab, seq_rows(b), :] = xg[mat_rows(b), c * LANES:(c + 1) * LANES]
    slabs = lambda ref: jnp.stack([ref[:, c * LANES:(c + 1) * LANES] for c in range(nslab)])
    ar = slabs(ar_ref)
    ai = slabs(ai_ref)

    def step(t, carry):
        hr, hi = carry
        rows = pl.ds(pl.multiple_of(t * nb, nb), nb)
        nr = ar * hr - ai * hi + x_ref[0:nslab, rows, :]
        ni = ar * hi + ai * hr + x_ref[nslab:2 * nslab, rows, :]
        x_ref[0:nslab, rows, :] = nr
        x_ref[nslab:2 * nslab, rows, :] = ni
        return nr, ni

    hr, hi = lax.fori_loop(0, tt, step, (slabs(hr_ref), slabs(hi_ref)))
    for c in range(nslab):
        hr_ref[:, c * LANES:(c + 1) * LANES] = hr[c]
        hi_ref[:, c * LANES:(c + 1) * LANES] = hi[c]

    def state_cols(first_slab, n):
        hs = [jnp.concatenate([x_ref[first_slab + c, seq_rows(b), :] for c in range(n)], axis=-1) for b in seqs]
        return (hs[0] if len(hs) == 1 else jnp.concatenate(hs, axis=0)).astype(BF16)

    ys = []
    for g in range(nsp):
        yg = None
        for part in range(2):
            hg = state_cols((part * ns + g * dsx) // LANES, dsx // LANES)
            t = _dot(hg, wc_ref[part * ns + g * dsx:part * ns + (g + 1) * dsx, g * du:(g + 1) * du])
            yg = t if yg is None else yg + t
        ys.append(yg)
    y = jax.nn.gelu(jnp.concatenate(ys, axis=-1) + d_ref[...] * u)
    gl = _dot(y.astype(BF16), wglu_ref[...])
    o = gl[:, 0:D_S5] * jax.nn.sigmoid(gl[:, D_S5:2 * D_S5])
    for b in seqs:
        o_ref[:, cols(b)] = o[mat_rows(b), :]


def _s5(u, h0r, h0i, wb, ar, ai, wc, d, wglu, *, tt, nb):
    L = u.shape[0]
    const = lambda a: pl.BlockSpec(a.shape, lambda i: (0,) * a.ndim)
    rows = pl.BlockSpec((tt, nb * D_S5), lambda i: (i, 0))
    st = pl.BlockSpec((nb, D_S5_STATE), lambda i: (0, 0))
    args = (u, h0r, h0i, wb, ar, ai, wc, d, wglu)
    if tt == 1:
        assert L == 1
        rows = pl.BlockSpec((nb, D_S5), lambda i: (0, 0))
        args = (u.reshape(nb, D_S5),) + args[1:]
    return pl.pallas_call(
        functools.partial(_s5_kernel, tt=tt, nb=nb),
        grid=(L // tt,),
        in_specs=[rows] + [const(a) for a in args[1:]],
        out_specs=[rows, st, st],
        out_shape=[jax.ShapeDtypeStruct(args[0].shape, F32),
                   jax.ShapeDtypeStruct((nb, D_S5_STATE), F32), jax.ShapeDtypeStruct((nb, D_S5_STATE), F32)],
        scratch_shapes=[pltpu.VMEM((2 * D_S5_STATE // LANES, tt * nb, LANES), F32)],
        compiler_params=_cparams("arbitrary"),
        name="s5",
    )(*args)


def _rel_bucket(dist):
    n = jnp.maximum(dist, 0)
    exact = N_BUCKETS // 2
    nf = jnp.maximum(n, exact).astype(F32)
    large = exact + (jnp.log(nf / exact) / math.log(MAX_DISTANCE / exact) * (N_BUCKETS - exact)).astype(jnp.int32)
    return jnp.where(n < exact, n, jnp.minimum(large, N_BUCKETS - 1))


def _block_diag(w):
    G, a, b = w.shape
    eye = jnp.eye(G, dtype=w.dtype)
    return (eye[:, None, :, None] * w[:, :, None, :]).reshape(G * a, G * b)


def _s5_weights(A_re, A_im, log_dt, B_re, B_im, C_re, C_im, D):
    dt = jnp.exp(log_dt.astype(F32))[:, None]
    lr, li = A_re.astype(F32), A_im.astype(F32)
    mag = jnp.exp(lr * dt)
    ar, ai = mag * jnp.cos(li * dt), mag * jnp.sin(li * dt)
    den = lr * lr + li * li
    nr = ar - 1.0
    cr, ci = (nr * lr + ai * li) / den, (ai * lr - nr * li) / den
    bbr = cr[..., None] * B_re - ci[..., None] * B_im
    bbi = cr[..., None] * B_im + ci[..., None] * B_re
    wb = jnp.concatenate([_block_diag(bbr.transpose(0, 2, 1)), _block_diag(bbi.transpose(0, 2, 1))], axis=1)
    wc = jnp.concatenate([_block_diag(C_re.transpose(0, 2, 1)), -_block_diag(C_im.transpose(0, 2, 1))], axis=0)
    return (wb.astype(BF16), ar.reshape(1, -1), ai.reshape(1, -1), wc.astype(BF16), D.reshape(1, -1).astype(F32))


def _pad_rows_front(a, rows):
    return jnp.pad(a, ((0, 0), (rows - a.shape[1], 0), (0, 0)))


def _trunk(x, st, W, *, decode):
    B, L, D = x.shape
    T = B * L
    tm = min(ROW_TILE, T)
    out = {}

    x2 = x.reshape(T, D)
    z = _norm_proj_wide(x2, W['norm_mix'][0], W['ev_w_in'], tm=tm, chunk=EV_IN_TILE, name="ev_in_proj")
    zc = z.shape[1]
    if decode:
        lp = SUBLANES
        z3 = jnp.pad(z.reshape(B, L, zc), ((0, 0), (0, lp - L), (0, 0)))
        tt = lp
    else:
        lp = L
        z3 = z.reshape(B, L, zc)
        tt = min(ROW_TILE, L)
    conv0 = _pad_rows_front(st['gdn_conv'], SUBLANES)
    mixed, tail, sg, sh_t = _even_mixer(z3, conv0, st['gdn'], jnp.swapaxes(st['hgrn'], -1, -2), W['gdn_conv_w'],
                                        W['gdn_neg_a'], W['gdn_dt_bias'], W['gdn_norm'], W['hgrn_norm'], W['hgrn_lb'],
                                        l_real=L, tt=tt)
    out['gdn_conv'] = tail[:, SUBLANES - (GDN_CONV - 1):, :]
    out['gdn'] = sg
    out['hgrn'] = jnp.swapaxes(sh_t, -1, -2)
    if decode:
        parts = [(mixed[:, :L].reshape(T, 2 * D_A), W['ev_w_out'], pl.BlockSpec((tm, 2 * D_A), lambda i: (i, 0)))]
    else:
        parts = [(mixed, W['ev_w_out'], pl.BlockSpec((None, tm, 2 * D_A), lambda b, i: (b, i, 0)))]
    x2, fs = _post_mixer(x2, parts, st, W, 0, B, L, decode)
    out['ffn0'] = fs

    dc = H_C * HD_C
    gqk = jnp.concatenate([jnp.tile(W['moba_gq'], H_C), jnp.tile(W['moba_gk'], H_C)])
    if decode:
        qkv, u_tm = _od_qkv_proj(x2, W['norm_mix'][1], W['od_w_in'], gqk, tm=tm)
        u_tm = u_tm.reshape(L, B * D_S5)
        out['moba_k'] = qkv[:, dc:2 * dc].reshape(B, L, H_C, HD_C)
        out['moba_v'] = qkv[:, 2 * dc:].reshape(B, L, H_C, HD_C)
        heads = lambda a: a.reshape(B, H_C, HD_C)
        o_c = _moba_step(heads(qkv[:, :dc]), heads(qkv[:, dc:2 * dc]), heads(qkv[:, 2 * dc:]),
                         st['cache_k'], st['cache_v'], st['page_table'], *W['moba_bias_step'])
        o_c = o_c.reshape(T, dc)
        tts = 1
    else:
        qkv, u_tm, kt, vt = _od_qkv_proj(x2, W['norm_mix'][1], W['od_w_in'], gqk, tm=tm, seq=(B, L))
        out['moba_k'] = jnp.transpose(kt.reshape(B, H_C, HD_C, L), (0, 3, 1, 2))
        out['moba_v'] = jnp.transpose(vt.reshape(B, H_C, HD_C, L), (0, 3, 1, 2))
        o_c = _moba_seq(qkv.reshape(B, L, 3 * dc), vt, W['moba_bias_tiles']).reshape(T, dc)
        tts = min(S5_TIME_TILE, L)
    o_d, hr, hi = _s5(u_tm, st['s5_re'], st['s5_im'], *W['s5'], W['s5_glu_w'], tt=tts, nb=B)
    out['s5_re'] = hr.reshape(B, S5_GROUPS, S5_STATE)
    out['s5_im'] = hi.reshape(B, S5_GROUPS, S5_STATE)
    if decode:
        parts = [(o_c, W['od_w_out'][:dc], pl.BlockSpec((tm, dc), lambda i: (i, 0))),
                 (o_d, W['od_w_out'][dc:], pl.BlockSpec((tm, D_S5), lambda i: (i, 0)))]
    else:
        parts = [(o_c.reshape(B, L, dc), W['od_w_out'][:dc], pl.BlockSpec((None, tm, dc), lambda b, i: (b, i, 0))),
                 (o_d, W['od_w_out'][dc:], pl.BlockSpec((tm, D_S5), lambda b, i: (i, b)))]
    x2, fs = _post_mixer(x2, parts, st, W, 1, B, L, decode)
    out['ffn1'] = fs
    out['y'] = x2.reshape(B, L, D)
    return out


def _post_mixer(x2, parts, st, W, l, B, L, decode):
    T, D = x2.shape
    dq = MEM_H * MEM_HD
    buf = st['ffn_conv'][l]
    if decode:
        x2 = _proj_res(x2, parts, tm=T, name="mixer_out_proj")
        q = _norm_proj(x2, W['norm_mem'][l], W['mem_wq'][l], tm=T, tn=dq, n_norm_tiles=1, group=MEM_HD,
                       gain=jnp.tile(W['mem_gq'][l], MEM_H), name="mem_q_proj")
        o = _mem_attn_step(q.reshape(B, 1, dq), st['mem_k'], st['mem_v'], l).reshape(T, dq)
        x3 = _proj_res(x2, [(o, W['mem_wo'][l], pl.BlockSpec((T, dq), lambda i: (i, 0)))], tm=T, name="mem_out_proj")
        y, gate = _ffn_step(x3, buf[:, 0], buf[:, 1], W['norm_ffn'][l], W['ffn_w_gu'][l],
                            W['ffn_conv_w'][l], W['ffn_conv_b'][l], W['ffn_w_down'][l])
        fs = jnp.stack([buf[:, 1], gate], axis=1)
        return y, fs
    y, tail = _post_seq(x2.reshape(B, L, D), parts, st['mem_k'][l], st['mem_v'][l], W['norm_mem'][l], W['mem_wq'][l],
                        W['mem_gq'][l], W['mem_wo'][l], _pad_rows_front(buf, SUBLANES), W['norm_ffn'][l],
                        W['ffn_w_gu'][l], W['ffn_conv_w'][l], W['ffn_conv_b'][l], W['ffn_w_down'][l],
                        tm=min(ROW_TILE, L), name="post_mixer_seq")
    return y.reshape(T, D), tail[:, SUBLANES - (FFN_CONV - 1):, :]


def _mem_kv(mem, g_norm, w_kv, g_k):
    B, M, D = mem.shape
    dq = MEM_H * MEM_HD
    gain = jnp.concatenate([jnp.tile(g_k, MEM_H), jnp.ones((dq,), F32)])
    kv = _norm_proj(mem.reshape(B * M, D), g_norm, w_kv, tm=min(ROW_TILE, B * M), tn=dq, n_norm_tiles=1, group=MEM_HD,
                    gain=gain, name="mem_kv_proj")
    return kv[:, :dq].reshape(B, M, dq), kv[:, dq:].reshape(B, M, dq)


def kernel(x_prompt, x_sample, state_gdn_conv, state_gdn, state_hgrn, cache_moba_k, cache_moba_v, state_s5_re, state_s5_im, cache_mem_k, cache_mem_v, state_ffn_conv, page_table, mem_prompt, rel_bias, hgrn_lb_raw, norm_mix, norm_mem, norm_memkv, norm_ffn, mem_wq, mem_wkv, mem_gq, mem_gk, mem_wo, ffn_w_gu, ffn_conv_w, ffn_conv_b, ffn_w_down, ev_w_in, ev_w_out, gdn_conv_w, gdn_A_log, gdn_dt_bias, gdn_norm, hgrn_norm, od_w_in, od_w_out, moba_gq, moba_gk, s5_A_re, s5_A_im, s5_log_dt, s5_B_re, s5_B_im, s5_C_re, s5_C_im, s5_D, s5_glu_w):
    Bp, Lq, D = x_prompt.shape
    Bs = x_sample.shape[0]
    dc = H_C * HD_C
    dq = MEM_H * MEM_HD
    n_pages = page_table.shape[1]
    past = n_pages * PAGE_SIZE

    d3 = 3 * D_A
    w_in = ev_w_in[0]
    ba_cols = w_in[:, 4 * D_A:4 * D_A + 2 * H_A]
    w_in = jnp.concatenate([w_in[:, :4 * D_A], w_in[:, 4 * D_A + 2 * H_A:], ba_cols,
                            jnp.zeros((D, LANES - 2 * H_A), F32)], axis=1)
    lb_all = jnp.cumsum(jax.nn.softmax(hgrn_lb_raw.astype(F32), axis=0), axis=0)
    bias_t = rel_bias.T.astype(F32)
    nblk = MOBA_BLOCK
    by_rel = bias_t[:, _rel_bucket(jnp.arange(2 * nblk, dtype=jnp.int32))]
    mm = jnp.arange(2 * nblk, dtype=jnp.int32)

    def toeplitz(shift):
        y = by_rel[:, (shift + mm) % (2 * nblk)]
        flat = jnp.tile(y, (1, nblk))[:, :nblk * (2 * nblk - 1)]
        return flat.reshape(H_C, nblk, 2 * nblk - 1)[:, :, :nblk]

    far = bias_t[:, _rel_bucket(jnp.int32(2 * nblk))]
    tiles = jnp.stack([toeplitz(0), toeplitz(nblk)], axis=1) - far[:, None, None, None]
    bias_step = (by_rel[:, nblk - jnp.arange(nblk, dtype=jnp.int32)][:, None, :],
                 jnp.stack([far, bias_t[:, 0]], axis=1))
    W = dict(
        norm_mix=norm_mix, norm_mem=norm_mem, norm_ffn=norm_ffn,
        ev_w_in=w_in.astype(BF16), ev_w_out=ev_w_out[0].astype(BF16),
        gdn_conv_w=gdn_conv_w[0], gdn_neg_a=-jnp.exp(gdn_A_log[0].astype(F32)), gdn_dt_bias=gdn_dt_bias[0],
        gdn_norm=gdn_norm[0], hgrn_norm=hgrn_norm[0], hgrn_lb=lb_all[0],
        od_w_in=od_w_in[0].astype(BF16), od_w_out=od_w_out[0].astype(BF16),
        moba_gq=moba_gq[0], moba_gk=moba_gk[0], moba_bias_tiles=tiles, moba_bias_step=bias_step,
        s5=_s5_weights(s5_A_re[0], s5_A_im[0], s5_log_dt[0], s5_B_re[0], s5_B_im[0], s5_C_re[0], s5_C_im[0], s5_D[0]),
        s5_glu_w=s5_glu_w[0].astype(BF16),
        mem_wq=mem_wq.astype(BF16), mem_gq=mem_gq, mem_wo=mem_wo.astype(BF16),
        ffn_w_gu=ffn_w_gu.astype(BF16), ffn_conv_w=ffn_conv_w, ffn_conv_b=ffn_conv_b, ffn_w_down=ffn_w_down.astype(BF16),
    )

    mk_l, mv_l = [], []
    for l in range(2):
        mk, mv = _mem_kv(mem_prompt, norm_memkv[l], mem_wkv[l].astype(BF16), mem_gk[l])
        mk_l.append(mk)
        mv_l.append(mv)
    p_mem_k, p_mem_v = jnp.stack(mk_l), jnp.stack(mv_l)
    zeros = lambda *s: jnp.zeros(s, F32)
    st_p = dict(gdn_conv=zeros(Bp, GDN_CONV - 1, d3), gdn=zeros(Bp, H_A, HD_A, HD_A), hgrn=zeros(Bp, H_B, DK_B, DV_B),
                s5_re=zeros(Bp, D_S5_STATE), s5_im=zeros(Bp, D_S5_STATE), mem_k=p_mem_k, mem_v=p_mem_v,
                ffn_conv=zeros(2, Bp, FFN_CONV - 1, D_FF))
    op = _trunk(x_prompt, st_p, W, decode=False)

    st_s = dict(gdn_conv=state_gdn_conv[0], gdn=state_gdn[0], hgrn=state_hgrn[0],
                s5_re=state_s5_re[0].reshape(Bs, D_S5_STATE), s5_im=state_s5_im[0].reshape(Bs, D_S5_STATE),
                mem_k=cache_mem_k, mem_v=cache_mem_v,
                ffn_conv=state_ffn_conv,
                cache_k=jnp.transpose(cache_moba_k[0], (0, 2, 3, 1)), cache_v=jnp.transpose(cache_moba_v[0], (0, 2, 3, 1)),
                page_table=page_table)
    os_ = _trunk(x_sample, st_s, W, decode=True)

    M = mem_prompt.shape[1]
    return (op['y'], os_['y'],
            op['gdn_conv'][None], op['gdn'][None], op['hgrn'][None], op['moba_k'][None], op['moba_v'][None],
            op['s5_re'][None], op['s5_im'][None],
            p_mem_k.reshape(2, Bp, M, MEM_H, MEM_HD), p_mem_v.reshape(2, Bp, M, MEM_H, MEM_HD),
            jnp.stack([op['ffn0'], op['ffn1']]),
            os_['gdn_conv'][None], os_['gdn'][None], os_['hgrn'][None], os_['moba_k'][None], os_['moba_v'][None],
            os_['s5_re'][None], os_['s5_im'][None], jnp.stack([os_['ffn0'], os_['ffn1']]))
```

```python
import functools
import math

import jax
import jax.numpy as jnp
from jax import lax
from jax.experimental import pallas as pl
from jax.experimental.pallas import tpu as pltpu

F32 = jnp.float32
BF16 = jnp.bfloat16
HI = lax.Precision.HIGHEST
EPS = 1e-6

LANES = 128
SUBLANES = 8
VMEM_LIMIT_BYTES = 56 * 1024 * 1024

ROW_TILE = 512
S5_TIME_TILE = 64
H_A, HD_A, GDN_CONV = 8, 64, 4
H_B, DK_B, DV_B = 8, 64, 64
GDN_CHUNK, HGRN_CHUNK = 64, 32
GDN_GROUP, HGRN_GROUP = 4, 4
H_C, HD_C = 8, 64
MOBA_BLOCK, MOBA_TOPK = 256, 3
N_BUCKETS, MAX_DISTANCE = 32, 128
S5_GROUPS, S5_GROUP_CH, S5_STATE = 32, 16, 64
MEM_H, MEM_HD = 4, 128
D_FF, FFN_CONV = 2816, 3
PAGE_SIZE = 128
D_A = H_A * HD_A
D_S5 = S5_GROUPS * S5_GROUP_CH
D_S5_STATE = S5_GROUPS * S5_STATE
S5_MM_SPLIT = 4
FF_CHUNK = D_FF // 2
EV_IN_COLS = 8 * D_A + LANES
EV_IN_TILE = EV_IN_COLS // 3


def _cparams(*sem):
    return pltpu.CompilerParams(dimension_semantics=sem, vmem_limit_bytes=VMEM_LIMIT_BYTES)


def _dot(a, b, precision=None):
    return jnp.dot(a, b, preferred_element_type=F32, precision=precision)


def _dot_nt(a, b, precision=None):
    return lax.dot_general(a, b, (((1,), (1,)), ((), ())), preferred_element_type=F32, precision=precision)


def _dot_tn(a, b, precision=None):
    return lax.dot_general(a, b, (((0,), (0,)), ((), ())), preferred_element_type=F32, precision=precision)


def _rms(x, g):
    return x * lax.rsqrt(jnp.mean(x * x, axis=-1, keepdims=True) + EPS) * g


def _silu(x):
    return x * jax.nn.sigmoid(x)


def _tri(n, strict=False):
    r = lax.broadcasted_iota(jnp.int32, (n, n), 0)
    c = lax.broadcasted_iota(jnp.int32, (n, n), 1)
    return (r > c) if strict else (r >= c)


def _norm_proj_kernel(x_ref, g_ref, w_ref, e_ref, gain_ref, o_ref, *, n_norm_tiles, group):
    xn = _rms(x_ref[...], g_ref[...])
    z = _dot(xn.astype(BF16), w_ref[...])
    j = pl.program_id(1)

    @pl.when(j < n_norm_tiles)
    def _():
        zz = z * z
        hi = zz.astype(BF16)
        lo = (zz - hi.astype(F32)).astype(BF16)
        s = _dot(hi, e_ref[...]) + _dot(lo, e_ref[...])
        o_ref[...] = z * lax.rsqrt(s * (1.0 / group) + EPS) * gain_ref[...]

    @pl.when(j >= n_norm_tiles)
    def _():
        o_ref[...] = z


def _norm_proj(x, g, w, *, tm, tn, n_norm_tiles, group, gain, name):
    T, D = x.shape
    N = w.shape[1]
    nj = N // tn
    gi = lax.broadcasted_iota(jnp.int32, (tn, tn), 0) // group
    gj = lax.broadcasted_iota(jnp.int32, (tn, tn), 1) // group
    e = (gi == gj).astype(BF16)
    out_shape = jax.ShapeDtypeStruct((T, N), F32)
    out_spec = pl.BlockSpec((tm, tn), lambda i, j: (i, j))
    return pl.pallas_call(
        functools.partial(_norm_proj_kernel, n_norm_tiles=n_norm_tiles, group=group),
        grid=(T // tm, nj),
        in_specs=[
            pl.BlockSpec((tm, D), lambda i, j: (i, 0)),
            pl.BlockSpec((1, D), lambda i, j: (0, 0)),
            pl.BlockSpec((D, tn), lambda i, j: (0, j)),
            pl.BlockSpec((tn, tn), lambda i, j: (0, 0)),
            pl.BlockSpec((1, tn), lambda i, j: (0, j)),
        ],
        out_specs=out_spec,
        out_shape=out_shape,
        compiler_params=_cparams("parallel", "arbitrary"),
        name=name,
    )(x, g.reshape(1, D), w, e, gain.reshape(1, N))


def _norm_proj_wide_kernel(x_ref, g_ref, w_ref, o_ref, *, chunk):
    xn = _rms(x_ref[...], g_ref[...]).astype(BF16)
    for c in range(w_ref.shape[1] // chunk):
        cs = slice(c * chunk, (c + 1) * chunk)
        o_ref[:, cs] = _dot(xn, w_ref[:, cs])


def _norm_proj_wide(x, g, w, *, tm, chunk, name):
    T, D = x.shape
    N = w.shape[1]
    const = lambda shape: pl.BlockSpec(shape, lambda i: (0, 0), pipeline_mode=pl.Buffered(1))
    return pl.pallas_call(
        functools.partial(_norm_proj_wide_kernel, chunk=chunk),
        grid=(T // tm,),
        in_specs=[pl.BlockSpec((tm, D), lambda i: (i, 0)), const((1, D)), const((D, N))],
        out_specs=pl.BlockSpec((tm, N), lambda i: (i, 0)),
        out_shape=jax.ShapeDtypeStruct((T, N), F32),
        compiler_params=_cparams("parallel"),
        name=name,
    )(x, g.reshape(1, D), w)


def _od_qkv_kernel(x_ref, g_ref, w_ref, e_ref, gain_ref, o_ref, u_ref, *t_refs):
    dc = H_C * HD_C
    xn = _rms(x_ref[...], g_ref[...]).astype(BF16)
    u_ref[...] = _dot(xn, w_ref[:, 3 * dc:3 * dc + D_S5])
    for c in range(3):
        cs = slice(c * dc, (c + 1) * dc)
        z = _dot(xn, w_ref[:, cs])
        if c < 2:
            zz = z * z
            hi = zz.astype(BF16)
            lo = (zz - hi.astype(F32)).astype(BF16)
            s = _dot(hi, e_ref[...]) + _dot(lo, e_ref[...])
            z = z * lax.rsqrt(s * (1.0 / HD_C) + EPS) * gain_ref[:, cs]
        o_ref[:, cs] = z
        if t_refs and c > 0:
            t_refs[c - 1][...] = z.T


def _od_qkv_proj(x, g, w, gain, *, tm, seq=None):
    T, D = x.shape
    dc = H_C * HD_C
    gi = lax.broadcasted_iota(jnp.int32, (dc, dc), 0) // HD_C
    gj = lax.broadcasted_iota(jnp.int32, (dc, dc), 1) // HD_C
    e = (gi == gj).astype(BF16)
    const = lambda shape: pl.BlockSpec(shape, lambda i: (0, 0), pipeline_mode=pl.Buffered(1))
    out_specs = [pl.BlockSpec((tm, 3 * dc), lambda i: (i, 0))]
    out_shape = [jax.ShapeDtypeStruct((T, 3 * dc), F32)]
    if seq is None:
        out_specs += [pl.BlockSpec((tm, D_S5), lambda i: (i, 0))]
        out_shape += [jax.ShapeDtypeStruct((T, D_S5), F32)]
    else:
        B, L = seq
        nt = L // tm
        tspec = pl.BlockSpec((None, dc, tm), lambda i: (i // nt, 0, i % nt))
        out_specs += [pl.BlockSpec((tm, D_S5), lambda i: (i % nt, i // nt)), tspec, tspec]
        out_shape += [jax.ShapeDtypeStruct((L, B * D_S5), F32)] + [jax.ShapeDtypeStruct((B, dc, L), F32)] * 2
    return pl.pallas_call(
        _od_qkv_kernel,
        grid=(T // tm,),
        in_specs=[pl.BlockSpec((tm, D), lambda i: (i, 0)), const((1, D)), const((D, 3 * dc + D_S5)), const((dc, dc)),
                  const((1, 2 * dc))],
        out_specs=out_specs,
        out_shape=out_shape,
        compiler_params=_cparams("parallel"),
        name="od_in_proj_qkv",
    )(x, g.reshape(1, D), w, e, gain.reshape(1, 2 * dc))


def _proj_res_kernel(*refs, n_in):
    x_ref, o_ref = refs[0], refs[-1]
    acc = x_ref[...]
    for k in range(n_in):
        a_ref, w_ref = refs[1 + 2 * k], refs[2 + 2 * k]
        acc = acc + _dot(a_ref[...].astype(BF16), w_ref[...])
    o_ref[...] = acc


def _proj_res(x, parts, *, tm, name):
    T, N = x.shape
    in_specs = [pl.BlockSpec((tm, N), lambda i: (i, 0))]
    args = [x]
    for a, w, spec in parts:
        in_specs += [spec, pl.BlockSpec(w.shape, lambda i: (0, 0))]
        args += [a, w]
    return pl.pallas_call(
        functools.partial(_proj_res_kernel, n_in=len(parts)),
        grid=(T // tm,),
        in_specs=in_specs,
        out_specs=pl.BlockSpec((tm, N), lambda i: (i, 0)),
        out_shape=jax.ShapeDtypeStruct((T, N), F32),
        compiler_params=_cparams("parallel"),
        name=name,
    )(*args)


def _neumann_inverse(mats, n):
    m = mats[0].shape[0]
    eye = (lax.broadcasted_iota(jnp.int32, (m, m), 0) == lax.broadcasted_iota(jnp.int32, (m, m), 1)).astype(F32)
    ts = [eye - a for a in mats]
    ps = [a.astype(BF16) for a in mats]
    k = 2
    while k < n:
        ps = [_dot(p, p).astype(BF16) for p in ps]
        ts = [t + _dot(t.astype(BF16), p) for t, p in zip(ts, ps)]
        k *= 2
    return ts


def _even_mixer_kernel(qkv_ref, zg_ref, hq_ref, hf_ref, hi_ref, hg_ref, ba_ref,
                       conv0_ref, sg0_ref, sh0_ref,
                       convw_ref, nega_ref, dtb_ref, gnorm_ref, hnorm_ref, lb_ref,
                       mixed_ref, tail_ref, sg_ref, sh_ref,
                       xp_ref, beta_ref, g_ref, sbd_ref,
                       *, tt, cg, ch, l_real):
    ti = pl.program_id(1)

    @pl.when(ti == 0)
    def _():
        xp_ref[0:SUBLANES, :] = conv0_ref[...]
        sh_ref[...] = sh0_ref[...]
        zero = jnp.zeros((HD_A, HD_A), F32)
        for p in range(H_A // 2):
            sbd_ref[p] = jnp.concatenate([jnp.concatenate([sg0_ref[2 * p], zero], axis=-1),
                                          jnp.concatenate([zero, sg0_ref[2 * p + 1]], axis=-1)], axis=0)

    xp_ref[SUBLANES:SUBLANES + tt, :] = qkv_ref[...]
    lr_last = l_real - (l_real - 1) // tt * tt
    tail_ref[...] = xp_ref[lr_last:lr_last + SUBLANES, :]

    padded = (l_real % tt) != 0
    if padded:
        row = lax.broadcasted_iota(jnp.int32, (tt, 1), 0) + ti * tt
        valid = row < l_real
    ba = ba_ref[...]
    beta_all = jax.nn.sigmoid(ba)
    g_all = nega_ref[...] * jax.nn.softplus(ba + dtb_ref[...])
    if padded:
        beta_all = jnp.where(valid, beta_all, 0.0)
        g_all = jnp.where(valid, g_all, 0.0)
    beta_ref[...] = beta_all
    g_ref[...] = g_all

    ltri_g = _tri(cg).astype(F32)
    hnorm = hnorm_ref[...]
    gnorm2 = jnp.concatenate([gnorm_ref[...], gnorm_ref[...]], axis=-1)
    pairs = range(H_A // 2)
    n2 = 2 * cg
    lane = lax.broadcasted_iota(jnp.int32, (1, LANES), 1)
    m0 = (lane < HD_A).astype(F32)
    m1 = 1.0 - m0
    ri = lax.broadcasted_iota(jnp.int32, (n2, n2), 0)
    cj = lax.broadcasted_iota(jnp.int32, (n2, n2), 1)
    same_head = (ri // cg) == (cj // cg)
    bd_tril = same_head & (ri >= cj)
    bd_stril = same_head & (ri > cj)

    def stack2(a):
        return jnp.concatenate([a * m0, a * m1], axis=0)

    def rep2(col0, col1):
        return jnp.concatenate([jnp.broadcast_to(col0, (cg, LANES)), jnp.broadcast_to(col1, (cg, LANES))], axis=0)

    ng = min(GDN_GROUP, tt // cg)

    def gdn_group(gi, _):
        items = [(c, p) for c in range(ng) for p in pairs]
        rows_c, gcum_c, gt_c, bet_c, qkv_c = [], [], [], [], []
        for c in range(ng):
            r0 = pl.multiple_of((gi * ng + c) * cg, cg)
            rows = pl.ds(r0, cg)
            win = xp_ref[pl.ds(r0, cg + SUBLANES), :]
            conv = convw_ref[GDN_CONV - 1:GDN_CONV, :] * win[SUBLANES:SUBLANES + cg]
            for j in range(GDN_CONV - 1):
                off = SUBLANES - (GDN_CONV - 1) + j
                conv = conv + convw_ref[j:j + 1, :] * win[off:off + cg]
            qkv_c.append(_silu(conv))
            gcum = _dot(ltri_g, g_ref[rows, :], HI)
            gpad = gcum if cg == LANES else jnp.concatenate([gcum, jnp.zeros((LANES - cg, LANES), F32)], axis=0)
            rows_c.append(rows)
            gcum_c.append(gcum)
            gt_c.append(gpad.T)
            bet_c.append(beta_ref[rows, :])
        qs, ks, vs, betas, gcs, decays, glasts, glrows = {}, {}, {}, {}, {}, {}, {}, {}
        for it in items:
            c, p = it
            rows = rows_c[c]
            h0, h1 = H_A + 2 * p, H_A + 2 * p + 1
            q = stack2(qkv_c[c][:, p * LANES:(p + 1) * LANES])
            k = stack2(qkv_c[c][:, D_A + p * LANES:D_A + (p + 1) * LANES])
            qs[it] = q * lax.rsqrt(jnp.sum(q * q, axis=-1, keepdims=True) + EPS) * (HD_A ** -0.5)
            ks[it] = k * lax.rsqrt(jnp.sum(k * k, axis=-1, keepdims=True) + EPS)
            vs[it] = stack2(qkv_c[c][:, 2 * D_A + p * LANES:2 * D_A + (p + 1) * LANES])
            betas[it] = rep2(bet_c[c][:, 2 * p:2 * p + 1], bet_c[c][:, 2 * p + 1:2 * p + 2])
            gc = rep2(gcum_c[c][:, h0:h0 + 1], gcum_c[c][:, h1:h1 + 1])
            gr = jnp.concatenate([gt_c[c][h0:h0 + 1, 0:cg], gt_c[c][h1:h1 + 1, 0:cg]], axis=-1)
            gcs[it] = gc
            glasts[it] = rep2(gcum_c[c][cg - 1:cg, h0:h0 + 1], gcum_c[c][cg - 1:cg, h1:h1 + 1])
            glrows[it] = jnp.exp(glasts[it][0:1, :]) * m0 + jnp.exp(glasts[it][cg:cg + 1, :]) * m1
            decays[it] = jnp.exp(jnp.where(bd_tril, gc[:, 0:n2] - gr, -jnp.inf))
        kqs = {it: _dot_nt(jnp.concatenate([ks[it], qs[it]], axis=0).astype(BF16), ks[it].astype(BF16))
               for it in items}
        ts = dict(zip(items, _neumann_inverse(
            [jnp.where(bd_stril, betas[it][:, 0:n2] * kqs[it][0:n2] * decays[it], 0.0) for it in items], cg)))
        egs = {it: jnp.exp(gcs[it]) for it in items}
        sols = {it: _dot(ts[it].astype(BF16),
                         jnp.concatenate([betas[it] * vs[it], (betas[it] * egs[it]) * ks[it]], axis=-1).astype(BF16))
                for it in items}
        wqs = {it: jnp.concatenate([sols[it][:, LANES:2 * LANES], qs[it] * egs[it]], axis=0).astype(BF16)
               for it in items}
        qks = {it: (kqs[it][n2:2 * n2] * decays[it]).astype(BF16) for it in items}
        kouts = {it: (ks[it] * jnp.exp(glasts[it] - gcs[it])).astype(BF16) for it in items}
        ss = [sbd_ref[p] for p in pairs]
        for c in range(ng):
            wss = [_dot(wqs[(c, p)], ss[p].astype(BF16)) for p in pairs]
            ubs = [(sols[(c, p)][:, 0:LANES] - wss[p][0:n2]).astype(BF16) for p in pairs]
            os_ = [wss[p][n2:2 * n2] + _dot(qks[(c, p)], ubs[p]) for p in pairs]
            ss = [ss[p] * glrows[(c, p)] + _dot_tn(kouts[(c, p)], ubs[p]) for p in pairs]
            for p in pairs:
                o = os_[p]
                on = o * lax.rsqrt(jnp.sum(o * o, axis=-1, keepdims=True) * (1.0 / HD_A) + EPS)
                cols = slice(p * LANES, (p + 1) * LANES)
                mixed_ref[rows_c[c], cols] = (on[0:cg] + on[cg:n2]) * gnorm2 * _silu(zg_ref[rows_c[c], cols])
        for p in pairs:
            sbd_ref[p] = ss[p]
        return 0

    lax.fori_loop(0, tt // (cg * ng), gdn_group, 0)
    xp_ref[0:SUBLANES, :] = xp_ref[tt:tt + SUBLANES, :]
    for p in pairs:
        sg_ref[2 * p] = sbd_ref[p, 0:HD_A, 0:HD_A]
        sg_ref[2 * p + 1] = sbd_ref[p, HD_A:2 * HD_A, HD_A:2 * HD_A]

    nh = min(HGRN_GROUP, tt // ch)
    gr = nh * ch
    ri = lax.broadcasted_iota(jnp.int32, (gr, gr), 0)
    cj = lax.broadcasted_iota(jnp.int32, (gr, gr), 1)
    same_chunk_causal = (ri >= cj) & ((ri // ch) == (cj // ch))
    ltri_h = same_chunk_causal.astype(F32)
    lb = lb_ref[...]
    r_mid = (ch - 1) // 2

    def hgrn_group(gi, _):
        r0 = pl.multiple_of(gi * gr, gr)
        rows = pl.ds(r0, gr)
        f = lb + (1.0 - lb) * jax.nn.sigmoid(hf_ref[rows, :])
        logf = jnp.log(f)
        kk = 1.0 - f
        if padded:
            vrow = (lax.broadcasted_iota(jnp.int32, (gr, 1), 0) + r0 + ti * tt) < l_real
            logf = jnp.where(vrow, logf, 0.0)
            kk = jnp.where(vrow, kk, 0.0)
        qh = _silu(hq_ref[rows, :])
        bc = _dot(ltri_h, logf, HI).reshape(nh, ch, D_A)
        bref = bc[:, r_mid:r_mid + 1, :]
        blast = bc[:, ch - 1:ch, :]
        grp = lambda a: a.reshape(gr, D_A)
        qh3, kk3 = qh.reshape(nh, ch, D_A), kk.reshape(nh, ch, D_A)
        qa = grp(qh3 * jnp.exp(bc - bref)).astype(BF16)
        kb = grp(kk3 * jnp.exp(bref - bc)).astype(BF16)
        q_in = grp(qh3 * jnp.exp(bc)).astype(BF16)
        k_out = grp(kk3 * jnp.exp(blast - bc)).astype(BF16)
        fl = jnp.exp(blast)
        vv = hi_ref[rows, :].astype(BF16)
        heads = range(H_B)
        sls = [slice(h * DK_B, (h + 1) * DK_B) for h in heads]
        atts = [jnp.where(same_chunk_causal, _dot_nt(qa[:, sl], kb[:, sl]), 0.0).astype(BF16) for sl in sls]
        intra = [_dot(atts[h], vv[:, sls[h]]) for h in heads]
        sts = [sh_ref[h] for h in heads]
        inter = [[] for _ in heads]
        for c in range(nh):
            rc = slice(c * ch, (c + 1) * ch)
            for h in heads:
                inter[h].append(_dot_nt(q_in[rc, sls[h]], sts[h].astype(BF16)))
            sts = [sts[h] * fl[c][:, sls[h]] + _dot_tn(vv[rc, sls[h]], k_out[rc, sls[h]]) for h in heads]
        for h in heads:
            sh_ref[h] = sts[h]
        os_ = [intra[h] + (inter[h][0] if nh == 1 else jnp.concatenate(inter[h], axis=0)) for h in heads]
        outs = [_rms(os_[h], hnorm) * _silu(hg_ref[rows, h * DV_B:(h + 1) * DV_B]) for h in heads]
        for p in range(H_B // 2):
            c0 = D_A + p * LANES
            mixed_ref[rows, c0:c0 + LANES] = jnp.concatenate([outs[2 * p], outs[2 * p + 1]], axis=-1)
        return 0

    lax.fori_loop(0, tt // gr, hgrn_group, 0)


def _even_mixer(z, conv0, sg0, sh0_t, conv_w, neg_a, dt_bias, gnorm, hnorm, lb, *, l_real, tt):
    B, Lp, _ = z.shape
    cg = min(GDN_CHUNK, tt)
    ch = min(HGRN_CHUNK, tt)
    d3 = 3 * D_A
    nt = Lp // tt
    col = lambda w, idx: pl.BlockSpec((None, tt, w), lambda b, i, idx=idx: (b, i, idx))
    state = lambda: pl.BlockSpec((None, H_A, HD_A, HD_A), lambda b, i: (b, 0, 0, 0))
    vec = lambda n: pl.BlockSpec((1, n), lambda b, i: (0, 0))
    nega_row = jnp.zeros((1, LANES), F32).at[0, H_A:2 * H_A].set(neg_a)
    dtb_row = jnp.zeros((1, LANES), F32).at[0, H_A:2 * H_A].set(dt_bias)
    return pl.pallas_call(
        functools.partial(_even_mixer_kernel, tt=tt, cg=cg, ch=ch, l_real=l_real),
        grid=(B, nt),
        in_specs=[
            col(d3, 0), col(D_A, 3), col(D_A, 4), col(D_A, 5), col(D_A, 6), col(D_A, 7), col(LANES, 32),
            pl.BlockSpec((None, SUBLANES, d3), lambda b, i: (b, 0, 0)),
            state(), state(),
            pl.BlockSpec((GDN_CONV, d3), lambda b, i: (0, 0)),
            vec(LANES), vec(LANES), vec(HD_A), vec(DV_B), vec(D_A),
        ],
        out_specs=[
            pl.BlockSpec((None, tt, 2 * D_A), lambda b, i: (b, i, 0)),
            pl.BlockSpec((None, SUBLANES, d3), lambda b, i: (b, 0, 0)),
            state(), state(),
        ],
        out_shape=[
            jax.ShapeDtypeStruct((B, Lp, 2 * D_A), F32),
            jax.ShapeDtypeStruct((B, SUBLANES, d3), F32),
            jax.ShapeDtypeStruct((B, H_A, HD_A, HD_A), F32),
            jax.ShapeDtypeStruct((B, H_B, DV_B, DK_B), F32),
        ],
        scratch_shapes=[
            pltpu.VMEM((tt + SUBLANES, d3), F32),
            pltpu.VMEM((tt, LANES), F32),
            pltpu.VMEM((tt, LANES), F32),
            pltpu.VMEM((H_A // 2, 2 * HD_A, 2 * HD_A), F32),
        ],
        compiler_params=_cparams("parallel", "arbitrary"),
        name="even_mixer",
    )(z, z, z, z, z, z, z, conv0, sg0, sh0_t, conv_w, nega_row, dtb_row,
      gnorm.reshape(1, HD_A), hnorm.reshape(1, DV_B), lb.reshape(1, D_A))


def _mem_attn_rows(x, mk_ref, mv_ref, g_ref, wq_ref, gq_ref, wo_ref):
    q = _dot(_rms(x, g_ref[...]).astype(BF16), wq_ref[...])
    scale = MEM_HD ** -0.5
    outs = []
    for h in range(MEM_H):
        sl = slice(h * MEM_HD, (h + 1) * MEM_HD)
        qh = _rms(q[:, sl], gq_ref[...])
        s = _dot_nt(qh.astype(BF16), mk_ref[:, sl].astype(BF16)) * scale
        s = s - jnp.max(s, axis=-1, keepdims=True)
        p = jnp.exp(s)
        p = p / jnp.sum(p, axis=-1, keepdims=True)
        outs.append(_dot(p.astype(BF16), mv_ref[:, sl].astype(BF16)))
    o = jnp.concatenate(outs, axis=-1)
    return x + _dot(o.astype(BF16), wo_ref[...])


def _mem_attn_step_kernel(q_ref, mk_ref, mv_ref, o_ref):
    scale = MEM_HD ** -0.5
    outs = []
    for h in range(MEM_H):
        qh = jnp.broadcast_to(q_ref[:, h * MEM_HD:(h + 1) * MEM_HD], (SUBLANES, MEM_HD))
        s = _dot_nt(qh.astype(BF16), mk_ref[:, h, :].astype(BF16)) * scale
        s = s - jnp.max(s, axis=-1, keepdims=True)
        p = jnp.exp(s)
        p = p / jnp.sum(p, axis=-1, keepdims=True)
        outs.append(_dot(p.astype(BF16), mv_ref[:, h, :].astype(BF16))[0:1])
    o_ref[...] = jnp.concatenate(outs, axis=-1)


def _mem_attn_step(q, cache_k, cache_v, l):
    B = q.shape[0]
    M = cache_k.shape[2]
    dq = MEM_H * MEM_HD
    mem = pl.BlockSpec((None, None, M, MEM_H, MEM_HD), lambda b: (l, b, 0, 0, 0))
    row = pl.BlockSpec((None, 1, dq), lambda b: (b, 0, 0))
    return pl.pallas_call(
        _mem_attn_step_kernel,
        grid=(B,),
        in_specs=[row, mem, mem],
        out_specs=row,
        out_shape=jax.ShapeDtypeStruct((B, 1, dq), F32),
        compiler_params=_cparams("parallel"),
        name="mem_attn_step",
    )(q, cache_k, cache_v)


def _post_seq_kernel(*refs, n_parts, tm):
    x_ref = refs[0]
    parts = refs[1:1 + 2 * n_parts]
    (mk_ref, mv_ref, gmem_ref, wq_ref, gq_ref, wo_ref, buf0_ref, g_ref, wgu_ref, cw_ref, cb_ref, wd_ref,
     o_ref, tail_ref, gp_ref, carry_ref) = refs[1 + 2 * n_parts:]
    ti = pl.program_id(1)

    @pl.when(ti == 0)
    def _():
        carry_ref[...] = buf0_ref[...]

    x = x_ref[...]
    for k in range(n_parts):
        x = x + _dot(parts[2 * k][...].astype(BF16), parts[2 * k + 1][...])
    x = _mem_attn_rows(x, mk_ref, mv_ref, gmem_ref, wq_ref, gq_ref, wo_ref)
    xn = _rms(x, g_ref[...]).astype(BF16)
    acc = x
    for c in range(D_FF // FF_CHUNK):
        cs = slice(c * FF_CHUNK, (c + 1) * FF_CHUNK)
        gate = _dot(xn, wgu_ref[:, cs])
        up = _dot(xn, wgu_ref[:, D_FF + c * FF_CHUNK:D_FF + (c + 1) * FF_CHUNK])
        gp_ref[0:SUBLANES, :] = carry_ref[:, cs]
        gp_ref[SUBLANES:SUBLANES + tm, :] = gate
        conv = cw_ref[FFN_CONV - 1:FFN_CONV, cs] * gate + cb_ref[:, cs]
        for j in range(FFN_CONV - 1):
            off = SUBLANES - (FFN_CONV - 1) + j
            conv = conv + cw_ref[j:j + 1, cs] * gp_ref[off:off + tm, :]
        last = gp_ref[tm:tm + SUBLANES, :]
        carry_ref[:, cs] = last
        tail_ref[:, cs] = last
        act = _silu(conv) * up
        acc = acc + _dot(act.astype(BF16), wd_ref[cs, :])
    o_ref[...] = acc


def _post_seq(x, parts, mk, mv, gmem, wq, gq, wo, buf0, g, wgu, cw, cb, wd, *, tm, name):
    B, L, D = x.shape
    M = mk.shape[1]
    dq = MEM_H * MEM_HD
    const = lambda shape: pl.BlockSpec(shape, lambda b, i: (0,) * len(shape), pipeline_mode=pl.Buffered(1))
    row_tile = pl.BlockSpec((None, tm, D), lambda b, i: (b, i, 0))
    mem = pl.BlockSpec((None, M, dq), lambda b, i: (b, 0, 0))
    in_specs, args = [row_tile], [x]
    for a, w, spec in parts:
        in_specs += [spec, const(w.shape)]
        args += [a, w]
    in_specs += [mem, mem, const((1, D)), const((D, dq)), const((1, MEM_HD)), const((dq, D)),
                 pl.BlockSpec((None, SUBLANES, D_FF), lambda b, i: (b, 0, 0)),
                 const((1, D)), const((D, 2 * D_FF)), const((FFN_CONV, D_FF)), const((1, D_FF)), const((D_FF, D))]
    args += [mk, mv, gmem.reshape(1, D), wq, gq.reshape(1, MEM_HD), wo, buf0, g.reshape(1, D), wgu, cw,
             cb.reshape(1, D_FF), wd]
    return pl.pallas_call(
        functools.partial(_post_seq_kernel, n_parts=len(parts), tm=tm),
        grid=(B, L // tm),
        in_specs=in_specs,
        out_specs=[row_tile, pl.BlockSpec((None, SUBLANES, D_FF), lambda b, i: (b, 0, 0))],
        out_shape=[jax.ShapeDtypeStruct((B, L, D), F32), jax.ShapeDtypeStruct((B, SUBLANES, D_FF), F32)],
        scratch_shapes=[pltpu.VMEM((tm + SUBLANES, FF_CHUNK), F32), pltpu.VMEM((SUBLANES, D_FF), F32)],
        compiler_params=_cparams("parallel", "arbitrary"),
        name=name,
    )(*args)


def _ffn_step_kernel(x_ref, p2_ref, p1_ref, g_ref, wgu_ref, cw_ref, cb_ref, wd_ref, o_ref, gate_ref):
    x = x_ref[...]
    xn = _rms(x, g_ref[...]).astype(BF16)
    acc = x
    for c in range(D_FF // FF_CHUNK):
        cs = slice(c * FF_CHUNK, (c + 1) * FF_CHUNK)
        gate = _dot(xn, wgu_ref[:, cs])
        up = _dot(xn, wgu_ref[:, D_FF + c * FF_CHUNK:D_FF + (c + 1) * FF_CHUNK])
        gate_ref[:, cs] = gate
        conv = cw_ref[0:1, cs] * p2_ref[:, cs] + cw_ref[1:2, cs] * p1_ref[:, cs] + cw_ref[2:3, cs] * gate + cb_ref[:, cs]
        act = _silu(conv) * up
        acc = acc + _dot(act.astype(BF16), wd_ref[cs, :])
    o_ref[...] = acc


def _ffn_step(x, p2, p1, g, wgu, cw, cb, wd):
    R, D = x.shape
    full = lambda a: pl.BlockSpec(a.shape, lambda i: (0,) * a.ndim)
    args = (x, p2, p1, g.reshape(1, D), wgu, cw, cb.reshape(1, D_FF), wd)
    return pl.pallas_call(
        _ffn_step_kernel,
        grid=(1,),
        in_specs=[full(a) for a in args],
        out_specs=[pl.BlockSpec((R, D), lambda i: (0, 0)), pl.BlockSpec((R, D_FF), lambda i: (0, 0))],
        out_shape=[jax.ShapeDtypeStruct((R, D), F32), jax.ShapeDtypeStruct((R, D_FF), F32)],
        compiler_params=_cparams("arbitrary"),
        name="conv_ffn_step",
    )(*args)


def _moba_query_tile(j, hs, qt_ref, kb_ref, vt_ref, kmean_ref, bias_ref, ot_ref):
    blk = MOBA_BLOCK
    w = (j + 1) * blk
    qts = [qt_ref[h] for h in hs]
    sts = [_dot(kb_ref[h, 0:w, :], (qt * (HD_C ** -0.5)).astype(BF16)) for h, qt in zip(hs, qts)]
    sels = []
    if j > MOBA_TOPK:
        iota_n = lax.broadcasted_iota(jnp.int32, (j, blk), 0).astype(F32)
        gs = [_dot(kmean_ref[h, 0:j, :], qt, HI) for h, qt in zip(hs, qts)]
        for g in gs:
            sel = jnp.zeros((j, blk), F32)
            for _ in range(MOBA_TOPK):
                m = jnp.max(g, axis=0, keepdims=True)
                idx = jnp.min(jnp.where(g == m, iota_n, 1e9), axis=0, keepdims=True)
                hit = iota_n == idx
                sel = jnp.where(hit, 1.0, sel)
                g = jnp.where(hit, -jnp.inf, g)
            sels.append(sel)
    key_le_query = lax.broadcasted_iota(jnp.int32, (blk, blk), 0) <= lax.broadcasted_iota(jnp.int32, (blk, blk), 1)
    pcats, ls = [], []
    for i, h in enumerate(hs):
        parts = []
        for n in range(j + 1):
            s = sts[i][n * blk:(n + 1) * blk, :]
            if j - n < 2:
                s = s + bias_ref[h, j - n]
            if n == j:
                s = jnp.where(key_le_query, s, -jnp.inf)
            elif j > MOBA_TOPK:
                s = jnp.where(sels[i][n:n + 1, :] > 0.5, s, -jnp.inf)
            parts.append(s)
        m = jnp.max(parts[0], axis=0, keepdims=True)
        for s in parts[1:]:
            m = jnp.maximum(m, jnp.max(s, axis=0, keepdims=True))
        ps = [jnp.exp(s - m) for s in parts]
        l = jnp.sum(ps[0], axis=0, keepdims=True)
        for p in ps[1:]:
            l = l + jnp.sum(p, axis=0, keepdims=True)
        pcats.append((ps[0] if j == 0 else jnp.concatenate(ps, axis=0)).astype(BF16))
        ls.append(l)
    outs = [_dot(vt_ref[h, :, 0:w], pc) for h, pc in zip(hs, pcats)]
    for h, o, l in zip(hs, outs, ls):
        ot_ref[h] = o / l


def _moba_seq_kernel(q_ref, k_ref, vtin_ref, bias_ref, o_ref, kmean_ref, kb_ref, vt_ref, qt_ref, ot_ref, *, nb):
    j = pl.program_id(1)
    blk = MOBA_BLOCK
    dc = H_C * HD_C

    @pl.when(j == 0)
    def _():
        for n in range(nb):
            km = jnp.mean(k_ref[n * blk:(n + 1) * blk, :], axis=0, keepdims=True)
            for h in range(H_C):
                kmean_ref[h, n:n + 1, :] = km[:, h * HD_C:(h + 1) * HD_C]

        def prep(n, _):
            rows = pl.ds(pl.multiple_of(n * blk, blk), blk)
            kblk = k_ref[rows, :]
            vt_ref[:, :, rows] = vtin_ref[:, rows].astype(BF16).reshape(H_C, HD_C, blk)
            for h in range(H_C):
                kb_ref[h, rows, :] = kblk[:, h * HD_C:(h + 1) * HD_C].astype(BF16)
            return 0
        lax.fori_loop(0, nb, prep, 0)

    qt_ref[...] = q_ref[...].T.reshape(H_C, HD_C, blk)
    for jj in range(nb):
        @pl.when(j == jj)
        def _(jj=jj):
            def head_pair(hp, _):
                _moba_query_tile(jj, [2 * hp, 2 * hp + 1], qt_ref, kb_ref, vt_ref, kmean_ref, bias_ref, ot_ref)
                return 0
            lax.fori_loop(0, H_C // 2, head_pair, 0)
    o_ref[...] = ot_ref[...].reshape(dc, blk).T


def _moba_seq(qkv, vt, bias_tiles):
    B, L, _ = qkv.shape
    nb = L // MOBA_BLOCK
    dc = H_C * HD_C
    return pl.pallas_call(
        functools.partial(_moba_seq_kernel, nb=nb),
        grid=(B, nb),
        in_specs=[
            pl.BlockSpec((None, MOBA_BLOCK, dc), lambda b, j: (b, j, 0)),
            pl.BlockSpec((None, L, dc), lambda b, j: (b, 0, 1)),
            pl.BlockSpec((None, dc, L), lambda b, j: (b, 0, 0)),
            pl.BlockSpec(bias_tiles.shape, lambda b, j: (0, 0, 0, 0), pipeline_mode=pl.Buffered(1)),
        ],
        out_specs=pl.BlockSpec((None, MOBA_BLOCK, dc), lambda b, j: (b, j, 0)),
        out_shape=jax.ShapeDtypeStruct((B, L, dc), F32),
        scratch_shapes=[pltpu.VMEM((H_C, nb, HD_C), F32), pltpu.VMEM((H_C, L, HD_C), BF16),
                        pltpu.VMEM((H_C, HD_C, L), BF16), pltpu.VMEM((H_C, HD_C, MOBA_BLOCK), F32),
                        pltpu.VMEM((H_C, HD_C, MOBA_BLOCK), F32)],
        compiler_params=_cparams("parallel", "arbitrary"),
        name="moba_seq",
    )(qkv, qkv, vt, bias_tiles)


MOBA_PAGES_PER_BLOCK = MOBA_BLOCK // PAGE_SIZE
MOBA_STEP_PAGES = 16


def _moba_score_kernel(pt_ref, q_ref, *refs, n_pages):
    del pt_ref
    ppg, ppb = MOBA_STEP_PAGES, MOBA_PAGES_PER_BLOCK
    kp_refs = refs[:ppg]
    s_ref, sel_ref, gate_ref = refs[ppg:]
    step = pl.program_id(1)
    nbp = n_pages // ppb
    q = q_ref[...]
    inv = 1.0 / (MOBA_BLOCK * HD_C ** -0.5)
    gsum = None
    for r in range(ppg):
        sc = jnp.sum(kp_refs[r][...] * q, axis=1)
        s_ref[:, r * PAGE_SIZE:(r + 1) * PAGE_SIZE] = sc
        gs = jnp.sum(sc, axis=-1, keepdims=True)
        gsum = gs if r % ppb == 0 else gsum + gs
        if r % ppb == ppb - 1:
            gate_ref[step * (ppg // ppb) + r // ppb] = jnp.broadcast_to(gsum * inv, (H_C, LANES))

    @pl.when(step == n_pages // ppg - 1)
    def _():
        g = gate_ref[...]
        iota_b = lax.broadcasted_iota(jnp.int32, (nbp, H_C, LANES), 0).astype(F32)
        for i in range(MOBA_TOPK):
            m = jnp.max(g, axis=0, keepdims=True)
            idx = jnp.min(jnp.where(g == m, iota_b, 1e9), axis=0, keepdims=True)
            sel_ref[i] = idx[0]
            g = jnp.where(iota_b == idx, -jnp.inf, g)


def _moba_attend_kernel(pt_ref, sel_ref, q_ref, kn_ref, vn_ref, blast_ref, bsc_ref, *refs, nbp):
    del pt_ref
    nsel, ppb = MOBA_TOPK, MOBA_PAGES_PER_BLOCK
    s_refs, v_refs, o_ref = refs[:nsel], refs[nsel:nsel + nsel * ppb], refs[-1]
    b, h = pl.program_id(0), pl.program_id(1)
    far = bsc_ref[h, 0]
    s_new = jnp.sum(q_ref[h] * kn_ref[h], axis=-1, keepdims=True) * (HD_C ** -0.5) + bsc_ref[h, 1]
    ss = []
    own_row = lax.broadcasted_iota(jnp.int32, (H_C, MOBA_BLOCK), 0) == h
    for i in range(nsel):
        blk = sel_ref[(b * H_C + h) * nsel + i]
        s = jnp.sum(jnp.where(own_row, s_refs[i][...], 0.0), axis=0, keepdims=True)
        ss.append(s + jnp.where(blk == nbp - 1, blast_ref[h], far))
    s_all = jnp.concatenate(ss, axis=-1)
    m = jnp.maximum(s_new, jnp.max(s_all, axis=-1, keepdims=True))
    w_new = jnp.exp(s_new - m)
    e = jnp.exp(s_all - m)
    den = w_new + jnp.sum(e, axis=-1, keepdims=True)
    v_all = jnp.concatenate([v_refs[k][...] for k in range(nsel * ppb)], axis=-1)
    eb = jnp.broadcast_to(e, (SUBLANES, nsel * MOBA_BLOCK)).astype(BF16)
    num = w_new * vn_ref[h] + _dot_nt(eb, v_all.astype(BF16))[0:1]
    o_ref[h] = num / den


def _moba_step(q, k_new, v_new, cache_k, cache_v, page_table, bias_last, bias_scalars):
    B = q.shape[0]
    n_pages = page_table.shape[1]
    ppg, ppb, nsel = MOBA_STEP_PAGES, MOBA_PAGES_PER_BLOCK, MOBA_TOPK
    assert n_pages % ppg == 0 and ppg % ppb == 0
    nbp = n_pages // ppb
    assert nbp >= nsel
    pt = page_table.reshape(-1)
    q_rep = jnp.broadcast_to((q * (HD_C ** -0.5))[..., None], (B, H_C, HD_C, LANES))
    kpage = lambda r: pl.BlockSpec((None, H_C, HD_C, PAGE_SIZE),
                                   lambda b, s, pt, r=r: (pt[b * n_pages + s * ppg + r], 0, 0, 0))
    scores, sel = pl.pallas_call(
        functools.partial(_moba_score_kernel, n_pages=n_pages),
        grid_spec=pltpu.PrefetchScalarGridSpec(
            num_scalar_prefetch=1,
            grid=(B, n_pages // ppg),
            in_specs=[pl.BlockSpec((None, H_C, HD_C, LANES), lambda b, s, pt: (b, 0, 0, 0))]
                     + [kpage(r) for r in range(ppg)],
            out_specs=[pl.BlockSpec((None, H_C, ppg * PAGE_SIZE), lambda b, s, pt: (b, 0, s)),
                       pl.BlockSpec((None, nsel, H_C, LANES), lambda b, s, pt: (b, 0, 0, 0))],
            scratch_shapes=[pltpu.VMEM((nbp, H_C, LANES), F32)],
        ),
        out_shape=[jax.ShapeDtypeStruct((B, H_C, n_pages * PAGE_SIZE), F32),
                   jax.ShapeDtypeStruct((B, nsel, H_C, LANES), F32)],
        compiler_params=_cparams("parallel", "arbitrary"),
        name="moba_step_score",
    )(pt, q_rep, *([cache_k] * ppg))
    sel_flat = jnp.swapaxes(sel[:, :, :, 0], 1, 2).astype(jnp.int32).reshape(-1)
    tok = pl.BlockSpec((None, H_C, 1, HD_C), lambda b, h, pt, sl: (b, 0, 0, 0))
    srow = lambda i: pl.BlockSpec((None, H_C, MOBA_BLOCK),
                                  lambda b, h, pt, sl, i=i: (b, 0, sl[(b * H_C + h) * nsel + i]))
    vpage = lambda i, r: pl.BlockSpec(
        (None, None, HD_C, PAGE_SIZE),
        lambda b, h, pt, sl, i=i, r=r: (pt[b * n_pages + sl[(b * H_C + h) * nsel + i] * ppb + r], h, 0, 0))
    row = lambda a: a.reshape(B, H_C, 1, HD_C)
    out = pl.pallas_call(
        functools.partial(_moba_attend_kernel, nbp=nbp),
        grid_spec=pltpu.PrefetchScalarGridSpec(
            num_scalar_prefetch=2,
            grid=(B, H_C),
            in_specs=[tok, tok, tok,
                      pl.BlockSpec(bias_last.shape, lambda b, h, pt, sl: (0, 0, 0)),
                      pl.BlockSpec(memory_space=pltpu.SMEM)]
                     + [srow(i) for i in range(nsel)]
                     + [vpage(i, r) for i in range(nsel) for r in range(ppb)],
            out_specs=tok,
        ),
        out_shape=jax.ShapeDtypeStruct((B, H_C, 1, HD_C), F32),
        compiler_params=_cparams("parallel", "arbitrary"),
        name="moba_step_attend",
    )(pt, sel_flat, row(q), row(k_new), row(v_new), bias_last, bias_scalars,
      *([scores] * nsel), *([cache_v] * (nsel * ppb)))
    return out.reshape(B, H_C, HD_C)


def _s5_kernel(u_ref, h0r_ref, h0i_ref, wb_ref, ar_ref, ai_ref, wc_ref, d_ref, wglu_ref,
               o_ref, hr_ref, hi_ref, x_ref, *, tt, nb):
    ti = pl.program_id(0)
    ns = D_S5_STATE

    @pl.when(ti == 0)
    def _():
        hr_ref[...] = h0r_ref[...]
        hi_ref[...] = h0i_ref[...]

    nslab = ns // LANES
    if tt == 1:
        seqs, seq_rows, mat_rows, cols = [0], lambda b: slice(None), lambda b: slice(None), lambda b: slice(None)
        u = u_ref[...]
    else:
        seqs = range(nb)
        seq_rows = lambda b: pl.ds(b, tt, stride=nb)
        mat_rows = lambda b: slice(b * tt, (b + 1) * tt)
        cols = lambda b: slice(b * D_S5, (b + 1) * D_S5)
        u = jnp.concatenate([u_ref[:, cols(b)] for b in seqs], axis=0)
    nsp = S5_MM_SPLIT
    du, dsx = D_S5 // nsp, ns // nsp
    ub = u.astype(BF16)
    for part in range(2):
        for g in range(nsp):
            xg = _dot(ub[:, g * du:(g + 1) * du], wb_ref[g * du:(g + 1) * du, part * ns + g * dsx:part * ns + (g + 1) * dsx])
            for b in seqs:
                for c in range(dsx // LANES):
                    slab = (part * ns + g * dsx) // LANES + c
                    x_ref[slab, seq_rows(b), :] = xg[mat_rows(b), c * LANES:(c + 1) * LANES]
    slabs = lambda ref: jnp.stack([ref[:, c * LANES:(c + 1) * LANES] for c in range(nslab)])
    ar = slabs(ar_ref)
    ai = slabs(ai_ref)

    def step(t, carry):
        hr, hi = carry
        rows = pl.ds(pl.multiple_of(t * nb, nb), nb)
        nr = ar * hr - ai * hi + x_ref[0:nslab, rows, :]
        ni = ar * hi + ai * hr + x_ref[nslab:2 * nslab, rows, :]
        x_ref[0:nslab, rows, :] = nr
        x_ref[nslab:2 * nslab, rows, :] = ni
        return nr, ni

    hr, hi = lax.fori_loop(0, tt, step, (slabs(hr_ref), slabs(hi_ref)))
    for c in range(nslab):
        hr_ref[:, c * LANES:(c + 1) * LANES] = hr[c]
        hi_ref[:, c * LANES:(c + 1) * LANES] = hi[c]

    def state_cols(first_slab, n):
        hs = [jnp.concatenate([x_ref[first_slab + c, seq_rows(b), :] for c in range(n)], axis=-1) for b in seqs]
        return (hs[0] if len(hs) == 1 else jnp.concatenate(hs, axis=0)).astype(BF16)

    ys = []
    for g in range(nsp):
        yg = None
        for part in range(2):
            hg = state_cols((part * ns + g * dsx) // LANES, dsx // LANES)
            t = _dot(hg, wc_ref[part * ns + g * dsx:part * ns + (g + 1) * dsx, g * du:(g + 1) * du])
            yg = t if yg is None else yg + t
        ys.append(yg)
    y = jax.nn.gelu(jnp.concatenate(ys, axis=-1) + d_ref[...] * u)
    gl = _dot(y.astype(BF16), wglu_ref[...])
    o = gl[:, 0:D_S5] * jax.nn.sigmoid(gl[:, D_S5:2 * D_S5])
    for b in seqs:
        o_ref[:, cols(b)] = o[mat_rows(b), :]


def _s5(u, h0r, h0i, wb, ar, ai, wc, d, wglu, *, tt, nb):
    L = u.shape[0]
    const = lambda a: pl.BlockSpec(a.shape, lambda i: (0,) * a.ndim)
    rows = pl.BlockSpec((tt, nb * D_S5), lambda i: (i, 0))
    st = pl.BlockSpec((nb, D_S5_STATE), lambda i: (0, 0))
    args = (u, h0r, h0i, wb, ar, ai, wc, d, wglu)
    if tt == 1:
        assert L == 1
        rows = pl.BlockSpec((nb, D_S5), lambda i: (0, 0))
        args = (u.reshape(nb, D_S5),) + args[1:]
    return pl.pallas_call(
        functools.partial(_s5_kernel, tt=tt, nb=nb),
        grid=(L // tt,),
        in_specs=[rows] + [const(a) for a in args[1:]],
        out_specs=[rows, st, st],
        out_shape=[jax.ShapeDtypeStruct(args[0].shape, F32),
                   jax.ShapeDtypeStruct((nb, D_S5_STATE), F32), jax.ShapeDtypeStruct((nb, D_S5_STATE), F32)],
        scratch_shapes=[pltpu.VMEM((2 * D_S5_STATE // LANES, tt * nb, LANES), F32)],
        compiler_params=_cparams("arbitrary"),
        name="s5",
    )(*args)


def _rel_bucket(dist):
    n = jnp.maximum(dist, 0)
    exact = N_BUCKETS // 2
    nf = jnp.maximum(n, exact).astype(F32)
    large = exact + (jnp.log(nf / exact) / math.log(MAX_DISTANCE / exact) * (N_BUCKETS - exact)).astype(jnp.int32)
    return jnp.where(n < exact, n, jnp.minimum(large, N_BUCKETS - 1))


def _block_diag(w):
    G, a, b = w.shape
    eye = jnp.eye(G, dtype=w.dtype)
    return (eye[:, None, :, None] * w[:, :, None, :]).reshape(G * a, G * b)


def _s5_weights(A_re, A_im, log_dt, B_re, B_im, C_re, C_im, D):
    dt = jnp.exp(log_dt.astype(F32))[:, None]
    lr, li = A_re.astype(F32), A_im.astype(F32)
    mag = jnp.exp(lr * dt)
    ar, ai = mag * jnp.cos(li * dt), mag * jnp.sin(li * dt)
    den = lr * lr + li * li
    nr = ar - 1.0
    cr, ci = (nr * lr + ai * li) / den, (ai * lr - nr * li) / den
    bbr = cr[..., None] * B_re - ci[..., None] * B_im
    bbi = cr[..., None] * B_im + ci[..., None] * B_re
    wb = jnp.concatenate([_block_diag(bbr.transpose(0, 2, 1)), _block_diag(bbi.transpose(0, 2, 1))], axis=1)
    wc = jnp.concatenate([_block_diag(C_re.transpose(0, 2, 1)), -_block_diag(C_im.transpose(0, 2, 1))], axis=0)
    return (wb.astype(BF16), ar.reshape(1, -1), ai.reshape(1, -1), wc.astype(BF16), D.reshape(1, -1).astype(F32))


def _pad_rows_front(a, rows):
    return jnp.pad(a, ((0, 0), (rows - a.shape[1], 0), (0, 0)))


def _trunk(x, st, W, *, decode):
    B, L, D = x.shape
    T = B * L
    tm = min(ROW_TILE, T)
    out = {}

    x2 = x.reshape(T, D)
    z = _norm_proj_wide(x2, W['norm_mix'][0], W['ev_w_in'], tm=tm, chunk=EV_IN_TILE, name="ev_in_proj")
    zc = z.shape[1]
    if decode:
        lp = SUBLANES
        z3 = jnp.pad(z.reshape(B, L, zc), ((0, 0), (0, lp - L), (0, 0)))
        tt = lp
    else:
        lp = L
        z3 = z.reshape(B, L, zc)
        tt = min(ROW_TILE, L)
    conv0 = _pad_rows_front(st['gdn_conv'], SUBLANES)
    mixed, tail, sg, sh_t = _even_mixer(z3, conv0, st['gdn'], jnp.swapaxes(st['hgrn'], -1, -2), W['gdn_conv_w'],
                                        W['gdn_neg_a'], W['gdn_dt_bias'], W['gdn_norm'], W['hgrn_norm'], W['hgrn_lb'],
                                        l_real=L, tt=tt)
    out['gdn_conv'] = tail[:, SUBLANES - (GDN_CONV - 1):, :]
    out['gdn'] = sg
    out['hgrn'] = jnp.swapaxes(sh_t, -1, -2)
    if decode:
        parts = [(mixed[:, :L].reshape(T, 2 * D_A), W['ev_w_out'], pl.BlockSpec((tm, 2 * D_A), lambda i: (i, 0)))]
    else:
        parts = [(mixed, W['ev_w_out'], pl.BlockSpec((None, tm, 2 * D_A), lambda b, i: (b, i, 0)))]
    x2, fs = _post_mixer(x2, parts, st, W, 0, B, L, decode)
    out['ffn0'] = fs

    dc = H_C * HD_C
    gqk = jnp.concatenate([jnp.tile(W['moba_gq'], H_C), jnp.tile(W['moba_gk'], H_C)])
    if decode:
        qkv, u_tm = _od_qkv_proj(x2, W['norm_mix'][1], W['od_w_in'], gqk, tm=tm)
        u_tm = u_tm.reshape(L, B * D_S5)
        out['moba_k'] = qkv[:, dc:2 * dc].reshape(B, L, H_C, HD_C)
        out['moba_v'] = qkv[:, 2 * dc:].reshape(B, L, H_C, HD_C)
        heads = lambda a: a.reshape(B, H_C, HD_C)
        o_c = _moba_step(heads(qkv[:, :dc]), heads(qkv[:, dc:2 * dc]), heads(qkv[:, 2 * dc:]),
                         st['cache_k'], st['cache_v'], st['page_table'], *W['moba_bias_step'])
        o_c = o_c.reshape(T, dc)
        tts = 1
    else:
        qkv, u_tm, kt, vt = _od_qkv_proj(x2, W['norm_mix'][1], W['od_w_in'], gqk, tm=tm, seq=(B, L))
        out['moba_k'] = jnp.transpose(kt.reshape(B, H_C, HD_C, L), (0, 3, 1, 2))
        out['moba_v'] = jnp.transpose(vt.reshape(B, H_C, HD_C, L), (0, 3, 1, 2))
        o_c = _moba_seq(qkv.reshape(B, L, 3 * dc), vt, W['moba_bias_tiles']).reshape(T, dc)
        tts = min(S5_TIME_TILE, L)
    o_d, hr, hi = _s5(u_tm, st['s5_re'], st['s5_im'], *W['s5'], W['s5_glu_w'], tt=tts, nb=B)
    out['s5_re'] = hr.reshape(B, S5_GROUPS, S5_STATE)
    out['s5_im'] = hi.reshape(B, S5_GROUPS, S5_STATE)
    if decode:
        parts = [(o_c, W['od_w_out'][:dc], pl.BlockSpec((tm, dc), lambda i: (i, 0))),
                 (o_d, W['od_w_out'][dc:], pl.BlockSpec((tm, D_S5), lambda i: (i, 0)))]
    else:
        parts = [(o_c.reshape(B, L, dc), W['od_w_out'][:dc], pl.BlockSpec((None, tm, dc), lambda b, i: (b, i, 0))),
                 (o_d, W['od_w_out'][dc:], pl.BlockSpec((tm, D_S5), lambda b, i: (i, b)))]
    x2, fs = _post_mixer(x2, parts, st, W, 1, B, L, decode)
    out['ffn1'] = fs
    out['y'] = x2.reshape(B, L, D)
    return out


def _post_mixer(x2, parts, st, W, l, B, L, decode):
    T, D = x2.shape
    dq = MEM_H * MEM_HD
    buf = st['ffn_conv'][l]
    if decode:
        x2 = _proj_res(x2, parts, tm=T, name="mixer_out_proj")
        q = _norm_proj(x2, W['norm_mem'][l], W['mem_wq'][l], tm=T, tn=dq, n_norm_tiles=1, group=MEM_HD,
                       gain=jnp.tile(W['mem_gq'][l], MEM_H), name="mem_q_proj")
        o = _mem_attn_step(q.reshape(B, 1, dq), st['mem_k'], st['mem_v'], l).reshape(T, dq)
        x3 = _proj_res(x2, [(o, W['mem_wo'][l], pl.BlockSpec((T, dq), lambda i: (i, 0)))], tm=T, name="mem_out_proj")
        y, gate = _ffn_step(x3, buf[:, 0], buf[:, 1], W['norm_ffn'][l], W['ffn_w_gu'][l],
                            W['ffn_conv_w'][l], W['ffn_conv_b'][l], W['ffn_w_down'][l])
        fs = jnp.stack([buf[:, 1], gate], axis=1)
        return y, fs
    y, tail = _post_seq(x2.reshape(B, L, D), parts, st['mem_k'][l], st['mem_v'][l], W['norm_mem'][l], W['mem_wq'][l],
                        W['mem_gq'][l], W['mem_wo'][l], _pad_rows_front(buf, SUBLANES), W['norm_ffn'][l],
                        W['ffn_w_gu'][l], W['ffn_conv_w'][l], W['ffn_conv_b'][l], W['ffn_w_down'][l],
                        tm=min(ROW_TILE, L), name="post_mixer_seq")
    return y.reshape(T, D), tail[:, SUBLANES - (FFN_CONV - 1):, :]


def _mem_kv(mem, g_norm, w_kv, g_k):
    B, M, D = mem.shape
    dq = MEM_H * MEM_HD
    gain = jnp.concatenate([jnp.tile(g_k, MEM_H), jnp.ones((dq,), F32)])
    kv = _norm_proj(mem.reshape(B * M, D), g_norm, w_kv, tm=min(ROW_TILE, B * M), tn=dq, n_norm_tiles=1, group=MEM_HD,
                    gain=gain, name="mem_kv_proj")
    return kv[:, :dq].reshape(B, M, dq), kv[:, dq:].reshape(B, M, dq)


def kernel(x_prompt, x_sample, state_gdn_conv, state_gdn, state_hgrn, cache_moba_k, cache_moba_v, state_s5_re, state_s5_im, cache_mem_k, cache_mem_v, state_ffn_conv, page_table, mem_prompt, rel_bias, hgrn_lb_raw, norm_mix, norm_mem, norm_memkv, norm_ffn, mem_wq, mem_wkv, mem_gq, mem_gk, mem_wo, ffn_w_gu, ffn_conv_w, ffn_conv_b, ffn_w_down, ev_w_in, ev_w_out, gdn_conv_w, gdn_A_log, gdn_dt_bias, gdn_norm, hgrn_norm, od_w_in, od_w_out, moba_gq, moba_gk, s5_A_re, s5_A_im, s5_log_dt, s5_B_re, s5_B_im, s5_C_re, s5_C_im, s5_D, s5_glu_w):
    Bp, Lq, D = x_prompt.shape
    Bs = x_sample.shape[0]
    dc = H_C * HD_C
    dq = MEM_H * MEM_HD
    n_pages = page_table.shape[1]
    past = n_pages * PAGE_SIZE

    d3 = 3 * D_A
    w_in = ev_w_in[0]
    ba_cols = w_in[:, 4 * D_A:4 * D_A + 2 * H_A]
    w_in = jnp.concatenate([w_in[:, :4 * D_A], w_in[:, 4 * D_A + 2 * H_A:], ba_cols,
                            jnp.zeros((D, LANES - 2 * H_A), F32)], axis=1)
    lb_all = jnp.cumsum(jax.nn.softmax(hgrn_lb_raw.astype(F32), axis=0), axis=0)
    bias_t = rel_bias.T.astype(F32)
    nblk = MOBA_BLOCK
    by_rel = bias_t[:, _rel_bucket(jnp.arange(2 * nblk, dtype=jnp.int32))]
    mm = jnp.arange(2 * nblk, dtype=jnp.int32)

    def toeplitz(shift):
        y = by_rel[:, (shift + mm) % (2 * nblk)]
        flat = jnp.tile(y, (1, nblk))[:, :nblk * (2 * nblk - 1)]
        return flat.reshape(H_C, nblk, 2 * nblk - 1)[:, :, :nblk]

    far = bias_t[:, _rel_bucket(jnp.int32(2 * nblk))]
    tiles = jnp.stack([toeplitz(0), toeplitz(nblk)], axis=1) - far[:, None, None, None]
    bias_step = (by_rel[:, nblk - jnp.arange(nblk, dtype=jnp.int32)][:, None, :],
                 jnp.stack([far, bias_t[:, 0]], axis=1))
    W = dict(
        norm_mix=norm_mix, norm_mem=norm_mem, norm_ffn=norm_ffn,
        ev_w_in=w_in.astype(BF16), ev_w_out=ev_w_out[0].astype(BF16),
        gdn_conv_w=gdn_conv_w[0], gdn_neg_a=-jnp.exp(gdn_A_log[0].astype(F32)), gdn_dt_bias=gdn_dt_bias[0],
        gdn_norm=gdn_norm[0], hgrn_norm=hgrn_norm[0], hgrn_lb=lb_all[0],
        od_w_in=od_w_in[0].astype(BF16), od_w_out=od_w_out[0].astype(BF16),
        moba_gq=moba_gq[0], moba_gk=moba_gk[0], moba_bias_tiles=tiles, moba_bias_step=bias_step,
        s5=_s5_weights(s5_A_re[0], s5_A_im[0], s5_log_dt[0], s5_B_re[0], s5_B_im[0], s5_C_re[0], s5_C_im[0], s5_D[0]),
        s5_glu_w=s5_glu_w[0].astype(BF16),
        mem_wq=mem_wq.astype(BF16), mem_gq=mem_gq, mem_wo=mem_wo.astype(BF16),
        ffn_w_gu=ffn_w_gu.astype(BF16), ffn_conv_w=ffn_conv_w, ffn_conv_b=ffn_conv_b, ffn_w_down=ffn_w_down.astype(BF16),
    )

    mk_l, mv_l = [], []
    for l in range(2):
        mk, mv = _mem_kv(mem_prompt, norm_memkv[l], mem_wkv[l].astype(BF16), mem_gk[l])
        mk_l.append(mk)
        mv_l.append(mv)
    p_mem_k, p_mem_v = jnp.stack(mk_l), jnp.stack(mv_l)
    zeros = lambda *s: jnp.zeros(s, F32)
    st_p = dict(gdn_conv=zeros(Bp, GDN_CONV - 1, d3), gdn=zeros(Bp, H_A, HD_A, HD_A), hgrn=zeros(Bp, H_B, DK_B, DV_B),
                s5_re=zeros(Bp, D_S5_STATE), s5_im=zeros(Bp, D_S5_STATE), mem_k=p_mem_k, mem_v=p_mem_v,
                ffn_conv=zeros(2, Bp, FFN_CONV - 1, D_FF))
    op = _trunk(x_prompt, st_p, W, decode=False)

    st_s = dict(gdn_conv=state_gdn_conv[0], gdn=state_gdn[0], hgrn=state_hgrn[0],
                s5_re=state_s5_re[0].reshape(Bs, D_S5_STATE), s5_im=state_s5_im[0].reshape(Bs, D_S5_STATE),
                mem_k=cache_mem_k, mem_v=cache_mem_v,
                ffn_conv=state_ffn_conv,
                cache_k=jnp.transpose(cache_moba_k[0], (0, 2, 3, 1)), cache_v=jnp.transpose(cache_moba_v[0], (0, 2, 3, 1)),
                page_table=page_table)
    os_ = _trunk(x_sample, st_s, W, decode=True)

    M = mem_prompt.shape[1]
    return (op['y'], os_['y'],
            op['gdn_conv'][None], op['gdn'][None], op['hgrn'][None], op['moba_k'][None], op['moba_v'][None],
            op['s5_re'][None], op['s5_im'][None],
            p_mem_k.reshape(2, Bp, M, MEM_H, MEM_HD), p_mem_v.reshape(2, Bp, M, MEM_H, MEM_HD),
            jnp.stack([op['ffn0'], op['ffn1']]),
            os_['gdn_conv'][None], os_['gdn'][None], os_['hgrn'][None], os_['moba_k'][None], os_['moba_v'][None],
            os_['s5_re'][None], os_['s5_im'][None], jnp.stack([os_['ffn0'], os_['ffn1']]))
```
